```python
import jax, jax.numpy as jnp
from jax import lax
import numpy as np

D_MODEL = 2048
BATCH = 4
SEQ = 4096
DEPTH = 2

GRID_W = 64
CTX_LEN = 256
Q_BLOCK = 128
ROPE_THETA = 10000.0
NORM_EPS = 1e-6

A_HEAD_DIM = 128
A_HEADS = D_MODEL // 2 // A_HEAD_DIM
A_KV_HEADS = A_HEADS // 4
B_NOPE_DIM = 128
B_ROPE_DIM = 64
B_V_DIM = 128
B_HEADS = D_MODEL // 4 // B_V_DIM
B_Q_RANK = D_MODEL // 4
B_KV_RANK = D_MODEL // 8
C_WINDOWS = (2, 4, 8, 16)
C_GROUPS = 4
C_GROUP_DIM = D_MODEL // 4 // C_GROUPS
C_WIDTH = C_GROUPS * C_GROUP_DIM

A_Q_COLS = A_HEADS * A_HEAD_DIM
A_KV_COLS = A_KV_HEADS * A_HEAD_DIM
IN_SPLITS = (A_Q_COLS, A_KV_COLS, A_KV_COLS, B_Q_RANK, B_KV_RANK, B_ROPE_DIM, C_WIDTH)
IN_COLS = A_Q_COLS + 2 * A_KV_COLS + B_Q_RANK + B_KV_RANK + B_ROPE_DIM + C_WIDTH
MIX_WIDTH = A_Q_COLS + B_HEADS * B_V_DIM + C_WIDTH
D_FF = ((8 * D_MODEL // 3 + 255) // 256) * 256
N_MOD = 9

kernel_name = 'hybrid_dit_prefix_block'


def rmsnorm(x, g):
    xf = x.astype(jnp.float32)
    y = xf * lax.rsqrt(jnp.mean(xf * xf, axis=-1, keepdims=True) + NORM_EPS)
    return (y * g.astype(jnp.float32)).astype(x.dtype)


def axial_rope_tables(rows, cols, dim):
    n = dim // 4
    inv = ROPE_THETA ** (-jnp.arange(n, dtype=jnp.float32) / n)
    ang = jnp.concatenate([rows[:, None] * inv[None, :], cols[:, None] * inv[None, :]], axis=-1)
    return jnp.cos(ang), jnp.sin(ang)


def apply_rope(x, cos, sin):
    half = x.shape[-1] // 2
    x1, x2 = x[..., :half], x[..., half:]
    cs = cos[None, :, None, :].astype(x.dtype)
    sn = sin[None, :, None, :].astype(x.dtype)
    return jnp.concatenate([x1 * cs - x2 * sn, x1 * sn + x2 * cs], axis=-1)


def swiglu(x, w_in, w_out):
    a, b = jnp.split(x @ w_in, 2, axis=-1)
    return (jax.nn.silu(a) * b) @ w_out


def block_attention(q, k, v):
    B, Sq, Hkv, G, dk = q.shape
    scale = dk ** -0.5
    nb = Sq // Q_BLOCK
    qb = jnp.moveaxis(q.reshape(B, nb, Q_BLOCK, Hkv, G, dk), 1, 0)

    def one_block(qblk):
        s = jnp.einsum('bqhgd,bkhd->bhgqk', qblk, k, preferred_element_type=jnp.float32) * scale
        p = jax.nn.softmax(s, axis=-1).astype(v.dtype)
        return jnp.einsum('bhgqk,bkhd->bqhgd', p, v)

    o = lax.map(one_block, qb)
    return jnp.moveaxis(o, 0, 1).reshape(B, Sq, Hkv * G * v.shape[-1])


def split_cols(u):
    idx, acc = [], 0
    for n in IN_SPLITS[:-1]:
        acc += n
        idx.append(acc)
    return jnp.split(u, idx, axis=-1)


def gqa_q(aq, qg, cos, sin):
    B, L, _ = aq.shape
    q = rmsnorm(aq.reshape(B, L, A_HEADS, A_HEAD_DIM), qg)
    if cos is not None:
        q = apply_rope(q, cos, sin)
    return q.reshape(B, L, A_KV_HEADS, A_HEADS // A_KV_HEADS, A_HEAD_DIM)


def gqa_kv(ak, av, kg, cos, sin):
    B, L, _ = ak.shape
    k = rmsnorm(ak.reshape(B, L, A_KV_HEADS, A_HEAD_DIM), kg)
    if cos is not None:
        k = apply_rope(k, cos, sin)
    return k, av.reshape(B, L, A_KV_HEADS, A_HEAD_DIM)


def mla_q(bq, qg, w_uq, cos, sin):
    B, L, _ = bq.shape
    q = (rmsnorm(bq, qg) @ w_uq).reshape(B, L, B_HEADS, B_NOPE_DIM + B_ROPE_DIM)
    q_nope, q_rope = q[..., :B_NOPE_DIM], q[..., B_NOPE_DIM:]
    if cos is not None:
        q_rope = apply_rope(q_rope, cos, sin)
    return jnp.concatenate([q_nope, q_rope], axis=-1)[:, :, :, None, :]


def mla_kv(bkv, bkr, kvg, w_ukv, cos, sin):
    B, L, _ = bkv.shape
    kv = (rmsnorm(bkv, kvg) @ w_ukv).reshape(B, L, B_HEADS, B_NOPE_DIM + B_V_DIM)
    k_nope, v = kv[..., :B_NOPE_DIM], kv[..., B_NOPE_DIM:]
    k_rope = bkr[:, :, None, :]
    if cos is not None:
        k_rope = apply_rope(k_rope, cos, sin)
    k = jnp.concatenate([k_nope, jnp.broadcast_to(k_rope, (B, L, B_HEADS, B_ROPE_DIM))], axis=-1)
    return k, v


def pool_mix(u, w_pool, scale):
    B, L, _ = u.shape
    ug = u.reshape(B, L, C_GROUPS, C_GROUP_DIM)
    cs = jnp.cumsum(ug.astype(jnp.float32), axis=1)
    cs = jnp.concatenate([jnp.zeros_like(cs[:, :1]), cs], axis=1)
    t = jnp.arange(L)
    means = []
    for g, w in enumerate(C_WINDOWS):
        lo = jnp.clip(t - w // 2, 0, L)
        hi = jnp.clip(t - w // 2 + w, 0, L)
        csg = cs[:, :, g]
        means.append((csg[:, hi] - csg[:, lo]) / (hi - lo).astype(jnp.float32)[None, :, None])
    pooled = jnp.stack(means, axis=2).astype(u.dtype) - ug
    y = jnp.einsum('blgc,gcd->blgd', pooled, w_pool)
    return y.reshape(B, L, C_WIDTH) * scale


def token_mixers(u, uc, rope_a, rope_b, p, ctx_queries):
    cos_a, sin_a = rope_a
    cos_b, sin_b = rope_b
    aq, ak, av, bq, bkv, bkr, cu = split_cols(u)
    aqc, akc, avc, bqc, bkvc, bkrc, cuc = split_cols(uc)
    kA_c, vA_c = gqa_kv(akc, avc, p['a_k_g'], None, None)
    kA, vA = gqa_kv(ak, av, p['a_k_g'], cos_a, sin_a)
    yA = block_attention(gqa_q(aq, p['a_q_g'], cos_a, sin_a),
                         jnp.concatenate([kA_c, kA], axis=1), jnp.concatenate([vA_c, vA], axis=1))
    kB_c, vB_c = mla_kv(bkvc, bkrc, p['b_kv_g'], p['b_w_ukv'], None, None)
    kB, vB = mla_kv(bkv, bkr, p['b_kv_g'], p['b_w_ukv'], cos_b, sin_b)
    yB = block_attention(mla_q(bq, p['b_q_g'], p['b_w_uq'], cos_b, sin_b),
                         jnp.concatenate([kB_c, kB], axis=1), jnp.concatenate([vB_c, vB], axis=1))
    yC = pool_mix(cu, p['c_w_pool'], p['c_scale'])
    y = jnp.concatenate([yA, yB, yC], axis=-1)
    if not ctx_queries:
        return y, None
    yAc = block_attention(gqa_q(aqc, p['a_q_g'], None, None), kA_c, vA_c)
    yBc = block_attention(mla_q(bqc, p['b_q_g'], p['b_w_uq'], None, None), kB_c, vB_c)
    yCc = pool_mix(cuc, p['c_w_pool'], p['c_scale'])
    return y, jnp.concatenate([yAc, yBc, yCc], axis=-1)


def modulated_norm(x, g, shift, scale):
    return rmsnorm(x, g) * (1 + scale) + shift


def layer(h, hc, mod, modc, rope_a, rope_b, p, ctx_out):
    m = jnp.split(mod, N_MOD, axis=-1)
    mc = jnp.split(modc, N_MOD, axis=-1)
    ng = p['norm_g']
    h = h + 0.5 * m[2] * swiglu(modulated_norm(h, ng[0], m[0], m[1]), p['ffn1_in'], p['ffn1_out'])
    hc = hc + 0.5 * mc[2] * swiglu(modulated_norm(hc, ng[0], mc[0], mc[1]), p['ffn1_in'], p['ffn1_out'])
    u = modulated_norm(h, ng[1], m[3], m[4]) @ p['w_in']
    uc = modulated_norm(hc, ng[1], mc[3], mc[4]) @ p['w_in']
    y, yc = token_mixers(u, uc, rope_a, rope_b, p, ctx_out)
    h = h + m[5] * (y @ p['w_out'])
    h = h + 0.5 * m[8] * swiglu(modulated_norm(h, ng[2], m[6], m[7]), p['ffn2_in'], p['ffn2_out'])
    if not ctx_out:
        return h, None
    hc = hc + mc[5] * (yc @ p['w_out'])
    hc = hc + 0.5 * mc[8] * swiglu(modulated_norm(hc, ng[2], mc[6], mc[7]), p['ffn2_in'], p['ffn2_out'])
    return h, hc


def setup_inputs(seed: int = 0) -> dict:
    key = jax.random.key(seed)
    ks = jax.random.split(key, 22)
    f32 = jnp.float32
    D = D_MODEL

    def nrm(k, shape, scale):
        return jax.random.normal(k, shape, f32) * scale

    def gain(k, shape):
        return 1.0 + 0.02 * jax.random.normal(k, shape, f32)

    return {
        'x': nrm(ks[0], (BATCH, SEQ, D), 1.0),
        'c': nrm(ks[1], (BATCH, D), 1.0),
        'ctx': nrm(ks[2], (BATCH, CTX_LEN, D), 1.0),
        'c_ctx': nrm(ks[3], (D,), 1.0),
        'w_mod': nrm(ks[4], (DEPTH, D, N_MOD * D), 0.5 * D ** -0.5),
        'b_mod': nrm(ks[5], (DEPTH, N_MOD * D), 0.01),
        'norm_g': gain(ks[6], (DEPTH, 3, D)),
        'ffn1_in': nrm(ks[7], (DEPTH, D, 2 * D_FF), D ** -0.5),
        'ffn1_out': nrm(ks[8], (DEPTH, D_FF, D), D_FF ** -0.5),
        'w_in': nrm(ks[9], (DEPTH, D, IN_COLS), D ** -0.5),
        'a_q_g': gain(ks[10], (DEPTH, A_HEAD_DIM)),
        'a_k_g': gain(ks[11], (DEPTH, A_HEAD_DIM)),
        'b_q_g': gain(ks[12], (DEPTH, B_Q_RANK)),
        'b_kv_g': gain(ks[13], (DEPTH, B_KV_RANK)),
        'b_w_uq': nrm(ks[14], (DEPTH, B_Q_RANK, B_HEADS * (B_NOPE_DIM + B_ROPE_DIM)), B_Q_RANK ** -0.5),
        'b_w_ukv': nrm(ks[15], (DEPTH, B_KV_RANK, B_HEADS * (B_NOPE_DIM + B_V_DIM)), B_KV_RANK ** -0.5),
        'c_w_pool': nrm(ks[16], (DEPTH, C_GROUPS, C_GROUP_DIM, C_GROUP_DIM), C_GROUP_DIM ** -0.5),
        'c_scale': gain(ks[17], (DEPTH, C_WIDTH)),
        'w_out': nrm(ks[18], (DEPTH, MIX_WIDTH, D), MIX_WIDTH ** -0.5),
        'ffn2_in': nrm(ks[19], (DEPTH, D, 2 * D_FF), D ** -0.5),
        'ffn2_out': nrm(ks[20], (DEPTH, D_FF, D), D_FF ** -0.5),
        'final_g': gain(ks[21], (D,)),
    }


def reference(x, c, ctx, c_ctx, w_mod, b_mod, norm_g, ffn1_in, ffn1_out, w_in, a_q_g, a_k_g,
              b_q_g, b_kv_g, b_w_uq, b_w_ukv, c_w_pool, c_scale, w_out, ffn2_in, ffn2_out, final_g):
    B, S, _ = x.shape
    ROWS = S // GRID_W
    rows = jnp.repeat(jnp.arange(ROWS, dtype=jnp.float32), GRID_W)
    cols = jnp.tile(jnp.arange(GRID_W, dtype=jnp.float32), ROWS)
    rope_a = axial_rope_tables(rows, cols, A_HEAD_DIM)
    rope_b = axial_rope_tables(rows, cols, B_ROPE_DIM)
    c_act = jax.nn.silu(c)
    cc_act = jax.nn.silu(c_ctx)
    h, hc = x, ctx
    for i in range(DEPTH):
        p = {
            'norm_g': norm_g[i], 'ffn1_in': ffn1_in[i], 'ffn1_out': ffn1_out[i], 'w_in': w_in[i],
            'a_q_g': a_q_g[i], 'a_k_g': a_k_g[i], 'b_q_g': b_q_g[i], 'b_kv_g': b_kv_g[i],
            'b_w_uq': b_w_uq[i], 'b_w_ukv': b_w_ukv[i], 'c_w_pool': c_w_pool[i], 'c_scale': c_scale[i],
            'w_out': w_out[i], 'ffn2_in': ffn2_in[i], 'ffn2_out': ffn2_out[i],
        }
        mod = (c_act @ w_mod[i] + b_mod[i])[:, None, :]
        modc = (cc_act @ w_mod[i] + b_mod[i])[None, None, :]
        h, hc = layer(h, hc, mod, modc, rope_a, rope_b, p, i < DEPTH - 1)
    return rmsnorm(h, final_g)
```

```python
import functools
import math

import jax
import jax.numpy as jnp
import numpy as np
from jax import lax
from jax.experimental import pallas as pl
from jax.experimental.pallas import tpu as pltpu

F32 = jnp.float32
BF16 = jnp.bfloat16

GRID_W = 64
ROPE_THETA = 10000.0
NORM_EPS = 1e-6
N_MOD = 9
A_HEAD_DIM = 128
A_GROUP = 4
B_NOPE_DIM = 128
B_ROPE_DIM = 64
B_V_DIM = 128
B_QK_PAD = 256
C_WINDOWS = (2, 4, 8, 16)
C_GROUP_DIM = 128
LOG2E = 1.4426950408889634

V7X_LANES = 128
V7X_SUBLANES = 8
V7X_VMEM_BYTES = 64 * 1024 * 1024
V7X_VMEM_RESERVE_BYTES = 8 * 1024 * 1024
MOD_ROWS = V7X_SUBLANES


def _nbytes(shape, dtype):
    return int(np.prod(shape)) * jnp.dtype(dtype).itemsize


def _vmem_limit(pipelined, resident, temps):
    need = 2 * sum(pipelined) + sum(resident) + temps
    return int(min(V7X_VMEM_BYTES - V7X_VMEM_RESERVE_BYTES, need + need // 4))


def _rms(x, eps=NORM_EPS):
    return x * lax.rsqrt(jnp.mean(x * x, axis=-1, keepdims=True) + eps)


def _resident(shape):
    nd = len(shape)
    return pl.BlockSpec(shape, lambda *_: (0,) * nd, pipeline_mode=pl.Buffered(1))


def _mod_kernel(c_ref, w_ref, b_ref, o_ref):
    c = c_ref[...]
    act = (c * jax.nn.sigmoid(c)).astype(BF16)
    o_ref[...] = jnp.dot(act, w_ref[...].astype(BF16), preferred_element_type=F32) + b_ref[...]


def _modulation(c_rows, w_mod, b_mod, *, tn=1024):
    depth, d, nm = w_mod.shape
    assert nm % tn == 0
    limit = _vmem_limit([_nbytes((d, tn), F32), _nbytes((MOD_ROWS, tn), F32)],
                        [_nbytes((MOD_ROWS, d), F32)], _nbytes((d, tn), BF16))
    return pl.pallas_call(
        _mod_kernel,
        out_shape=jax.ShapeDtypeStruct((depth, MOD_ROWS, nm), F32),
        grid=(depth, nm // tn),
        in_specs=[
            pl.BlockSpec((MOD_ROWS, d), lambda l, j: (0, 0)),
            pl.BlockSpec((None, d, tn), lambda l, j: (l, 0, j)),
            pl.BlockSpec((None, 1, tn), lambda l, j: (l, 0, j)),
        ],
        out_specs=pl.BlockSpec((None, MOD_ROWS, tn), lambda l, j: (l, 0, j)),
        compiler_params=pltpu.CompilerParams(
            dimension_semantics=("parallel", "parallel"), vmem_limit_bytes=limit),
        name="modulation",
    )(c_rows, w_mod, b_mod.reshape(depth, 1, nm))


def _ffn_kernel(h_ref, sh_ref, sc_ref, gt_ref, ng_ref, wa_ref, wb_ref, wo_ref, fg_ref,
                o_ref, xn_ref, acc_ref, *, mod_row, final_norm):
    k = pl.program_id(1)
    row = mod_row(pl.program_id(0))

    @pl.when(k == 0)
    def _():
        y = _rms(h_ref[...]) * ng_ref[...]
        y = y * (1.0 + sc_ref[pl.ds(row, 1), :]) + sh_ref[pl.ds(row, 1), :]
        xn_ref[...] = y.astype(BF16)
        acc_ref[...] = jnp.zeros_like(acc_ref)

    xn = xn_ref[...]
    a = jnp.dot(xn, wa_ref[...], preferred_element_type=F32)
    b = jnp.dot(xn, wb_ref[...], preferred_element_type=F32)
    g = (a * jax.nn.sigmoid(a) * b).astype(BF16)
    acc_ref[...] += jnp.dot(g, wo_ref[...], preferred_element_type=F32)

    @pl.when(k == pl.num_programs(1) - 1)
    def _():
        out = h_ref[...] + 0.5 * gt_ref[pl.ds(row, 1), :] * acc_ref[...]
        if final_norm:
            out = _rms(out) * fg_ref[...]
        o_ref[...] = out


def _ffn(h, mod, j0, norm_g, w_in, w_out, final_g, *, mod_row, final_norm, tm, tf):
    n, d = h.shape
    dff = w_out.shape[0]
    assert n % tm == 0 and dff % tf == 0
    nk = dff // tf
    mod_spec = lambda j: pl.BlockSpec((MOD_ROWS, d), lambda i, k: (0, j))
    limit = _vmem_limit(
        [_nbytes((tm, d), F32) * 2, _nbytes((d, tf), BF16) * 2, _nbytes((tf, d), BF16)],
        [_nbytes((tm, d), BF16), _nbytes((tm, d), F32)],
        3 * _nbytes((tm, tf), F32) + _nbytes((tm, d), F32))
    return pl.pallas_call(
        functools.partial(_ffn_kernel, mod_row=mod_row, final_norm=final_norm),
        out_shape=jax.ShapeDtypeStruct((n, d), F32),
        grid=(n // tm, nk),
        in_specs=[
            pl.BlockSpec((tm, d), lambda i, k: (i, 0)),
            mod_spec(j0), mod_spec(j0 + 1), mod_spec(j0 + 2),
            pl.BlockSpec((1, d), lambda i, k: (0, 0)),
            pl.BlockSpec((d, tf), lambda i, k: (0, k)),
            pl.BlockSpec((d, tf), lambda i, k: (0, k + nk)),
            pl.BlockSpec((tf, d), lambda i, k: (k, 0)),
            pl.BlockSpec((1, d), lambda i, k: (0, 0)),
        ],
        out_specs=pl.BlockSpec((tm, d), lambda i, k: (i, 0)),
        scratch_shapes=[pltpu.VMEM((tm, d), BF16), pltpu.VMEM((tm, d), F32)],
        compiler_params=pltpu.CompilerParams(
            dimension_semantics=("parallel", "arbitrary"), vmem_limit_bytes=limit),
        name="swiglu_half_step",
    )(h, mod, mod, mod, norm_g, w_in, w_in, w_out, final_g)


def _rope(x, cos, sin_signed):
    return x * cos + pltpu.roll(x, V7X_LANES // 2, 1) * sin_signed


def _proj_kernel(h_ref, sh_ref, sc_ref, ng_ref, win_ref, aqg_ref, akg_ref, bqg_ref, bkvg_ref,
                 wuq_ref, wukv_ref, ca_ref, sa_ref, cb_ref, sb_ref,
                 qa_ref, ka_ref, va_ref, qb_ref, kb_ref, vb_ref, cu_ref, *, mod_row, dims):
    a_q, a_kv, b_q, b_kv, b_heads, c_w = dims
    row = mod_row(pl.program_id(0))
    y = _rms(h_ref[...]) * ng_ref[...]
    y = y * (1.0 + sc_ref[pl.ds(row, 1), :]) + sh_ref[pl.ds(row, 1), :]
    u = jnp.dot(y.astype(BF16), win_ref[...], preferred_element_type=F32)

    ca, sa, cb, sb = ca_ref[...], sa_ref[...], cb_ref[...], sb_ref[...]
    hd = A_HEAD_DIM
    o_ak, o_av, o_bq = a_q, a_q + a_kv, a_q + 2 * a_kv
    o_bkv, o_bkr = o_bq + b_q, o_bq + b_q + b_kv
    o_cu = o_bkr + V7X_LANES

    qg = aqg_ref[...] * (hd ** -0.5 * LOG2E)
    for h in range(a_q // hd):
        x = _rms(u[:, h * hd:(h + 1) * hd]) * qg
        qa_ref[:, h * hd:(h + 1) * hd] = _rope(x, ca, sa).astype(BF16)
    kg = akg_ref[...]
    for h in range(a_kv // hd):
        x = _rms(u[:, o_ak + h * hd:o_ak + (h + 1) * hd]) * kg
        ka_ref[:, h * hd:(h + 1) * hd] = _rope(x, ca, sa).astype(BF16)
    va_ref[...] = u[:, o_av:o_av + a_kv].astype(BF16)

    xq = (_rms(u[:, o_bq:o_bq + b_q]) * bqg_ref[...]).astype(BF16)
    q = jnp.dot(xq, wuq_ref[...], preferred_element_type=F32)
    q_scale = (B_NOPE_DIM + B_ROPE_DIM) ** -0.5 * LOG2E
    xkv = (_rms(u[:, o_bkv:o_bkv + b_kv]) * bkvg_ref[...]).astype(BF16)
    kv = jnp.dot(xkv, wukv_ref[...], preferred_element_type=F32)
    k_rope = _rope(u[:, o_bkr:o_bkr + V7X_LANES], cb, sb).astype(BF16)
    for h in range(b_heads):
        o = h * B_QK_PAD
        qb_ref[:, o:o + B_NOPE_DIM] = (q[:, o:o + B_NOPE_DIM] * q_scale).astype(BF16)
        qr = _rope(q[:, o + B_NOPE_DIM:o + B_QK_PAD], cb, sb)
        qb_ref[:, o + B_NOPE_DIM:o + B_QK_PAD] = (qr * q_scale).astype(BF16)
        kb_ref[:, o:o + B_NOPE_DIM] = kv[:, h * B_NOPE_DIM:(h + 1) * B_NOPE_DIM].astype(BF16)
        kb_ref[:, o + B_NOPE_DIM:o + B_QK_PAD] = k_rope
    vb_ref[...] = kv[:, b_heads * B_NOPE_DIM:].astype(BF16)

    cu_ref[...] = u[:, o_cu:o_cu + c_w]


def _proj(h, mod, j0, norm_g, w_in_p, a_q_g, a_k_g, b_q_g, b_kv_g, w_uq_p, w_ukv_p, tables,
          *, mod_row, table_block, dims, tm):
    n, d = h.shape
    a_q, a_kv, b_q, b_kv, b_heads, c_w = dims
    assert n % tm == 0
    ncols = w_in_p.shape[1]
    mod_spec = lambda j: pl.BlockSpec((MOD_ROWS, d), lambda i: (0, j))
    tab_spec = pl.BlockSpec((tm, V7X_LANES), lambda i: (table_block(i), 0))
    row_spec = lambda w: pl.BlockSpec((tm, w), lambda i: (i, 0))
    out_widths = [(a_q, BF16), (a_kv, BF16), (a_kv, BF16), (b_heads * B_QK_PAD, BF16),
                  (b_heads * B_QK_PAD, BF16), (b_heads * B_V_DIM, BF16), (c_w, F32)]
    limit = _vmem_limit(
        [_nbytes((tm, d), F32)] + [_nbytes((tm, w), t) for w, t in out_widths]
        + [4 * _nbytes((tm, V7X_LANES), F32)],
        [_nbytes(w_in_p.shape, BF16), _nbytes(w_uq_p.shape, BF16), _nbytes(w_ukv_p.shape, BF16)],
        _nbytes((tm, ncols), F32) + _nbytes((tm, d), F32) + 2 * _nbytes((tm, 4 * B_QK_PAD), F32))
    return pl.pallas_call(
        functools.partial(_proj_kernel, mod_row=mod_row, dims=dims),
        out_shape=[jax.ShapeDtypeStruct((n, w), t) for w, t in out_widths],
        grid=(n // tm,),
        in_specs=[
            row_spec(d), mod_spec(j0), mod_spec(j0 + 1),
            _resident((1, d)), _resident(w_in_p.shape),
            _resident((1, A_HEAD_DIM)), _resident((1, A_HEAD_DIM)),
            _resident((1, b_q)), _resident((1, b_kv)),
            _resident(w_uq_p.shape), _resident(w_ukv_p.shape),
            tab_spec, tab_spec, tab_spec, tab_spec,
        ],
        out_specs=[row_spec(w) for w, _ in out_widths],
        compiler_params=pltpu.CompilerParams(
            dimension_semantics=("parallel",), vmem_limit_bytes=limit),
        name="mixer_in_proj",
    )(h, mod, mod, norm_g, w_in_p, a_q_g, a_k_g, b_q_g, b_kv_g, w_uq_p, w_ukv_p, *tables)


def _attn_kernel(*refs, n_stack, dk, dv, chunks):
    q_ref, o_ref = refs[0], refs[-1]
    kv_refs = refs[1:-1]
    tq = q_ref.shape[0]
    q = jnp.concatenate([q_ref[:, h * dk:(h + 1) * dk] for h in range(n_stack)], axis=0)
    m_rows = n_stack * tq

    def step(k, v, carry):
        m, l, acc = carry
        s = lax.dot_general(q, k, (((1,), (1,)), ((), ())), preferred_element_type=F32)
        m_new = jnp.maximum(m, jnp.max(s, axis=-1, keepdims=True))
        alpha = jnp.exp2(m - m_new)
        p = jnp.exp2(s - m_new)
        l = alpha * l + jnp.sum(p, axis=-1, keepdims=True)
        acc = alpha * acc + jnp.dot(p.astype(BF16), v, preferred_element_type=F32)
        return m_new, l, acc

    carry = (jnp.full((m_rows, 1), -jnp.inf, F32), jnp.zeros((m_rows, 1), F32),
             jnp.zeros((m_rows, dv), F32))
    for seg, chunk in enumerate(chunks):
        k_ref, v_ref = kv_refs[2 * seg], kv_refs[2 * seg + 1]
        n_chunks = k_ref.shape[0] // chunk
        if n_chunks == 1:
            carry = step(k_ref[...], v_ref[...], carry)
        else:
            def body(c, carry, k_ref=k_ref, v_ref=v_ref, chunk=chunk):
                rows = pl.ds(pl.multiple_of(c * chunk, chunk), chunk)
                return step(k_ref[rows, :], v_ref[rows, :], carry)
            carry = lax.fori_loop(0, n_chunks, body, carry)
    _, l, acc = carry
    out = acc * (1.0 / l)
    for h in range(n_stack):
        o_ref[:, h * dv:(h + 1) * dv] = out[h * tq:(h + 1) * tq].astype(BF16)


def _attention(q, segments, *, batch, n_kv, n_stack, dk, dv, tq, chunk):
    nq = q.shape[0]
    lq = nq // batch
    assert lq % tq == 0
    tiles = lq // tq
    in_specs = [pl.BlockSpec((tq, n_stack * dk), lambda b, g, i: (b * tiles + i, g))]
    args, chunks, kv_bytes = [q], [], []
    for k, v in segments:
        lk = k.shape[0] // batch
        chunks.append(min(chunk, lk))
        assert lk % chunks[-1] == 0
        in_specs.append(pl.BlockSpec((lk, dk), lambda b, g, i: (b, g)))
        in_specs.append(pl.BlockSpec((lk, dv), lambda b, g, i: (b, g)))
        args += [k, v]
        kv_bytes += [_nbytes((lk, dk), BF16), _nbytes((lk, dv), BF16)]
    m_rows = n_stack * tq
    limit = _vmem_limit(
        [_nbytes((tq, n_stack * dk), BF16), _nbytes((tq, n_stack * dv), BF16)] + kv_bytes, [],
        3 * _nbytes((m_rows, max(chunks)), F32) + 4 * _nbytes((m_rows, V7X_LANES), F32))
    return pl.pallas_call(
        functools.partial(_attn_kernel, n_stack=n_stack, dk=dk, dv=dv, chunks=tuple(chunks)),
        out_shape=jax.ShapeDtypeStruct((nq, n_kv * n_stack * dv), BF16),
        grid=(batch, n_kv, tiles),
        in_specs=in_specs,
        out_specs=pl.BlockSpec((tq, n_stack * dv), lambda b, g, i: (b * tiles + i, g)),
        compiler_params=pltpu.CompilerParams(
            dimension_semantics=("parallel", "parallel", "arbitrary"), vmem_limit_bytes=limit),
        name="attention",
    )(*args)


def _out_kernel(h_ref, gt_ref, ya_ref, yb_ref, cu_ref, cup_ref, cun_ref, wp_ref, cs_ref, wo_ref,
                o_ref, ext_ref, *, mod_row, seg_len):
    i = pl.program_id(0)
    tm = h_ref.shape[0]
    halo = V7X_SUBLANES
    pos0 = lax.rem(i * tm, seg_len)
    ext_ref[0:halo, :] = jnp.where(pos0 > 0, cup_ref[...], 0.0)
    ext_ref[halo:halo + tm, :] = cu_ref[...]
    ext_ref[halo + tm:, :] = jnp.where(pos0 + tm < seg_len, cun_ref[...], 0.0)

    pos = pos0 + lax.broadcasted_iota(jnp.int32, (tm, 1), 0)
    a_w, b_w = ya_ref.shape[1], yb_ref.shape[1]
    y = jnp.dot(ya_ref[...], wo_ref[0:a_w, :], preferred_element_type=F32)
    y += jnp.dot(yb_ref[...], wo_ref[a_w:a_w + b_w, :], preferred_element_type=F32)
    gd = C_GROUP_DIM
    for g, w in enumerate(C_WINDOWS):
        cols = slice(g * gd, (g + 1) * gd)
        tot = ext_ref[halo - w // 2:halo - w // 2 + tm, cols]
        for off in range(1 - w // 2, w // 2):
            tot = tot + ext_ref[halo + off:halo + off + tm, cols]
        lo = jnp.clip(pos - w // 2, 0, seg_len)
        hi = jnp.clip(pos - w // 2 + w, 0, seg_len)
        pooled = tot / (hi - lo).astype(F32) - cu_ref[:, cols]
        yc = jnp.dot(pooled.astype(BF16), wp_ref[g], preferred_element_type=F32) * cs_ref[:, cols]
        r0 = a_w + b_w + g * gd
        y += jnp.dot(yc.astype(BF16), wo_ref[r0:r0 + gd, :], preferred_element_type=F32)
    o_ref[...] = h_ref[...] + gt_ref[pl.ds(mod_row(i), 1), :] * y


def _mix_out(h, mod, j_gate, ya, yb, cu, w_pool, c_scale, w_out, *, mod_row, seg_len, tm):
    n, d = h.shape
    c_w = cu.shape[1]
    assert n % tm == 0 and seg_len % tm == 0
    halo = V7X_SUBLANES
    per_tile = tm // halo
    last = n // halo - 1
    row_spec = lambda w: pl.BlockSpec((tm, w), lambda i: (i, 0))
    limit = _vmem_limit(
        [2 * _nbytes((tm, d), F32), _nbytes((tm, ya.shape[1]), BF16),
         _nbytes((tm, yb.shape[1]), BF16), _nbytes((tm, c_w), F32)],
        [_nbytes(w_out.shape, BF16), _nbytes(w_pool.shape, BF16), _nbytes((tm + 2 * halo, c_w), F32)],
        2 * _nbytes((tm, d), F32))
    return pl.pallas_call(
        functools.partial(_out_kernel, mod_row=mod_row, seg_len=seg_len),
        out_shape=jax.ShapeDtypeStruct((n, d), F32),
        grid=(n // tm,),
        in_specs=[
            row_spec(d),
            pl.BlockSpec((MOD_ROWS, d), lambda i: (0, j_gate)),
            row_spec(ya.shape[1]), row_spec(yb.shape[1]), row_spec(c_w),
            pl.BlockSpec((halo, c_w), lambda i: (jnp.maximum(i * per_tile - 1, 0), 0)),
            pl.BlockSpec((halo, c_w), lambda i: (jnp.minimum((i + 1) * per_tile, last), 0)),
            _resident(w_pool.shape), _resident((1, c_w)), _resident(w_out.shape),
        ],
        out_specs=row_spec(d),
        scratch_shapes=[pltpu.VMEM((tm + 2 * halo, c_w), F32)],
        compiler_params=pltpu.CompilerParams(
            dimension_semantics=("parallel",), vmem_limit_bytes=limit),
        name="mixer_out_proj",
    )(h, mod, ya, yb, cu, cu, cu, w_pool, c_scale, w_out)


def _rope_tables(seq, dim):
    n = dim // 4
    t = jnp.arange(seq)
    rows = (t // GRID_W).astype(F32)
    cols = (t % GRID_W).astype(F32)
    inv = ROPE_THETA ** (-jnp.arange(n, dtype=F32) / n)
    ang = jnp.concatenate([rows[:, None] * inv[None, :], cols[:, None] * inv[None, :]], axis=-1)
    cos, sin = jnp.cos(ang), jnp.sin(ang)
    pad = jnp.zeros((seq, V7X_LANES - dim), F32)
    return (jnp.concatenate([cos, cos, pad], axis=-1), jnp.concatenate([-sin, sin, pad], axis=-1))


def _identity_tables(rows, dim):
    one = jnp.concatenate([jnp.ones((rows, dim), F32), jnp.zeros((rows, V7X_LANES - dim), F32)], -1)
    return one, jnp.zeros((rows, V7X_LANES), F32)


def _w_in_columns(a_q, a_kv, b_q, b_kv, c_w):
    o = a_q + 2 * a_kv + b_q + b_kv
    half = B_ROPE_DIM // 2
    k1, k2 = np.arange(o, o + half), np.arange(o + half, o + B_ROPE_DIM)
    return np.concatenate([np.arange(o), k1, k2, k2, k1, np.arange(o + B_ROPE_DIM, o + B_ROPE_DIM + c_w)])


def _w_uq_columns(heads):
    hd = B_NOPE_DIM + B_ROPE_DIM
    half = B_ROPE_DIM // 2
    out = []
    for h in range(heads):
        b = h * hd
        x1 = np.arange(b + B_NOPE_DIM, b + B_NOPE_DIM + half)
        x2 = x1 + half
        out += [np.arange(b, b + B_NOPE_DIM), x1, x2, x2, x1]
    return np.concatenate(out)


def _w_ukv_columns(heads):
    hd = B_NOPE_DIM + B_V_DIM
    ks = [np.arange(h * hd, h * hd + B_NOPE_DIM) for h in range(heads)]
    vs = [np.arange(h * hd + B_NOPE_DIM, (h + 1) * hd) for h in range(heads)]
    return np.concatenate(ks + vs)


def kernel(x, c, ctx, c_ctx, w_mod, b_mod, norm_g, ffn1_in, ffn1_out, w_in, a_q_g, a_k_g, b_q_g,
           b_kv_g, b_w_uq, b_w_ukv, c_w_pool, c_scale, w_out, ffn2_in, ffn2_out, final_g):
    batch, seq, d = x.shape
    ctx_len = ctx.shape[1]
    depth = w_mod.shape[0]
    assert batch + 1 <= MOD_ROWS
    b_q, b_kv = b_q_g.shape[1], b_kv_g.shape[1]
    b_heads = b_w_ukv.shape[2] // (B_NOPE_DIM + B_V_DIM)
    c_w = c_scale.shape[1]
    a_q = w_out.shape[1] - b_heads * B_V_DIM - c_w
    a_kv = a_q // A_GROUP
    dims = (a_q, a_kv, b_q, b_kv, b_heads, c_w)

    tm, tm_ctx, tf = 512, 256, 512
    tiles_per_seq = seq // tm
    lat_row = lambda i: lax.div(i, tiles_per_seq)
    ctx_row = lambda i: batch

    c_rows = jnp.concatenate([c, c_ctx[None, :], jnp.zeros((MOD_ROWS - batch - 1, d), F32)], axis=0)
    mod = _modulation(c_rows, w_mod, b_mod)

    tab_lat = _rope_tables(seq, A_HEAD_DIM) + _rope_tables(seq, B_ROPE_DIM)
    tab_ctx = _identity_tables(tm_ctx, A_HEAD_DIM) + _identity_tables(tm_ctx, B_ROPE_DIM)
    cols_in = _w_in_columns(a_q, a_kv, b_q, b_kv, c_w)
    cols_uq, cols_ukv = _w_uq_columns(b_heads), _w_ukv_columns(b_heads)

    h = x.reshape(batch * seq, d)
    hc = ctx.reshape(batch * ctx_len, d)
    for i in range(depth):
        last = i == depth - 1
        ng = norm_g[i]
        f1_in, f1_out = ffn1_in[i].astype(BF16), ffn1_out[i].astype(BF16)
        f2_in, f2_out = ffn2_in[i].astype(BF16), ffn2_out[i].astype(BF16)
        w_in_p = w_in[i][:, cols_in].astype(BF16)
        w_uq_p = b_w_uq[i][:, cols_uq].astype(BF16)
        w_ukv_p = b_w_ukv[i][:, cols_ukv].astype(BF16)
        w_o, w_pool = w_out[i].astype(BF16), c_w_pool[i].astype(BF16)
        gains = (a_q_g[i][None], a_k_g[i][None], b_q_g[i][None], b_kv_g[i][None])
        fg = final_g[None]

        ffn = functools.partial(_ffn, final_g=fg, tf=tf)
        h = ffn(h, mod[i], 0, ng[0:1], f1_in, f1_out, mod_row=lat_row, final_norm=False, tm=tm)
        hc = ffn(hc, mod[i], 0, ng[0:1], f1_in, f1_out, mod_row=ctx_row, final_norm=False, tm=tm_ctx)

        proj = functools.partial(_proj, norm_g=ng[1:2], w_in_p=w_in_p, a_q_g=gains[0], a_k_g=gains[1],
                                 b_q_g=gains[2], b_kv_g=gains[3], w_uq_p=w_uq_p, w_ukv_p=w_ukv_p,
                                 dims=dims)
        qa, ka, va, qb, kb, vb, cu = proj(h, mod[i], 3, tables=tab_lat, mod_row=lat_row,
                                          table_block=lambda t: lax.rem(t, tiles_per_seq), tm=tm)
        qac, kac, vac, qbc, kbc, vbc, cuc = proj(hc, mod[i], 3, tables=tab_ctx, mod_row=ctx_row,
                                                 table_block=lambda t: 0, tm=tm_ctx)

        attn_a = functools.partial(_attention, batch=batch, n_kv=a_kv // A_HEAD_DIM, n_stack=A_GROUP,
                                   dk=A_HEAD_DIM, dv=A_HEAD_DIM, tq=256, chunk=512)
        attn_b = functools.partial(_attention, batch=batch, n_kv=b_heads, n_stack=1,
                                   dk=B_QK_PAD, dv=B_V_DIM, tq=512, chunk=512)
        ya = attn_a(qa, [(kac, vac), (ka, va)])
        yb = attn_b(qb, [(kbc, vbc), (kb, vb)])
        mix = functools.partial(_mix_out, w_pool=w_pool, c_scale=c_scale[i][None], w_out=w_o)
        h = mix(h, mod[i], 5, ya, yb, cu, mod_row=lat_row, seg_len=seq, tm=tm)
        h = ffn(h, mod[i], 6, ng[2:3], f2_in, f2_out, mod_row=lat_row, final_norm=last, tm=tm)
        if not last:
            yac = attn_a(qac, [(kac, vac)], tq=ctx_len)
            ybc = attn_b(qbc, [(kbc, vbc)], tq=ctx_len)
            hc = mix(hc, mod[i], 5, yac, ybc, cuc, mod_row=ctx_row, seg_len=ctx_len, tm=tm_ctx)
            hc = ffn(hc, mod[i], 6, ng[2:3], f2_in, f2_out, mod_row=ctx_row, final_norm=False, tm=tm_ctx)
    return h.reshape(batch, seq, d)
```

```python
import functools
import math

import jax
import jax.numpy as jnp
import numpy as np
from jax import lax
from jax.experimental import pallas as pl
from jax.experimental.pallas import tpu as pltpu

F32 = jnp.float32
BF16 = jnp.bfloat16

GRID_W = 64
ROPE_THETA = 10000.0
NORM_EPS = 1e-6
N_MOD = 9
A_HEAD_DIM = 128
A_GROUP = 4
B_NOPE_DIM = 128
B_ROPE_DIM = 64
B_V_DIM = 128
B_QK_PAD = 256
C_WINDOWS = (2, 4, 8, 16)
C_GROUP_DIM = 128
LOG2E = 1.4426950408889634

V7X_LANES = 128
V7X_SUBLANES = 8
V7X_VMEM_BYTES = 64 * 1024 * 1024
V7X_VMEM_RESERVE_BYTES = 8 * 1024 * 1024
MOD_ROWS = V7X_SUBLANES


def _nbytes(shape, dtype):
    return int(np.prod(shape)) * jnp.dtype(dtype).itemsize


def _vmem_limit(pipelined, resident, temps):
    need = 2 * sum(pipelined) + sum(resident) + temps
    return int(min(V7X_VMEM_BYTES - V7X_VMEM_RESERVE_BYTES, need + need // 4))


def _rms(x, eps=NORM_EPS):
    return x * lax.rsqrt(jnp.mean(x * x, axis=-1, keepdims=True) + eps)


def _resident(shape):
    nd = len(shape)
    return pl.BlockSpec(shape, lambda *_: (0,) * nd, pipeline_mode=pl.Buffered(1))


def _mod_kernel(c_ref, w_ref, b_ref, o_ref):
    c = c_ref[...]
    act = (c * jax.nn.sigmoid(c)).astype(BF16)
    o_ref[...] = jnp.dot(act, w_ref[...].astype(BF16), preferred_element_type=F32) + b_ref[...]


def _modulation(c_rows, w_mod, b_mod, *, tn=1024):
    depth, d, nm = w_mod.shape
    assert nm % tn == 0
    limit = _vmem_limit([_nbytes((d, tn), F32), _nbytes((MOD_ROWS, tn), F32)],
                        [_nbytes((MOD_ROWS, d), F32)], _nbytes((d, tn), BF16))
    return pl.pallas_call(
        _mod_kernel,
        out_shape=jax.ShapeDtypeStruct((depth, MOD_ROWS, nm), F32),
        grid=(depth, nm // tn),
        in_specs=[
            pl.BlockSpec((MOD_ROWS, d), lambda l, j: (0, 0)),
            pl.BlockSpec((None, d, tn), lambda l, j: (l, 0, j)),
            pl.BlockSpec((None, 1, tn), lambda l, j: (l, 0, j)),
        ],
        out_specs=pl.BlockSpec((None, MOD_ROWS, tn), lambda l, j: (l, 0, j)),
        compiler_params=pltpu.CompilerParams(
            dimension_semantics=("parallel", "parallel"), vmem_limit_bytes=limit),
        name="modulation",
    )(c_rows, w_mod, b_mod.reshape(depth, 1, nm))


def _ffn_kernel(h_ref, sh_ref, sc_ref, gt_ref, ng_ref, wa_ref, wb_ref, wo_ref, fg_ref,
                o_ref, xn_ref, acc_ref, *, mod_row, final_norm):
    k = pl.program_id(1)
    row = mod_row(pl.program_id(0))

    @pl.when(k == 0)
    def _():
        y = _rms(h_ref[...]) * ng_ref[...]
        y = y * (1.0 + sc_ref[pl.ds(row, 1), :]) + sh_ref[pl.ds(row, 1), :]
        xn_ref[...] = y.astype(BF16)
        acc_ref[...] = jnp.zeros_like(acc_ref)

    xn = xn_ref[...]
    a = jnp.dot(xn, wa_ref[...], preferred_element_type=F32)
    b = jnp.dot(xn, wb_ref[...], preferred_element_type=F32)
    g = (a * jax.nn.sigmoid(a) * b).astype(BF16)
    acc_ref[...] += jnp.dot(g, wo_ref[...], preferred_element_type=F32)

    @pl.when(k == pl.num_programs(1) - 1)
    def _():
        out = h_ref[...] + 0.5 * gt_ref[pl.ds(row, 1), :] * acc_ref[...]
        if final_norm:
            out = _rms(out) * fg_ref[...]
        o_ref[...] = out


def _ffn(h, mod, j0, norm_g, w_in, w_out, final_g, *, mod_row, final_norm, tm, tf):
    n, d = h.shape
    dff = w_out.shape[0]
    assert n % tm == 0 and dff % tf == 0
    nk = dff // tf
    mod_spec = lambda j: pl.BlockSpec((MOD_ROWS, d), lambda i, k: (0, j))
    limit = _vmem_limit(
        [_nbytes((tm, d), F32) * 2, _nbytes((d, tf), BF16) * 2, _nbytes((tf, d), BF16)],
        [_nbytes((tm, d), BF16), _nbytes((tm, d), F32)],
        3 * _nbytes((tm, tf), F32) + _nbytes((tm, d), F32))
    return pl.pallas_call(
        functools.partial(_ffn_kernel, mod_row=mod_row, final_norm=final_norm),
        out_shape=jax.ShapeDtypeStruct((n, d), F32),
        grid=(n // tm, nk),
        in_specs=[
            pl.BlockSpec((tm, d), lambda i, k: (i, 0)),
            mod_spec(j0), mod_spec(j0 + 1), mod_spec(j0 + 2),
            pl.BlockSpec((1, d), lambda i, k: (0, 0)),
            pl.BlockSpec((d, tf), lambda i, k: (0, k)),
            pl.BlockSpec((d, tf), lambda i, k: (0, k + nk)),
            pl.BlockSpec((tf, d), lambda i, k: (k, 0)),
            pl.BlockSpec((1, d), lambda i, k: (0, 0)),
        ],
        out_specs=pl.BlockSpec((tm, d), lambda i, k: (i, 0)),
        scratch_shapes=[pltpu.VMEM((tm, d), BF16), pltpu.VMEM((tm, d), F32)],
        compiler_params=pltpu.CompilerParams(
            dimension_semantics=("parallel", "arbitrary"), vmem_limit_bytes=limit),
        name="swiglu_half_step",
    )(h, mod, mod, mod, norm_g, w_in, w_in, w_out, final_g)


def _rope(x, cos, sin_signed):
    return x * cos + pltpu.roll(x, V7X_LANES // 2, 1) * sin_signed


def _proj_kernel(h_ref, sh_ref, sc_ref, ng_ref, win_ref, aqg_ref, akg_ref, bqg_ref, bkvg_ref,
                 wuq_ref, wukv_ref, ca_ref, sa_ref, cb_ref, sb_ref,
                 qa_ref, ka_ref, va_ref, qb_ref, kb_ref, vb_ref, cu_ref, *, mod_row, dims):
    a_q, a_kv, b_q, b_kv, b_heads, c_w = dims
    row = mod_row(pl.program_id(0))
    y = _rms(h_ref[...]) * ng_ref[...]
    y = y * (1.0 + sc_ref[pl.ds(row, 1), :]) + sh_ref[pl.ds(row, 1), :]
    u = jnp.dot(y.astype(BF16), win_ref[...], preferred_element_type=F32)

    ca, sa, cb, sb = ca_ref[...], sa_ref[...], cb_ref[...], sb_ref[...]
    hd = A_HEAD_DIM
    o_ak, o_av, o_bq = a_q, a_q + a_kv, a_q + 2 * a_kv
    o_bkv, o_bkr = o_bq + b_q, o_bq + b_q + b_kv
    o_cu = o_bkr + V7X_LANES

    qg = aqg_ref[...] * (hd ** -0.5 * LOG2E)
    for h in range(a_q // hd):
        x = _rms(u[:, h * hd:(h + 1) * hd]) * qg
        qa_ref[:, h * hd:(h + 1) * hd] = _rope(x, ca, sa).astype(BF16)
    kg = akg_ref[...]
    for h in range(a_kv // hd):
        x = _rms(u[:, o_ak + h * hd:o_ak + (h + 1) * hd]) * kg
        ka_ref[:, h * hd:(h + 1) * hd] = _rope(x, ca, sa).astype(BF16)
    va_ref[...] = u[:, o_av:o_av + a_kv].astype(BF16)

    xq = (_rms(u[:, o_bq:o_bq + b_q]) * bqg_ref[...]).astype(BF16)
    q = jnp.dot(xq, wuq_ref[...], preferred_element_type=F32)
    q_scale = (B_NOPE_DIM + B_ROPE_DIM) ** -0.5 * LOG2E
    xkv = (_rms(u[:, o_bkv:o_bkv + b_kv]) * bkvg_ref[...]).astype(BF16)
    kv = jnp.dot(xkv, wukv_ref[...], preferred_element_type=F32)
    k_rope = _rope(u[:, o_bkr:o_bkr + V7X_LANES], cb, sb).astype(BF16)
    for h in range(b_heads):
        o = h * B_QK_PAD
        qb_ref[:, o:o + B_NOPE_DIM] = (q[:, o:o + B_NOPE_DIM] * q_scale).astype(BF16)
        qr = _rope(q[:, o + B_NOPE_DIM:o + B_QK_PAD], cb, sb)
        qb_ref[:, o + B_NOPE_DIM:o + B_QK_PAD] = (qr * q_scale).astype(BF16)
        kb_ref[:, o:o + B_NOPE_DIM] = kv[:, h * B_NOPE_DIM:(h + 1) * B_NOPE_DIM].astype(BF16)
        kb_ref[:, o + B_NOPE_DIM:o + B_QK_PAD] = k_rope
    vb_ref[...] = kv[:, b_heads * B_NOPE_DIM:].astype(BF16)

    cu_ref[...] = u[:, o_cu:o_cu + c_w]


def _proj(h, mod, j0, norm_g, w_in_p, a_q_g, a_k_g, b_q_g, b_kv_g, w_uq_p, w_ukv_p, tables,
          *, mod_row, table_block, dims, tm):
    n, d = h.shape
    a_q, a_kv, b_q, b_kv, b_heads, c_w = dims
    assert n % tm == 0
    ncols = w_in_p.shape[1]
    mod_spec = lambda j: pl.BlockSpec((MOD_ROWS, d), lambda i: (0, j))
    tab_spec = pl.BlockSpec((tm, V7X_LANES), lambda i: (table_block(i), 0))
    row_spec = lambda w: pl.BlockSpec((tm, w), lambda i: (i, 0))
    out_widths = [(a_q, BF16), (a_kv, BF16), (a_kv, BF16), (b_heads * B_QK_PAD, BF16),
                  (b_heads * B_QK_PAD, BF16), (b_heads * B_V_DIM, BF16), (c_w, F32)]
    limit = _vmem_limit(
        [_nbytes((tm, d), F32)] + [_nbytes((tm, w), t) for w, t in out_widths]
        + [4 * _nbytes((tm, V7X_LANES), F32)],
        [_nbytes(w_in_p.shape, BF16), _nbytes(w_uq_p.shape, BF16), _nbytes(w_ukv_p.shape, BF16)],
        _nbytes((tm, ncols), F32) + _nbytes((tm, d), F32) + 2 * _nbytes((tm, 4 * B_QK_PAD), F32))
    return pl.pallas_call(
        functools.partial(_proj_kernel, mod_row=mod_row, dims=dims),
        out_shape=[jax.ShapeDtypeStruct((n, w), t) for w, t in out_widths],
        grid=(n // tm,),
        in_specs=[
            row_spec(d), mod_spec(j0), mod_spec(j0 + 1),
            _resident((1, d)), _resident(w_in_p.shape),
            _resident((1, A_HEAD_DIM)), _resident((1, A_HEAD_DIM)),
            _resident((1, b_q)), _resident((1, b_kv)),
            _resident(w_uq_p.shape), _resident(w_ukv_p.shape),
            tab_spec, tab_spec, tab_spec, tab_spec,
        ],
        out_specs=[row_spec(w) for w, _ in out_widths],
        compiler_params=pltpu.CompilerParams(
            dimension_semantics=("parallel",), vmem_limit_bytes=limit),
        name="mixer_in_proj",
    )(h, mod, mod, norm_g, w_in_p, a_q_g, a_k_g, b_q_g, b_kv_g, w_uq_p, w_ukv_p, *tables)


def _attn_kernel(*refs, n_stack, dk, dv, chunks):
    q_ref, o_ref = refs[0], refs[-1]
    kv_refs = refs[1:-1]
    tq = q_ref.shape[0]
    q = jnp.concatenate([q_ref[:, h * dk:(h + 1) * dk] for h in range(n_stack)], axis=0)
    m_rows = n_stack * tq

    def step(k, v, carry):
        m, acc = carry
        s = lax.dot_general(q, k, (((1,), (1,)), ((), ())), preferred_element_type=F32)
        m_new = jnp.maximum(m, jnp.max(s, axis=-1, keepdims=True))
        p = jnp.exp2(s - m_new).astype(BF16)
        v_ones = jnp.concatenate([v, jnp.ones((v.shape[0], V7X_LANES), BF16)], axis=1)
        acc = jnp.exp2(m - m_new) * acc + jnp.dot(p, v_ones, preferred_element_type=F32)
        return m_new, acc

    carry = (jnp.full((m_rows, 1), -jnp.inf, F32), jnp.zeros((m_rows, dv + V7X_LANES), F32))
    for seg, chunk in enumerate(chunks):
        k_ref, v_ref = kv_refs[2 * seg], kv_refs[2 * seg + 1]
        n_chunks = k_ref.shape[0] // chunk
        for c in range(n_chunks):
            carry = step(k_ref[c * chunk:(c + 1) * chunk, :], v_ref[c * chunk:(c + 1) * chunk, :], carry)
    _, acc = carry
    out = acc[:, :dv] / acc[:, dv:dv + dv]
    for h in range(n_stack):
        o_ref[:, h * dv:(h + 1) * dv] = out[h * tq:(h + 1) * tq].astype(BF16)


def _attention(q, segments, *, batch, n_kv, n_stack, dk, dv, tq, chunk):
    nq = q.shape[0]
    lq = nq // batch
    assert lq % tq == 0 and dv == V7X_LANES
    tiles = lq // tq
    in_specs = [pl.BlockSpec((tq, n_stack * dk), lambda b, g, i: (b * tiles + i, g))]
    args, chunks, kv_bytes = [q], [], []
    for k, v in segments:
        lk = k.shape[0] // batch
        chunks.append(min(chunk, lk))
        assert lk % chunks[-1] == 0
        in_specs.append(pl.BlockSpec((lk, dk), lambda b, g, i: (b, g)))
        in_specs.append(pl.BlockSpec((lk, dv), lambda b, g, i: (b, g)))
        args += [k, v]
        kv_bytes += [_nbytes((lk, dk), BF16), _nbytes((lk, dv), BF16)]
    m_rows = n_stack * tq
    limit = _vmem_limit(
        [_nbytes((tq, n_stack * dk), BF16), _nbytes((tq, n_stack * dv), BF16)] + kv_bytes, [],
        24 * _nbytes((m_rows, max(chunks)), F32) + 4 * _nbytes((m_rows, V7X_LANES), F32))
    return pl.pallas_call(
        functools.partial(_attn_kernel, n_stack=n_stack, dk=dk, dv=dv, chunks=tuple(chunks)),
        out_shape=jax.ShapeDtypeStruct((nq, n_kv * n_stack * dv), BF16),
        grid=(batch, n_kv, tiles),
        in_specs=in_specs,
        out_specs=pl.BlockSpec((tq, n_stack * dv), lambda b, g, i: (b * tiles + i, g)),
        compiler_params=pltpu.CompilerParams(
            dimension_semantics=("parallel", "parallel", "arbitrary"), vmem_limit_bytes=limit),
        name="attention",
    )(*args)


def _out_kernel(h_ref, gt_ref, ya_ref, yb_ref, cu_ref, cup_ref, cun_ref, wp_ref, cs_ref, wo_ref,
                o_ref, ext_ref, *, mod_row, seg_len):
    i = pl.program_id(0)
    tm = h_ref.shape[0]
    halo = V7X_SUBLANES
    pos0 = lax.rem(i * tm, seg_len)
    ext_ref[0:halo, :] = jnp.where(pos0 > 0, cup_ref[...], 0.0)
    ext_ref[halo:halo + tm, :] = cu_ref[...]
    ext_ref[halo + tm:, :] = jnp.where(pos0 + tm < seg_len, cun_ref[...], 0.0)

    pos = pos0 + lax.broadcasted_iota(jnp.int32, (tm, 1), 0)
    a_w, b_w = ya_ref.shape[1], yb_ref.shape[1]
    y = jnp.dot(ya_ref[...], wo_ref[0:a_w, :], preferred_element_type=F32)
    y += jnp.dot(yb_ref[...], wo_ref[a_w:a_w + b_w, :], preferred_element_type=F32)
    gd = C_GROUP_DIM
    for g, w in enumerate(C_WINDOWS):
        cols = slice(g * gd, (g + 1) * gd)
        tot = ext_ref[halo - w // 2:halo - w // 2 + tm, cols]
        for off in range(1 - w // 2, w // 2):
            tot = tot + ext_ref[halo + off:halo + off + tm, cols]
        lo = jnp.clip(pos - w // 2, 0, seg_len)
        hi = jnp.clip(pos - w // 2 + w, 0, seg_len)
        pooled = tot / (hi - lo).astype(F32) - cu_ref[:, cols]
        yc = jnp.dot(pooled.astype(BF16), wp_ref[g], preferred_element_type=F32) * cs_ref[:, cols]
        r0 = a_w + b_w + g * gd
        y += jnp.dot(yc.astype(BF16), wo_ref[r0:r0 + gd, :], preferred_element_type=F32)
    o_ref[...] = h_ref[...] + gt_ref[pl.ds(mod_row(i), 1), :] * y


def _mix_out(h, mod, j_gate, ya, yb, cu, w_pool, c_scale, w_out, *, mod_row, seg_len, tm):
    n, d = h.shape
    c_w = cu.shape[1]
    assert n % tm == 0 and seg_len % tm == 0
    halo = V7X_SUBLANES
    per_tile = tm // halo
    last = n // halo - 1
    row_spec = lambda w: pl.BlockSpec((tm, w), lambda i: (i, 0))
    limit = _vmem_limit(
        [2 * _nbytes((tm, d), F32), _nbytes((tm, ya.shape[1]), BF16),
         _nbytes((tm, yb.shape[1]), BF16), _nbytes((tm, c_w), F32)],
        [_nbytes(w_out.shape, BF16), _nbytes(w_pool.shape, BF16), _nbytes((tm + 2 * halo, c_w), F32)],
        2 * _nbytes((tm, d), F32))
    return pl.pallas_call(
        functools.partial(_out_kernel, mod_row=mod_row, seg_len=seg_len),
        out_shape=jax.ShapeDtypeStruct((n, d), F32),
        grid=(n // tm,),
        in_specs=[
            row_spec(d),
            pl.BlockSpec((MOD_ROWS, d), lambda i: (0, j_gate)),
            row_spec(ya.shape[1]), row_spec(yb.shape[1]), row_spec(c_w),
            pl.BlockSpec((halo, c_w), lambda i: (jnp.maximum(i * per_tile - 1, 0), 0)),
            pl.BlockSpec((halo, c_w), lambda i: (jnp.minimum((i + 1) * per_tile, last), 0)),
            _resident(w_pool.shape), _resident((1, c_w)), _resident(w_out.shape),
        ],
        out_specs=row_spec(d),
        scratch_shapes=[pltpu.VMEM((tm + 2 * halo, c_w), F32)],
        compiler_params=pltpu.CompilerParams(
            dimension_semantics=("parallel",), vmem_limit_bytes=limit),
        name="mixer_out_proj",
    )(h, mod, ya, yb, cu, cu, cu, w_pool, c_scale, w_out)


def _rope_tables(seq, dim):
    n = dim // 4
    t = jnp.arange(seq)
    rows = (t // GRID_W).astype(F32)
    cols = (t % GRID_W).astype(F32)
    inv = ROPE_THETA ** (-jnp.arange(n, dtype=F32) / n)
    ang = jnp.concatenate([rows[:, None] * inv[None, :], cols[:, None] * inv[None, :]], axis=-1)
    cos, sin = jnp.cos(ang), jnp.sin(ang)
    pad = jnp.zeros((seq, V7X_LANES - dim), F32)
    return (jnp.concatenate([cos, cos, pad], axis=-1), jnp.concatenate([-sin, sin, pad], axis=-1))


def _identity_tables(rows, dim):
    one = jnp.concatenate([jnp.ones((rows, dim), F32), jnp.zeros((rows, V7X_LANES - dim), F32)], -1)
    return one, jnp.zeros((rows, V7X_LANES), F32)


def _w_in_columns(a_q, a_kv, b_q, b_kv, c_w):
    o = a_q + 2 * a_kv + b_q + b_kv
    half = B_ROPE_DIM // 2
    k1, k2 = np.arange(o, o + half), np.arange(o + half, o + B_ROPE_DIM)
    return np.concatenate([np.arange(o), k1, k2, k2, k1, np.arange(o + B_ROPE_DIM, o + B_ROPE_DIM + c_w)])


def _w_uq_columns(heads):
    hd = B_NOPE_DIM + B_ROPE_DIM
    half = B_ROPE_DIM // 2
    out = []
    for h in range(heads):
        b = h * hd
        x1 = np.arange(b + B_NOPE_DIM, b + B_NOPE_DIM + half)
        x2 = x1 + half
        out += [np.arange(b, b + B_NOPE_DIM), x1, x2, x2, x1]
    return np.concatenate(out)


def _w_ukv_columns(heads):
    hd = B_NOPE_DIM + B_V_DIM
    ks = [np.arange(h * hd, h * hd + B_NOPE_DIM) for h in range(heads)]
    vs = [np.arange(h * hd + B_NOPE_DIM, (h + 1) * hd) for h in range(heads)]
    return np.concatenate(ks + vs)


def kernel(x, c, ctx, c_ctx, w_mod, b_mod, norm_g, ffn1_in, ffn1_out, w_in, a_q_g, a_k_g, b_q_g,
           b_kv_g, b_w_uq, b_w_ukv, c_w_pool, c_scale, w_out, ffn2_in, ffn2_out, final_g):
    batch, seq, d = x.shape
    ctx_len = ctx.shape[1]
    depth = w_mod.shape[0]
    assert batch + 1 <= MOD_ROWS
    b_q, b_kv = b_q_g.shape[1], b_kv_g.shape[1]
    b_heads = b_w_ukv.shape[2] // (B_NOPE_DIM + B_V_DIM)
    c_w = c_scale.shape[1]
    a_q = w_out.shape[1] - b_heads * B_V_DIM - c_w
    a_kv = a_q // A_GROUP
    dims = (a_q, a_kv, b_q, b_kv, b_heads, c_w)

    tm, tm_ctx, tf = 512, 256, 512
    tiles_per_seq = seq // tm
    lat_row = lambda i: lax.div(i, tiles_per_seq)
    ctx_row = lambda i: batch

    c_rows = jnp.concatenate([c, c_ctx[None, :], jnp.zeros((MOD_ROWS - batch - 1, d), F32)], axis=0)
    mod = _modulation(c_rows, w_mod, b_mod)

    tab_lat = _rope_tables(seq, A_HEAD_DIM) + _rope_tables(seq, B_ROPE_DIM)
    tab_ctx = _identity_tables(tm_ctx, A_HEAD_DIM) + _identity_tables(tm_ctx, B_ROPE_DIM)
    cols_in = _w_in_columns(a_q, a_kv, b_q, b_kv, c_w)
    cols_uq, cols_ukv = _w_uq_columns(b_heads), _w_ukv_columns(b_heads)

    h = x.reshape(batch * seq, d)
    hc = ctx.reshape(batch * ctx_len, d)
    for i in range(depth):
        last = i == depth - 1
        ng = norm_g[i]
        f1_in, f1_out = ffn1_in[i].astype(BF16), ffn1_out[i].astype(BF16)
        f2_in, f2_out = ffn2_in[i].astype(BF16), ffn2_out[i].astype(BF16)
        w_in_p = w_in[i][:, cols_in].astype(BF16)
        w_uq_p = b_w_uq[i][:, cols_uq].astype(BF16)
        w_ukv_p = b_w_ukv[i][:, cols_ukv].astype(BF16)
        w_o, w_pool = w_out[i].astype(BF16), c_w_pool[i].astype(BF16)
        gains = (a_q_g[i][None], a_k_g[i][None], b_q_g[i][None], b_kv_g[i][None])
        fg = final_g[None]

        ffn = functools.partial(_ffn, final_g=fg, tf=tf)
        h = ffn(h, mod[i], 0, ng[0:1], f1_in, f1_out, mod_row=lat_row, final_norm=False, tm=tm)
        hc = ffn(hc, mod[i], 0, ng[0:1], f1_in, f1_out, mod_row=ctx_row, final_norm=False, tm=tm_ctx)

        proj = functools.partial(_proj, norm_g=ng[1:2], w_in_p=w_in_p, a_q_g=gains[0], a_k_g=gains[1],
                                 b_q_g=gains[2], b_kv_g=gains[3], w_uq_p=w_uq_p, w_ukv_p=w_ukv_p,
                                 dims=dims)
        qa, ka, va, qb, kb, vb, cu = proj(h, mod[i], 3, tables=tab_lat, mod_row=lat_row,
                                          table_block=lambda t: lax.rem(t, tiles_per_seq), tm=tm)
        qac, kac, vac, qbc, kbc, vbc, cuc = proj(hc, mod[i], 3, tables=tab_ctx, mod_row=ctx_row,
                                                 table_block=lambda t: 0, tm=tm_ctx)

        attn_a = functools.partial(_attention, batch=batch, n_kv=a_kv // A_HEAD_DIM, n_stack=A_GROUP,
                                   dk=A_HEAD_DIM, dv=A_HEAD_DIM, tq=256, chunk=256)
        attn_b = functools.partial(_attention, batch=batch, n_kv=b_heads, n_stack=1,
                                   dk=B_QK_PAD, dv=B_V_DIM, tq=1024, chunk=256)
        ya = attn_a(qa, [(kac, vac), (ka, va)])
        yb = attn_b(qb, [(kbc, vbc), (kb, vb)])
        mix = functools.partial(_mix_out, w_pool=w_pool, c_scale=c_scale[i][None], w_out=w_o)
        h = mix(h, mod[i], 5, ya, yb, cu, mod_row=lat_row, seg_len=seq, tm=tm)
        h = ffn(h, mod[i], 6, ng[2:3], f2_in, f2_out, mod_row=lat_row, final_norm=last, tm=tm)
        if not last:
            yac = attn_a(qac, [(kac, vac)], tq=ctx_len)
            ybc = attn_b(qbc, [(kbc, vbc)], tq=ctx_len)
            hc = mix(hc, mod[i], 5, yac, ybc, cuc, mod_row=ctx_row, seg_len=ctx_len, tm=tm_ctx)
            hc = ffn(hc, mod[i], 6, ng[2:3], f2_in, f2_out, mod_row=ctx_row, final_norm=False, tm=tm_ctx)
    return h.reshape(batch, seq, d)
```

```python
import functools
import math

import jax
import jax.numpy as jnp
import numpy as np
from jax import lax
from jax.experimental import pallas as pl
from jax.experimental.pallas import tpu as pltpu

F32 = jnp.float32
BF16 = jnp.bfloat16

GRID_W = 64
ROPE_THETA = 10000.0
NORM_EPS = 1e-6
N_MOD = 9
A_HEAD_DIM = 128
A_GROUP = 4
B_NOPE_DIM = 128
B_ROPE_DIM = 64
B_V_DIM = 128
B_QK_PAD = 256
C_WINDOWS = (2, 4, 8, 16)
C_GROUP_DIM = 128
LOG2E = 1.4426950408889634

V7X_LANES = 128
V7X_SUBLANES = 8
V7X_VMEM_BYTES = 64 * 1024 * 1024
V7X_VMEM_RESERVE_BYTES = 8 * 1024 * 1024
MOD_ROWS = V7X_SUBLANES


def _nbytes(shape, dtype):
    return int(np.prod(shape)) * jnp.dtype(dtype).itemsize


def _vmem_limit(pipelined, resident, temps):
    need = 2 * sum(pipelined) + sum(resident) + temps
    return int(min(V7X_VMEM_BYTES - V7X_VMEM_RESERVE_BYTES, need + need // 4))


def _rms(x, eps=NORM_EPS):
    return x * lax.rsqrt(jnp.mean(x * x, axis=-1, keepdims=True) + eps)


def _resident(shape):
    nd = len(shape)
    return pl.BlockSpec(shape, lambda *_: (0,) * nd, pipeline_mode=pl.Buffered(1))


def _mod_kernel(c_ref, w_ref, b_ref, o_ref):
    c = c_ref[...]
    act = (c * jax.nn.sigmoid(c)).astype(BF16)
    o_ref[...] = jnp.dot(act, w_ref[...].astype(BF16), preferred_element_type=F32) + b_ref[...]


def _modulation(c_rows, w_mod, b_mod, *, tn=1024):
    depth, d, nm = w_mod.shape
    assert nm % tn == 0
    limit = _vmem_limit([_nbytes((d, tn), F32), _nbytes((MOD_ROWS, tn), F32)],
                        [_nbytes((MOD_ROWS, d), F32)], _nbytes((d, tn), BF16))
    return pl.pallas_call(
        _mod_kernel,
        out_shape=jax.ShapeDtypeStruct((depth, MOD_ROWS, nm), F32),
        grid=(depth, nm // tn),
        in_specs=[
            pl.BlockSpec((MOD_ROWS, d), lambda l, j: (0, 0)),
            pl.BlockSpec((None, d, tn), lambda l, j: (l, 0, j)),
            pl.BlockSpec((None, 1, tn), lambda l, j: (l, 0, j)),
        ],
        out_specs=pl.BlockSpec((None, MOD_ROWS, tn), lambda l, j: (l, 0, j)),
        compiler_params=pltpu.CompilerParams(
            dimension_semantics=("parallel", "parallel"), vmem_limit_bytes=limit),
        name="modulation",
    )(c_rows, w_mod, b_mod.reshape(depth, 1, nm))


def _ffn_kernel(h_ref, hn_ref, sh_ref, sc_ref, gt_ref, ng_ref, wa_ref, wb_ref, wo_ref, fg_ref,
                o_ref, xn0_ref, xn1_ref, *, mod_row, final_norm, slice_rows):
    i, k = pl.program_id(0), pl.program_id(1)
    n_tiles, nk = pl.num_programs(0), pl.num_programs(1)
    tm = h_ref.shape[0]
    parity = lax.rem(i, 2)

    def modnorm(x, row):
        y = _rms(x) * ng_ref[...]
        return (y * (1.0 + sc_ref[pl.ds(row, 1), :]) + sh_ref[pl.ds(row, 1), :]).astype(BF16)

    @pl.when((i == 0) & (k == 0))
    def _():
        xn0_ref[...] = modnorm(h_ref[...], mod_row(i))

    def step(first, cur_ref, nxt_ref):
        r0 = pl.multiple_of(jnp.minimum(k * slice_rows, tm - slice_rows), 16)
        nxt_row = mod_row(jnp.minimum(i + 1, n_tiles - 1))
        nxt_ref[pl.ds(r0, slice_rows), :] = modnorm(hn_ref[pl.ds(r0, slice_rows), :], nxt_row)

        xn = cur_ref[...]
        a = jnp.dot(xn, wa_ref[...], preferred_element_type=F32)
        b = jnp.dot(xn, wb_ref[...], preferred_element_type=F32)
        g = (a * jax.nn.sigmoid(a) * b).astype(BF16)
        part = jnp.dot(g, wo_ref[...], preferred_element_type=F32)
        part = part * (0.5 * gt_ref[pl.ds(mod_row(i), 1), :])
        o_ref[...] = (h_ref[...] if first else o_ref[...]) + part

    for first in (True, False):
        for par, (cur_ref, nxt_ref) in enumerate(((xn0_ref, xn1_ref), (xn1_ref, xn0_ref))):
            cond = ((k == 0) if first else (k > 0)) & (parity == par)
            pl.when(cond)(functools.partial(step, first, cur_ref, nxt_ref))

    if final_norm:
        @pl.when(k == nk - 1)
        def _():
            o_ref[...] = _rms(o_ref[...]) * fg_ref[...]


def _ffn(h, mod, j0, norm_g, w_in, w_out, final_g, *, mod_row, final_norm, tm, tf):
    n, d = h.shape
    dff = w_out.shape[0]
    assert n % tm == 0 and dff % tf == 0
    nk, n_tiles = dff // tf, n // tm
    slice_rows = -(-tm // nk // 16) * 16
    assert slice_rows * nk >= tm
    mod_spec = lambda j: pl.BlockSpec((MOD_ROWS, d), lambda i, k: (0, j))
    limit = _vmem_limit(
        [_nbytes((tm, d), F32) * 3, _nbytes((d, tf), BF16) * 2, _nbytes((tf, d), BF16)],
        [2 * _nbytes((tm, d), BF16)],
        4 * _nbytes((tm, tf), F32) + _nbytes((tm, d), F32))
    return pl.pallas_call(
        functools.partial(_ffn_kernel, mod_row=mod_row, final_norm=final_norm, slice_rows=slice_rows),
        out_shape=jax.ShapeDtypeStruct((n, d), F32),
        grid=(n_tiles, nk),
        in_specs=[
            pl.BlockSpec((tm, d), lambda i, k: (i, 0)),
            pl.BlockSpec((tm, d), lambda i, k: (jnp.minimum(i + 1, n_tiles - 1), 0)),
            mod_spec(j0), mod_spec(j0 + 1), mod_spec(j0 + 2),
            pl.BlockSpec((1, d), lambda i, k: (0, 0)),
            pl.BlockSpec((d, tf), lambda i, k: (0, k)),
            pl.BlockSpec((d, tf), lambda i, k: (0, k + nk)),
            pl.BlockSpec((tf, d), lambda i, k: (k, 0)),
            pl.BlockSpec((1, d), lambda i, k: (0, 0)),
        ],
        out_specs=pl.BlockSpec((tm, d), lambda i, k: (i, 0)),
        scratch_shapes=[pltpu.VMEM((tm, d), BF16), pltpu.VMEM((tm, d), BF16)],
        compiler_params=pltpu.CompilerParams(
            dimension_semantics=("arbitrary", "arbitrary"), vmem_limit_bytes=limit),
        name="swiglu_half_step",
    )(h, h, mod, mod, mod, norm_g, w_in, w_in, w_out, final_g)


def _rope(x, cos, sin_signed):
    return x * cos + pltpu.roll(x, V7X_LANES // 2, 1) * sin_signed


def _proj_kernel(h_ref, sh_ref, sc_ref, ng_ref, win_ref, aqg_ref, akg_ref, bqg_ref, bkvg_ref,
                 wuq_ref, wukv_ref, ca_ref, sa_ref, cb_ref, sb_ref,
                 qa_ref, ka_ref, va_ref, qb_ref, kb_ref, vb_ref, cu_ref, *, mod_row, dims):
    a_q, a_kv, b_q, b_kv, b_heads, c_w = dims
    row = mod_row(pl.program_id(0))
    y = _rms(h_ref[...]) * ng_ref[...]
    y = y * (1.0 + sc_ref[pl.ds(row, 1), :]) + sh_ref[pl.ds(row, 1), :]
    u = jnp.dot(y.astype(BF16), win_ref[...], preferred_element_type=F32)

    ca, sa, cb, sb = ca_ref[...], sa_ref[...], cb_ref[...], sb_ref[...]
    hd = A_HEAD_DIM
    o_ak, o_av, o_bq = a_q, a_q + a_kv, a_q + 2 * a_kv
    o_bkv, o_bkr = o_bq + b_q, o_bq + b_q + b_kv
    o_cu = o_bkr + V7X_LANES

    qg = aqg_ref[...] * (hd ** -0.5 * LOG2E)
    for h in range(a_q // hd):
        x = _rms(u[:, h * hd:(h + 1) * hd]) * qg
        qa_ref[:, h * hd:(h + 1) * hd] = _rope(x, ca, sa).astype(BF16)
    kg = akg_ref[...]
    for h in range(a_kv // hd):
        x = _rms(u[:, o_ak + h * hd:o_ak + (h + 1) * hd]) * kg
        ka_ref[:, h * hd:(h + 1) * hd] = _rope(x, ca, sa).astype(BF16)
    va_ref[...] = u[:, o_av:o_av + a_kv].astype(BF16)

    xq = (_rms(u[:, o_bq:o_bq + b_q]) * bqg_ref[...]).astype(BF16)
    q = jnp.dot(xq, wuq_ref[...], preferred_element_type=F32)
    q_scale = (B_NOPE_DIM + B_ROPE_DIM) ** -0.5 * LOG2E
    xkv = (_rms(u[:, o_bkv:o_bkv + b_kv]) * bkvg_ref[...]).astype(BF16)
    kv = jnp.dot(xkv, wukv_ref[...], preferred_element_type=F32)
    k_rope = _rope(u[:, o_bkr:o_bkr + V7X_LANES], cb, sb).astype(BF16)
    for h in range(b_heads):
        o = h * B_QK_PAD
        qb_ref[:, o:o + B_NOPE_DIM] = (q[:, o:o + B_NOPE_DIM] * q_scale).astype(BF16)
        qr = _rope(q[:, o + B_NOPE_DIM:o + B_QK_PAD], cb, sb)
        qb_ref[:, o + B_NOPE_DIM:o + B_QK_PAD] = (qr * q_scale).astype(BF16)
        kb_ref[:, o:o + B_NOPE_DIM] = kv[:, h * B_NOPE_DIM:(h + 1) * B_NOPE_DIM].astype(BF16)
        kb_ref[:, o + B_NOPE_DIM:o + B_QK_PAD] = k_rope
    vb_ref[...] = kv[:, b_heads * B_NOPE_DIM:].astype(BF16)

    cu_ref[...] = u[:, o_cu:o_cu + c_w]


def _proj(h, mod, j0, norm_g, w_in_p, a_q_g, a_k_g, b_q_g, b_kv_g, w_uq_p, w_ukv_p, tables,
          *, mod_row, table_block, dims, tm):
    n, d = h.shape
    a_q, a_kv, b_q, b_kv, b_heads, c_w = dims
    assert n % tm == 0
    ncols = w_in_p.shape[1]
    mod_spec = lambda j: pl.BlockSpec((MOD_ROWS, d), lambda i: (0, j))
    tab_spec = pl.BlockSpec((tm, V7X_LANES), lambda i: (table_block(i), 0))
    row_spec = lambda w: pl.BlockSpec((tm, w), lambda i: (i, 0))
    out_widths = [(a_q, BF16), (a_kv, BF16), (a_kv, BF16), (b_heads * B_QK_PAD, BF16),
                  (b_heads * B_QK_PAD, BF16), (b_heads * B_V_DIM, BF16), (c_w, F32)]
    limit = _vmem_limit(
        [_nbytes((tm, d), F32)] + [_nbytes((tm, w), t) for w, t in out_widths]
        + [4 * _nbytes((tm, V7X_LANES), F32)],
        [_nbytes(w_in_p.shape, BF16), _nbytes(w_uq_p.shape, BF16), _nbytes(w_ukv_p.shape, BF16)],
        _nbytes((tm, ncols), F32) + _nbytes((tm, d), F32) + 2 * _nbytes((tm, 4 * B_QK_PAD), F32))
    return pl.pallas_call(
        functools.partial(_proj_kernel, mod_row=mod_row, dims=dims),
        out_shape=[jax.ShapeDtypeStruct((n, w), t) for w, t in out_widths],
        grid=(n // tm,),
        in_specs=[
            row_spec(d), mod_spec(j0), mod_spec(j0 + 1),
            _resident((1, d)), _resident(w_in_p.shape),
            _resident((1, A_HEAD_DIM)), _resident((1, A_HEAD_DIM)),
            _resident((1, b_q)), _resident((1, b_kv)),
            _resident(w_uq_p.shape), _resident(w_ukv_p.shape),
            tab_spec, tab_spec, tab_spec, tab_spec,
        ],
        out_specs=[row_spec(w) for w, _ in out_widths],
        compiler_params=pltpu.CompilerParams(
            dimension_semantics=("parallel",), vmem_limit_bytes=limit),
        name="mixer_in_proj",
    )(h, mod, mod, norm_g, w_in_p, a_q_g, a_k_g, b_q_g, b_kv_g, w_uq_p, w_ukv_p, *tables)


def _attn_kernel(*refs, n_stack, dk, dv, chunks):
    q_ref, o_ref = refs[0], refs[-1]
    kv_refs = refs[1:-1]
    tq = q_ref.shape[0]
    q = jnp.concatenate([q_ref[:, h * dk:(h + 1) * dk] for h in range(n_stack)], axis=0)
    m_rows = n_stack * tq

    def step(k, v, carry):
        m, acc = carry
        s = lax.dot_general(q, k, (((1,), (1,)), ((), ())), preferred_element_type=F32)
        m_new = jnp.maximum(m, jnp.max(s, axis=-1, keepdims=True))
        p = jnp.exp2(s - m_new).astype(BF16)
        v_ones = jnp.concatenate([v, jnp.ones((v.shape[0], V7X_LANES), BF16)], axis=1)
        acc = jnp.exp2(m - m_new) * acc + jnp.dot(p, v_ones, preferred_element_type=F32)
        return m_new, acc

    carry = (jnp.full((m_rows, 1), -jnp.inf, F32), jnp.zeros((m_rows, dv + V7X_LANES), F32))
    for seg, chunk in enumerate(chunks):
        k_ref, v_ref = kv_refs[2 * seg], kv_refs[2 * seg + 1]
        n_chunks = k_ref.shape[0] // chunk
        for c in range(n_chunks):
            carry = step(k_ref[c * chunk:(c + 1) * chunk, :], v_ref[c * chunk:(c + 1) * chunk, :], carry)
    _, acc = carry
    out = acc[:, :dv] / acc[:, dv:dv + dv]
    for h in range(n_stack):
        o_ref[:, h * dv:(h + 1) * dv] = out[h * tq:(h + 1) * tq].astype(BF16)


def _attention(q, segments, *, batch, n_kv, n_stack, dk, dv, tq, chunk):
    nq = q.shape[0]
    lq = nq // batch
    assert lq % tq == 0 and dv == V7X_LANES
    tiles = lq // tq
    in_specs = [pl.BlockSpec((tq, n_stack * dk), lambda b, g, i: (b * tiles + i, g))]
    args, chunks, kv_bytes = [q], [], []
    for k, v in segments:
        lk = k.shape[0] // batch
        chunks.append(min(chunk, lk))
        assert lk % chunks[-1] == 0
        in_specs.append(pl.BlockSpec((lk, dk), lambda b, g, i: (b, g)))
        in_specs.append(pl.BlockSpec((lk, dv), lambda b, g, i: (b, g)))
        args += [k, v]
        kv_bytes += [_nbytes((lk, dk), BF16), _nbytes((lk, dv), BF16)]
    m_rows = n_stack * tq
    limit = _vmem_limit(
        [_nbytes((tq, n_stack * dk), BF16), _nbytes((tq, n_stack * dv), BF16)] + kv_bytes, [],
        24 * _nbytes((m_rows, max(chunks)), F32) + 4 * _nbytes((m_rows, V7X_LANES), F32))
    return pl.pallas_call(
        functools.partial(_attn_kernel, n_stack=n_stack, dk=dk, dv=dv, chunks=tuple(chunks)),
        out_shape=jax.ShapeDtypeStruct((nq, n_kv * n_stack * dv), BF16),
        grid=(batch, n_kv, tiles),
        in_specs=in_specs,
        out_specs=pl.BlockSpec((tq, n_stack * dv), lambda b, g, i: (b * tiles + i, g)),
        compiler_params=pltpu.CompilerParams(
            dimension_semantics=("parallel", "parallel", "arbitrary"), vmem_limit_bytes=limit),
        name="attention",
    )(*args)


def _out_kernel(h_ref, gt_ref, ya_ref, yb_ref, cu_ref, cup_ref, cun_ref, wp_ref, cs_ref, wo_ref,
                o_ref, ext_ref, *, mod_row, seg_len):
    i = pl.program_id(0)
    tm = h_ref.shape[0]
    halo = V7X_SUBLANES
    pos0 = lax.rem(i * tm, seg_len)
    ext_ref[0:halo, :] = jnp.where(pos0 > 0, cup_ref[...], 0.0)
    ext_ref[halo:halo + tm, :] = cu_ref[...]
    ext_ref[halo + tm:, :] = jnp.where(pos0 + tm < seg_len, cun_ref[...], 0.0)

    pos = pos0 + lax.broadcasted_iota(jnp.int32, (tm, 1), 0)
    a_w, b_w = ya_ref.shape[1], yb_ref.shape[1]
    y = jnp.dot(ya_ref[...], wo_ref[0:a_w, :], preferred_element_type=F32)
    y += jnp.dot(yb_ref[...], wo_ref[a_w:a_w + b_w, :], preferred_element_type=F32)
    gd = C_GROUP_DIM
    for g, w in enumerate(C_WINDOWS):
        cols = slice(g * gd, (g + 1) * gd)
        tot = ext_ref[halo - w // 2:halo - w // 2 + tm, cols]
        for off in range(1 - w // 2, w // 2):
            tot = tot + ext_ref[halo + off:halo + off + tm, cols]
        lo = jnp.clip(pos - w // 2, 0, seg_len)
        hi = jnp.clip(pos - w // 2 + w, 0, seg_len)
        pooled = tot / (hi - lo).astype(F32) - cu_ref[:, cols]
        yc = jnp.dot(pooled.astype(BF16), wp_ref[g], preferred_element_type=F32) * cs_ref[:, cols]
        r0 = a_w + b_w + g * gd
        y += jnp.dot(yc.astype(BF16), wo_ref[r0:r0 + gd, :], preferred_element_type=F32)
    o_ref[...] = h_ref[...] + gt_ref[pl.ds(mod_row(i), 1), :] * y


def _mix_out(h, mod, j_gate, ya, yb, cu, w_pool, c_scale, w_out, *, mod_row, seg_len, tm):
    n, d = h.shape
    c_w = cu.shape[1]
    assert n % tm == 0 and seg_len % tm == 0
    halo = V7X_SUBLANES
    per_tile = tm // halo
    last = n // halo - 1
    row_spec = lambda w: pl.BlockSpec((tm, w), lambda i: (i, 0))
    limit = _vmem_limit(
        [2 * _nbytes((tm, d), F32), _nbytes((tm, ya.shape[1]), BF16),
         _nbytes((tm, yb.shape[1]), BF16), _nbytes((tm, c_w), F32)],
        [_nbytes(w_out.shape, BF16), _nbytes(w_pool.shape, BF16), _nbytes((tm + 2 * halo, c_w), F32)],
        2 * _nbytes((tm, d), F32))
    return pl.pallas_call(
        functools.partial(_out_kernel, mod_row=mod_row, seg_len=seg_len),
        out_shape=jax.ShapeDtypeStruct((n, d), F32),
        grid=(n // tm,),
        in_specs=[
            row_spec(d),
            pl.BlockSpec((MOD_ROWS, d), lambda i: (0, j_gate)),
            row_spec(ya.shape[1]), row_spec(yb.shape[1]), row_spec(c_w),
            pl.BlockSpec((halo, c_w), lambda i: (jnp.maximum(i * per_tile - 1, 0), 0)),
            pl.BlockSpec((halo, c_w), lambda i: (jnp.minimum((i + 1) * per_tile, last), 0)),
            _resident(w_pool.shape), _resident((1, c_w)), _resident(w_out.shape),
        ],
        out_specs=row_spec(d),
        scratch_shapes=[pltpu.VMEM((tm + 2 * halo, c_w), F32)],
        compiler_params=pltpu.CompilerParams(
            dimension_semantics=("parallel",), vmem_limit_bytes=limit),
        name="mixer_out_proj",
    )(h, mod, ya, yb, cu, cu, cu, w_pool, c_scale, w_out)


def _rope_tables(seq, dim):
    n = dim // 4
    t = jnp.arange(seq)
    rows = (t // GRID_W).astype(F32)
    cols = (t % GRID_W).astype(F32)
    inv = ROPE_THETA ** (-jnp.arange(n, dtype=F32) / n)
    ang = jnp.concatenate([rows[:, None] * inv[None, :], cols[:, None] * inv[None, :]], axis=-1)
    cos, sin = jnp.cos(ang), jnp.sin(ang)
    pad = jnp.zeros((seq, V7X_LANES - dim), F32)
    return (jnp.concatenate([cos, cos, pad], axis=-1), jnp.concatenate([-sin, sin, pad], axis=-1))


def _identity_tables(rows, dim):
    one = jnp.concatenate([jnp.ones((rows, dim), F32), jnp.zeros((rows, V7X_LANES - dim), F32)], -1)
    return one, jnp.zeros((rows, V7X_LANES), F32)


def _w_in_columns(a_q, a_kv, b_q, b_kv, c_w):
    o = a_q + 2 * a_kv + b_q + b_kv
    half = B_ROPE_DIM // 2
    k1, k2 = np.arange(o, o + half), np.arange(o + half, o + B_ROPE_DIM)
    return np.concatenate([np.arange(o), k1, k2, k2, k1, np.arange(o + B_ROPE_DIM, o + B_ROPE_DIM + c_w)])


def _w_uq_columns(heads):
    hd = B_NOPE_DIM + B_ROPE_DIM
    half = B_ROPE_DIM // 2
    out = []
    for h in range(heads):
        b = h * hd
        x1 = np.arange(b + B_NOPE_DIM, b + B_NOPE_DIM + half)
        x2 = x1 + half
        out += [np.arange(b, b + B_NOPE_DIM), x1, x2, x2, x1]
    return np.concatenate(out)


def _w_ukv_columns(heads):
    hd = B_NOPE_DIM + B_V_DIM
    ks = [np.arange(h * hd, h * hd + B_NOPE_DIM) for h in range(heads)]
    vs = [np.arange(h * hd + B_NOPE_DIM, (h + 1) * hd) for h in range(heads)]
    return np.concatenate(ks + vs)


def kernel(x, c, ctx, c_ctx, w_mod, b_mod, norm_g, ffn1_in, ffn1_out, w_in, a_q_g, a_k_g, b_q_g,
           b_kv_g, b_w_uq, b_w_ukv, c_w_pool, c_scale, w_out, ffn2_in, ffn2_out, final_g):
    batch, seq, d = x.shape
    ctx_len = ctx.shape[1]
    depth = w_mod.shape[0]
    assert batch + 1 <= MOD_ROWS
    b_q, b_kv = b_q_g.shape[1], b_kv_g.shape[1]
    b_heads = b_w_ukv.shape[2] // (B_NOPE_DIM + B_V_DIM)
    c_w = c_scale.shape[1]
    a_q = w_out.shape[1] - b_heads * B_V_DIM - c_w
    a_kv = a_q // A_GROUP
    dims = (a_q, a_kv, b_q, b_kv, b_heads, c_w)

    tm, tm_ctx, tf = 512, 256, 512
    tiles_per_seq = seq // tm
    lat_row = lambda i: lax.div(i, tiles_per_seq)
    ctx_row = lambda i: batch

    c_rows = jnp.concatenate([c, c_ctx[None, :], jnp.zeros((MOD_ROWS - batch - 1, d), F32)], axis=0)
    mod = _modulation(c_rows, w_mod, b_mod)

    tab_lat = _rope_tables(seq, A_HEAD_DIM) + _rope_tables(seq, B_ROPE_DIM)
    tab_ctx = _identity_tables(tm_ctx, A_HEAD_DIM) + _identity_tables(tm_ctx, B_ROPE_DIM)
    cols_in = _w_in_columns(a_q, a_kv, b_q, b_kv, c_w)
    cols_uq, cols_ukv = _w_uq_columns(b_heads), _w_ukv_columns(b_heads)

    h = x.reshape(batch * seq, d)
    hc = ctx.reshape(batch * ctx_len, d)
    for i in range(depth):
        last = i == depth - 1
        ng = norm_g[i]
        f1_in, f1_out = ffn1_in[i].astype(BF16), ffn1_out[i].astype(BF16)
        f2_in, f2_out = ffn2_in[i].astype(BF16), ffn2_out[i].astype(BF16)
        w_in_p = w_in[i][:, cols_in].astype(BF16)
        w_uq_p = b_w_uq[i][:, cols_uq].astype(BF16)
        w_ukv_p = b_w_ukv[i][:, cols_ukv].astype(BF16)
        w_o, w_pool = w_out[i].astype(BF16), c_w_pool[i].astype(BF16)
        gains = (a_q_g[i][None], a_k_g[i][None], b_q_g[i][None], b_kv_g[i][None])
        fg = final_g[None]

        ffn = functools.partial(_ffn, final_g=fg, tf=tf)
        h = ffn(h, mod[i], 0, ng[0:1], f1_in, f1_out, mod_row=lat_row, final_norm=False, tm=tm)
        hc = ffn(hc, mod[i], 0, ng[0:1], f1_in, f1_out, mod_row=ctx_row, final_norm=False, tm=tm_ctx)

        proj = functools.partial(_proj, norm_g=ng[1:2], w_in_p=w_in_p, a_q_g=gains[0], a_k_g=gains[1],
                                 b_q_g=gains[2], b_kv_g=gains[3], w_uq_p=w_uq_p, w_ukv_p=w_ukv_p,
                                 dims=dims)
        qa, ka, va, qb, kb, vb, cu = proj(h, mod[i], 3, tables=tab_lat, mod_row=lat_row,
                                          table_block=lambda t: lax.rem(t, tiles_per_seq), tm=tm)
        qac, kac, vac, qbc, kbc, vbc, cuc = proj(hc, mod[i], 3, tables=tab_ctx, mod_row=ctx_row,
                                                 table_block=lambda t: 0, tm=tm_ctx)

        attn_a = functools.partial(_attention, batch=batch, n_kv=a_kv // A_HEAD_DIM, n_stack=A_GROUP,
                                   dk=A_HEAD_DIM, dv=A_HEAD_DIM, tq=256, chunk=256)
        attn_b = functools.partial(_attention, batch=batch, n_kv=b_heads, n_stack=1,
                                   dk=B_QK_PAD, dv=B_V_DIM, tq=1024, chunk=256)
        ya = attn_a(qa, [(kac, vac), (ka, va)])
        yb = attn_b(qb, [(kbc, vbc), (kb, vb)])
        mix = functools.partial(_mix_out, w_pool=w_pool, c_scale=c_scale[i][None], w_out=w_o)
        h = mix(h, mod[i], 5, ya, yb, cu, mod_row=lat_row, seg_len=seq, tm=tm)
        h = ffn(h, mod[i], 6, ng[2:3], f2_in, f2_out, mod_row=lat_row, final_norm=last, tm=tm)
        if not last:
            yac = attn_a(qac, [(kac, vac)], tq=ctx_len)
            ybc = attn_b(qbc, [(kbc, vbc)], tq=ctx_len)
            hc = mix(hc, mod[i], 5, yac, ybc, cuc, mod_row=ctx_row, seg_len=ctx_len, tm=tm_ctx)
            hc = ffn(hc, mod[i], 6, ng[2:3], f2_in, f2_out, mod_row=ctx_row, final_norm=False, tm=tm_ctx)
    return h.reshape(batch, seq, d)
```

```python
import functools
import math

import jax
import jax.numpy as jnp
import numpy as np
from jax import lax
from jax.experimental import pallas as pl
from jax.experimental.pallas import tpu as pltpu

F32 = jnp.float32
BF16 = jnp.bfloat16

GRID_W = 64
ROPE_THETA = 10000.0
NORM_EPS = 1e-6
N_MOD = 9
A_HEAD_DIM = 128
A_GROUP = 4
B_NOPE_DIM = 128
B_ROPE_DIM = 64
B_V_DIM = 128
B_QK_PAD = 256
C_WINDOWS = (2, 4, 8, 16)
C_GROUP_DIM = 128
LOG2E = 1.4426950408889634

V7X_LANES = 128
V7X_SUBLANES = 8
V7X_VMEM_BYTES = 64 * 1024 * 1024
V7X_VMEM_RESERVE_BYTES = 8 * 1024 * 1024
MOD_ROWS = V7X_SUBLANES


def _nbytes(shape, dtype):
    return int(np.prod(shape)) * jnp.dtype(dtype).itemsize


def _vmem_limit(pipelined, resident, temps):
    need = 2 * sum(pipelined) + sum(resident) + temps
    return int(min(V7X_VMEM_BYTES - V7X_VMEM_RESERVE_BYTES, need + need // 4))


def _rms(x, eps=NORM_EPS):
    return x * lax.rsqrt(jnp.mean(x * x, axis=-1, keepdims=True) + eps)


def _resident(shape):
    nd = len(shape)
    return pl.BlockSpec(shape, lambda *_: (0,) * nd, pipeline_mode=pl.Buffered(1))


def _mod_kernel(c_ref, w_ref, b_ref, o_ref):
    c = c_ref[...]
    act = (c * jax.nn.sigmoid(c)).astype(BF16)
    o_ref[...] = jnp.dot(act, w_ref[...].astype(BF16), preferred_element_type=F32) + b_ref[...]


def _modulation(c_rows, w_mod, b_mod, *, tn=1024):
    depth, d, nm = w_mod.shape
    assert nm % tn == 0
    limit = _vmem_limit([_nbytes((d, tn), F32), _nbytes((MOD_ROWS, tn), F32)],
                        [_nbytes((MOD_ROWS, d), F32)], _nbytes((d, tn), BF16))
    return pl.pallas_call(
        _mod_kernel,
        out_shape=jax.ShapeDtypeStruct((depth, MOD_ROWS, nm), F32),
        grid=(depth, nm // tn),
        in_specs=[
            pl.BlockSpec((MOD_ROWS, d), lambda l, j: (0, 0)),
            pl.BlockSpec((None, d, tn), lambda l, j: (l, 0, j)),
            pl.BlockSpec((None, 1, tn), lambda l, j: (l, 0, j)),
        ],
        out_specs=pl.BlockSpec((None, MOD_ROWS, tn), lambda l, j: (l, 0, j)),
        compiler_params=pltpu.CompilerParams(
            dimension_semantics=("parallel", "parallel"), vmem_limit_bytes=limit),
        name="modulation",
    )(c_rows, w_mod, b_mod.reshape(depth, 1, nm))


def _ffn_kernel(h_ref, hn_ref, sh_ref, sc_ref, gt_ref, ng_ref, wab_ref, wo_ref, fg_ref,
                o_ref, xn0_ref, xn1_ref, *, mod_row, final_norm, slice_rows):
    i, k = pl.program_id(0), pl.program_id(1)
    n_tiles, nk = pl.num_programs(0), pl.num_programs(1)
    tm = h_ref.shape[0]
    parity = lax.rem(i, 2)

    def modnorm(x, row):
        y = _rms(x) * ng_ref[...]
        return (y * (1.0 + sc_ref[pl.ds(row, 1), :]) + sh_ref[pl.ds(row, 1), :]).astype(BF16)

    @pl.when((i == 0) & (k == 0))
    def _():
        xn0_ref[...] = modnorm(h_ref[...], mod_row(i))

    def step(first, cur_ref, nxt_ref):
        r0 = pl.multiple_of(jnp.minimum(k * slice_rows, tm - slice_rows), 16)
        nxt_row = mod_row(jnp.minimum(i + 1, n_tiles - 1))
        nxt_ref[pl.ds(r0, slice_rows), :] = modnorm(hn_ref[pl.ds(r0, slice_rows), :], nxt_row)

        xn = cur_ref[...]
        ab = jnp.dot(xn, wab_ref[...], preferred_element_type=F32)
        tf = wo_ref.shape[0]
        a, b = ab[:, :tf], ab[:, tf:]
        g = (a * jax.nn.sigmoid(a) * b).astype(BF16)
        part = jnp.dot(g, wo_ref[...], preferred_element_type=F32)
        part = part * (0.5 * gt_ref[pl.ds(mod_row(i), 1), :])
        o_ref[...] = (h_ref[...] if first else o_ref[...]) + part

    for first in (True, False):
        for par, (cur_ref, nxt_ref) in enumerate(((xn0_ref, xn1_ref), (xn1_ref, xn0_ref))):
            cond = ((k == 0) if first else (k > 0)) & (parity == par)
            pl.when(cond)(functools.partial(step, first, cur_ref, nxt_ref))

    if final_norm:
        @pl.when(k == nk - 1)
        def _():
            o_ref[...] = _rms(o_ref[...]) * fg_ref[...]


def _ffn_weights(w_in, w_out, tf):
    d, dff = w_in.shape[0], w_out.shape[0]
    nk = dff // tf
    w_ab = w_in.reshape(d, 2, nk, tf).transpose(2, 0, 1, 3).reshape(nk, d, 2 * tf)
    return w_ab.astype(BF16), w_out.astype(BF16)


def _ffn(h, mod, j0, norm_g, w_ab, w_out, final_g, *, mod_row, final_norm, tm):
    n, d = h.shape
    nk, tf = w_ab.shape[0], w_ab.shape[2] // 2
    assert n % tm == 0 and w_out.shape[0] == nk * tf
    n_tiles = n // tm
    slice_rows = -(-tm // nk // 16) * 16
    assert slice_rows * nk >= tm
    mod_spec = lambda j: pl.BlockSpec((MOD_ROWS, d), lambda i, k: (0, j))
    limit = _vmem_limit(
        [_nbytes((tm, d), F32) * 3, _nbytes((d, tf), BF16) * 2, _nbytes((tf, d), BF16)],
        [2 * _nbytes((tm, d), BF16)],
        4 * _nbytes((tm, tf), F32) + _nbytes((tm, d), F32))
    return pl.pallas_call(
        functools.partial(_ffn_kernel, mod_row=mod_row, final_norm=final_norm, slice_rows=slice_rows),
        out_shape=jax.ShapeDtypeStruct((n, d), F32),
        grid=(n_tiles, nk),
        in_specs=[
            pl.BlockSpec((tm, d), lambda i, k: (i, 0)),
            pl.BlockSpec((tm, d), lambda i, k: (jnp.minimum(i + 1, n_tiles - 1), 0)),
            mod_spec(j0), mod_spec(j0 + 1), mod_spec(j0 + 2),
            pl.BlockSpec((1, d), lambda i, k: (0, 0)),
            pl.BlockSpec((None, d, 2 * tf), lambda i, k: (k, 0, 0)),
            pl.BlockSpec((tf, d), lambda i, k: (k, 0)),
            pl.BlockSpec((1, d), lambda i, k: (0, 0)),
        ],
        out_specs=pl.BlockSpec((tm, d), lambda i, k: (i, 0)),
        scratch_shapes=[pltpu.VMEM((tm, d), BF16), pltpu.VMEM((tm, d), BF16)],
        compiler_params=pltpu.CompilerParams(
            dimension_semantics=("arbitrary", "arbitrary"), vmem_limit_bytes=limit),
        name="swiglu_half_step",
    )(h, h, mod, mod, mod, norm_g, w_ab, w_out, final_g)


def _rope(x, cos, sin_signed):
    return x * cos + pltpu.roll(x, V7X_LANES // 2, 1) * sin_signed


def _proj_kernel(h_ref, sh_ref, sc_ref, ng_ref, win_ref, aqg_ref, akg_ref, bqg_ref, bkvg_ref,
                 wuq_ref, wukv_ref, ca_ref, sa_ref, cb_ref, sb_ref,
                 qa_ref, ka_ref, va_ref, qb_ref, kb_ref, vb_ref, cu_ref, *, mod_row, dims):
    a_q, a_kv, b_q, b_kv, b_heads, c_w = dims
    row = mod_row(pl.program_id(0))
    y = _rms(h_ref[...]) * ng_ref[...]
    y = y * (1.0 + sc_ref[pl.ds(row, 1), :]) + sh_ref[pl.ds(row, 1), :]
    u = jnp.dot(y.astype(BF16), win_ref[...], preferred_element_type=F32)

    ca, sa, cb, sb = ca_ref[...], sa_ref[...], cb_ref[...], sb_ref[...]
    hd = A_HEAD_DIM
    o_ak, o_av, o_bq = a_q, a_q + a_kv, a_q + 2 * a_kv
    o_bkv, o_bkr = o_bq + b_q, o_bq + b_q + b_kv
    o_cu = o_bkr + V7X_LANES

    qg = aqg_ref[...] * (hd ** -0.5 * LOG2E)
    for h in range(a_q // hd):
        x = _rms(u[:, h * hd:(h + 1) * hd]) * qg
        qa_ref[:, h * hd:(h + 1) * hd] = _rope(x, ca, sa).astype(BF16)
    kg = akg_ref[...]
    for h in range(a_kv // hd):
        x = _rms(u[:, o_ak + h * hd:o_ak + (h + 1) * hd]) * kg
        ka_ref[:, h * hd:(h + 1) * hd] = _rope(x, ca, sa).astype(BF16)
    va_ref[...] = u[:, o_av:o_av + a_kv].astype(BF16)

    xq = (_rms(u[:, o_bq:o_bq + b_q]) * bqg_ref[...]).astype(BF16)
    q = jnp.dot(xq, wuq_ref[...], preferred_element_type=F32)
    q_scale = (B_NOPE_DIM + B_ROPE_DIM) ** -0.5 * LOG2E
    xkv = (_rms(u[:, o_bkv:o_bkv + b_kv]) * bkvg_ref[...]).astype(BF16)
    kv = jnp.dot(xkv, wukv_ref[...], preferred_element_type=F32)
    k_rope = _rope(u[:, o_bkr:o_bkr + V7X_LANES], cb, sb).astype(BF16)
    for h in range(b_heads):
        o = h * B_QK_PAD
        qb_ref[:, o:o + B_NOPE_DIM] = (q[:, o:o + B_NOPE_DIM] * q_scale).astype(BF16)
        qr = _rope(q[:, o + B_NOPE_DIM:o + B_QK_PAD], cb, sb)
        qb_ref[:, o + B_NOPE_DIM:o + B_QK_PAD] = (qr * q_scale).astype(BF16)
        kb_ref[:, o:o + B_NOPE_DIM] = kv[:, h * B_NOPE_DIM:(h + 1) * B_NOPE_DIM].astype(BF16)
        kb_ref[:, o + B_NOPE_DIM:o + B_QK_PAD] = k_rope
    vb_ref[...] = kv[:, b_heads * B_NOPE_DIM:].astype(BF16)

    cu_ref[...] = u[:, o_cu:o_cu + c_w]


def _proj(h, mod, j0, norm_g, w_in_p, a_q_g, a_k_g, b_q_g, b_kv_g, w_uq_p, w_ukv_p, tables,
          *, mod_row, table_block, dims, tm):
    n, d = h.shape
    a_q, a_kv, b_q, b_kv, b_heads, c_w = dims
    assert n % tm == 0
    ncols = w_in_p.shape[1]
    mod_spec = lambda j: pl.BlockSpec((MOD_ROWS, d), lambda i: (0, j))
    tab_spec = pl.BlockSpec((tm, V7X_LANES), lambda i: (table_block(i), 0))
    row_spec = lambda w: pl.BlockSpec((tm, w), lambda i: (i, 0))
    out_widths = [(a_q, BF16), (a_kv, BF16), (a_kv, BF16), (b_heads * B_QK_PAD, BF16),
                  (b_heads * B_QK_PAD, BF16), (b_heads * B_V_DIM, BF16), (c_w, F32)]
    limit = _vmem_limit(
        [_nbytes((tm, d), F32)] + [_nbytes((tm, w), t) for w, t in out_widths]
        + [4 * _nbytes((tm, V7X_LANES), F32)],
        [_nbytes(w_in_p.shape, BF16), _nbytes(w_uq_p.shape, BF16), _nbytes(w_ukv_p.shape, BF16)],
        _nbytes((tm, ncols), F32) + _nbytes((tm, d), F32) + 2 * _nbytes((tm, 4 * B_QK_PAD), F32))
    return pl.pallas_call(
        functools.partial(_proj_kernel, mod_row=mod_row, dims=dims),
        out_shape=[jax.ShapeDtypeStruct((n, w), t) for w, t in out_widths],
        grid=(n // tm,),
        in_specs=[
            row_spec(d), mod_spec(j0), mod_spec(j0 + 1),
            _resident((1, d)), _resident(w_in_p.shape),
            _resident((1, A_HEAD_DIM)), _resident((1, A_HEAD_DIM)),
            _resident((1, b_q)), _resident((1, b_kv)),
            _resident(w_uq_p.shape), _resident(w_ukv_p.shape),
            tab_spec, tab_spec, tab_spec, tab_spec,
        ],
        out_specs=[row_spec(w) for w, _ in out_widths],
        compiler_params=pltpu.CompilerParams(
            dimension_semantics=("parallel",), vmem_limit_bytes=limit),
        name="mixer_in_proj",
    )(h, mod, mod, norm_g, w_in_p, a_q_g, a_k_g, b_q_g, b_kv_g, w_uq_p, w_ukv_p, *tables)


def _attn_kernel(*refs, n_stack, dk, dv, chunks):
    q_ref, o_ref = refs[0], refs[-1]
    kv_refs = refs[1:-1]
    tq = q_ref.shape[0]
    q = jnp.concatenate([q_ref[:, h * dk:(h + 1) * dk] for h in range(n_stack)], axis=0)
    m_rows = n_stack * tq

    def step(k, v, carry):
        m, acc = carry
        s = lax.dot_general(q, k, (((1,), (1,)), ((), ())), preferred_element_type=F32)
        m_new = jnp.maximum(m, jnp.max(s, axis=-1, keepdims=True))
        p = jnp.exp2(s - m_new).astype(BF16)
        v_ones = jnp.concatenate([v, jnp.ones((v.shape[0], V7X_LANES), BF16)], axis=1)
        acc = jnp.exp2(m - m_new) * acc + jnp.dot(p, v_ones, preferred_element_type=F32)
        return m_new, acc

    carry = (jnp.full((m_rows, 1), -jnp.inf, F32), jnp.zeros((m_rows, dv + V7X_LANES), F32))
    for seg, chunk in enumerate(chunks):
        k_ref, v_ref = kv_refs[2 * seg], kv_refs[2 * seg + 1]
        n_chunks = k_ref.shape[0] // chunk
        for c in range(n_chunks):
            carry = step(k_ref[c * chunk:(c + 1) * chunk, :], v_ref[c * chunk:(c + 1) * chunk, :], carry)
    _, acc = carry
    out = acc[:, :dv] / acc[:, dv:dv + dv]
    for h in range(n_stack):
        o_ref[:, h * dv:(h + 1) * dv] = out[h * tq:(h + 1) * tq].astype(BF16)


def _attention(q, segments, *, batch, n_kv, n_stack, dk, dv, tq, chunk):
    nq = q.shape[0]
    lq = nq // batch
    assert lq % tq == 0 and dv == V7X_LANES
    tiles = lq // tq
    in_specs = [pl.BlockSpec((tq, n_stack * dk), lambda b, g, i: (b * tiles + i, g))]
    args, chunks, kv_bytes = [q], [], []
    for k, v in segments:
        lk = k.shape[0] // batch
        chunks.append(min(chunk, lk))
        assert lk % chunks[-1] == 0
        in_specs.append(pl.BlockSpec((lk, dk), lambda b, g, i: (b, g)))
        in_specs.append(pl.BlockSpec((lk, dv), lambda b, g, i: (b, g)))
        args += [k, v]
        kv_bytes += [_nbytes((lk, dk), BF16), _nbytes((lk, dv), BF16)]
    m_rows = n_stack * tq
    limit = _vmem_limit(
        [_nbytes((tq, n_stack * dk), BF16), _nbytes((tq, n_stack * dv), BF16)] + kv_bytes, [],
        24 * _nbytes((m_rows, max(chunks)), F32) + 4 * _nbytes((m_rows, V7X_LANES), F32))
    return pl.pallas_call(
        functools.partial(_attn_kernel, n_stack=n_stack, dk=dk, dv=dv, chunks=tuple(chunks)),
        out_shape=jax.ShapeDtypeStruct((nq, n_kv * n_stack * dv), BF16),
        grid=(batch, n_kv, tiles),
        in_specs=in_specs,
        out_specs=pl.BlockSpec((tq, n_stack * dv), lambda b, g, i: (b * tiles + i, g)),
        compiler_params=pltpu.CompilerParams(
            dimension_semantics=("parallel", "parallel", "arbitrary"), vmem_limit_bytes=limit),
        name="attention",
    )(*args)


def _out_kernel(h_ref, gt_ref, ya_ref, yb_ref, cu_ref, cup_ref, cun_ref, wp_ref, cs_ref, wo_ref,
                o_ref, ext_ref, *, mod_row, seg_len):
    i = pl.program_id(0)
    tm = h_ref.shape[0]
    halo = V7X_SUBLANES
    pos0 = lax.rem(i * tm, seg_len)
    ext_ref[0:halo, :] = jnp.where(pos0 > 0, cup_ref[...], 0.0)
    ext_ref[halo:halo + tm, :] = cu_ref[...]
    ext_ref[halo + tm:, :] = jnp.where(pos0 + tm < seg_len, cun_ref[...], 0.0)

    pos = pos0 + lax.broadcasted_iota(jnp.int32, (tm, 1), 0)
    a_w, b_w = ya_ref.shape[1], yb_ref.shape[1]
    y = jnp.dot(ya_ref[...], wo_ref[0:a_w, :], preferred_element_type=F32)
    y += jnp.dot(yb_ref[...], wo_ref[a_w:a_w + b_w, :], preferred_element_type=F32)
    gd = C_GROUP_DIM
    for g, w in enumerate(C_WINDOWS):
        cols = slice(g * gd, (g + 1) * gd)
        tot = ext_ref[halo - w // 2:halo - w // 2 + tm, cols]
        for off in range(1 - w // 2, w // 2):
            tot = tot + ext_ref[halo + off:halo + off + tm, cols]
        lo = jnp.clip(pos - w // 2, 0, seg_len)
        hi = jnp.clip(pos - w // 2 + w, 0, seg_len)
        pooled = tot / (hi - lo).astype(F32) - cu_ref[:, cols]
        yc = jnp.dot(pooled.astype(BF16), wp_ref[g], preferred_element_type=F32) * cs_ref[:, cols]
        r0 = a_w + b_w + g * gd
        y += jnp.dot(yc.astype(BF16), wo_ref[r0:r0 + gd, :], preferred_element_type=F32)
    o_ref[...] = h_ref[...] + gt_ref[pl.ds(mod_row(i), 1), :] * y


def _mix_out(h, mod, j_gate, ya, yb, cu, w_pool, c_scale, w_out, *, mod_row, seg_len, tm):
    n, d = h.shape
    c_w = cu.shape[1]
    assert n % tm == 0 and seg_len % tm == 0
    halo = V7X_SUBLANES
    per_tile = tm // halo
    last = n // halo - 1
    row_spec = lambda w: pl.BlockSpec((tm, w), lambda i: (i, 0))
    limit = _vmem_limit(
        [2 * _nbytes((tm, d), F32), _nbytes((tm, ya.shape[1]), BF16),
         _nbytes((tm, yb.shape[1]), BF16), _nbytes((tm, c_w), F32)],
        [_nbytes(w_out.shape, BF16), _nbytes(w_pool.shape, BF16), _nbytes((tm + 2 * halo, c_w), F32)],
        2 * _nbytes((tm, d), F32))
    return pl.pallas_call(
        functools.partial(_out_kernel, mod_row=mod_row, seg_len=seg_len),
        out_shape=jax.ShapeDtypeStruct((n, d), F32),
        grid=(n // tm,),
        in_specs=[
            row_spec(d),
            pl.BlockSpec((MOD_ROWS, d), lambda i: (0, j_gate)),
            row_spec(ya.shape[1]), row_spec(yb.shape[1]), row_spec(c_w),
            pl.BlockSpec((halo, c_w), lambda i: (jnp.maximum(i * per_tile - 1, 0), 0)),
            pl.BlockSpec((halo, c_w), lambda i: (jnp.minimum((i + 1) * per_tile, last), 0)),
            _resident(w_pool.shape), _resident((1, c_w)), _resident(w_out.shape),
        ],
        out_specs=row_spec(d),
        scratch_shapes=[pltpu.VMEM((tm + 2 * halo, c_w), F32)],
        compiler_params=pltpu.CompilerParams(
            dimension_semantics=("parallel",), vmem_limit_bytes=limit),
        name="mixer_out_proj",
    )(h, mod, ya, yb, cu, cu, cu, w_pool, c_scale, w_out)


def _rope_tables(seq, dim):
    n = dim // 4
    t = jnp.arange(seq)
    rows = (t // GRID_W).astype(F32)
    cols = (t % GRID_W).astype(F32)
    inv = ROPE_THETA ** (-jnp.arange(n, dtype=F32) / n)
    ang = jnp.concatenate([rows[:, None] * inv[None, :], cols[:, None] * inv[None, :]], axis=-1)
    cos, sin = jnp.cos(ang), jnp.sin(ang)
    pad = jnp.zeros((seq, V7X_LANES - dim), F32)
    return (jnp.concatenate([cos, cos, pad], axis=-1), jnp.concatenate([-sin, sin, pad], axis=-1))


def _identity_tables(rows, dim):
    one = jnp.concatenate([jnp.ones((rows, dim), F32), jnp.zeros((rows, V7X_LANES - dim), F32)], -1)
    return one, jnp.zeros((rows, V7X_LANES), F32)


def _w_in_columns(a_q, a_kv, b_q, b_kv, c_w):
    o = a_q + 2 * a_kv + b_q + b_kv
    half = B_ROPE_DIM // 2
    k1, k2 = np.arange(o, o + half), np.arange(o + half, o + B_ROPE_DIM)
    return np.concatenate([np.arange(o), k1, k2, k2, k1, np.arange(o + B_ROPE_DIM, o + B_ROPE_DIM + c_w)])


def _w_uq_columns(heads):
    hd = B_NOPE_DIM + B_ROPE_DIM
    half = B_ROPE_DIM // 2
    out = []
    for h in range(heads):
        b = h * hd
        x1 = np.arange(b + B_NOPE_DIM, b + B_NOPE_DIM + half)
        x2 = x1 + half
        out += [np.arange(b, b + B_NOPE_DIM), x1, x2, x2, x1]
    return np.concatenate(out)


def _w_ukv_columns(heads):
    hd = B_NOPE_DIM + B_V_DIM
    ks = [np.arange(h * hd, h * hd + B_NOPE_DIM) for h in range(heads)]
    vs = [np.arange(h * hd + B_NOPE_DIM, (h + 1) * hd) for h in range(heads)]
    return np.concatenate(ks + vs)


def kernel(x, c, ctx, c_ctx, w_mod, b_mod, norm_g, ffn1_in, ffn1_out, w_in, a_q_g, a_k_g, b_q_g,
           b_kv_g, b_w_uq, b_w_ukv, c_w_pool, c_scale, w_out, ffn2_in, ffn2_out, final_g):
    batch, seq, d = x.shape
    ctx_len = ctx.shape[1]
    depth = w_mod.shape[0]
    assert batch + 1 <= MOD_ROWS
    b_q, b_kv = b_q_g.shape[1], b_kv_g.shape[1]
    b_heads = b_w_ukv.shape[2] // (B_NOPE_DIM + B_V_DIM)
    c_w = c_scale.shape[1]
    a_q = w_out.shape[1] - b_heads * B_V_DIM - c_w
    a_kv = a_q // A_GROUP
    dims = (a_q, a_kv, b_q, b_kv, b_heads, c_w)

    tm, tm_ctx, tf = 512, 256, 512
    tiles_per_seq = seq // tm
    lat_row = lambda i: lax.div(i, tiles_per_seq)
    ctx_row = lambda i: batch

    c_rows = jnp.concatenate([c, c_ctx[None, :], jnp.zeros((MOD_ROWS - batch - 1, d), F32)], axis=0)
    mod = _modulation(c_rows, w_mod, b_mod)

    tab_lat = _rope_tables(seq, A_HEAD_DIM) + _rope_tables(seq, B_ROPE_DIM)
    tab_ctx = _identity_tables(tm_ctx, A_HEAD_DIM) + _identity_tables(tm_ctx, B_ROPE_DIM)
    cols_in = _w_in_columns(a_q, a_kv, b_q, b_kv, c_w)
    cols_uq, cols_ukv = _w_uq_columns(b_heads), _w_ukv_columns(b_heads)

    h = x.reshape(batch * seq, d)
    hc = ctx.reshape(batch * ctx_len, d)
    for i in range(depth):
        last = i == depth - 1
        ng = norm_g[i]
        f1_in, f1_out = _ffn_weights(ffn1_in[i], ffn1_out[i], tf)
        f2_in, f2_out = _ffn_weights(ffn2_in[i], ffn2_out[i], tf)
        w_in_p = w_in[i][:, cols_in].astype(BF16)
        w_uq_p = b_w_uq[i][:, cols_uq].astype(BF16)
        w_ukv_p = b_w_ukv[i][:, cols_ukv].astype(BF16)
        w_o, w_pool = w_out[i].astype(BF16), c_w_pool[i].astype(BF16)
        gains = (a_q_g[i][None], a_k_g[i][None], b_q_g[i][None], b_kv_g[i][None])
        fg = final_g[None]

        ffn = functools.partial(_ffn, final_g=fg)
        h = ffn(h, mod[i], 0, ng[0:1], f1_in, f1_out, mod_row=lat_row, final_norm=False, tm=tm)
        hc = ffn(hc, mod[i], 0, ng[0:1], f1_in, f1_out, mod_row=ctx_row, final_norm=False, tm=tm_ctx)

        proj = functools.partial(_proj, norm_g=ng[1:2], w_in_p=w_in_p, a_q_g=gains[0], a_k_g=gains[1],
                                 b_q_g=gains[2], b_kv_g=gains[3], w_uq_p=w_uq_p, w_ukv_p=w_ukv_p,
                                 dims=dims)
        qa, ka, va, qb, kb, vb, cu = proj(h, mod[i], 3, tables=tab_lat, mod_row=lat_row,
                                          table_block=lambda t: lax.rem(t, tiles_per_seq), tm=tm)
        qac, kac, vac, qbc, kbc, vbc, cuc = proj(hc, mod[i], 3, tables=tab_ctx, mod_row=ctx_row,
                                                 table_block=lambda t: 0, tm=tm_ctx)

        attn_a = functools.partial(_attention, batch=batch, n_kv=a_kv // A_HEAD_DIM, n_stack=A_GROUP,
                                   dk=A_HEAD_DIM, dv=A_HEAD_DIM, tq=256, chunk=256)
        attn_b = functools.partial(_attention, batch=batch, n_kv=b_heads, n_stack=1,
                                   dk=B_QK_PAD, dv=B_V_DIM, tq=1024, chunk=256)
        ya = attn_a(qa, [(kac, vac), (ka, va)])
        yb = attn_b(qb, [(kbc, vbc), (kb, vb)])
        mix = functools.partial(_mix_out, w_pool=w_pool, c_scale=c_scale[i][None], w_out=w_o)
        h = mix(h, mod[i], 5, ya, yb, cu, mod_row=lat_row, seg_len=seq, tm=tm)
        h = ffn(h, mod[i], 6, ng[2:3], f2_in, f2_out, mod_row=lat_row, final_norm=last, tm=tm)
        if not last:
            yac = attn_a(qac, [(kac, vac)], tq=ctx_len)
            ybc = attn_b(qbc, [(kbc, vbc)], tq=ctx_len)
            hc = mix(hc, mod[i], 5, yac, ybc, cuc, mod_row=ctx_row, seg_len=ctx_len, tm=tm_ctx)
            hc = ffn(hc, mod[i], 6, ng[2:3], f2_in, f2_out, mod_row=ctx_row, final_norm=False, tm=tm_ctx)
    return h.reshape(batch, seq, d)
```

```python
import functools
import math

import jax
import jax.numpy as jnp
import numpy as np
from jax import lax
from jax.experimental import pallas as pl
from jax.experimental.pallas import tpu as pltpu

F32 = jnp.float32
BF16 = jnp.bfloat16

GRID_W = 64
ROPE_THETA = 10000.0
NORM_EPS = 1e-6
N_MOD = 9
A_HEAD_DIM = 128
A_GROUP = 4
B_NOPE_DIM = 128
B_ROPE_DIM = 64
B_V_DIM = 128
B_QK_PAD = 256
C_WINDOWS = (2, 4, 8, 16)
C_GROUP_DIM = 128
LOG2E = 1.4426950408889634

V7X_LANES = 128
V7X_SUBLANES = 8
V7X_VMEM_BYTES = 64 * 1024 * 1024
V7X_VMEM_RESERVE_BYTES = 8 * 1024 * 1024
MOD_ROWS = V7X_SUBLANES


def _nbytes(shape, dtype):
    return int(np.prod(shape)) * jnp.dtype(dtype).itemsize


def _vmem_limit(pipelined, resident, temps):
    need = 2 * sum(pipelined) + sum(resident) + temps
    return int(min(V7X_VMEM_BYTES - V7X_VMEM_RESERVE_BYTES, need + need // 4))


def _rms(x, eps=NORM_EPS):
    return x * lax.rsqrt(jnp.mean(x * x, axis=-1, keepdims=True) + eps)


def _resident(shape):
    nd = len(shape)
    return pl.BlockSpec(shape, lambda *_: (0,) * nd, pipeline_mode=pl.Buffered(1))


def _mod_kernel(c_ref, w_ref, b_ref, o_ref):
    c = c_ref[...]
    act = (c * jax.nn.sigmoid(c)).astype(BF16)
    o_ref[...] = jnp.dot(act, w_ref[...].astype(BF16), preferred_element_type=F32) + b_ref[...]


def _modulation(c_rows, w_mod, b_mod, *, tn=1024):
    depth, d, nm = w_mod.shape
    assert nm % tn == 0
    limit = _vmem_limit([_nbytes((d, tn), F32), _nbytes((MOD_ROWS, tn), F32)],
                        [_nbytes((MOD_ROWS, d), F32)], _nbytes((d, tn), BF16))
    return pl.pallas_call(
        _mod_kernel,
        out_shape=jax.ShapeDtypeStruct((depth, MOD_ROWS, nm), F32),
        grid=(depth, nm // tn),
        in_specs=[
            pl.BlockSpec((MOD_ROWS, d), lambda l, j: (0, 0)),
            pl.BlockSpec((None, d, tn), lambda l, j: (l, 0, j)),
            pl.BlockSpec((None, 1, tn), lambda l, j: (l, 0, j)),
        ],
        out_specs=pl.BlockSpec((None, MOD_ROWS, tn), lambda l, j: (l, 0, j)),
        compiler_params=pltpu.CompilerParams(
            dimension_semantics=("parallel", "parallel"), vmem_limit_bytes=limit),
        name="modulation",
    )(c_rows, w_mod, b_mod.reshape(depth, 1, nm))


def _ffn_kernel(h_ref, hn_ref, sh_ref, sc_ref, gt_ref, ng_ref, wa_ref, wb_ref, wo_ref, fg_ref,
                o_ref, xn0_ref, xn1_ref, *, mod_row, final_norm, n_slices):
    i, k = pl.program_id(0), pl.program_id(1)
    n_tiles, nk = pl.num_programs(0), pl.num_programs(1)
    slice_rows = hn_ref.shape[0]
    parity = lax.rem(i, 2)

    def modnorm(x, row):
        y = _rms(x) * ng_ref[...]
        return (y * (1.0 + sc_ref[pl.ds(row, 1), :]) + sh_ref[pl.ds(row, 1), :]).astype(BF16)

    @pl.when((i == 0) & (k == 0))
    def _():
        xn0_ref[...] = modnorm(h_ref[...], mod_row(i))

    def step(first, cur_ref, nxt_ref):
        r0 = pl.multiple_of(jnp.minimum(k, n_slices - 1) * slice_rows, 16)
        nxt_row = mod_row(jnp.minimum(i + 1, n_tiles - 1))
        nxt_ref[pl.ds(r0, slice_rows), :] = modnorm(hn_ref[...], nxt_row)

        xn = cur_ref[...]
        a = jnp.dot(xn, wa_ref[...], preferred_element_type=F32)
        b = jnp.dot(xn, wb_ref[...], preferred_element_type=F32)
        g = (a * jax.nn.sigmoid(a) * b).astype(BF16)
        part = jnp.dot(g, wo_ref[...], preferred_element_type=F32)
        part = part * (0.5 * gt_ref[pl.ds(mod_row(i), 1), :])
        o_ref[...] = (h_ref[...] if first else o_ref[...]) + part

    for first in (True, False):
        for par, (cur_ref, nxt_ref) in enumerate(((xn0_ref, xn1_ref), (xn1_ref, xn0_ref))):
            cond = ((k == 0) if first else (k > 0)) & (parity == par)
            pl.when(cond)(functools.partial(step, first, cur_ref, nxt_ref))

    if final_norm:
        @pl.when(k == nk - 1)
        def _():
            o_ref[...] = _rms(o_ref[...]) * fg_ref[...]


def _ffn(h, mod, j0, norm_g, w_in, w_out, final_g, *, mod_row, final_norm, tm, tf):
    n, d = h.shape
    dff = w_out.shape[0]
    assert n % tm == 0 and dff % tf == 0
    nk, n_tiles = dff // tf, n // tm
    n_slices = 8
    slice_rows = tm // n_slices
    assert n_slices <= nk and slice_rows % 16 == 0
    mod_spec = lambda j: pl.BlockSpec((MOD_ROWS, d), lambda i, k: (0, j))
    limit = _vmem_limit(
        [_nbytes((tm, d), F32) * 2, _nbytes((slice_rows, d), F32),
         _nbytes((d, tf), BF16) * 2, _nbytes((tf, d), BF16)],
        [2 * _nbytes((tm, d), BF16)],
        4 * _nbytes((tm, tf), F32) + _nbytes((tm, d), F32))

    def next_slice(i, k):
        return (jnp.minimum(i + 1, n_tiles - 1) * n_slices + jnp.minimum(k, n_slices - 1), 0)

    return pl.pallas_call(
        functools.partial(_ffn_kernel, mod_row=mod_row, final_norm=final_norm, n_slices=n_slices),
        out_shape=jax.ShapeDtypeStruct((n, d), F32),
        grid=(n_tiles, nk),
        in_specs=[
            pl.BlockSpec((tm, d), lambda i, k: (i, 0)),
            pl.BlockSpec((slice_rows, d), next_slice),
            mod_spec(j0), mod_spec(j0 + 1), mod_spec(j0 + 2),
            pl.BlockSpec((1, d), lambda i, k: (0, 0)),
            pl.BlockSpec((d, tf), lambda i, k: (0, k)),
            pl.BlockSpec((d, tf), lambda i, k: (0, k + nk)),
            pl.BlockSpec((tf, d), lambda i, k: (k, 0)),
            pl.BlockSpec((1, d), lambda i, k: (0, 0)),
        ],
        out_specs=pl.BlockSpec((tm, d), lambda i, k: (i, 0)),
        scratch_shapes=[pltpu.VMEM((tm, d), BF16), pltpu.VMEM((tm, d), BF16)],
        compiler_params=pltpu.CompilerParams(
            dimension_semantics=("arbitrary", "arbitrary"), vmem_limit_bytes=limit),
        name="swiglu_half_step",
    )(h, h, mod, mod, mod, norm_g, w_in, w_in, w_out, final_g)


def _rope(x, cos, sin_signed):
    return x * cos + pltpu.roll(x, V7X_LANES // 2, 1) * sin_signed


def _proj_kernel(h_ref, sh_ref, sc_ref, ng_ref, win_ref, aqg_ref, akg_ref, bqg_ref, bkvg_ref,
                 wuq_ref, wukv_ref, ca_ref, sa_ref, cb_ref, sb_ref,
                 qa_ref, ka_ref, va_ref, qb_ref, kb_ref, vb_ref, cu_ref, *, mod_row, dims):
    a_q, a_kv, b_q, b_kv, b_heads, c_w = dims
    row = mod_row(pl.program_id(0))
    y = _rms(h_ref[...]) * ng_ref[...]
    y = y * (1.0 + sc_ref[pl.ds(row, 1), :]) + sh_ref[pl.ds(row, 1), :]
    u = jnp.dot(y.astype(BF16), win_ref[...], preferred_element_type=F32)

    ca, sa, cb, sb = ca_ref[...], sa_ref[...], cb_ref[...], sb_ref[...]
    hd = A_HEAD_DIM
    o_ak, o_av, o_bq = a_q, a_q + a_kv, a_q + 2 * a_kv
    o_bkv, o_bkr = o_bq + b_q, o_bq + b_q + b_kv
    o_cu = o_bkr + V7X_LANES

    qg = aqg_ref[...] * (hd ** -0.5 * LOG2E)
    for h in range(a_q // hd):
        x = _rms(u[:, h * hd:(h + 1) * hd]) * qg
        qa_ref[:, h * hd:(h + 1) * hd] = _rope(x, ca, sa).astype(BF16)
    kg = akg_ref[...]
    for h in range(a_kv // hd):
        x = _rms(u[:, o_ak + h * hd:o_ak + (h + 1) * hd]) * kg
        ka_ref[:, h * hd:(h + 1) * hd] = _rope(x, ca, sa).astype(BF16)
    va_ref[...] = u[:, o_av:o_av + a_kv].astype(BF16)

    xq = (_rms(u[:, o_bq:o_bq + b_q]) * bqg_ref[...]).astype(BF16)
    q = jnp.dot(xq, wuq_ref[...], preferred_element_type=F32)
    q_scale = (B_NOPE_DIM + B_ROPE_DIM) ** -0.5 * LOG2E
    xkv = (_rms(u[:, o_bkv:o_bkv + b_kv]) * bkvg_ref[...]).astype(BF16)
    kv = jnp.dot(xkv, wukv_ref[...], preferred_element_type=F32)
    k_rope = _rope(u[:, o_bkr:o_bkr + V7X_LANES], cb, sb).astype(BF16)
    for h in range(b_heads):
        o = h * B_QK_PAD
        qb_ref[:, o:o + B_NOPE_DIM] = (q[:, o:o + B_NOPE_DIM] * q_scale).astype(BF16)
        qr = _rope(q[:, o + B_NOPE_DIM:o + B_QK_PAD], cb, sb)
        qb_ref[:, o + B_NOPE_DIM:o + B_QK_PAD] = (qr * q_scale).astype(BF16)
        kb_ref[:, o:o + B_NOPE_DIM] = kv[:, h * B_NOPE_DIM:(h + 1) * B_NOPE_DIM].astype(BF16)
        kb_ref[:, o + B_NOPE_DIM:o + B_QK_PAD] = k_rope
    vb_ref[...] = kv[:, b_heads * B_NOPE_DIM:].astype(BF16)

    cu_ref[...] = u[:, o_cu:o_cu + c_w]


def _proj(h, mod, j0, norm_g, w_in_p, a_q_g, a_k_g, b_q_g, b_kv_g, w_uq_p, w_ukv_p, tables,
          *, mod_row, table_block, dims, tm):
    n, d = h.shape
    a_q, a_kv, b_q, b_kv, b_heads, c_w = dims
    assert n % tm == 0
    ncols = w_in_p.shape[1]
    mod_spec = lambda j: pl.BlockSpec((MOD_ROWS, d), lambda i: (0, j))
    tab_spec = pl.BlockSpec((tm, V7X_LANES), lambda i: (table_block(i), 0))
    row_spec = lambda w: pl.BlockSpec((tm, w), lambda i: (i, 0))
    out_widths = [(a_q, BF16), (a_kv, BF16), (a_kv, BF16), (b_heads * B_QK_PAD, BF16),
                  (b_heads * B_QK_PAD, BF16), (b_heads * B_V_DIM, BF16), (c_w, F32)]
    limit = _vmem_limit(
        [_nbytes((tm, d), F32)] + [_nbytes((tm, w), t) for w, t in out_widths]
        + [4 * _nbytes((tm, V7X_LANES), F32)],
        [_nbytes(w_in_p.shape, BF16), _nbytes(w_uq_p.shape, BF16), _nbytes(w_ukv_p.shape, BF16)],
        _nbytes((tm, ncols), F32) + _nbytes((tm, d), F32) + 2 * _nbytes((tm, 4 * B_QK_PAD), F32))
    return pl.pallas_call(
        functools.partial(_proj_kernel, mod_row=mod_row, dims=dims),
        out_shape=[jax.ShapeDtypeStruct((n, w), t) for w, t in out_widths],
        grid=(n // tm,),
        in_specs=[
            row_spec(d), mod_spec(j0), mod_spec(j0 + 1),
            _resident((1, d)), _resident(w_in_p.shape),
            _resident((1, A_HEAD_DIM)), _resident((1, A_HEAD_DIM)),
            _resident((1, b_q)), _resident((1, b_kv)),
            _resident(w_uq_p.shape), _resident(w_ukv_p.shape),
            tab_spec, tab_spec, tab_spec, tab_spec,
        ],
        out_specs=[row_spec(w) for w, _ in out_widths],
        compiler_params=pltpu.CompilerParams(
            dimension_semantics=("parallel",), vmem_limit_bytes=limit),
        name="mixer_in_proj",
    )(h, mod, mod, norm_g, w_in_p, a_q_g, a_k_g, b_q_g, b_kv_g, w_uq_p, w_ukv_p, *tables)


def _attn_kernel(*refs, n_stack, dk, dv, chunks):
    q_ref, o_ref = refs[0], refs[-1]
    kv_refs = refs[1:-1]
    tq = q_ref.shape[0]
    q = jnp.concatenate([q_ref[:, h * dk:(h + 1) * dk] for h in range(n_stack)], axis=0)
    m_rows = n_stack * tq

    def step(k, v, carry):
        m, acc = carry
        s = lax.dot_general(q, k, (((1,), (1,)), ((), ())), preferred_element_type=F32)
        m_new = jnp.maximum(m, jnp.max(s, axis=-1, keepdims=True))
        p = jnp.exp2(s - m_new).astype(BF16)
        v_ones = jnp.concatenate([v, jnp.ones((v.shape[0], V7X_LANES), BF16)], axis=1)
        acc = jnp.exp2(m - m_new) * acc + jnp.dot(p, v_ones, preferred_element_type=F32)
        return m_new, acc

    carry = (jnp.full((m_rows, 1), -jnp.inf, F32), jnp.zeros((m_rows, dv + V7X_LANES), F32))
    for seg, chunk in enumerate(chunks):
        k_ref, v_ref = kv_refs[2 * seg], kv_refs[2 * seg + 1]
        n_chunks = k_ref.shape[0] // chunk
        for c in range(n_chunks):
            carry = step(k_ref[c * chunk:(c + 1) * chunk, :], v_ref[c * chunk:(c + 1) * chunk, :], carry)
    _, acc = carry
    out = acc[:, :dv] / acc[:, dv:dv + dv]
    for h in range(n_stack):
        o_ref[:, h * dv:(h + 1) * dv] = out[h * tq:(h + 1) * tq].astype(BF16)


def _attention(q, segments, *, batch, n_kv, n_stack, dk, dv, tq, chunk):
    nq = q.shape[0]
    lq = nq // batch
    assert lq % tq == 0 and dv == V7X_LANES
    tiles = lq // tq
    in_specs = [pl.BlockSpec((tq, n_stack * dk), lambda b, g, i: (b * tiles + i, g))]
    args, chunks, kv_bytes = [q], [], []
    for k, v in segments:
        lk = k.shape[0] // batch
        chunks.append(min(chunk, lk))
        assert lk % chunks[-1] == 0
        in_specs.append(pl.BlockSpec((lk, dk), lambda b, g, i: (b, g)))
        in_specs.append(pl.BlockSpec((lk, dv), lambda b, g, i: (b, g)))
        args += [k, v]
        kv_bytes += [_nbytes((lk, dk), BF16), _nbytes((lk, dv), BF16)]
    m_rows = n_stack * tq
    limit = _vmem_limit(
        [_nbytes((tq, n_stack * dk), BF16), _nbytes((tq, n_stack * dv), BF16)] + kv_bytes, [],
        24 * _nbytes((m_rows, max(chunks)), F32) + 4 * _nbytes((m_rows, V7X_LANES), F32))
    return pl.pallas_call(
        functools.partial(_attn_kernel, n_stack=n_stack, dk=dk, dv=dv, chunks=tuple(chunks)),
        out_shape=jax.ShapeDtypeStruct((nq, n_kv * n_stack * dv), BF16),
        grid=(batch, n_kv, tiles),
        in_specs=in_specs,
        out_specs=pl.BlockSpec((tq, n_stack * dv), lambda b, g, i: (b * tiles + i, g)),
        compiler_params=pltpu.CompilerParams(
            dimension_semantics=("parallel", "parallel", "arbitrary"), vmem_limit_bytes=limit),
        name="attention",
    )(*args)


def _out_kernel(h_ref, gt_ref, ya_ref, yb_ref, cu_ref, cup_ref, cun_ref, wp_ref, cs_ref, wo_ref,
                o_ref, ext_ref, *, mod_row, seg_len):
    i = pl.program_id(0)
    tm = h_ref.shape[0]
    halo = V7X_SUBLANES
    pos0 = lax.rem(i * tm, seg_len)
    ext_ref[0:halo, :] = jnp.where(pos0 > 0, cup_ref[...], 0.0)
    ext_ref[halo:halo + tm, :] = cu_ref[...]
    ext_ref[halo + tm:, :] = jnp.where(pos0 + tm < seg_len, cun_ref[...], 0.0)

    pos = pos0 + lax.broadcasted_iota(jnp.int32, (tm, 1), 0)
    a_w, b_w = ya_ref.shape[1], yb_ref.shape[1]
    y = jnp.dot(ya_ref[...], wo_ref[0:a_w, :], preferred_element_type=F32)
    y += jnp.dot(yb_ref[...], wo_ref[a_w:a_w + b_w, :], preferred_element_type=F32)
    gd = C_GROUP_DIM
    for g, w in enumerate(C_WINDOWS):
        cols = slice(g * gd, (g + 1) * gd)
        tot = ext_ref[halo - w // 2:halo - w // 2 + tm, cols]
        for off in range(1 - w // 2, w // 2):
            tot = tot + ext_ref[halo + off:halo + off + tm, cols]
        lo = jnp.clip(pos - w // 2, 0, seg_len)
        hi = jnp.clip(pos - w // 2 + w, 0, seg_len)
        pooled = tot / (hi - lo).astype(F32) - cu_ref[:, cols]
        yc = jnp.dot(pooled.astype(BF16), wp_ref[g], preferred_element_type=F32) * cs_ref[:, cols]
        r0 = a_w + b_w + g * gd
        y += jnp.dot(yc.astype(BF16), wo_ref[r0:r0 + gd, :], preferred_element_type=F32)
    o_ref[...] = h_ref[...] + gt_ref[pl.ds(mod_row(i), 1), :] * y


def _mix_out(h, mod, j_gate, ya, yb, cu, w_pool, c_scale, w_out, *, mod_row, seg_len, tm):
    n, d = h.shape
    c_w = cu.shape[1]
    assert n % tm == 0 and seg_len % tm == 0
    halo = V7X_SUBLANES
    per_tile = tm // halo
    last = n // halo - 1
    row_spec = lambda w: pl.BlockSpec((tm, w), lambda i: (i, 0))
    limit = _vmem_limit(
        [2 * _nbytes((tm, d), F32), _nbytes((tm, ya.shape[1]), BF16),
         _nbytes((tm, yb.shape[1]), BF16), _nbytes((tm, c_w), F32)],
        [_nbytes(w_out.shape, BF16), _nbytes(w_pool.shape, BF16), _nbytes((tm + 2 * halo, c_w), F32)],
        2 * _nbytes((tm, d), F32))
    return pl.pallas_call(
        functools.partial(_out_kernel, mod_row=mod_row, seg_len=seg_len),
        out_shape=jax.ShapeDtypeStruct((n, d), F32),
        grid=(n // tm,),
        in_specs=[
            row_spec(d),
            pl.BlockSpec((MOD_ROWS, d), lambda i: (0, j_gate)),
            row_spec(ya.shape[1]), row_spec(yb.shape[1]), row_spec(c_w),
            pl.BlockSpec((halo, c_w), lambda i: (jnp.maximum(i * per_tile - 1, 0), 0)),
            pl.BlockSpec((halo, c_w), lambda i: (jnp.minimum((i + 1) * per_tile, last), 0)),
            _resident(w_pool.shape), _resident((1, c_w)), _resident(w_out.shape),
        ],
        out_specs=row_spec(d),
        scratch_shapes=[pltpu.VMEM((tm + 2 * halo, c_w), F32)],
        compiler_params=pltpu.CompilerParams(
            dimension_semantics=("parallel",), vmem_limit_bytes=limit),
        name="mixer_out_proj",
    )(h, mod, ya, yb, cu, cu, cu, w_pool, c_scale, w_out)


def _rope_tables(seq, dim):
    n = dim // 4
    t = jnp.arange(seq)
    rows = (t // GRID_W).astype(F32)
    cols = (t % GRID_W).astype(F32)
    inv = ROPE_THETA ** (-jnp.arange(n, dtype=F32) / n)
    ang = jnp.concatenate([rows[:, None] * inv[None, :], cols[:, None] * inv[None, :]], axis=-1)
    cos, sin = jnp.cos(ang), jnp.sin(ang)
    pad = jnp.zeros((seq, V7X_LANES - dim), F32)
    return (jnp.concatenate([cos, cos, pad], axis=-1), jnp.concatenate([-sin, sin, pad], axis=-1))


def _identity_tables(rows, dim):
    one = jnp.concatenate([jnp.ones((rows, dim), F32), jnp.zeros((rows, V7X_LANES - dim), F32)], -1)
    return one, jnp.zeros((rows, V7X_LANES), F32)


def _w_in_columns(a_q, a_kv, b_q, b_kv, c_w):
    o = a_q + 2 * a_kv + b_q + b_kv
    half = B_ROPE_DIM // 2
    k1, k2 = np.arange(o, o + half), np.arange(o + half, o + B_ROPE_DIM)
    return np.concatenate([np.arange(o), k1, k2, k2, k1, np.arange(o + B_ROPE_DIM, o + B_ROPE_DIM + c_w)])


def _w_uq_columns(heads):
    hd = B_NOPE_DIM + B_ROPE_DIM
    half = B_ROPE_DIM // 2
    out = []
    for h in range(heads):
        b = h * hd
        x1 = np.arange(b + B_NOPE_DIM, b + B_NOPE_DIM + half)
        x2 = x1 + half
        out += [np.arange(b, b + B_NOPE_DIM), x1, x2, x2, x1]
    return np.concatenate(out)


def _w_ukv_columns(heads):
    hd = B_NOPE_DIM + B_V_DIM
    ks = [np.arange(h * hd, h * hd + B_NOPE_DIM) for h in range(heads)]
    vs = [np.arange(h * hd + B_NOPE_DIM, (h + 1) * hd) for h in range(heads)]
    return np.concatenate(ks + vs)


def kernel(x, c, ctx, c_ctx, w_mod, b_mod, norm_g, ffn1_in, ffn1_out, w_in, a_q_g, a_k_g, b_q_g,
           b_kv_g, b_w_uq, b_w_ukv, c_w_pool, c_scale, w_out, ffn2_in, ffn2_out, final_g):
    batch, seq, d = x.shape
    ctx_len = ctx.shape[1]
    depth = w_mod.shape[0]
    assert batch + 1 <= MOD_ROWS
    b_q, b_kv = b_q_g.shape[1], b_kv_g.shape[1]
    b_heads = b_w_ukv.shape[2] // (B_NOPE_DIM + B_V_DIM)
    c_w = c_scale.shape[1]
    a_q = w_out.shape[1] - b_heads * B_V_DIM - c_w
    a_kv = a_q // A_GROUP
    dims = (a_q, a_kv, b_q, b_kv, b_heads, c_w)

    tm, tm_ctx, tf = 512, 256, 512
    tiles_per_seq = seq // tm
    lat_row = lambda i: lax.div(i, tiles_per_seq)
    ctx_row = lambda i: batch

    c_rows = jnp.concatenate([c, c_ctx[None, :], jnp.zeros((MOD_ROWS - batch - 1, d), F32)], axis=0)
    mod = _modulation(c_rows, w_mod, b_mod)

    tab_lat = _rope_tables(seq, A_HEAD_DIM) + _rope_tables(seq, B_ROPE_DIM)
    tab_ctx = _identity_tables(tm_ctx, A_HEAD_DIM) + _identity_tables(tm_ctx, B_ROPE_DIM)
    cols_in = _w_in_columns(a_q, a_kv, b_q, b_kv, c_w)
    cols_uq, cols_ukv = _w_uq_columns(b_heads), _w_ukv_columns(b_heads)

    h = x.reshape(batch * seq, d)
    hc = ctx.reshape(batch * ctx_len, d)
    for i in range(depth):
        last = i == depth - 1
        ng = norm_g[i]
        f1_in, f1_out = ffn1_in[i].astype(BF16), ffn1_out[i].astype(BF16)
        f2_in, f2_out = ffn2_in[i].astype(BF16), ffn2_out[i].astype(BF16)
        w_in_p = w_in[i][:, cols_in].astype(BF16)
        w_uq_p = b_w_uq[i][:, cols_uq].astype(BF16)
        w_ukv_p = b_w_ukv[i][:, cols_ukv].astype(BF16)
        w_o, w_pool = w_out[i].astype(BF16), c_w_pool[i].astype(BF16)
        gains = (a_q_g[i][None], a_k_g[i][None], b_q_g[i][None], b_kv_g[i][None])
        fg = final_g[None]

        ffn = functools.partial(_ffn, final_g=fg, tf=tf)
        h = ffn(h, mod[i], 0, ng[0:1], f1_in, f1_out, mod_row=lat_row, final_norm=False, tm=tm)
        hc = ffn(hc, mod[i], 0, ng[0:1], f1_in, f1_out, mod_row=ctx_row, final_norm=False, tm=tm_ctx)

        proj = functools.partial(_proj, norm_g=ng[1:2], w_in_p=w_in_p, a_q_g=gains[0], a_k_g=gains[1],
                                 b_q_g=gains[2], b_kv_g=gains[3], w_uq_p=w_uq_p, w_ukv_p=w_ukv_p,
                                 dims=dims)
        qa, ka, va, qb, kb, vb, cu = proj(h, mod[i], 3, tables=tab_lat, mod_row=lat_row,
                                          table_block=lambda t: lax.rem(t, tiles_per_seq), tm=tm)
        qac, kac, vac, qbc, kbc, vbc, cuc = proj(hc, mod[i], 3, tables=tab_ctx, mod_row=ctx_row,
                                                 table_block=lambda t: 0, tm=tm_ctx)

        attn_a = functools.partial(_attention, batch=batch, n_kv=a_kv // A_HEAD_DIM, n_stack=A_GROUP,
                                   dk=A_HEAD_DIM, dv=A_HEAD_DIM, tq=256, chunk=256)
        attn_b = functools.partial(_attention, batch=batch, n_kv=b_heads, n_stack=1,
                                   dk=B_QK_PAD, dv=B_V_DIM, tq=1024, chunk=256)
        ya = attn_a(qa, [(kac, vac), (ka, va)])
        yb = attn_b(qb, [(kbc, vbc), (kb, vb)])
        mix = functools.partial(_mix_out, w_pool=w_pool, c_scale=c_scale[i][None], w_out=w_o)
        h = mix(h, mod[i], 5, ya, yb, cu, mod_row=lat_row, seg_len=seq, tm=tm)
        h = ffn(h, mod[i], 6, ng[2:3], f2_in, f2_out, mod_row=lat_row, final_norm=last, tm=tm)
        if not last:
            yac = attn_a(qac, [(kac, vac)], tq=ctx_len)
            ybc = attn_b(qbc, [(kbc, vbc)], tq=ctx_len)
            hc = mix(hc, mod[i], 5, yac, ybc, cuc, mod_row=ctx_row, seg_len=ctx_len, tm=tm_ctx)
            hc = ffn(hc, mod[i], 6, ng[2:3], f2_in, f2_out, mod_row=ctx_row, final_norm=False, tm=tm_ctx)
    return h.reshape(batch, seq, d)
```

```python
import functools
import math

import jax
import jax.numpy as jnp
import numpy as np
from jax import lax
from jax.experimental import pallas as pl
from jax.experimental.pallas import tpu as pltpu

F32 = jnp.float32
BF16 = jnp.bfloat16

GRID_W = 64
ROPE_THETA = 10000.0
NORM_EPS = 1e-6
N_MOD = 9
A_HEAD_DIM = 128
A_GROUP = 4
B_NOPE_DIM = 128
B_ROPE_DIM = 64
B_V_DIM = 128
B_QK_PAD = 256
C_WINDOWS = (2, 4, 8, 16)
C_GROUP_DIM = 128
LOG2E = 1.4426950408889634

V7X_LANES = 128
V7X_SUBLANES = 8
V7X_VMEM_BYTES = 64 * 1024 * 1024
V7X_VMEM_RESERVE_BYTES = 8 * 1024 * 1024
MOD_ROWS = V7X_SUBLANES


def _nbytes(shape, dtype):
    return int(np.prod(shape)) * jnp.dtype(dtype).itemsize


def _vmem_limit(pipelined, resident, temps):
    need = 2 * sum(pipelined) + sum(resident) + temps
    return int(min(V7X_VMEM_BYTES - V7X_VMEM_RESERVE_BYTES, need + need // 4))


def _rms(x, eps=NORM_EPS):
    return x * lax.rsqrt(jnp.mean(x * x, axis=-1, keepdims=True) + eps)


def _resident(shape):
    nd = len(shape)
    return pl.BlockSpec(shape, lambda *_: (0,) * nd, pipeline_mode=pl.Buffered(1))


def _mod_kernel(c_ref, w_ref, b_ref, o_ref):
    c = c_ref[...]
    act = (c * jax.nn.sigmoid(c)).astype(BF16)
    o_ref[...] = jnp.dot(act, w_ref[...].astype(BF16), preferred_element_type=F32) + b_ref[...]


def _modulation(c_rows, w_mod, b_mod, *, tn=1024):
    depth, d, nm = w_mod.shape
    assert nm % tn == 0
    limit = _vmem_limit([_nbytes((d, tn), F32), _nbytes((MOD_ROWS, tn), F32)],
                        [_nbytes((MOD_ROWS, d), F32)], _nbytes((d, tn), BF16))
    return pl.pallas_call(
        _mod_kernel,
        out_shape=jax.ShapeDtypeStruct((depth, MOD_ROWS, nm), F32),
        grid=(depth, nm // tn),
        in_specs=[
            pl.BlockSpec((MOD_ROWS, d), lambda l, j: (0, 0)),
            pl.BlockSpec((None, d, tn), lambda l, j: (l, 0, j)),
            pl.BlockSpec((None, 1, tn), lambda l, j: (l, 0, j)),
        ],
        out_specs=pl.BlockSpec((None, MOD_ROWS, tn), lambda l, j: (l, 0, j)),
        compiler_params=pltpu.CompilerParams(
            dimension_semantics=("parallel", "parallel"), vmem_limit_bytes=limit),
        name="modulation",
    )(c_rows, w_mod, b_mod.reshape(depth, 1, nm))


def _ffn_kernel(h_ref, hn_ref, sh_ref, sc_ref, gt_ref, ng_ref, wa_ref, wb_ref, wo_ref, fg_ref,
                o_ref, xn0_ref, xn1_ref, *, mod_row, final_norm, n_slices):
    i, k = pl.program_id(0), pl.program_id(1)
    n_tiles, nk = pl.num_programs(0), pl.num_programs(1)
    slice_rows = hn_ref.shape[0]
    parity = lax.rem(i, 2)

    def modnorm(x, row):
        y = _rms(x) * ng_ref[...]
        return (y * (1.0 + sc_ref[pl.ds(row, 1), :]) + sh_ref[pl.ds(row, 1), :]).astype(BF16)

    @pl.when((i == 0) & (k == 0))
    def _():
        xn0_ref[...] = modnorm(h_ref[...], mod_row(i))

    def step(first, cur_ref, nxt_ref):
        r0 = pl.multiple_of(jnp.minimum(k, n_slices - 1) * slice_rows, 16)
        nxt_row = mod_row(jnp.minimum(i + 1, n_tiles - 1))
        nxt_ref[pl.ds(r0, slice_rows), :] = modnorm(hn_ref[...], nxt_row)

        xn = cur_ref[...]
        a = jnp.dot(xn, wa_ref[...], preferred_element_type=F32)
        b = jnp.dot(xn, wb_ref[...], preferred_element_type=F32)
        g = (a * jax.nn.sigmoid(a) * b).astype(BF16)
        part = jnp.dot(g, wo_ref[...], preferred_element_type=F32)
        part = part * (0.5 * gt_ref[pl.ds(mod_row(i), 1), :])
        o_ref[...] = (h_ref[...] if first else o_ref[...]) + part

    for first in (True, False):
        for par, (cur_ref, nxt_ref) in enumerate(((xn0_ref, xn1_ref), (xn1_ref, xn0_ref))):
            cond = ((k == 0) if first else (k > 0)) & (parity == par)
            pl.when(cond)(functools.partial(step, first, cur_ref, nxt_ref))

    if final_norm:
        @pl.when(k == nk - 1)
        def _():
            o_ref[...] = _rms(o_ref[...]) * fg_ref[...]


def _ffn(h, mod, j0, norm_g, w_in, w_out, final_g, *, mod_row, final_norm, tm, tf):
    n, d = h.shape
    dff = w_out.shape[0]
    assert n % tm == 0 and dff % tf == 0
    nk, n_tiles = dff // tf, n // tm
    slice_rows = 64
    n_slices = tm // slice_rows
    assert n_slices <= nk and tm % slice_rows == 0
    mod_spec = lambda j: pl.BlockSpec((MOD_ROWS, d), lambda i, k: (0, j))
    limit = _vmem_limit(
        [_nbytes((tm, d), F32) * 2, _nbytes((slice_rows, d), F32),
         _nbytes((d, tf), BF16) * 2, _nbytes((tf, d), BF16)],
        [2 * _nbytes((tm, d), BF16)],
        4 * _nbytes((tm, tf), F32) + _nbytes((tm, d), F32))

    def next_slice(i, k):
        return (jnp.minimum(i + 1, n_tiles - 1) * n_slices + jnp.minimum(k, n_slices - 1), 0)

    return pl.pallas_call(
        functools.partial(_ffn_kernel, mod_row=mod_row, final_norm=final_norm, n_slices=n_slices),
        out_shape=jax.ShapeDtypeStruct((n, d), F32),
        grid=(n_tiles, nk),
        in_specs=[
            pl.BlockSpec((tm, d), lambda i, k: (i, 0)),
            pl.BlockSpec((slice_rows, d), next_slice),
            mod_spec(j0), mod_spec(j0 + 1), mod_spec(j0 + 2),
            pl.BlockSpec((1, d), lambda i, k: (0, 0)),
            pl.BlockSpec((d, tf), lambda i, k: (0, k)),
            pl.BlockSpec((d, tf), lambda i, k: (0, k + nk)),
            pl.BlockSpec((tf, d), lambda i, k: (k, 0)),
            pl.BlockSpec((1, d), lambda i, k: (0, 0)),
        ],
        out_specs=pl.BlockSpec((tm, d), lambda i, k: (i, 0)),
        scratch_shapes=[pltpu.VMEM((tm, d), BF16), pltpu.VMEM((tm, d), BF16)],
        compiler_params=pltpu.CompilerParams(
            dimension_semantics=("arbitrary", "arbitrary"), vmem_limit_bytes=limit),
        name="swiglu_half_step",
    )(h, h, mod, mod, mod, norm_g, w_in, w_in, w_out, final_g)


def _rope(x, cos, sin_signed):
    return x * cos + pltpu.roll(x, V7X_LANES // 2, 1) * sin_signed


def _proj_kernel(h_ref, sh_ref, sc_ref, ng_ref, win_ref, aqg_ref, akg_ref, bqg_ref, bkvg_ref,
                 wuq_ref, wukv_ref, ca_ref, sa_ref, cb_ref, sb_ref,
                 qa_ref, ka_ref, va_ref, qb_ref, kb_ref, vb_ref, cu_ref, *, mod_row, dims):
    a_q, a_kv, b_q, b_kv, b_heads, c_w = dims
    row = mod_row(pl.program_id(0))
    y = _rms(h_ref[...]) * ng_ref[...]
    y = y * (1.0 + sc_ref[pl.ds(row, 1), :]) + sh_ref[pl.ds(row, 1), :]
    u = jnp.dot(y.astype(BF16), win_ref[...], preferred_element_type=F32)

    ca, sa, cb, sb = ca_ref[...], sa_ref[...], cb_ref[...], sb_ref[...]
    hd = A_HEAD_DIM
    o_ak, o_av, o_bq = a_q, a_q + a_kv, a_q + 2 * a_kv
    o_bkv, o_bkr = o_bq + b_q, o_bq + b_q + b_kv
    o_cu = o_bkr + V7X_LANES

    qg = aqg_ref[...] * (hd ** -0.5 * LOG2E)
    for h in range(a_q // hd):
        x = _rms(u[:, h * hd:(h + 1) * hd]) * qg
        qa_ref[:, h * hd:(h + 1) * hd] = _rope(x, ca, sa).astype(BF16)
    kg = akg_ref[...]
    for h in range(a_kv // hd):
        x = _rms(u[:, o_ak + h * hd:o_ak + (h + 1) * hd]) * kg
        ka_ref[:, h * hd:(h + 1) * hd] = _rope(x, ca, sa).astype(BF16)
    va_ref[...] = u[:, o_av:o_av + a_kv].astype(BF16)

    xq = (_rms(u[:, o_bq:o_bq + b_q]) * bqg_ref[...]).astype(BF16)
    q = jnp.dot(xq, wuq_ref[...], preferred_element_type=F32)
    q_scale = (B_NOPE_DIM + B_ROPE_DIM) ** -0.5 * LOG2E
    xkv = (_rms(u[:, o_bkv:o_bkv + b_kv]) * bkvg_ref[...]).astype(BF16)
    kv = jnp.dot(xkv, wukv_ref[...], preferred_element_type=F32)
    k_rope = _rope(u[:, o_bkr:o_bkr + V7X_LANES], cb, sb).astype(BF16)
    for h in range(b_heads):
        o = h * B_QK_PAD
        qb_ref[:, o:o + B_NOPE_DIM] = (q[:, o:o + B_NOPE_DIM] * q_scale).astype(BF16)
        qr = _rope(q[:, o + B_NOPE_DIM:o + B_QK_PAD], cb, sb)
        qb_ref[:, o + B_NOPE_DIM:o + B_QK_PAD] = (qr * q_scale).astype(BF16)
        kb_ref[:, o:o + B_NOPE_DIM] = kv[:, h * B_NOPE_DIM:(h + 1) * B_NOPE_DIM].astype(BF16)
        kb_ref[:, o + B_NOPE_DIM:o + B_QK_PAD] = k_rope
    vb_ref[...] = kv[:, b_heads * B_NOPE_DIM:].astype(BF16)

    cu_ref[...] = u[:, o_cu:o_cu + c_w]


def _proj(h, mod, j0, norm_g, w_in_p, a_q_g, a_k_g, b_q_g, b_kv_g, w_uq_p, w_ukv_p, tables,
          *, mod_row, table_block, dims, tm):
    n, d = h.shape
    a_q, a_kv, b_q, b_kv, b_heads, c_w = dims
    assert n % tm == 0
    ncols = w_in_p.shape[1]
    mod_spec = lambda j: pl.BlockSpec((MOD_ROWS, d), lambda i: (0, j))
    tab_spec = pl.BlockSpec((tm, V7X_LANES), lambda i: (table_block(i), 0))
    row_spec = lambda w: pl.BlockSpec((tm, w), lambda i: (i, 0))
    out_widths = [(a_q, BF16), (a_kv, BF16), (a_kv, BF16), (b_heads * B_QK_PAD, BF16),
                  (b_heads * B_QK_PAD, BF16), (b_heads * B_V_DIM, BF16), (c_w, F32)]
    limit = _vmem_limit(
        [_nbytes((tm, d), F32)] + [_nbytes((tm, w), t) for w, t in out_widths]
        + [4 * _nbytes((tm, V7X_LANES), F32)],
        [_nbytes(w_in_p.shape, BF16), _nbytes(w_uq_p.shape, BF16), _nbytes(w_ukv_p.shape, BF16)],
        _nbytes((tm, ncols), F32) + _nbytes((tm, d), F32) + 2 * _nbytes((tm, 4 * B_QK_PAD), F32))
    return pl.pallas_call(
        functools.partial(_proj_kernel, mod_row=mod_row, dims=dims),
        out_shape=[jax.ShapeDtypeStruct((n, w), t) for w, t in out_widths],
        grid=(n // tm,),
        in_specs=[
            row_spec(d), mod_spec(j0), mod_spec(j0 + 1),
            _resident((1, d)), _resident(w_in_p.shape),
            _resident((1, A_HEAD_DIM)), _resident((1, A_HEAD_DIM)),
            _resident((1, b_q)), _resident((1, b_kv)),
            _resident(w_uq_p.shape), _resident(w_ukv_p.shape),
            tab_spec, tab_spec, tab_spec, tab_spec,
        ],
        out_specs=[row_spec(w) for w, _ in out_widths],
        compiler_params=pltpu.CompilerParams(
            dimension_semantics=("parallel",), vmem_limit_bytes=limit),
        name="mixer_in_proj",
    )(h, mod, mod, norm_g, w_in_p, a_q_g, a_k_g, b_q_g, b_kv_g, w_uq_p, w_ukv_p, *tables)


def _attn_kernel(*refs, n_stack, dk, dv, chunks):
    q_ref, o_ref = refs[0], refs[-1]
    kv_refs = refs[1:-1]
    tq = q_ref.shape[0]
    q = jnp.concatenate([q_ref[:, h * dk:(h + 1) * dk] for h in range(n_stack)], axis=0)
    m_rows = n_stack * tq

    def step(k, v, carry):
        m, acc = carry
        s = lax.dot_general(q, k, (((1,), (1,)), ((), ())), preferred_element_type=F32)
        m_new = jnp.maximum(m, jnp.max(s, axis=-1, keepdims=True))
        p = jnp.exp2(s - m_new).astype(BF16)
        v_ones = jnp.concatenate([v, jnp.ones((v.shape[0], V7X_LANES), BF16)], axis=1)
        acc = jnp.exp2(m - m_new) * acc + jnp.dot(p, v_ones, preferred_element_type=F32)
        return m_new, acc

    carry = (jnp.full((m_rows, 1), -jnp.inf, F32), jnp.zeros((m_rows, dv + V7X_LANES), F32))
    for seg, chunk in enumerate(chunks):
        k_ref, v_ref = kv_refs[2 * seg], kv_refs[2 * seg + 1]
        n_chunks = k_ref.shape[0] // chunk
        for c in range(n_chunks):
            carry = step(k_ref[c * chunk:(c + 1) * chunk, :], v_ref[c * chunk:(c + 1) * chunk, :], carry)
    _, acc = carry
    out = acc[:, :dv] / acc[:, dv:dv + dv]
    for h in range(n_stack):
        o_ref[:, h * dv:(h + 1) * dv] = out[h * tq:(h + 1) * tq].astype(BF16)


def _attention(q, segments, *, batch, n_kv, n_stack, dk, dv, tq, chunk):
    nq = q.shape[0]
    lq = nq // batch
    assert lq % tq == 0 and dv == V7X_LANES
    tiles = lq // tq
    in_specs = [pl.BlockSpec((tq, n_stack * dk), lambda b, g, i: (b * tiles + i, g))]
    args, chunks, kv_bytes = [q], [], []
    for k, v in segments:
        lk = k.shape[0] // batch
        chunks.append(min(chunk, lk))
        assert lk % chunks[-1] == 0
        in_specs.append(pl.BlockSpec((lk, dk), lambda b, g, i: (b, g)))
        in_specs.append(pl.BlockSpec((lk, dv), lambda b, g, i: (b, g)))
        args += [k, v]
        kv_bytes += [_nbytes((lk, dk), BF16), _nbytes((lk, dv), BF16)]
    m_rows = n_stack * tq
    limit = _vmem_limit(
        [_nbytes((tq, n_stack * dk), BF16), _nbytes((tq, n_stack * dv), BF16)] + kv_bytes, [],
        24 * _nbytes((m_rows, max(chunks)), F32) + 4 * _nbytes((m_rows, V7X_LANES), F32))
    return pl.pallas_call(
        functools.partial(_attn_kernel, n_stack=n_stack, dk=dk, dv=dv, chunks=tuple(chunks)),
        out_shape=jax.ShapeDtypeStruct((nq, n_kv * n_stack * dv), BF16),
        grid=(batch, n_kv, tiles),
        in_specs=in_specs,
        out_specs=pl.BlockSpec((tq, n_stack * dv), lambda b, g, i: (b * tiles + i, g)),
        compiler_params=pltpu.CompilerParams(
            dimension_semantics=("parallel", "parallel", "arbitrary"), vmem_limit_bytes=limit),
        name="attention",
    )(*args)


def _out_kernel(h_ref, gt_ref, ya_ref, yb_ref, cu_ref, cup_ref, cun_ref, wp_ref, cs_ref, wo_ref,
                o_ref, ext_ref, *, mod_row, seg_len):
    i = pl.program_id(0)
    tm = h_ref.shape[0]
    halo = V7X_SUBLANES
    pos0 = lax.rem(i * tm, seg_len)
    ext_ref[0:halo, :] = jnp.where(pos0 > 0, cup_ref[...], 0.0)
    ext_ref[halo:halo + tm, :] = cu_ref[...]
    ext_ref[halo + tm:, :] = jnp.where(pos0 + tm < seg_len, cun_ref[...], 0.0)

    pos = pos0 + lax.broadcasted_iota(jnp.int32, (tm, 1), 0)
    a_w, b_w = ya_ref.shape[1], yb_ref.shape[1]
    y = jnp.dot(ya_ref[...], wo_ref[0:a_w, :], preferred_element_type=F32)
    y += jnp.dot(yb_ref[...], wo_ref[a_w:a_w + b_w, :], preferred_element_type=F32)
    gd = C_GROUP_DIM
    for g, w in enumerate(C_WINDOWS):
        cols = slice(g * gd, (g + 1) * gd)
        tot = ext_ref[halo - w // 2:halo - w // 2 + tm, cols]
        for off in range(1 - w // 2, w // 2):
            tot = tot + ext_ref[halo + off:halo + off + tm, cols]
        lo = jnp.clip(pos - w // 2, 0, seg_len)
        hi = jnp.clip(pos - w // 2 + w, 0, seg_len)
        pooled = tot / (hi - lo).astype(F32) - cu_ref[:, cols]
        yc = jnp.dot(pooled.astype(BF16), wp_ref[g], preferred_element_type=F32) * cs_ref[:, cols]
        r0 = a_w + b_w + g * gd
        y += jnp.dot(yc.astype(BF16), wo_ref[r0:r0 + gd, :], preferred_element_type=F32)
    o_ref[...] = h_ref[...] + gt_ref[pl.ds(mod_row(i), 1), :] * y


def _mix_out(h, mod, j_gate, ya, yb, cu, w_pool, c_scale, w_out, *, mod_row, seg_len, tm):
    n, d = h.shape
    c_w = cu.shape[1]
    assert n % tm == 0 and seg_len % tm == 0
    halo = V7X_SUBLANES
    per_tile = tm // halo
    last = n // halo - 1
    row_spec = lambda w: pl.BlockSpec((tm, w), lambda i: (i, 0))
    limit = _vmem_limit(
        [2 * _nbytes((tm, d), F32), _nbytes((tm, ya.shape[1]), BF16),
         _nbytes((tm, yb.shape[1]), BF16), _nbytes((tm, c_w), F32)],
        [_nbytes(w_out.shape, BF16), _nbytes(w_pool.shape, BF16), _nbytes((tm + 2 * halo, c_w), F32)],
        2 * _nbytes((tm, d), F32))
    return pl.pallas_call(
        functools.partial(_out_kernel, mod_row=mod_row, seg_len=seg_len),
        out_shape=jax.ShapeDtypeStruct((n, d), F32),
        grid=(n // tm,),
        in_specs=[
            row_spec(d),
            pl.BlockSpec((MOD_ROWS, d), lambda i: (0, j_gate)),
            row_spec(ya.shape[1]), row_spec(yb.shape[1]), row_spec(c_w),
            pl.BlockSpec((halo, c_w), lambda i: (jnp.maximum(i * per_tile - 1, 0), 0)),
            pl.BlockSpec((halo, c_w), lambda i: (jnp.minimum((i + 1) * per_tile, last), 0)),
            _resident(w_pool.shape), _resident((1, c_w)), _resident(w_out.shape),
        ],
        out_specs=row_spec(d),
        scratch_shapes=[pltpu.VMEM((tm + 2 * halo, c_w), F32)],
        compiler_params=pltpu.CompilerParams(
            dimension_semantics=("parallel",), vmem_limit_bytes=limit),
        name="mixer_out_proj",
    )(h, mod, ya, yb, cu, cu, cu, w_pool, c_scale, w_out)


def _rope_tables(seq, dim):
    n = dim // 4
    t = jnp.arange(seq)
    rows = (t // GRID_W).astype(F32)
    cols = (t % GRID_W).astype(F32)
    inv = ROPE_THETA ** (-jnp.arange(n, dtype=F32) / n)
    ang = jnp.concatenate([rows[:, None] * inv[None, :], cols[:, None] * inv[None, :]], axis=-1)
    cos, sin = jnp.cos(ang), jnp.sin(ang)
    pad = jnp.zeros((seq, V7X_LANES - dim), F32)
    return (jnp.concatenate([cos, cos, pad], axis=-1), jnp.concatenate([-sin, sin, pad], axis=-1))


def _identity_tables(rows, dim):
    one = jnp.concatenate([jnp.ones((rows, dim), F32), jnp.zeros((rows, V7X_LANES - dim), F32)], -1)
    return one, jnp.zeros((rows, V7X_LANES), F32)


def _w_in_columns(a_q, a_kv, b_q, b_kv, c_w):
    o = a_q + 2 * a_kv + b_q + b_kv
    half = B_ROPE_DIM // 2
    k1, k2 = np.arange(o, o + half), np.arange(o + half, o + B_ROPE_DIM)
    return np.concatenate([np.arange(o), k1, k2, k2, k1, np.arange(o + B_ROPE_DIM, o + B_ROPE_DIM + c_w)])


def _w_uq_columns(heads):
    hd = B_NOPE_DIM + B_ROPE_DIM
    half = B_ROPE_DIM // 2
    out = []
    for h in range(heads):
        b = h * hd
        x1 = np.arange(b + B_NOPE_DIM, b + B_NOPE_DIM + half)
        x2 = x1 + half
        out += [np.arange(b, b + B_NOPE_DIM), x1, x2, x2, x1]
    return np.concatenate(out)


def _w_ukv_columns(heads):
    hd = B_NOPE_DIM + B_V_DIM
    ks = [np.arange(h * hd, h * hd + B_NOPE_DIM) for h in range(heads)]
    vs = [np.arange(h * hd + B_NOPE_DIM, (h + 1) * hd) for h in range(heads)]
    return np.concatenate(ks + vs)


def kernel(x, c, ctx, c_ctx, w_mod, b_mod, norm_g, ffn1_in, ffn1_out, w_in, a_q_g, a_k_g, b_q_g,
           b_kv_g, b_w_uq, b_w_ukv, c_w_pool, c_scale, w_out, ffn2_in, ffn2_out, final_g):
    batch, seq, d = x.shape
    ctx_len = ctx.shape[1]
    depth = w_mod.shape[0]
    assert batch + 1 <= MOD_ROWS
    b_q, b_kv = b_q_g.shape[1], b_kv_g.shape[1]
    b_heads = b_w_ukv.shape[2] // (B_NOPE_DIM + B_V_DIM)
    c_w = c_scale.shape[1]
    a_q = w_out.shape[1] - b_heads * B_V_DIM - c_w
    a_kv = a_q // A_GROUP
    dims = (a_q, a_kv, b_q, b_kv, b_heads, c_w)

    tm, tm_ctx = 512, 256
    tiles_per_seq = seq // tm
    lat_row = lambda i: lax.div(i, tiles_per_seq)
    ctx_row = lambda i: batch
    ffn_lat = dict(tm=1024, tf=256, mod_row=lambda i: lax.div(i, seq // 1024))
    ffn_ctx = dict(tm=512, tf=512, mod_row=ctx_row)

    c_rows = jnp.concatenate([c, c_ctx[None, :], jnp.zeros((MOD_ROWS - batch - 1, d), F32)], axis=0)
    mod = _modulation(c_rows, w_mod, b_mod)

    tab_lat = _rope_tables(seq, A_HEAD_DIM) + _rope_tables(seq, B_ROPE_DIM)
    tab_ctx = _identity_tables(tm_ctx, A_HEAD_DIM) + _identity_tables(tm_ctx, B_ROPE_DIM)
    cols_in = _w_in_columns(a_q, a_kv, b_q, b_kv, c_w)
    cols_uq, cols_ukv = _w_uq_columns(b_heads), _w_ukv_columns(b_heads)

    h = x.reshape(batch * seq, d)
    hc = ctx.reshape(batch * ctx_len, d)
    for i in range(depth):
        last = i == depth - 1
        ng = norm_g[i]
        f1_in, f1_out = ffn1_in[i].astype(BF16), ffn1_out[i].astype(BF16)
        f2_in, f2_out = ffn2_in[i].astype(BF16), ffn2_out[i].astype(BF16)
        w_in_p = w_in[i][:, cols_in].astype(BF16)
        w_uq_p = b_w_uq[i][:, cols_uq].astype(BF16)
        w_ukv_p = b_w_ukv[i][:, cols_ukv].astype(BF16)
        w_o, w_pool = w_out[i].astype(BF16), c_w_pool[i].astype(BF16)
        gains = (a_q_g[i][None], a_k_g[i][None], b_q_g[i][None], b_kv_g[i][None])
        fg = final_g[None]

        ffn = functools.partial(_ffn, final_g=fg)
        h = ffn(h, mod[i], 0, ng[0:1], f1_in, f1_out, final_norm=False, **ffn_lat)
        hc = ffn(hc, mod[i], 0, ng[0:1], f1_in, f1_out, final_norm=False, **ffn_ctx)

        proj = functools.partial(_proj, norm_g=ng[1:2], w_in_p=w_in_p, a_q_g=gains[0], a_k_g=gains[1],
                                 b_q_g=gains[2], b_kv_g=gains[3], w_uq_p=w_uq_p, w_ukv_p=w_ukv_p,
                                 dims=dims)
        qa, ka, va, qb, kb, vb, cu = proj(h, mod[i], 3, tables=tab_lat, mod_row=lat_row,
                                          table_block=lambda t: lax.rem(t, tiles_per_seq), tm=tm)
        qac, kac, vac, qbc, kbc, vbc, cuc = proj(hc, mod[i], 3, tables=tab_ctx, mod_row=ctx_row,
                                                 table_block=lambda t: 0, tm=tm_ctx)

        attn_a = functools.partial(_attention, batch=batch, n_kv=a_kv // A_HEAD_DIM, n_stack=A_GROUP,
                                   dk=A_HEAD_DIM, dv=A_HEAD_DIM, tq=256, chunk=256)
        attn_b = functools.partial(_attention, batch=batch, n_kv=b_heads, n_stack=1,
                                   dk=B_QK_PAD, dv=B_V_DIM, tq=1024, chunk=256)
        ya = attn_a(qa, [(kac, vac), (ka, va)])
        yb = attn_b(qb, [(kbc, vbc), (kb, vb)])
        mix = functools.partial(_mix_out, w_pool=w_pool, c_scale=c_scale[i][None], w_out=w_o)
        h = mix(h, mod[i], 5, ya, yb, cu, mod_row=lat_row, seg_len=seq, tm=tm)
        h = ffn(h, mod[i], 6, ng[2:3], f2_in, f2_out, final_norm=last, **ffn_lat)
        if not last:
            yac = attn_a(qac, [(kac, vac)], tq=ctx_len)
            ybc = attn_b(qbc, [(kbc, vbc)], tq=ctx_len)
            hc = mix(hc, mod[i], 5, yac, ybc, cuc, mod_row=ctx_row, seg_len=ctx_len, tm=tm_ctx)
            hc = ffn(hc, mod[i], 6, ng[2:3], f2_in, f2_out, final_norm=False, **ffn_ctx)
    return h.reshape(batch, seq, d)
```

```python
import functools
import math

import jax
import jax.numpy as jnp
import numpy as np
from jax import lax
from jax.experimental import pallas as pl
from jax.experimental.pallas import tpu as pltpu

F32 = jnp.float32
BF16 = jnp.bfloat16

GRID_W = 64
ROPE_THETA = 10000.0
NORM_EPS = 1e-6
N_MOD = 9
A_HEAD_DIM = 128
A_GROUP = 4
B_NOPE_DIM = 128
B_ROPE_DIM = 64
B_V_DIM = 128
B_QK_PAD = 256
C_WINDOWS = (2, 4, 8, 16)
C_GROUP_DIM = 128
LOG2E = 1.4426950408889634

V7X_LANES = 128
V7X_SUBLANES = 8
V7X_VMEM_BYTES = 64 * 1024 * 1024
V7X_VMEM_RESERVE_BYTES = 8 * 1024 * 1024
MOD_ROWS = V7X_SUBLANES


def _nbytes(shape, dtype):
    return int(np.prod(shape)) * jnp.dtype(dtype).itemsize


def _vmem_limit(pipelined, resident, temps):
    need = 2 * sum(pipelined) + sum(resident) + temps
    return int(min(V7X_VMEM_BYTES - V7X_VMEM_RESERVE_BYTES, need + need // 4))


def _rms(x, eps=NORM_EPS):
    return x * lax.rsqrt(jnp.mean(x * x, axis=-1, keepdims=True) + eps)


def _resident(shape):
    nd = len(shape)
    return pl.BlockSpec(shape, lambda *_: (0,) * nd, pipeline_mode=pl.Buffered(1))


def _mod_kernel(c_ref, w_ref, b_ref, o_ref):
    c = c_ref[...]
    act = (c * jax.nn.sigmoid(c)).astype(BF16)
    o_ref[...] = jnp.dot(act, w_ref[...].astype(BF16), preferred_element_type=F32) + b_ref[...]


def _modulation(c_rows, w_mod, b_mod, *, tn=1024):
    depth, d, nm = w_mod.shape
    assert nm % tn == 0
    limit = _vmem_limit([_nbytes((d, tn), F32), _nbytes((MOD_ROWS, tn), F32)],
                        [_nbytes((MOD_ROWS, d), F32)], _nbytes((d, tn), BF16))
    return pl.pallas_call(
        _mod_kernel,
        out_shape=jax.ShapeDtypeStruct((depth, MOD_ROWS, nm), F32),
        grid=(depth, nm // tn),
        in_specs=[
            pl.BlockSpec((MOD_ROWS, d), lambda l, j: (0, 0)),
            pl.BlockSpec((None, d, tn), lambda l, j: (l, 0, j)),
            pl.BlockSpec((None, 1, tn), lambda l, j: (l, 0, j)),
        ],
        out_specs=pl.BlockSpec((None, MOD_ROWS, tn), lambda l, j: (l, 0, j)),
        compiler_params=pltpu.CompilerParams(
            dimension_semantics=("parallel", "parallel"), vmem_limit_bytes=limit),
        name="modulation",
    )(c_rows, w_mod, b_mod.reshape(depth, 1, nm))


def _ffn_kernel(hn_ref, sh_ref, sc_ref, gt_ref, ng_ref, wa_ref, wb_ref, wo_ref, fg_ref,
                o_ref, xn0_ref, xn1_ref, hs_ref, *, mod_row, final_norm, n_slices):
    r, k = pl.program_id(0), pl.program_id(1)
    n_tiles, nk = pl.num_programs(0) - 1, pl.num_programs(1)
    slice_rows = hn_ref.shape[0]

    def stage_next_slice(nxt_ref):
        s = jnp.clip(k - 1, 0, n_slices - 1)
        rows = pl.ds(pl.multiple_of(s * slice_rows, 16), slice_rows)
        row = mod_row(jnp.minimum(r, n_tiles - 1))
        x = hn_ref[...]
        hs_ref[rows, :] = x
        y = _rms(x) * ng_ref[...]
        y = y * (1.0 + sc_ref[pl.ds(row, 1), :]) + sh_ref[pl.ds(row, 1), :]
        nxt_ref[rows, :] = y.astype(BF16)

    def step(first, cur_ref, nxt_ref):
        if not first:
            stage_next_slice(nxt_ref)
        xn = cur_ref[...]
        a = jnp.dot(xn, wa_ref[...], preferred_element_type=F32)
        b = jnp.dot(xn, wb_ref[...], preferred_element_type=F32)
        g = (a * jax.nn.sigmoid(a) * b).astype(BF16)
        part = jnp.dot(g, wo_ref[...], preferred_element_type=F32)
        part = part * (0.5 * gt_ref[pl.ds(mod_row(r - 1), 1), :])
        o_ref[...] = (hs_ref[...] if first else o_ref[...]) + part

    pl.when((r == 0) & (k > 0))(functools.partial(stage_next_slice, xn0_ref))
    for first in (True, False):
        for par, (nxt_ref, cur_ref) in enumerate(((xn0_ref, xn1_ref), (xn1_ref, xn0_ref))):
            cond = ((k == 0) if first else (k > 0)) & (r > 0) & (lax.rem(r, 2) == par)
            pl.when(cond)(functools.partial(step, first, cur_ref, nxt_ref))

    if final_norm:
        @pl.when((k == nk - 1) & (r > 0))
        def _():
            o_ref[...] = _rms(o_ref[...]) * fg_ref[...]


def _ffn(h, mod, j0, norm_g, w_in, w_out, final_g, *, mod_row, final_norm, tm, tf, slice_rows):
    n, d = h.shape
    dff = w_out.shape[0]
    assert n % tm == 0 and dff % tf == 0 and tm % slice_rows == 0
    nk, n_tiles, n_slices = dff // tf, n // tm, tm // slice_rows
    assert n_slices <= nk - 1
    mod_spec = lambda j: pl.BlockSpec((MOD_ROWS, d), lambda r, k: (0, j))
    limit = _vmem_limit(
        [_nbytes((tm, d), F32), _nbytes((slice_rows, d), F32),
         _nbytes((d, tf), BF16) * 2, _nbytes((tf, d), BF16)],
        [2 * _nbytes((tm, d), BF16), _nbytes((tm, d), F32)],
        4 * _nbytes((tm, tf), F32) + _nbytes((tm, d), F32))

    def next_slice(r, k):
        tile = jnp.minimum(r, n_tiles - 1)
        return (tile * n_slices + jnp.clip(k - 1, 0, n_slices - 1), 0)

    chunk = lambda r, k: jnp.where(r == 0, 0, k)
    return pl.pallas_call(
        functools.partial(_ffn_kernel, mod_row=mod_row, final_norm=final_norm, n_slices=n_slices),
        out_shape=jax.ShapeDtypeStruct((n, d), F32),
        grid=(n_tiles + 1, nk),
        in_specs=[
            pl.BlockSpec((slice_rows, d), next_slice),
            mod_spec(j0), mod_spec(j0 + 1), mod_spec(j0 + 2),
            pl.BlockSpec((1, d), lambda r, k: (0, 0)),
            pl.BlockSpec((d, tf), lambda r, k: (0, chunk(r, k))),
            pl.BlockSpec((d, tf), lambda r, k: (0, chunk(r, k) + nk)),
            pl.BlockSpec((tf, d), lambda r, k: (chunk(r, k), 0)),
            pl.BlockSpec((1, d), lambda r, k: (0, 0)),
        ],
        out_specs=pl.BlockSpec((tm, d), lambda r, k: (jnp.maximum(r - 1, 0), 0)),
        scratch_shapes=[pltpu.VMEM((tm, d), BF16), pltpu.VMEM((tm, d), BF16), pltpu.VMEM((tm, d), F32)],
        compiler_params=pltpu.CompilerParams(
            dimension_semantics=("arbitrary", "arbitrary"), vmem_limit_bytes=limit),
        name="swiglu_half_step",
    )(h, mod, mod, mod, norm_g, w_in, w_in, w_out, final_g)


def _rope(x, cos, sin_signed):
    return x * cos + pltpu.roll(x, V7X_LANES // 2, 1) * sin_signed


def _proj_kernel(h_ref, sh_ref, sc_ref, ng_ref, win_ref, aqg_ref, akg_ref, bqg_ref, bkvg_ref,
                 wuq_ref, wukv_ref, ca_ref, sa_ref, cb_ref, sb_ref,
                 qa_ref, ka_ref, va_ref, qb_ref, kb_ref, vb_ref, cu_ref, *, mod_row, dims):
    a_q, a_kv, b_q, b_kv, b_heads, c_w = dims
    row = mod_row(pl.program_id(0))
    y = _rms(h_ref[...]) * ng_ref[...]
    y = y * (1.0 + sc_ref[pl.ds(row, 1), :]) + sh_ref[pl.ds(row, 1), :]
    u = jnp.dot(y.astype(BF16), win_ref[...], preferred_element_type=F32)

    ca, sa, cb, sb = ca_ref[...], sa_ref[...], cb_ref[...], sb_ref[...]
    hd = A_HEAD_DIM
    o_ak, o_av, o_bq = a_q, a_q + a_kv, a_q + 2 * a_kv
    o_bkv, o_bkr = o_bq + b_q, o_bq + b_q + b_kv
    o_cu = o_bkr + V7X_LANES

    qg = aqg_ref[...] * (hd ** -0.5 * LOG2E)
    for h in range(a_q // hd):
        x = _rms(u[:, h * hd:(h + 1) * hd]) * qg
        qa_ref[:, h * hd:(h + 1) * hd] = _rope(x, ca, sa).astype(BF16)
    kg = akg_ref[...]
    for h in range(a_kv // hd):
        x = _rms(u[:, o_ak + h * hd:o_ak + (h + 1) * hd]) * kg
        ka_ref[:, h * hd:(h + 1) * hd] = _rope(x, ca, sa).astype(BF16)
    va_ref[...] = u[:, o_av:o_av + a_kv].astype(BF16)

    xq = (_rms(u[:, o_bq:o_bq + b_q]) * bqg_ref[...]).astype(BF16)
    q = jnp.dot(xq, wuq_ref[...], preferred_element_type=F32)
    q_scale = (B_NOPE_DIM + B_ROPE_DIM) ** -0.5 * LOG2E
    xkv = (_rms(u[:, o_bkv:o_bkv + b_kv]) * bkvg_ref[...]).astype(BF16)
    kv = jnp.dot(xkv, wukv_ref[...], preferred_element_type=F32)
    k_rope = _rope(u[:, o_bkr:o_bkr + V7X_LANES], cb, sb).astype(BF16)
    for h in range(b_heads):
        o = h * B_QK_PAD
        qb_ref[:, o:o + B_NOPE_DIM] = (q[:, o:o + B_NOPE_DIM] * q_scale).astype(BF16)
        qr = _rope(q[:, o + B_NOPE_DIM:o + B_QK_PAD], cb, sb)
        qb_ref[:, o + B_NOPE_DIM:o + B_QK_PAD] = (qr * q_scale).astype(BF16)
        kb_ref[:, o:o + B_NOPE_DIM] = kv[:, h * B_NOPE_DIM:(h + 1) * B_NOPE_DIM].astype(BF16)
        kb_ref[:, o + B_NOPE_DIM:o + B_QK_PAD] = k_rope
    vb_ref[...] = kv[:, b_heads * B_NOPE_DIM:].astype(BF16)

    cu_ref[...] = u[:, o_cu:o_cu + c_w]


def _proj(h, mod, j0, norm_g, w_in_p, a_q_g, a_k_g, b_q_g, b_kv_g, w_uq_p, w_ukv_p, tables,
          *, mod_row, table_block, dims, tm):
    n, d = h.shape
    a_q, a_kv, b_q, b_kv, b_heads, c_w = dims
    assert n % tm == 0
    ncols = w_in_p.shape[1]
    mod_spec = lambda j: pl.BlockSpec((MOD_ROWS, d), lambda i: (0, j))
    tab_spec = pl.BlockSpec((tm, V7X_LANES), lambda i: (table_block(i), 0))
    row_spec = lambda w: pl.BlockSpec((tm, w), lambda i: (i, 0))
    out_widths = [(a_q, BF16), (a_kv, BF16), (a_kv, BF16), (b_heads * B_QK_PAD, BF16),
                  (b_heads * B_QK_PAD, BF16), (b_heads * B_V_DIM, BF16), (c_w, F32)]
    limit = _vmem_limit(
        [_nbytes((tm, d), F32)] + [_nbytes((tm, w), t) for w, t in out_widths]
        + [4 * _nbytes((tm, V7X_LANES), F32)],
        [_nbytes(w_in_p.shape, BF16), _nbytes(w_uq_p.shape, BF16), _nbytes(w_ukv_p.shape, BF16)],
        _nbytes((tm, ncols), F32) + _nbytes((tm, d), F32) + 2 * _nbytes((tm, 4 * B_QK_PAD), F32))
    return pl.pallas_call(
        functools.partial(_proj_kernel, mod_row=mod_row, dims=dims),
        out_shape=[jax.ShapeDtypeStruct((n, w), t) for w, t in out_widths],
        grid=(n // tm,),
        in_specs=[
            row_spec(d), mod_spec(j0), mod_spec(j0 + 1),
            _resident((1, d)), _resident(w_in_p.shape),
            _resident((1, A_HEAD_DIM)), _resident((1, A_HEAD_DIM)),
            _resident((1, b_q)), _resident((1, b_kv)),
            _resident(w_uq_p.shape), _resident(w_ukv_p.shape),
            tab_spec, tab_spec, tab_spec, tab_spec,
        ],
        out_specs=[row_spec(w) for w, _ in out_widths],
        compiler_params=pltpu.CompilerParams(
            dimension_semantics=("parallel",), vmem_limit_bytes=limit),
        name="mixer_in_proj",
    )(h, mod, mod, norm_g, w_in_p, a_q_g, a_k_g, b_q_g, b_kv_g, w_uq_p, w_ukv_p, *tables)


def _attn_kernel(*refs, n_stack, dk, dv, chunks):
    q_ref, o_ref = refs[0], refs[-1]
    kv_refs = refs[1:-1]
    tq = q_ref.shape[0]
    q = jnp.concatenate([q_ref[:, h * dk:(h + 1) * dk] for h in range(n_stack)], axis=0)
    m_rows = n_stack * tq

    def step(k, v, carry):
        m, acc = carry
        s = lax.dot_general(q, k, (((1,), (1,)), ((), ())), preferred_element_type=F32)
        m_new = jnp.maximum(m, jnp.max(s, axis=-1, keepdims=True))
        p = jnp.exp2(s - m_new).astype(BF16)
        v_ones = jnp.concatenate([v, jnp.ones((v.shape[0], V7X_LANES), BF16)], axis=1)
        acc = jnp.exp2(m - m_new) * acc + jnp.dot(p, v_ones, preferred_element_type=F32)
        return m_new, acc

    carry = (jnp.full((m_rows, 1), -jnp.inf, F32), jnp.zeros((m_rows, dv + V7X_LANES), F32))
    for seg, chunk in enumerate(chunks):
        k_ref, v_ref = kv_refs[2 * seg], kv_refs[2 * seg + 1]
        n_chunks = k_ref.shape[0] // chunk
        for c in range(n_chunks):
            carry = step(k_ref[c * chunk:(c + 1) * chunk, :], v_ref[c * chunk:(c + 1) * chunk, :], carry)
    _, acc = carry
    out = acc[:, :dv] / acc[:, dv:dv + dv]
    for h in range(n_stack):
        o_ref[:, h * dv:(h + 1) * dv] = out[h * tq:(h + 1) * tq].astype(BF16)


def _attention(q, segments, *, batch, n_kv, n_stack, dk, dv, tq, chunk):
    nq = q.shape[0]
    lq = nq // batch
    assert lq % tq == 0 and dv == V7X_LANES
    tiles = lq // tq
    in_specs = [pl.BlockSpec((tq, n_stack * dk), lambda b, g, i: (b * tiles + i, g))]
    args, chunks, kv_bytes = [q], [], []
    for k, v in segments:
        lk = k.shape[0] // batch
        chunks.append(min(chunk, lk))
        assert lk % chunks[-1] == 0
        in_specs.append(pl.BlockSpec((lk, dk), lambda b, g, i: (b, g)))
        in_specs.append(pl.BlockSpec((lk, dv), lambda b, g, i: (b, g)))
        args += [k, v]
        kv_bytes += [_nbytes((lk, dk), BF16), _nbytes((lk, dv), BF16)]
    m_rows = n_stack * tq
    limit = _vmem_limit(
        [_nbytes((tq, n_stack * dk), BF16), _nbytes((tq, n_stack * dv), BF16)] + kv_bytes, [],
        24 * _nbytes((m_rows, max(chunks)), F32) + 4 * _nbytes((m_rows, V7X_LANES), F32))
    return pl.pallas_call(
        functools.partial(_attn_kernel, n_stack=n_stack, dk=dk, dv=dv, chunks=tuple(chunks)),
        out_shape=jax.ShapeDtypeStruct((nq, n_kv * n_stack * dv), BF16),
        grid=(batch, n_kv, tiles),
        in_specs=in_specs,
        out_specs=pl.BlockSpec((tq, n_stack * dv), lambda b, g, i: (b * tiles + i, g)),
        compiler_params=pltpu.CompilerParams(
            dimension_semantics=("parallel", "parallel", "arbitrary"), vmem_limit_bytes=limit),
        name="attention",
    )(*args)


def _out_kernel(h_ref, gt_ref, ya_ref, yb_ref, cu_ref, cup_ref, cun_ref, wp_ref, cs_ref, wo_ref,
                o_ref, ext_ref, *, mod_row, seg_len):
    i = pl.program_id(0)
    tm = h_ref.shape[0]
    halo = V7X_SUBLANES
    pos0 = lax.rem(i * tm, seg_len)
    ext_ref[0:halo, :] = jnp.where(pos0 > 0, cup_ref[...], 0.0)
    ext_ref[halo:halo + tm, :] = cu_ref[...]
    ext_ref[halo + tm:, :] = jnp.where(pos0 + tm < seg_len, cun_ref[...], 0.0)

    pos = pos0 + lax.broadcasted_iota(jnp.int32, (tm, 1), 0)
    a_w, b_w = ya_ref.shape[1], yb_ref.shape[1]
    y = jnp.dot(ya_ref[...], wo_ref[0:a_w, :], preferred_element_type=F32)
    y += jnp.dot(yb_ref[...], wo_ref[a_w:a_w + b_w, :], preferred_element_type=F32)
    gd = C_GROUP_DIM
    for g, w in enumerate(C_WINDOWS):
        cols = slice(g * gd, (g + 1) * gd)
        tot = ext_ref[halo - w // 2:halo - w // 2 + tm, cols]
        for off in range(1 - w // 2, w // 2):
            tot = tot + ext_ref[halo + off:halo + off + tm, cols]
        lo = jnp.clip(pos - w // 2, 0, seg_len)
        hi = jnp.clip(pos - w // 2 + w, 0, seg_len)
        pooled = tot / (hi - lo).astype(F32) - cu_ref[:, cols]
        yc = jnp.dot(pooled.astype(BF16), wp_ref[g], preferred_element_type=F32) * cs_ref[:, cols]
        r0 = a_w + b_w + g * gd
        y += jnp.dot(yc.astype(BF16), wo_ref[r0:r0 + gd, :], preferred_element_type=F32)
    o_ref[...] = h_ref[...] + gt_ref[pl.ds(mod_row(i), 1), :] * y


def _mix_out(h, mod, j_gate, ya, yb, cu, w_pool, c_scale, w_out, *, mod_row, seg_len, tm):
    n, d = h.shape
    c_w = cu.shape[1]
    assert n % tm == 0 and seg_len % tm == 0
    halo = V7X_SUBLANES
    per_tile = tm // halo
    last = n // halo - 1
    row_spec = lambda w: pl.BlockSpec((tm, w), lambda i: (i, 0))
    limit = _vmem_limit(
        [2 * _nbytes((tm, d), F32), _nbytes((tm, ya.shape[1]), BF16),
         _nbytes((tm, yb.shape[1]), BF16), _nbytes((tm, c_w), F32)],
        [_nbytes(w_out.shape, BF16), _nbytes(w_pool.shape, BF16), _nbytes((tm + 2 * halo, c_w), F32)],
        2 * _nbytes((tm, d), F32))
    return pl.pallas_call(
        functools.partial(_out_kernel, mod_row=mod_row, seg_len=seg_len),
        out_shape=jax.ShapeDtypeStruct((n, d), F32),
        grid=(n // tm,),
        in_specs=[
            row_spec(d),
            pl.BlockSpec((MOD_ROWS, d), lambda i: (0, j_gate)),
            row_spec(ya.shape[1]), row_spec(yb.shape[1]), row_spec(c_w),
            pl.BlockSpec((halo, c_w), lambda i: (jnp.maximum(i * per_tile - 1, 0), 0)),
            pl.BlockSpec((halo, c_w), lambda i: (jnp.minimum((i + 1) * per_tile, last), 0)),
            _resident(w_pool.shape), _resident((1, c_w)), _resident(w_out.shape),
        ],
        out_specs=row_spec(d),
        scratch_shapes=[pltpu.VMEM((tm + 2 * halo, c_w), F32)],
        compiler_params=pltpu.CompilerParams(
            dimension_semantics=("parallel",), vmem_limit_bytes=limit),
        name="mixer_out_proj",
    )(h, mod, ya, yb, cu, cu, cu, w_pool, c_scale, w_out)


def _rope_tables(seq, dim):
    n = dim // 4
    t = jnp.arange(seq)
    rows = (t // GRID_W).astype(F32)
    cols = (t % GRID_W).astype(F32)
    inv = ROPE_THETA ** (-jnp.arange(n, dtype=F32) / n)
    ang = jnp.concatenate([rows[:, None] * inv[None, :], cols[:, None] * inv[None, :]], axis=-1)
    cos, sin = jnp.cos(ang), jnp.sin(ang)
    pad = jnp.zeros((seq, V7X_LANES - dim), F32)
    return (jnp.concatenate([cos, cos, pad], axis=-1), jnp.concatenate([-sin, sin, pad], axis=-1))


def _identity_tables(rows, dim):
    one = jnp.concatenate([jnp.ones((rows, dim), F32), jnp.zeros((rows, V7X_LANES - dim), F32)], -1)
    return one, jnp.zeros((rows, V7X_LANES), F32)


def _w_in_columns(a_q, a_kv, b_q, b_kv, c_w):
    o = a_q + 2 * a_kv + b_q + b_kv
    half = B_ROPE_DIM // 2
    k1, k2 = np.arange(o, o + half), np.arange(o + half, o + B_ROPE_DIM)
    return np.concatenate([np.arange(o), k1, k2, k2, k1, np.arange(o + B_ROPE_DIM, o + B_ROPE_DIM + c_w)])


def _w_uq_columns(heads):
    hd = B_NOPE_DIM + B_ROPE_DIM
    half = B_ROPE_DIM // 2
    out = []
    for h in range(heads):
        b = h * hd
        x1 = np.arange(b + B_NOPE_DIM, b + B_NOPE_DIM + half)
        x2 = x1 + half
        out += [np.arange(b, b + B_NOPE_DIM), x1, x2, x2, x1]
    return np.concatenate(out)


def _w_ukv_columns(heads):
    hd = B_NOPE_DIM + B_V_DIM
    ks = [np.arange(h * hd, h * hd + B_NOPE_DIM) for h in range(heads)]
    vs = [np.arange(h * hd + B_NOPE_DIM, (h + 1) * hd) for h in range(heads)]
    return np.concatenate(ks + vs)


def kernel(x, c, ctx, c_ctx, w_mod, b_mod, norm_g, ffn1_in, ffn1_out, w_in, a_q_g, a_k_g, b_q_g,
           b_kv_g, b_w_uq, b_w_ukv, c_w_pool, c_scale, w_out, ffn2_in, ffn2_out, final_g):
    batch, seq, d = x.shape
    ctx_len = ctx.shape[1]
    depth = w_mod.shape[0]
    assert batch + 1 <= MOD_ROWS
    b_q, b_kv = b_q_g.shape[1], b_kv_g.shape[1]
    b_heads = b_w_ukv.shape[2] // (B_NOPE_DIM + B_V_DIM)
    c_w = c_scale.shape[1]
    a_q = w_out.shape[1] - b_heads * B_V_DIM - c_w
    a_kv = a_q // A_GROUP
    dims = (a_q, a_kv, b_q, b_kv, b_heads, c_w)

    tm, tm_ctx = 512, 256
    tiles_per_seq = seq // tm
    lat_row = lambda i: lax.div(i, tiles_per_seq)
    ctx_row = lambda i: batch
    ffn_lat = dict(tm=1024, tf=512, slice_rows=128, mod_row=lambda t: lax.div(t, seq // 1024))
    ffn_ctx = dict(tm=512, tf=512, slice_rows=64, mod_row=ctx_row)

    c_rows = jnp.concatenate([c, c_ctx[None, :], jnp.zeros((MOD_ROWS - batch - 1, d), F32)], axis=0)
    mod = _modulation(c_rows, w_mod, b_mod)

    tab_lat = _rope_tables(seq, A_HEAD_DIM) + _rope_tables(seq, B_ROPE_DIM)
    tab_ctx = _identity_tables(tm_ctx, A_HEAD_DIM) + _identity_tables(tm_ctx, B_ROPE_DIM)
    cols_in = _w_in_columns(a_q, a_kv, b_q, b_kv, c_w)
    cols_uq, cols_ukv = _w_uq_columns(b_heads), _w_ukv_columns(b_heads)

    h = x.reshape(batch * seq, d)
    hc = ctx.reshape(batch * ctx_len, d)
    for i in range(depth):
        last = i == depth - 1
        ng = norm_g[i]
        f1_in, f1_out = ffn1_in[i].astype(BF16), ffn1_out[i].astype(BF16)
        f2_in, f2_out = ffn2_in[i].astype(BF16), ffn2_out[i].astype(BF16)
        w_in_p = w_in[i][:, cols_in].astype(BF16)
        w_uq_p = b_w_uq[i][:, cols_uq].astype(BF16)
        w_ukv_p = b_w_ukv[i][:, cols_ukv].astype(BF16)
        w_o, w_pool = w_out[i].astype(BF16), c_w_pool[i].astype(BF16)
        gains = (a_q_g[i][None], a_k_g[i][None], b_q_g[i][None], b_kv_g[i][None])
        fg = final_g[None]

        ffn = functools.partial(_ffn, final_g=fg)
        h = ffn(h, mod[i], 0, ng[0:1], f1_in, f1_out, final_norm=False, **ffn_lat)
        hc = ffn(hc, mod[i], 0, ng[0:1], f1_in, f1_out, final_norm=False, **ffn_ctx)

        proj = functools.partial(_proj, norm_g=ng[1:2], w_in_p=w_in_p, a_q_g=gains[0], a_k_g=gains[1],
                                 b_q_g=gains[2], b_kv_g=gains[3], w_uq_p=w_uq_p, w_ukv_p=w_ukv_p,
                                 dims=dims)
        qa, ka, va, qb, kb, vb, cu = proj(h, mod[i], 3, tables=tab_lat, mod_row=lat_row,
                                          table_block=lambda t: lax.rem(t, tiles_per_seq), tm=tm)
        qac, kac, vac, qbc, kbc, vbc, cuc = proj(hc, mod[i], 3, tables=tab_ctx, mod_row=ctx_row,
                                                 table_block=lambda t: 0, tm=tm_ctx)

        attn_a = functools.partial(_attention, batch=batch, n_kv=a_kv // A_HEAD_DIM, n_stack=A_GROUP,
                                   dk=A_HEAD_DIM, dv=A_HEAD_DIM, tq=256, chunk=256)
        attn_b = functools.partial(_attention, batch=batch, n_kv=b_heads, n_stack=1,
                                   dk=B_QK_PAD, dv=B_V_DIM, tq=1024, chunk=256)
        ya = attn_a(qa, [(kac, vac), (ka, va)])
        yb = attn_b(qb, [(kbc, vbc), (kb, vb)])
        mix = functools.partial(_mix_out, w_pool=w_pool, c_scale=c_scale[i][None], w_out=w_o)
        h = mix(h, mod[i], 5, ya, yb, cu, mod_row=lat_row, seg_len=seq, tm=tm)
        h = ffn(h, mod[i], 6, ng[2:3], f2_in, f2_out, final_norm=last, **ffn_lat)
        if not last:
            yac = attn_a(qac, [(kac, vac)], tq=ctx_len)
            ybc = attn_b(qbc, [(kbc, vbc)], tq=ctx_len)
            hc = mix(hc, mod[i], 5, yac, ybc, cuc, mod_row=ctx_row, seg_len=ctx_len, tm=tm_ctx)
            hc = ffn(hc, mod[i], 6, ng[2:3], f2_in, f2_out, final_norm=False, **ffn_ctx)
    return h.reshape(batch, seq, d)
```

```python
import functools
import math

import jax
import jax.numpy as jnp
import numpy as np
from jax import lax
from jax.experimental import pallas as pl
from jax.experimental.pallas import tpu as pltpu

F32 = jnp.float32
BF16 = jnp.bfloat16

GRID_W = 64
ROPE_THETA = 10000.0
NORM_EPS = 1e-6
N_MOD = 9
A_HEAD_DIM = 128
A_GROUP = 4
B_NOPE_DIM = 128
B_ROPE_DIM = 64
B_V_DIM = 128
B_QK_PAD = 256
C_WINDOWS = (2, 4, 8, 16)
C_GROUP_DIM = 128
LOG2E = 1.4426950408889634

V7X_LANES = 128
V7X_SUBLANES = 8
V7X_VMEM_BYTES = 64 * 1024 * 1024
V7X_VMEM_RESERVE_BYTES = 8 * 1024 * 1024
MOD_ROWS = V7X_SUBLANES


def _nbytes(shape, dtype):
    return int(np.prod(shape)) * jnp.dtype(dtype).itemsize


def _vmem_limit(pipelined, resident, temps):
    need = 2 * sum(pipelined) + sum(resident) + temps
    return int(min(V7X_VMEM_BYTES - V7X_VMEM_RESERVE_BYTES, need + need // 4))


def _rms(x, eps=NORM_EPS):
    return x * lax.rsqrt(jnp.mean(x * x, axis=-1, keepdims=True) + eps)


def _resident(shape):
    nd = len(shape)
    return pl.BlockSpec(shape, lambda *_: (0,) * nd, pipeline_mode=pl.Buffered(1))


CAST_BLOCK_BYTES = 8 * 1024 * 1024


def _cast_kernel(w_ref, o_ref):
    o_ref[...] = w_ref[...].astype(BF16)


def _layer_bf16(w, layer):
    _, rows, cols = w.shape
    n_blocks = next(nb for nb in range(1, rows + 1)
                    if rows % nb == 0 and (rows // nb) % 16 == 0
                    and _nbytes((rows // nb, cols), F32) <= CAST_BLOCK_BYTES)
    tr = rows // n_blocks
    limit = _vmem_limit([_nbytes((tr, cols), F32), _nbytes((tr, cols), BF16)], [], 0)
    return pl.pallas_call(
        _cast_kernel,
        out_shape=jax.ShapeDtypeStruct((rows, cols), BF16),
        grid=(n_blocks,),
        in_specs=[pl.BlockSpec((None, tr, cols), lambda j: (layer, j, 0))],
        out_specs=pl.BlockSpec((tr, cols), lambda j: (j, 0)),
        compiler_params=pltpu.CompilerParams(
            dimension_semantics=("parallel",), vmem_limit_bytes=limit),
        name="weight_to_bf16",
    )(w)


def _mod_kernel(c_ref, w_ref, b_ref, o_ref):
    c = c_ref[...]
    act = (c * jax.nn.sigmoid(c)).astype(BF16)
    o_ref[...] = jnp.dot(act, w_ref[...].astype(BF16), preferred_element_type=F32) + b_ref[...]


def _modulation(c_rows, w_mod, b_mod, *, tn=1024):
    depth, d, nm = w_mod.shape
    assert nm % tn == 0
    limit = _vmem_limit([_nbytes((d, tn), F32), _nbytes((MOD_ROWS, tn), F32)],
                        [_nbytes((MOD_ROWS, d), F32)], _nbytes((d, tn), BF16))
    return pl.pallas_call(
        _mod_kernel,
        out_shape=jax.ShapeDtypeStruct((depth, MOD_ROWS, nm), F32),
        grid=(depth, nm // tn),
        in_specs=[
            pl.BlockSpec((MOD_ROWS, d), lambda l, j: (0, 0)),
            pl.BlockSpec((None, d, tn), lambda l, j: (l, 0, j)),
            pl.BlockSpec((None, 1, tn), lambda l, j: (l, 0, j)),
        ],
        out_specs=pl.BlockSpec((None, MOD_ROWS, tn), lambda l, j: (l, 0, j)),
        compiler_params=pltpu.CompilerParams(
            dimension_semantics=("parallel", "parallel"), vmem_limit_bytes=limit),
        name="modulation",
    )(c_rows, w_mod, b_mod.reshape(depth, 1, nm))


def _ffn_kernel(hn_ref, sh_ref, sc_ref, gt_ref, ng_ref, wa_ref, wb_ref, wo_ref, fg_ref,
                o_ref, xn0_ref, xn1_ref, hs_ref, *, mod_row, final_norm, n_slices):
    r, k = pl.program_id(0), pl.program_id(1)
    n_tiles, nk = pl.num_programs(0) - 1, pl.num_programs(1)
    slice_rows = hn_ref.shape[0]

    def stage_next_slice(nxt_ref):
        s = jnp.clip(k - 1, 0, n_slices - 1)
        rows = pl.ds(pl.multiple_of(s * slice_rows, 16), slice_rows)
        row = mod_row(jnp.minimum(r, n_tiles - 1))
        x = hn_ref[...]
        hs_ref[rows, :] = x
        y = _rms(x) * ng_ref[...]
        y = y * (1.0 + sc_ref[pl.ds(row, 1), :]) + sh_ref[pl.ds(row, 1), :]
        nxt_ref[rows, :] = y.astype(BF16)

    def step(first, cur_ref, nxt_ref):
        if not first:
            stage_next_slice(nxt_ref)
        xn = cur_ref[...]
        a = jnp.dot(xn, wa_ref[...], preferred_element_type=F32)
        b = jnp.dot(xn, wb_ref[...], preferred_element_type=F32)
        g = (a * jax.nn.sigmoid(a) * b).astype(BF16)
        part = jnp.dot(g, wo_ref[...], preferred_element_type=F32)
        part = part * (0.5 * gt_ref[pl.ds(mod_row(r - 1), 1), :])
        o_ref[...] = (hs_ref[...] if first else o_ref[...]) + part

    pl.when((r == 0) & (k > 0))(functools.partial(stage_next_slice, xn0_ref))
    for first in (True, False):
        for par, (nxt_ref, cur_ref) in enumerate(((xn0_ref, xn1_ref), (xn1_ref, xn0_ref))):
            cond = ((k == 0) if first else (k > 0)) & (r > 0) & (lax.rem(r, 2) == par)
            pl.when(cond)(functools.partial(step, first, cur_ref, nxt_ref))

    if final_norm:
        @pl.when((k == nk - 1) & (r > 0))
        def _():
            o_ref[...] = _rms(o_ref[...]) * fg_ref[...]


def _ffn(h, mod, j0, norm_g, w_in, w_out, final_g, *, mod_row, final_norm, tm, tf, slice_rows):
    n, d = h.shape
    dff = w_out.shape[0]
    assert n % tm == 0 and dff % tf == 0 and tm % slice_rows == 0
    nk, n_tiles, n_slices = dff // tf, n // tm, tm // slice_rows
    assert n_slices <= nk - 1
    mod_spec = lambda j: pl.BlockSpec((MOD_ROWS, d), lambda r, k: (0, j))
    limit = _vmem_limit(
        [_nbytes((tm, d), F32), _nbytes((slice_rows, d), F32),
         _nbytes((d, tf), BF16) * 2, _nbytes((tf, d), BF16)],
        [2 * _nbytes((tm, d), BF16), _nbytes((tm, d), F32)],
        4 * _nbytes((tm, tf), F32) + _nbytes((tm, d), F32))

    def next_slice(r, k):
        tile = jnp.minimum(r, n_tiles - 1)
        return (tile * n_slices + jnp.clip(k - 1, 0, n_slices - 1), 0)

    chunk = lambda r, k: jnp.where(r == 0, 0, k)
    return pl.pallas_call(
        functools.partial(_ffn_kernel, mod_row=mod_row, final_norm=final_norm, n_slices=n_slices),
        out_shape=jax.ShapeDtypeStruct((n, d), F32),
        grid=(n_tiles + 1, nk),
        in_specs=[
            pl.BlockSpec((slice_rows, d), next_slice),
            mod_spec(j0), mod_spec(j0 + 1), mod_spec(j0 + 2),
            pl.BlockSpec((1, d), lambda r, k: (0, 0)),
            pl.BlockSpec((d, tf), lambda r, k: (0, chunk(r, k))),
            pl.BlockSpec((d, tf), lambda r, k: (0, chunk(r, k) + nk)),
            pl.BlockSpec((tf, d), lambda r, k: (chunk(r, k), 0)),
            pl.BlockSpec((1, d), lambda r, k: (0, 0)),
        ],
        out_specs=pl.BlockSpec((tm, d), lambda r, k: (jnp.maximum(r - 1, 0), 0)),
        scratch_shapes=[pltpu.VMEM((tm, d), BF16), pltpu.VMEM((tm, d), BF16), pltpu.VMEM((tm, d), F32)],
        compiler_params=pltpu.CompilerParams(
            dimension_semantics=("arbitrary", "arbitrary"), vmem_limit_bytes=limit),
        name="swiglu_half_step",
    )(h, mod, mod, mod, norm_g, w_in, w_in, w_out, final_g)


def _rope(x, cos, sin_signed):
    return x * cos + pltpu.roll(x, V7X_LANES // 2, 1) * sin_signed


def _proj_kernel(h_ref, sh_ref, sc_ref, ng_ref, win_ref, aqg_ref, akg_ref, bqg_ref, bkvg_ref,
                 wuq_ref, wukv_ref, ca_ref, sa_ref, cb_ref, sb_ref,
                 qa_ref, ka_ref, va_ref, qb_ref, kb_ref, vb_ref, cu_ref, *, mod_row, dims):
    a_q, a_kv, b_q, b_kv, b_heads, c_w = dims
    row = mod_row(pl.program_id(0))
    y = _rms(h_ref[...]) * ng_ref[...]
    y = y * (1.0 + sc_ref[pl.ds(row, 1), :]) + sh_ref[pl.ds(row, 1), :]
    u = jnp.dot(y.astype(BF16), win_ref[...], preferred_element_type=F32)

    ca, sa, cb, sb = ca_ref[...], sa_ref[...], cb_ref[...], sb_ref[...]
    hd = A_HEAD_DIM
    o_ak, o_av, o_bq = a_q, a_q + a_kv, a_q + 2 * a_kv
    o_bkv, o_bkr = o_bq + b_q, o_bq + b_q + b_kv
    o_cu = o_bkr + V7X_LANES

    qg = aqg_ref[...] * (hd ** -0.5 * LOG2E)
    for h in range(a_q // hd):
        x = _rms(u[:, h * hd:(h + 1) * hd]) * qg
        qa_ref[:, h * hd:(h + 1) * hd] = _rope(x, ca, sa).astype(BF16)
    kg = akg_ref[...]
    for h in range(a_kv // hd):
        x = _rms(u[:, o_ak + h * hd:o_ak + (h + 1) * hd]) * kg
        ka_ref[:, h * hd:(h + 1) * hd] = _rope(x, ca, sa).astype(BF16)
    va_ref[...] = u[:, o_av:o_av + a_kv].astype(BF16)

    xq = (_rms(u[:, o_bq:o_bq + b_q]) * bqg_ref[...]).astype(BF16)
    q = jnp.dot(xq, wuq_ref[...], preferred_element_type=F32)
    q_scale = (B_NOPE_DIM + B_ROPE_DIM) ** -0.5 * LOG2E
    xkv = (_rms(u[:, o_bkv:o_bkv + b_kv]) * bkvg_ref[...]).astype(BF16)
    kv = jnp.dot(xkv, wukv_ref[...], preferred_element_type=F32)
    k_rope = _rope(u[:, o_bkr:o_bkr + V7X_LANES], cb, sb).astype(BF16)
    for h in range(b_heads):
        o = h * B_QK_PAD
        qb_ref[:, o:o + B_NOPE_DIM] = (q[:, o:o + B_NOPE_DIM] * q_scale).astype(BF16)
        qr = _rope(q[:, o + B_NOPE_DIM:o + B_QK_PAD], cb, sb)
        qb_ref[:, o + B_NOPE_DIM:o + B_QK_PAD] = (qr * q_scale).astype(BF16)
        kb_ref[:, o:o + B_NOPE_DIM] = kv[:, h * B_NOPE_DIM:(h + 1) * B_NOPE_DIM].astype(BF16)
        kb_ref[:, o + B_NOPE_DIM:o + B_QK_PAD] = k_rope
    vb_ref[...] = kv[:, b_heads * B_NOPE_DIM:].astype(BF16)

    cu_ref[...] = u[:, o_cu:o_cu + c_w]


def _proj(h, mod, j0, norm_g, w_in_p, a_q_g, a_k_g, b_q_g, b_kv_g, w_uq_p, w_ukv_p, tables,
          *, mod_row, table_block, dims, tm):
    n, d = h.shape
    a_q, a_kv, b_q, b_kv, b_heads, c_w = dims
    assert n % tm == 0
    ncols = w_in_p.shape[1]
    mod_spec = lambda j: pl.BlockSpec((MOD_ROWS, d), lambda i: (0, j))
    tab_spec = pl.BlockSpec((tm, V7X_LANES), lambda i: (table_block(i), 0))
    row_spec = lambda w: pl.BlockSpec((tm, w), lambda i: (i, 0))
    out_widths = [(a_q, BF16), (a_kv, BF16), (a_kv, BF16), (b_heads * B_QK_PAD, BF16),
                  (b_heads * B_QK_PAD, BF16), (b_heads * B_V_DIM, BF16), (c_w, F32)]
    limit = _vmem_limit(
        [_nbytes((tm, d), F32)] + [_nbytes((tm, w), t) for w, t in out_widths]
        + [4 * _nbytes((tm, V7X_LANES), F32)],
        [_nbytes(w_in_p.shape, BF16), _nbytes(w_uq_p.shape, BF16), _nbytes(w_ukv_p.shape, BF16)],
        _nbytes((tm, ncols), F32) + _nbytes((tm, d), F32) + 2 * _nbytes((tm, 4 * B_QK_PAD), F32))
    return pl.pallas_call(
        functools.partial(_proj_kernel, mod_row=mod_row, dims=dims),
        out_shape=[jax.ShapeDtypeStruct((n, w), t) for w, t in out_widths],
        grid=(n // tm,),
        in_specs=[
            row_spec(d), mod_spec(j0), mod_spec(j0 + 1),
            _resident((1, d)), _resident(w_in_p.shape),
            _resident((1, A_HEAD_DIM)), _resident((1, A_HEAD_DIM)),
            _resident((1, b_q)), _resident((1, b_kv)),
            _resident(w_uq_p.shape), _resident(w_ukv_p.shape),
            tab_spec, tab_spec, tab_spec, tab_spec,
        ],
        out_specs=[row_spec(w) for w, _ in out_widths],
        compiler_params=pltpu.CompilerParams(
            dimension_semantics=("parallel",), vmem_limit_bytes=limit),
        name="mixer_in_proj",
    )(h, mod, mod, norm_g, w_in_p, a_q_g, a_k_g, b_q_g, b_kv_g, w_uq_p, w_ukv_p, *tables)


def _attn_kernel(*refs, n_stack, dk, dv, chunks):
    q_ref, o_ref = refs[0], refs[-1]
    kv_refs = refs[1:-1]
    tq = q_ref.shape[0]
    q = jnp.concatenate([q_ref[:, h * dk:(h + 1) * dk] for h in range(n_stack)], axis=0)
    m_rows = n_stack * tq

    def step(k, v, carry):
        m, acc = carry
        s = lax.dot_general(q, k, (((1,), (1,)), ((), ())), preferred_element_type=F32)
        m_new = jnp.maximum(m, jnp.max(s, axis=-1, keepdims=True))
        p = jnp.exp2(s - m_new).astype(BF16)
        v_ones = jnp.concatenate([v, jnp.ones((v.shape[0], V7X_LANES), BF16)], axis=1)
        acc = jnp.exp2(m - m_new) * acc + jnp.dot(p, v_ones, preferred_element_type=F32)
        return m_new, acc

    carry = (jnp.full((m_rows, 1), -jnp.inf, F32), jnp.zeros((m_rows, dv + V7X_LANES), F32))
    for seg, chunk in enumerate(chunks):
        k_ref, v_ref = kv_refs[2 * seg], kv_refs[2 * seg + 1]
        n_chunks = k_ref.shape[0] // chunk
        for c in range(n_chunks):
            carry = step(k_ref[c * chunk:(c + 1) * chunk, :], v_ref[c * chunk:(c + 1) * chunk, :], carry)
    _, acc = carry
    out = acc[:, :dv] / acc[:, dv:dv + dv]
    for h in range(n_stack):
        o_ref[:, h * dv:(h + 1) * dv] = out[h * tq:(h + 1) * tq].astype(BF16)


def _attention(q, segments, *, batch, n_kv, n_stack, dk, dv, tq, chunk):
    nq = q.shape[0]
    lq = nq // batch
    assert lq % tq == 0 and dv == V7X_LANES
    tiles = lq // tq
    in_specs = [pl.BlockSpec((tq, n_stack * dk), lambda b, g, i: (b * tiles + i, g))]
    args, chunks, kv_bytes = [q], [], []
    for k, v in segments:
        lk = k.shape[0] // batch
        chunks.append(min(chunk, lk))
        assert lk % chunks[-1] == 0
        in_specs.append(pl.BlockSpec((lk, dk), lambda b, g, i: (b, g)))
        in_specs.append(pl.BlockSpec((lk, dv), lambda b, g, i: (b, g)))
        args += [k, v]
        kv_bytes += [_nbytes((lk, dk), BF16), _nbytes((lk, dv), BF16)]
    m_rows = n_stack * tq
    limit = _vmem_limit(
        [_nbytes((tq, n_stack * dk), BF16), _nbytes((tq, n_stack * dv), BF16)] + kv_bytes, [],
        24 * _nbytes((m_rows, max(chunks)), F32) + 4 * _nbytes((m_rows, V7X_LANES), F32))
    return pl.pallas_call(
        functools.partial(_attn_kernel, n_stack=n_stack, dk=dk, dv=dv, chunks=tuple(chunks)),
        out_shape=jax.ShapeDtypeStruct((nq, n_kv * n_stack * dv), BF16),
        grid=(batch, n_kv, tiles),
        in_specs=in_specs,
        out_specs=pl.BlockSpec((tq, n_stack * dv), lambda b, g, i: (b * tiles + i, g)),
        compiler_params=pltpu.CompilerParams(
            dimension_semantics=("parallel", "parallel", "arbitrary"), vmem_limit_bytes=limit),
        name="attention",
    )(*args)


def _out_kernel(h_ref, gt_ref, ya_ref, yb_ref, cu_ref, cup_ref, cun_ref, wp_ref, cs_ref, wo_ref,
                o_ref, ext_ref, *, mod_row, seg_len):
    i = pl.program_id(0)
    tm = h_ref.shape[0]
    halo = V7X_SUBLANES
    pos0 = lax.rem(i * tm, seg_len)
    ext_ref[0:halo, :] = jnp.where(pos0 > 0, cup_ref[...], 0.0)
    ext_ref[halo:halo + tm, :] = cu_ref[...]
    ext_ref[halo + tm:, :] = jnp.where(pos0 + tm < seg_len, cun_ref[...], 0.0)

    pos = pos0 + lax.broadcasted_iota(jnp.int32, (tm, 1), 0)
    a_w, b_w = ya_ref.shape[1], yb_ref.shape[1]
    y = jnp.dot(ya_ref[...], wo_ref[0:a_w, :], preferred_element_type=F32)
    y += jnp.dot(yb_ref[...], wo_ref[a_w:a_w + b_w, :], preferred_element_type=F32)
    gd = C_GROUP_DIM
    for g, w in enumerate(C_WINDOWS):
        cols = slice(g * gd, (g + 1) * gd)
        tot = ext_ref[halo - w // 2:halo - w // 2 + tm, cols]
        for off in range(1 - w // 2, w // 2):
            tot = tot + ext_ref[halo + off:halo + off + tm, cols]
        lo = jnp.clip(pos - w // 2, 0, seg_len)
        hi = jnp.clip(pos - w // 2 + w, 0, seg_len)
        pooled = tot / (hi - lo).astype(F32) - cu_ref[:, cols]
        yc = jnp.dot(pooled.astype(BF16), wp_ref[g], preferred_element_type=F32) * cs_ref[:, cols]
        r0 = a_w + b_w + g * gd
        y += jnp.dot(yc.astype(BF16), wo_ref[r0:r0 + gd, :], preferred_element_type=F32)
    o_ref[...] = h_ref[...] + gt_ref[pl.ds(mod_row(i), 1), :] * y


def _mix_out(h, mod, j_gate, ya, yb, cu, w_pool, c_scale, w_out, *, mod_row, seg_len, tm):
    n, d = h.shape
    c_w = cu.shape[1]
    assert n % tm == 0 and seg_len % tm == 0
    halo = V7X_SUBLANES
    per_tile = tm // halo
    last = n // halo - 1
    row_spec = lambda w: pl.BlockSpec((tm, w), lambda i: (i, 0))
    limit = _vmem_limit(
        [2 * _nbytes((tm, d), F32), _nbytes((tm, ya.shape[1]), BF16),
         _nbytes((tm, yb.shape[1]), BF16), _nbytes((tm, c_w), F32)],
        [_nbytes(w_out.shape, BF16), _nbytes(w_pool.shape, BF16), _nbytes((tm + 2 * halo, c_w), F32)],
        2 * _nbytes((tm, d), F32))
    return pl.pallas_call(
        functools.partial(_out_kernel, mod_row=mod_row, seg_len=seg_len),
        out_shape=jax.ShapeDtypeStruct((n, d), F32),
        grid=(n // tm,),
        in_specs=[
            row_spec(d),
            pl.BlockSpec((MOD_ROWS, d), lambda i: (0, j_gate)),
            row_spec(ya.shape[1]), row_spec(yb.shape[1]), row_spec(c_w),
            pl.BlockSpec((halo, c_w), lambda i: (jnp.maximum(i * per_tile - 1, 0), 0)),
            pl.BlockSpec((halo, c_w), lambda i: (jnp.minimum((i + 1) * per_tile, last), 0)),
            _resident(w_pool.shape), _resident((1, c_w)), _resident(w_out.shape),
        ],
        out_specs=row_spec(d),
        scratch_shapes=[pltpu.VMEM((tm + 2 * halo, c_w), F32)],
        compiler_params=pltpu.CompilerParams(
            dimension_semantics=("parallel",), vmem_limit_bytes=limit),
        name="mixer_out_proj",
    )(h, mod, ya, yb, cu, cu, cu, w_pool, c_scale, w_out)


def _rope_tables(seq, dim):
    n = dim // 4
    t = jnp.arange(seq)
    rows = (t // GRID_W).astype(F32)
    cols = (t % GRID_W).astype(F32)
    inv = ROPE_THETA ** (-jnp.arange(n, dtype=F32) / n)
    ang = jnp.concatenate([rows[:, None] * inv[None, :], cols[:, None] * inv[None, :]], axis=-1)
    cos, sin = jnp.cos(ang), jnp.sin(ang)
    pad = jnp.zeros((seq, V7X_LANES - dim), F32)
    return (jnp.concatenate([cos, cos, pad], axis=-1), jnp.concatenate([-sin, sin, pad], axis=-1))


def _identity_tables(rows, dim):
    one = jnp.concatenate([jnp.ones((rows, dim), F32), jnp.zeros((rows, V7X_LANES - dim), F32)], -1)
    return one, jnp.zeros((rows, V7X_LANES), F32)


def _w_in_columns(a_q, a_kv, b_q, b_kv, c_w):
    o = a_q + 2 * a_kv + b_q + b_kv
    half = B_ROPE_DIM // 2
    k1, k2 = np.arange(o, o + half), np.arange(o + half, o + B_ROPE_DIM)
    return np.concatenate([np.arange(o), k1, k2, k2, k1, np.arange(o + B_ROPE_DIM, o + B_ROPE_DIM + c_w)])


def _w_uq_columns(heads):
    hd = B_NOPE_DIM + B_ROPE_DIM
    half = B_ROPE_DIM // 2
    out = []
    for h in range(heads):
        b = h * hd
        x1 = np.arange(b + B_NOPE_DIM, b + B_NOPE_DIM + half)
        x2 = x1 + half
        out += [np.arange(b, b + B_NOPE_DIM), x1, x2, x2, x1]
    return np.concatenate(out)


def _w_ukv_columns(heads):
    hd = B_NOPE_DIM + B_V_DIM
    ks = [np.arange(h * hd, h * hd + B_NOPE_DIM) for h in range(heads)]
    vs = [np.arange(h * hd + B_NOPE_DIM, (h + 1) * hd) for h in range(heads)]
    return np.concatenate(ks + vs)


def kernel(x, c, ctx, c_ctx, w_mod, b_mod, norm_g, ffn1_in, ffn1_out, w_in, a_q_g, a_k_g, b_q_g,
           b_kv_g, b_w_uq, b_w_ukv, c_w_pool, c_scale, w_out, ffn2_in, ffn2_out, final_g):
    batch, seq, d = x.shape
    ctx_len = ctx.shape[1]
    depth = w_mod.shape[0]
    assert batch + 1 <= MOD_ROWS
    b_q, b_kv = b_q_g.shape[1], b_kv_g.shape[1]
    b_heads = b_w_ukv.shape[2] // (B_NOPE_DIM + B_V_DIM)
    c_w = c_scale.shape[1]
    a_q = w_out.shape[1] - b_heads * B_V_DIM - c_w
    a_kv = a_q // A_GROUP
    dims = (a_q, a_kv, b_q, b_kv, b_heads, c_w)

    tm, tm_ctx = 512, 256
    tiles_per_seq = seq // tm
    lat_row = lambda i: lax.div(i, tiles_per_seq)
    ctx_row = lambda i: batch
    ffn_lat = dict(tm=1024, tf=512, slice_rows=128, mod_row=lambda t: lax.div(t, seq // 1024))
    ffn_ctx = dict(tm=512, tf=512, slice_rows=64, mod_row=ctx_row)

    c_rows = jnp.concatenate([c, c_ctx[None, :], jnp.zeros((MOD_ROWS - batch - 1, d), F32)], axis=0)
    mod = _modulation(c_rows, w_mod, b_mod)

    tab_lat = _rope_tables(seq, A_HEAD_DIM) + _rope_tables(seq, B_ROPE_DIM)
    tab_ctx = _identity_tables(tm_ctx, A_HEAD_DIM) + _identity_tables(tm_ctx, B_ROPE_DIM)
    cols_in = _w_in_columns(a_q, a_kv, b_q, b_kv, c_w)
    cols_uq, cols_ukv = _w_uq_columns(b_heads), _w_ukv_columns(b_heads)

    h = x.reshape(batch * seq, d)
    hc = ctx.reshape(batch * ctx_len, d)
    for i in range(depth):
        last = i == depth - 1
        ng = norm_g[i]
        f1_in, f1_out = _layer_bf16(ffn1_in, i), _layer_bf16(ffn1_out, i)
        f2_in, f2_out = _layer_bf16(ffn2_in, i), _layer_bf16(ffn2_out, i)
        w_in_p = w_in[i][:, cols_in].astype(BF16)
        w_uq_p = b_w_uq[i][:, cols_uq].astype(BF16)
        w_ukv_p = b_w_ukv[i][:, cols_ukv].astype(BF16)
        w_o, w_pool = _layer_bf16(w_out, i), c_w_pool[i].astype(BF16)
        gains = (a_q_g[i][None], a_k_g[i][None], b_q_g[i][None], b_kv_g[i][None])
        fg = final_g[None]

        ffn = functools.partial(_ffn, final_g=fg)
        h = ffn(h, mod[i], 0, ng[0:1], f1_in, f1_out, final_norm=False, **ffn_lat)
        hc = ffn(hc, mod[i], 0, ng[0:1], f1_in, f1_out, final_norm=False, **ffn_ctx)

        proj = functools.partial(_proj, norm_g=ng[1:2], w_in_p=w_in_p, a_q_g=gains[0], a_k_g=gains[1],
                                 b_q_g=gains[2], b_kv_g=gains[3], w_uq_p=w_uq_p, w_ukv_p=w_ukv_p,
                                 dims=dims)
        qa, ka, va, qb, kb, vb, cu = proj(h, mod[i], 3, tables=tab_lat, mod_row=lat_row,
                                          table_block=lambda t: lax.rem(t, tiles_per_seq), tm=tm)
        qac, kac, vac, qbc, kbc, vbc, cuc = proj(hc, mod[i], 3, tables=tab_ctx, mod_row=ctx_row,
                                                 table_block=lambda t: 0, tm=tm_ctx)

        attn_a = functools.partial(_attention, batch=batch, n_kv=a_kv // A_HEAD_DIM, n_stack=A_GROUP,
                                   dk=A_HEAD_DIM, dv=A_HEAD_DIM, tq=256, chunk=256)
        attn_b = functools.partial(_attention, batch=batch, n_kv=b_heads, n_stack=1,
                                   dk=B_QK_PAD, dv=B_V_DIM, tq=1024, chunk=256)
        ya = attn_a(qa, [(kac, vac), (ka, va)])
        yb = attn_b(qb, [(kbc, vbc), (kb, vb)])
        mix = functools.partial(_mix_out, w_pool=w_pool, c_scale=c_scale[i][None], w_out=w_o)
        h = mix(h, mod[i], 5, ya, yb, cu, mod_row=lat_row, seg_len=seq, tm=tm)
        h = ffn(h, mod[i], 6, ng[2:3], f2_in, f2_out, final_norm=last, **ffn_lat)
        if not last:
            yac = attn_a(qac, [(kac, vac)], tq=ctx_len)
            ybc = attn_b(qbc, [(kbc, vbc)], tq=ctx_len)
            hc = mix(hc, mod[i], 5, yac, ybc, cuc, mod_row=ctx_row, seg_len=ctx_len, tm=tm_ctx)
            hc = ffn(hc, mod[i], 6, ng[2:3], f2_in, f2_out, final_norm=False, **ffn_ctx)
    return h.reshape(batch, seq, d)
```

```python
import functools
import math

import jax
import jax.numpy as jnp
import numpy as np
from jax import lax
from jax.experimental import pallas as pl
from jax.experimental.pallas import tpu as pltpu

F32 = jnp.float32
BF16 = jnp.bfloat16

GRID_W = 64
ROPE_THETA = 10000.0
NORM_EPS = 1e-6
N_MOD = 9
A_HEAD_DIM = 128
A_GROUP = 4
B_NOPE_DIM = 128
B_ROPE_DIM = 64
B_V_DIM = 128
B_QK_PAD = 256
C_WINDOWS = (2, 4, 8, 16)
C_GROUP_DIM = 128
POOL_CHUNK_ROWS = 64
LOG2E = 1.4426950408889634

V7X_LANES = 128
V7X_SUBLANES = 8
V7X_VMEM_BYTES = 64 * 1024 * 1024
V7X_VMEM_RESERVE_BYTES = 8 * 1024 * 1024
MOD_ROWS = V7X_SUBLANES


def _nbytes(shape, dtype):
    return int(np.prod(shape)) * jnp.dtype(dtype).itemsize


def _vmem_limit(pipelined, resident, temps):
    need = 2 * sum(pipelined) + sum(resident) + temps
    return int(min(V7X_VMEM_BYTES - V7X_VMEM_RESERVE_BYTES, need + need // 4))


def _rms(x, eps=NORM_EPS):
    return x * lax.rsqrt(jnp.mean(x * x, axis=-1, keepdims=True) + eps)


def _resident(shape):
    nd = len(shape)
    return pl.BlockSpec(shape, lambda *_: (0,) * nd, pipeline_mode=pl.Buffered(1))


CAST_BLOCK_BYTES = 8 * 1024 * 1024


def _cast_kernel(w_ref, o_ref):
    o_ref[...] = w_ref[...].astype(BF16)


def _layer_bf16(w, layer):
    _, rows, cols = w.shape
    n_blocks = next(nb for nb in range(1, rows + 1)
                    if rows % nb == 0 and (rows // nb) % 16 == 0
                    and _nbytes((rows // nb, cols), F32) <= CAST_BLOCK_BYTES)
    tr = rows // n_blocks
    limit = _vmem_limit([_nbytes((tr, cols), F32), _nbytes((tr, cols), BF16)], [], 0)
    return pl.pallas_call(
        _cast_kernel,
        out_shape=jax.ShapeDtypeStruct((rows, cols), BF16),
        grid=(n_blocks,),
        in_specs=[pl.BlockSpec((None, tr, cols), lambda j: (layer, j, 0))],
        out_specs=pl.BlockSpec((tr, cols), lambda j: (j, 0)),
        compiler_params=pltpu.CompilerParams(
            dimension_semantics=("parallel",), vmem_limit_bytes=limit),
        name="weight_to_bf16",
    )(w)


def _mod_kernel(c_ref, w_ref, b_ref, o_ref):
    c = c_ref[...]
    act = (c * jax.nn.sigmoid(c)).astype(BF16)
    o_ref[...] = jnp.dot(act, w_ref[...].astype(BF16), preferred_element_type=F32) + b_ref[...]


def _modulation(c_rows, w_mod, b_mod, *, tn=1024):
    depth, d, nm = w_mod.shape
    assert nm % tn == 0
    limit = _vmem_limit([_nbytes((d, tn), F32), _nbytes((MOD_ROWS, tn), F32)],
                        [_nbytes((MOD_ROWS, d), F32)], _nbytes((d, tn), BF16))
    return pl.pallas_call(
        _mod_kernel,
        out_shape=jax.ShapeDtypeStruct((depth, MOD_ROWS, nm), F32),
        grid=(depth, nm // tn),
        in_specs=[
            pl.BlockSpec((MOD_ROWS, d), lambda l, j: (0, 0)),
            pl.BlockSpec((None, d, tn), lambda l, j: (l, 0, j)),
            pl.BlockSpec((None, 1, tn), lambda l, j: (l, 0, j)),
        ],
        out_specs=pl.BlockSpec((None, MOD_ROWS, tn), lambda l, j: (l, 0, j)),
        compiler_params=pltpu.CompilerParams(
            dimension_semantics=("parallel", "parallel"), vmem_limit_bytes=limit),
        name="modulation",
    )(c_rows, w_mod, b_mod.reshape(depth, 1, nm))


def _ffn_kernel(hn_ref, sh_ref, sc_ref, gt_ref, ng_ref, wa_ref, wb_ref, wo_ref, fg_ref,
                o_ref, xn0_ref, xn1_ref, hs_ref, *, mod_row, final_norm, n_slices):
    r, k = pl.program_id(0), pl.program_id(1)
    n_tiles, nk = pl.num_programs(0) - 1, pl.num_programs(1)
    slice_rows = hn_ref.shape[0]

    def stage_next_slice(nxt_ref):
        s = jnp.clip(k - 1, 0, n_slices - 1)
        rows = pl.ds(pl.multiple_of(s * slice_rows, 16), slice_rows)
        row = mod_row(jnp.minimum(r, n_tiles - 1))
        x = hn_ref[...]
        hs_ref[rows, :] = x
        y = _rms(x) * ng_ref[...]
        y = y * (1.0 + sc_ref[pl.ds(row, 1), :]) + sh_ref[pl.ds(row, 1), :]
        nxt_ref[rows, :] = y.astype(BF16)

    def step(first, cur_ref, nxt_ref):
        if not first:
            stage_next_slice(nxt_ref)
        xn = cur_ref[...]
        a = jnp.dot(xn, wa_ref[...], preferred_element_type=F32)
        b = jnp.dot(xn, wb_ref[...], preferred_element_type=F32)
        g = (a * jax.nn.sigmoid(a) * b).astype(BF16)
        part = jnp.dot(g, wo_ref[...], preferred_element_type=F32)
        part = part * (0.5 * gt_ref[pl.ds(mod_row(r - 1), 1), :])
        o_ref[...] = (hs_ref[...] if first else o_ref[...]) + part

    pl.when((r == 0) & (k > 0))(functools.partial(stage_next_slice, xn0_ref))
    for first in (True, False):
        for par, (nxt_ref, cur_ref) in enumerate(((xn0_ref, xn1_ref), (xn1_ref, xn0_ref))):
            cond = ((k == 0) if first else (k > 0)) & (r > 0) & (lax.rem(r, 2) == par)
            pl.when(cond)(functools.partial(step, first, cur_ref, nxt_ref))

    if final_norm:
        @pl.when((k == nk - 1) & (r > 0))
        def _():
            o_ref[...] = _rms(o_ref[...]) * fg_ref[...]


def _ffn(h, mod, j0, norm_g, w_in, w_out, final_g, *, mod_row, final_norm, tm, tf, slice_rows):
    n, d = h.shape
    dff = w_out.shape[0]
    assert n % tm == 0 and dff % tf == 0 and tm % slice_rows == 0
    nk, n_tiles, n_slices = dff // tf, n // tm, tm // slice_rows
    assert n_slices <= nk - 1
    mod_spec = lambda j: pl.BlockSpec((MOD_ROWS, d), lambda r, k: (0, j))
    limit = _vmem_limit(
        [_nbytes((tm, d), F32), _nbytes((slice_rows, d), F32),
         _nbytes((d, tf), BF16) * 2, _nbytes((tf, d), BF16)],
        [2 * _nbytes((tm, d), BF16), _nbytes((tm, d), F32)],
        4 * _nbytes((tm, tf), F32) + _nbytes((tm, d), F32))

    def next_slice(r, k):
        tile = jnp.minimum(r, n_tiles - 1)
        return (tile * n_slices + jnp.clip(k - 1, 0, n_slices - 1), 0)

    chunk = lambda r, k: jnp.where(r == 0, 0, k)
    return pl.pallas_call(
        functools.partial(_ffn_kernel, mod_row=mod_row, final_norm=final_norm, n_slices=n_slices),
        out_shape=jax.ShapeDtypeStruct((n, d), F32),
        grid=(n_tiles + 1, nk),
        in_specs=[
            pl.BlockSpec((slice_rows, d), next_slice),
            mod_spec(j0), mod_spec(j0 + 1), mod_spec(j0 + 2),
            pl.BlockSpec((1, d), lambda r, k: (0, 0)),
            pl.BlockSpec((d, tf), lambda r, k: (0, chunk(r, k))),
            pl.BlockSpec((d, tf), lambda r, k: (0, chunk(r, k) + nk)),
            pl.BlockSpec((tf, d), lambda r, k: (chunk(r, k), 0)),
            pl.BlockSpec((1, d), lambda r, k: (0, 0)),
        ],
        out_specs=pl.BlockSpec((tm, d), lambda r, k: (jnp.maximum(r - 1, 0), 0)),
        scratch_shapes=[pltpu.VMEM((tm, d), BF16), pltpu.VMEM((tm, d), BF16), pltpu.VMEM((tm, d), F32)],
        compiler_params=pltpu.CompilerParams(
            dimension_semantics=("arbitrary", "arbitrary"), vmem_limit_bytes=limit),
        name="swiglu_half_step",
    )(h, mod, mod, mod, norm_g, w_in, w_in, w_out, final_g)


def _rope(x, cos, sin_signed):
    return x * cos + pltpu.roll(x, V7X_LANES // 2, 1) * sin_signed


def _proj_kernel(h_ref, sh_ref, sc_ref, ng_ref, win_ref, aqg_ref, akg_ref, bqg_ref, bkvg_ref,
                 wuq_ref, wukv_ref, ca_ref, sa_ref, cb_ref, sb_ref,
                 qa_ref, ka_ref, va_ref, qb_ref, kb_ref, vb_ref, cu_ref, *, mod_row, dims):
    a_q, a_kv, b_q, b_kv, b_heads, c_w = dims
    row = mod_row(pl.program_id(0))
    y = _rms(h_ref[...]) * ng_ref[...]
    y = y * (1.0 + sc_ref[pl.ds(row, 1), :]) + sh_ref[pl.ds(row, 1), :]
    u = jnp.dot(y.astype(BF16), win_ref[...], preferred_element_type=F32)

    ca, sa, cb, sb = ca_ref[...], sa_ref[...], cb_ref[...], sb_ref[...]
    hd = A_HEAD_DIM
    o_ak, o_av, o_bq = a_q, a_q + a_kv, a_q + 2 * a_kv
    o_bkv, o_bkr = o_bq + b_q, o_bq + b_q + b_kv
    o_cu = o_bkr + V7X_LANES

    qg = aqg_ref[...] * (hd ** -0.5 * LOG2E)
    for h in range(a_q // hd):
        x = _rms(u[:, h * hd:(h + 1) * hd]) * qg
        qa_ref[:, h * hd:(h + 1) * hd] = _rope(x, ca, sa).astype(BF16)
    kg = akg_ref[...]
    for h in range(a_kv // hd):
        x = _rms(u[:, o_ak + h * hd:o_ak + (h + 1) * hd]) * kg
        ka_ref[:, h * hd:(h + 1) * hd] = _rope(x, ca, sa).astype(BF16)
    va_ref[...] = u[:, o_av:o_av + a_kv].astype(BF16)

    xq = (_rms(u[:, o_bq:o_bq + b_q]) * bqg_ref[...]).astype(BF16)
    q = jnp.dot(xq, wuq_ref[...], preferred_element_type=F32)
    q_scale = (B_NOPE_DIM + B_ROPE_DIM) ** -0.5 * LOG2E
    xkv = (_rms(u[:, o_bkv:o_bkv + b_kv]) * bkvg_ref[...]).astype(BF16)
    kv = jnp.dot(xkv, wukv_ref[...], preferred_element_type=F32)
    k_rope = _rope(u[:, o_bkr:o_bkr + V7X_LANES], cb, sb).astype(BF16)
    for h in range(b_heads):
        o = h * B_QK_PAD
        qb_ref[:, o:o + B_NOPE_DIM] = (q[:, o:o + B_NOPE_DIM] * q_scale).astype(BF16)
        qr = _rope(q[:, o + B_NOPE_DIM:o + B_QK_PAD], cb, sb)
        qb_ref[:, o + B_NOPE_DIM:o + B_QK_PAD] = (qr * q_scale).astype(BF16)
        kb_ref[:, o:o + B_NOPE_DIM] = kv[:, h * B_NOPE_DIM:(h + 1) * B_NOPE_DIM].astype(BF16)
        kb_ref[:, o + B_NOPE_DIM:o + B_QK_PAD] = k_rope
    vb_ref[...] = kv[:, b_heads * B_NOPE_DIM:].astype(BF16)

    cu_ref[...] = u[:, o_cu:o_cu + c_w]


def _proj(h, mod, j0, norm_g, w_in_p, a_q_g, a_k_g, b_q_g, b_kv_g, w_uq_p, w_ukv_p, tables,
          *, mod_row, table_block, dims, tm):
    n, d = h.shape
    a_q, a_kv, b_q, b_kv, b_heads, c_w = dims
    assert n % tm == 0
    ncols = w_in_p.shape[1]
    mod_spec = lambda j: pl.BlockSpec((MOD_ROWS, d), lambda i: (0, j))
    tab_spec = pl.BlockSpec((tm, V7X_LANES), lambda i: (table_block(i), 0))
    row_spec = lambda w: pl.BlockSpec((tm, w), lambda i: (i, 0))
    out_widths = [(a_q, BF16), (a_kv, BF16), (a_kv, BF16), (b_heads * B_QK_PAD, BF16),
                  (b_heads * B_QK_PAD, BF16), (b_heads * B_V_DIM, BF16), (c_w, F32)]
    limit = _vmem_limit(
        [_nbytes((tm, d), F32)] + [_nbytes((tm, w), t) for w, t in out_widths]
        + [4 * _nbytes((tm, V7X_LANES), F32)],
        [_nbytes(w_in_p.shape, BF16), _nbytes(w_uq_p.shape, BF16), _nbytes(w_ukv_p.shape, BF16)],
        _nbytes((tm, ncols), F32) + _nbytes((tm, d), F32) + 2 * _nbytes((tm, 4 * B_QK_PAD), F32))
    return pl.pallas_call(
        functools.partial(_proj_kernel, mod_row=mod_row, dims=dims),
        out_shape=[jax.ShapeDtypeStruct((n, w), t) for w, t in out_widths],
        grid=(n // tm,),
        in_specs=[
            row_spec(d), mod_spec(j0), mod_spec(j0 + 1),
            _resident((1, d)), _resident(w_in_p.shape),
            _resident((1, A_HEAD_DIM)), _resident((1, A_HEAD_DIM)),
            _resident((1, b_q)), _resident((1, b_kv)),
            _resident(w_uq_p.shape), _resident(w_ukv_p.shape),
            tab_spec, tab_spec, tab_spec, tab_spec,
        ],
        out_specs=[row_spec(w) for w, _ in out_widths],
        compiler_params=pltpu.CompilerParams(
            dimension_semantics=("parallel",), vmem_limit_bytes=limit),
        name="mixer_in_proj",
    )(h, mod, mod, norm_g, w_in_p, a_q_g, a_k_g, b_q_g, b_kv_g, w_uq_p, w_ukv_p, *tables)


def _attn_kernel(*refs, n_stack, dk, dv, chunks):
    q_ref, o_ref = refs[0], refs[-1]
    kv_refs = refs[1:-1]
    tq = q_ref.shape[0]
    q = jnp.concatenate([q_ref[:, h * dk:(h + 1) * dk] for h in range(n_stack)], axis=0)
    m_rows = n_stack * tq

    def step(k, v, carry):
        m, acc = carry
        s = lax.dot_general(q, k, (((1,), (1,)), ((), ())), preferred_element_type=F32)
        m_new = jnp.maximum(m, jnp.max(s, axis=-1, keepdims=True))
        p = jnp.exp2(s - m_new).astype(BF16)
        v_ones = jnp.concatenate([v, jnp.ones((v.shape[0], V7X_LANES), BF16)], axis=1)
        acc = jnp.exp2(m - m_new) * acc + jnp.dot(p, v_ones, preferred_element_type=F32)
        return m_new, acc

    carry = (jnp.full((m_rows, 1), -jnp.inf, F32), jnp.zeros((m_rows, dv + V7X_LANES), F32))
    for seg, chunk in enumerate(chunks):
        k_ref, v_ref = kv_refs[2 * seg], kv_refs[2 * seg + 1]
        n_chunks = k_ref.shape[0] // chunk
        for c in range(n_chunks):
            carry = step(k_ref[c * chunk:(c + 1) * chunk, :], v_ref[c * chunk:(c + 1) * chunk, :], carry)
    _, acc = carry
    out = acc[:, :dv] / acc[:, dv:dv + dv]
    for h in range(n_stack):
        o_ref[:, h * dv:(h + 1) * dv] = out[h * tq:(h + 1) * tq].astype(BF16)


def _attention(q, segments, *, batch, n_kv, n_stack, dk, dv, tq, chunk):
    nq = q.shape[0]
    lq = nq // batch
    assert lq % tq == 0 and dv == V7X_LANES
    tiles = lq // tq
    in_specs = [pl.BlockSpec((tq, n_stack * dk), lambda b, g, i: (b * tiles + i, g))]
    args, chunks, kv_bytes = [q], [], []
    for k, v in segments:
        lk = k.shape[0] // batch
        chunks.append(min(chunk, lk))
        assert lk % chunks[-1] == 0
        in_specs.append(pl.BlockSpec((lk, dk), lambda b, g, i: (b, g)))
        in_specs.append(pl.BlockSpec((lk, dv), lambda b, g, i: (b, g)))
        args += [k, v]
        kv_bytes += [_nbytes((lk, dk), BF16), _nbytes((lk, dv), BF16)]
    m_rows = n_stack * tq
    limit = _vmem_limit(
        [_nbytes((tq, n_stack * dk), BF16), _nbytes((tq, n_stack * dv), BF16)] + kv_bytes, [],
        24 * _nbytes((m_rows, max(chunks)), F32) + 4 * _nbytes((m_rows, V7X_LANES), F32))
    return pl.pallas_call(
        functools.partial(_attn_kernel, n_stack=n_stack, dk=dk, dv=dv, chunks=tuple(chunks)),
        out_shape=jax.ShapeDtypeStruct((nq, n_kv * n_stack * dv), BF16),
        grid=(batch, n_kv, tiles),
        in_specs=in_specs,
        out_specs=pl.BlockSpec((tq, n_stack * dv), lambda b, g, i: (b * tiles + i, g)),
        compiler_params=pltpu.CompilerParams(
            dimension_semantics=("parallel", "parallel", "arbitrary"), vmem_limit_bytes=limit),
        name="attention",
    )(*args)


def _out_kernel(h_ref, gt_ref, ya_ref, yb_ref, cu_ref, cup_ref, cun_ref, wp_ref, cs_ref, wo_ref,
                o_ref, ext_ref, pooled_ref, *, mod_row, seg_len):
    i = pl.program_id(0)
    tm = h_ref.shape[0]
    halo = V7X_SUBLANES
    pos0 = lax.rem(i * tm, seg_len)
    ext_ref[0:halo, :] = jnp.where(pos0 > 0, cup_ref[...], 0.0)
    ext_ref[halo:halo + tm, :] = cu_ref[...]
    ext_ref[halo + tm:, :] = jnp.where(pos0 + tm < seg_len, cun_ref[...], 0.0)

    pos = pos0 + lax.broadcasted_iota(jnp.int32, (tm, 1), 0)
    a_w, b_w = ya_ref.shape[1], yb_ref.shape[1]
    gd = C_GROUP_DIM
    for c0 in range(0, tm, POOL_CHUNK_ROWS):
        c1 = c0 + POOL_CHUNK_ROWS
        for g, w in enumerate(C_WINDOWS):
            cols = slice(g * gd, (g + 1) * gd)
            tot = ext_ref[halo - w // 2 + c0:halo - w // 2 + c1, cols]
            for off in range(1 - w // 2, w // 2):
                tot = tot + ext_ref[halo + off + c0:halo + off + c1, cols]
            lo = jnp.clip(pos[c0:c1] - w // 2, 0, seg_len)
            hi = jnp.clip(pos[c0:c1] - w // 2 + w, 0, seg_len)
            pooled = tot / (hi - lo).astype(F32) - cu_ref[c0:c1, cols]
            pooled_ref[c0:c1, cols] = pooled.astype(BF16)
    y = jnp.dot(ya_ref[...], wo_ref[0:a_w, :], preferred_element_type=F32)
    y += jnp.dot(yb_ref[...], wo_ref[a_w:a_w + b_w, :], preferred_element_type=F32)
    yc = []
    for g in range(len(C_WINDOWS)):
        cols = slice(g * gd, (g + 1) * gd)
        mixed = jnp.dot(pooled_ref[:, cols], wp_ref[g], preferred_element_type=F32) * cs_ref[:, cols]
        yc.append(mixed.astype(BF16))
    y += jnp.dot(jnp.concatenate(yc, axis=1), wo_ref[a_w + b_w:, :], preferred_element_type=F32)
    o_ref[...] = h_ref[...] + gt_ref[pl.ds(mod_row(i), 1), :] * y


def _mix_out(h, mod, j_gate, ya, yb, cu, w_pool, c_scale, w_out, *, mod_row, seg_len, tm):
    n, d = h.shape
    c_w = cu.shape[1]
    assert n % tm == 0 and seg_len % tm == 0
    halo = V7X_SUBLANES
    per_tile = tm // halo
    last = n // halo - 1
    row_spec = lambda w: pl.BlockSpec((tm, w), lambda i: (i, 0))
    limit = _vmem_limit(
        [2 * _nbytes((tm, d), F32), _nbytes((tm, ya.shape[1]), BF16),
         _nbytes((tm, yb.shape[1]), BF16), _nbytes((tm, c_w), F32)],
        [_nbytes(w_out.shape, BF16), _nbytes(w_pool.shape, BF16), _nbytes((tm + 2 * halo, c_w), F32)],
        2 * _nbytes((tm, d), F32))
    return pl.pallas_call(
        functools.partial(_out_kernel, mod_row=mod_row, seg_len=seg_len),
        out_shape=jax.ShapeDtypeStruct((n, d), F32),
        grid=(n // tm,),
        in_specs=[
            row_spec(d),
            pl.BlockSpec((MOD_ROWS, d), lambda i: (0, j_gate)),
            row_spec(ya.shape[1]), row_spec(yb.shape[1]), row_spec(c_w),
            pl.BlockSpec((halo, c_w), lambda i: (jnp.maximum(i * per_tile - 1, 0), 0)),
            pl.BlockSpec((halo, c_w), lambda i: (jnp.minimum((i + 1) * per_tile, last), 0)),
            _resident(w_pool.shape), _resident((1, c_w)), _resident(w_out.shape),
        ],
        out_specs=row_spec(d),
        scratch_shapes=[pltpu.VMEM((tm + 2 * halo, c_w), F32), pltpu.VMEM((tm, c_w), BF16)],
        compiler_params=pltpu.CompilerParams(
            dimension_semantics=("parallel",), vmem_limit_bytes=limit),
        name="mixer_out_proj",
    )(h, mod, ya, yb, cu, cu, cu, w_pool, c_scale, w_out)


def _rope_tables(seq, dim):
    n = dim // 4
    t = jnp.arange(seq)
    rows = (t // GRID_W).astype(F32)
    cols = (t % GRID_W).astype(F32)
    inv = ROPE_THETA ** (-jnp.arange(n, dtype=F32) / n)
    ang = jnp.concatenate([rows[:, None] * inv[None, :], cols[:, None] * inv[None, :]], axis=-1)
    cos, sin = jnp.cos(ang), jnp.sin(ang)
    pad = jnp.zeros((seq, V7X_LANES - dim), F32)
    return (jnp.concatenate([cos, cos, pad], axis=-1), jnp.concatenate([-sin, sin, pad], axis=-1))


def _identity_tables(rows, dim):
    one = jnp.concatenate([jnp.ones((rows, dim), F32), jnp.zeros((rows, V7X_LANES - dim), F32)], -1)
    return one, jnp.zeros((rows, V7X_LANES), F32)


def _w_in_columns(a_q, a_kv, b_q, b_kv, c_w):
    o = a_q + 2 * a_kv + b_q + b_kv
    half = B_ROPE_DIM // 2
    k1, k2 = np.arange(o, o + half), np.arange(o + half, o + B_ROPE_DIM)
    return np.concatenate([np.arange(o), k1, k2, k2, k1, np.arange(o + B_ROPE_DIM, o + B_ROPE_DIM + c_w)])


def _w_uq_columns(heads):
    hd = B_NOPE_DIM + B_ROPE_DIM
    half = B_ROPE_DIM // 2
    out = []
    for h in range(heads):
        b = h * hd
        x1 = np.arange(b + B_NOPE_DIM, b + B_NOPE_DIM + half)
        x2 = x1 + half
        out += [np.arange(b, b + B_NOPE_DIM), x1, x2, x2, x1]
    return np.concatenate(out)


def _w_ukv_columns(heads):
    hd = B_NOPE_DIM + B_V_DIM
    ks = [np.arange(h * hd, h * hd + B_NOPE_DIM) for h in range(heads)]
    vs = [np.arange(h * hd + B_NOPE_DIM, (h + 1) * hd) for h in range(heads)]
    return np.concatenate(ks + vs)


def kernel(x, c, ctx, c_ctx, w_mod, b_mod, norm_g, ffn1_in, ffn1_out, w_in, a_q_g, a_k_g, b_q_g,
           b_kv_g, b_w_uq, b_w_ukv, c_w_pool, c_scale, w_out, ffn2_in, ffn2_out, final_g):
    batch, seq, d = x.shape
    ctx_len = ctx.shape[1]
    depth = w_mod.shape[0]
    assert batch + 1 <= MOD_ROWS
    b_q, b_kv = b_q_g.shape[1], b_kv_g.shape[1]
    b_heads = b_w_ukv.shape[2] // (B_NOPE_DIM + B_V_DIM)
    c_w = c_scale.shape[1]
    a_q = w_out.shape[1] - b_heads * B_V_DIM - c_w
    a_kv = a_q // A_GROUP
    dims = (a_q, a_kv, b_q, b_kv, b_heads, c_w)

    tm, tm_ctx = 512, 256
    tiles_per_seq = seq // tm
    lat_row = lambda i: lax.div(i, tiles_per_seq)
    ctx_row = lambda i: batch
    ffn_lat = dict(tm=1024, tf=512, slice_rows=128, mod_row=lambda t: lax.div(t, seq // 1024))
    ffn_ctx = dict(tm=512, tf=512, slice_rows=64, mod_row=ctx_row)

    c_rows = jnp.concatenate([c, c_ctx[None, :], jnp.zeros((MOD_ROWS - batch - 1, d), F32)], axis=0)
    mod = _modulation(c_rows, w_mod, b_mod)

    tab_lat = _rope_tables(seq, A_HEAD_DIM) + _rope_tables(seq, B_ROPE_DIM)
    tab_ctx = _identity_tables(tm_ctx, A_HEAD_DIM) + _identity_tables(tm_ctx, B_ROPE_DIM)
    cols_in = _w_in_columns(a_q, a_kv, b_q, b_kv, c_w)
    cols_uq, cols_ukv = _w_uq_columns(b_heads), _w_ukv_columns(b_heads)

    h = x.reshape(batch * seq, d)
    hc = ctx.reshape(batch * ctx_len, d)
    for i in range(depth):
        last = i == depth - 1
        ng = norm_g[i]
        f1_in, f1_out = _layer_bf16(ffn1_in, i), _layer_bf16(ffn1_out, i)
        f2_in, f2_out = _layer_bf16(ffn2_in, i), _layer_bf16(ffn2_out, i)
        w_in_p = w_in[i][:, cols_in].astype(BF16)
        w_uq_p = b_w_uq[i][:, cols_uq].astype(BF16)
        w_ukv_p = b_w_ukv[i][:, cols_ukv].astype(BF16)
        w_o, w_pool = _layer_bf16(w_out, i), c_w_pool[i].astype(BF16)
        gains = (a_q_g[i][None], a_k_g[i][None], b_q_g[i][None], b_kv_g[i][None])
        fg = final_g[None]

        ffn = functools.partial(_ffn, final_g=fg)
        h = ffn(h, mod[i], 0, ng[0:1], f1_in, f1_out, final_norm=False, **ffn_lat)
        hc = ffn(hc, mod[i], 0, ng[0:1], f1_in, f1_out, final_norm=False, **ffn_ctx)

        proj = functools.partial(_proj, norm_g=ng[1:2], w_in_p=w_in_p, a_q_g=gains[0], a_k_g=gains[1],
                                 b_q_g=gains[2], b_kv_g=gains[3], w_uq_p=w_uq_p, w_ukv_p=w_ukv_p,
                                 dims=dims)
        qa, ka, va, qb, kb, vb, cu = proj(h, mod[i], 3, tables=tab_lat, mod_row=lat_row,
                                          table_block=lambda t: lax.rem(t, tiles_per_seq), tm=tm)
        qac, kac, vac, qbc, kbc, vbc, cuc = proj(hc, mod[i], 3, tables=tab_ctx, mod_row=ctx_row,
                                                 table_block=lambda t: 0, tm=tm_ctx)

        attn_a = functools.partial(_attention, batch=batch, n_kv=a_kv // A_HEAD_DIM, n_stack=A_GROUP,
                                   dk=A_HEAD_DIM, dv=A_HEAD_DIM, tq=256, chunk=256)
        attn_b = functools.partial(_attention, batch=batch, n_kv=b_heads, n_stack=1,
                                   dk=B_QK_PAD, dv=B_V_DIM, tq=1024, chunk=256)
        ya = attn_a(qa, [(kac, vac), (ka, va)])
        yb = attn_b(qb, [(kbc, vbc), (kb, vb)])
        mix = functools.partial(_mix_out, w_pool=w_pool, c_scale=c_scale[i][None], w_out=w_o)
        h = mix(h, mod[i], 5, ya, yb, cu, mod_row=lat_row, seg_len=seq, tm=tm)
        h = ffn(h, mod[i], 6, ng[2:3], f2_in, f2_out, final_norm=last, **ffn_lat)
        if not last:
            yac = attn_a(qac, [(kac, vac)], tq=ctx_len)
            ybc = attn_b(qbc, [(kbc, vbc)], tq=ctx_len)
            hc = mix(hc, mod[i], 5, yac, ybc, cuc, mod_row=ctx_row, seg_len=ctx_len, tm=tm_ctx)
            hc = ffn(hc, mod[i], 6, ng[2:3], f2_in, f2_out, final_norm=False, **ffn_ctx)
    return h.reshape(batch, seq, d)
```

```python
import functools
import math

import jax
import jax.numpy as jnp
import numpy as np
from jax import lax
from jax.experimental import pallas as pl
from jax.experimental.pallas import tpu as pltpu

F32 = jnp.float32
BF16 = jnp.bfloat16

GRID_W = 64
ROPE_THETA = 10000.0
NORM_EPS = 1e-6
N_MOD = 9
A_HEAD_DIM = 128
A_GROUP = 4
B_NOPE_DIM = 128
B_ROPE_DIM = 64
B_V_DIM = 128
B_QK_PAD = 256
C_WINDOWS = (2, 4, 8, 16)
C_GROUP_DIM = 128
POOL_CHUNK_ROWS = 64
LOG2E = 1.4426950408889634

V7X_LANES = 128
V7X_SUBLANES = 8
V7X_VMEM_BYTES = 64 * 1024 * 1024
V7X_VMEM_RESERVE_BYTES = 8 * 1024 * 1024
MOD_ROWS = V7X_SUBLANES


def _nbytes(shape, dtype):
    return int(np.prod(shape)) * jnp.dtype(dtype).itemsize


def _vmem_limit(pipelined, resident, temps):
    need = 2 * sum(pipelined) + sum(resident) + temps
    return int(min(V7X_VMEM_BYTES - V7X_VMEM_RESERVE_BYTES, need + need // 4))


def _rms(x, eps=NORM_EPS):
    return x * lax.rsqrt(jnp.mean(x * x, axis=-1, keepdims=True) + eps)


def _resident(shape):
    nd = len(shape)
    return pl.BlockSpec(shape, lambda *_: (0,) * nd, pipeline_mode=pl.Buffered(1))


CAST_BLOCK_BYTES = 8 * 1024 * 1024


def _cast_kernel(w_ref, o_ref):
    o_ref[...] = w_ref[...].astype(BF16)


def _layer_bf16(w, layer):
    _, rows, cols = w.shape
    n_blocks = next(nb for nb in range(1, rows + 1)
                    if rows % nb == 0 and (rows // nb) % 16 == 0
                    and _nbytes((rows // nb, cols), F32) <= CAST_BLOCK_BYTES)
    tr = rows // n_blocks
    limit = _vmem_limit([_nbytes((tr, cols), F32), _nbytes((tr, cols), BF16)], [], 0)
    return pl.pallas_call(
        _cast_kernel,
        out_shape=jax.ShapeDtypeStruct((rows, cols), BF16),
        grid=(n_blocks,),
        in_specs=[pl.BlockSpec((None, tr, cols), lambda j: (layer, j, 0))],
        out_specs=pl.BlockSpec((tr, cols), lambda j: (j, 0)),
        compiler_params=pltpu.CompilerParams(
            dimension_semantics=("parallel",), vmem_limit_bytes=limit),
        name="weight_to_bf16",
    )(w)


def _mod_kernel(c_ref, w_ref, b_ref, o_ref):
    c = c_ref[...]
    act = (c * jax.nn.sigmoid(c)).astype(BF16)
    o_ref[...] = jnp.dot(act, w_ref[...].astype(BF16), preferred_element_type=F32) + b_ref[...]


def _modulation(c_rows, w_mod, b_mod, *, tn=1024):
    depth, d, nm = w_mod.shape
    assert nm % tn == 0
    limit = _vmem_limit([_nbytes((d, tn), F32), _nbytes((MOD_ROWS, tn), F32)],
                        [_nbytes((MOD_ROWS, d), F32)], _nbytes((d, tn), BF16))
    return pl.pallas_call(
        _mod_kernel,
        out_shape=jax.ShapeDtypeStruct((depth, MOD_ROWS, nm), F32),
        grid=(depth, nm // tn),
        in_specs=[
            pl.BlockSpec((MOD_ROWS, d), lambda l, j: (0, 0)),
            pl.BlockSpec((None, d, tn), lambda l, j: (l, 0, j)),
            pl.BlockSpec((None, 1, tn), lambda l, j: (l, 0, j)),
        ],
        out_specs=pl.BlockSpec((None, MOD_ROWS, tn), lambda l, j: (l, 0, j)),
        compiler_params=pltpu.CompilerParams(
            dimension_semantics=("parallel", "parallel"), vmem_limit_bytes=limit),
        name="modulation",
    )(c_rows, w_mod, b_mod.reshape(depth, 1, nm))


def _ffn_kernel(hn_ref, sh_ref, sc_ref, gt_ref, ng_ref, wa_ref, wb_ref, wo_ref, fg_ref,
                o_ref, xn0_ref, xn1_ref, hs_ref, *, mod_row, final_norm, n_slices):
    r, k = pl.program_id(0), pl.program_id(1)
    n_tiles, nk = pl.num_programs(0) - 1, pl.num_programs(1)
    slice_rows = hn_ref.shape[0]

    def stage_next_slice(nxt_ref):
        s = jnp.clip(k - 1, 0, n_slices - 1)
        rows = pl.ds(pl.multiple_of(s * slice_rows, 16), slice_rows)
        row = mod_row(jnp.minimum(r, n_tiles - 1))
        x = hn_ref[...]
        hs_ref[rows, :] = x
        y = _rms(x) * ng_ref[...]
        y = y * (1.0 + sc_ref[pl.ds(row, 1), :]) + sh_ref[pl.ds(row, 1), :]
        nxt_ref[rows, :] = y.astype(BF16)

    def step(first, cur_ref, nxt_ref):
        if not first:
            stage_next_slice(nxt_ref)
        xn = cur_ref[...]
        a = jnp.dot(xn, wa_ref[...], preferred_element_type=F32)
        b = jnp.dot(xn, wb_ref[...], preferred_element_type=F32)
        g = (a * jax.nn.sigmoid(a) * b).astype(BF16)
        part = jnp.dot(g, wo_ref[...], preferred_element_type=F32)
        part = part * (0.5 * gt_ref[pl.ds(mod_row(r - 1), 1), :])
        o_ref[...] = (hs_ref[...] if first else o_ref[...]) + part

    pl.when((r == 0) & (k > 0))(functools.partial(stage_next_slice, xn0_ref))
    for first in (True, False):
        for par, (nxt_ref, cur_ref) in enumerate(((xn0_ref, xn1_ref), (xn1_ref, xn0_ref))):
            cond = ((k == 0) if first else (k > 0)) & (r > 0) & (lax.rem(r, 2) == par)
            pl.when(cond)(functools.partial(step, first, cur_ref, nxt_ref))

    if final_norm:
        @pl.when((k == nk - 1) & (r > 0))
        def _():
            o_ref[...] = _rms(o_ref[...]) * fg_ref[...]


def _ffn(h, mod, j0, norm_g, w_in, w_out, final_g, *, mod_row, final_norm, tm, tf, slice_rows):
    n, d = h.shape
    dff = w_out.shape[0]
    assert n % tm == 0 and dff % tf == 0 and tm % slice_rows == 0
    nk, n_tiles, n_slices = dff // tf, n // tm, tm // slice_rows
    assert n_slices <= nk - 1
    mod_spec = lambda j: pl.BlockSpec((MOD_ROWS, d), lambda r, k: (0, j))
    limit = _vmem_limit(
        [_nbytes((tm, d), F32), _nbytes((slice_rows, d), F32),
         _nbytes((d, tf), BF16) * 2, _nbytes((tf, d), BF16)],
        [2 * _nbytes((tm, d), BF16), _nbytes((tm, d), F32)],
        4 * _nbytes((tm, tf), F32) + _nbytes((tm, d), F32))

    def next_slice(r, k):
        tile = jnp.minimum(r, n_tiles - 1)
        return (tile * n_slices + jnp.clip(k - 1, 0, n_slices - 1), 0)

    chunk = lambda r, k: jnp.where(r == 0, 0, k)
    return pl.pallas_call(
        functools.partial(_ffn_kernel, mod_row=mod_row, final_norm=final_norm, n_slices=n_slices),
        out_shape=jax.ShapeDtypeStruct((n, d), F32),
        grid=(n_tiles + 1, nk),
        in_specs=[
            pl.BlockSpec((slice_rows, d), next_slice),
            mod_spec(j0), mod_spec(j0 + 1), mod_spec(j0 + 2),
            pl.BlockSpec((1, d), lambda r, k: (0, 0)),
            pl.BlockSpec((d, tf), lambda r, k: (0, chunk(r, k))),
            pl.BlockSpec((d, tf), lambda r, k: (0, chunk(r, k) + nk)),
            pl.BlockSpec((tf, d), lambda r, k: (chunk(r, k), 0)),
            pl.BlockSpec((1, d), lambda r, k: (0, 0)),
        ],
        out_specs=pl.BlockSpec((tm, d), lambda r, k: (jnp.maximum(r - 1, 0), 0)),
        scratch_shapes=[pltpu.VMEM((tm, d), BF16), pltpu.VMEM((tm, d), BF16), pltpu.VMEM((tm, d), F32)],
        compiler_params=pltpu.CompilerParams(
            dimension_semantics=("arbitrary", "arbitrary"), vmem_limit_bytes=limit),
        name="swiglu_half_step",
    )(h, mod, mod, mod, norm_g, w_in, w_in, w_out, final_g)


def _rope(x, cos, sin_signed):
    return x * cos + pltpu.roll(x, V7X_LANES // 2, 1) * sin_signed


def _proj_kernel(h_ref, sh_ref, sc_ref, ng_ref, win_ref, aqg_ref, akg_ref, bqg_ref, bkvg_ref,
                 wuq_ref, wukv_ref, ca_ref, sa_ref, cb_ref, sb_ref,
                 qa_ref, ka_ref, va_ref, qb_ref, kb_ref, vb_ref, cu_ref, *, mod_row, dims):
    a_q, a_kv, b_q, b_kv, b_heads, c_w = dims
    row = mod_row(pl.program_id(0))
    y = _rms(h_ref[...]) * ng_ref[...]
    y = y * (1.0 + sc_ref[pl.ds(row, 1), :]) + sh_ref[pl.ds(row, 1), :]
    u = jnp.dot(y.astype(BF16), win_ref[...], preferred_element_type=F32)

    ca, sa, cb, sb = ca_ref[...], sa_ref[...], cb_ref[...], sb_ref[...]
    hd = A_HEAD_DIM
    o_ak, o_av, o_bq = a_q, a_q + a_kv, a_q + 2 * a_kv
    o_bkv, o_bkr = o_bq + b_q, o_bq + b_q + b_kv
    o_cu = o_bkr + V7X_LANES

    qg = aqg_ref[...] * (hd ** -0.5 * LOG2E)
    for h in range(a_q // hd):
        x = _rms(u[:, h * hd:(h + 1) * hd]) * qg
        qa_ref[:, h * hd:(h + 1) * hd] = _rope(x, ca, sa).astype(BF16)
    kg = akg_ref[...]
    for h in range(a_kv // hd):
        x = _rms(u[:, o_ak + h * hd:o_ak + (h + 1) * hd]) * kg
        ka_ref[:, h * hd:(h + 1) * hd] = _rope(x, ca, sa).astype(BF16)
    va_ref[...] = u[:, o_av:o_av + a_kv].astype(BF16)

    xq = (_rms(u[:, o_bq:o_bq + b_q]) * bqg_ref[...]).astype(BF16)
    q = jnp.dot(xq, wuq_ref[...], preferred_element_type=F32)
    q_scale = (B_NOPE_DIM + B_ROPE_DIM) ** -0.5 * LOG2E
    xkv = (_rms(u[:, o_bkv:o_bkv + b_kv]) * bkvg_ref[...]).astype(BF16)
    kv = jnp.dot(xkv, wukv_ref[...], preferred_element_type=F32)
    k_rope = _rope(u[:, o_bkr:o_bkr + V7X_LANES], cb, sb).astype(BF16)
    for h in range(b_heads):
        o = h * B_QK_PAD
        qb_ref[:, o:o + B_NOPE_DIM] = (q[:, o:o + B_NOPE_DIM] * q_scale).astype(BF16)
        qr = _rope(q[:, o + B_NOPE_DIM:o + B_QK_PAD], cb, sb)
        qb_ref[:, o + B_NOPE_DIM:o + B_QK_PAD] = (qr * q_scale).astype(BF16)
        kb_ref[:, o:o + B_NOPE_DIM] = kv[:, h * B_NOPE_DIM:(h + 1) * B_NOPE_DIM].astype(BF16)
        kb_ref[:, o + B_NOPE_DIM:o + B_QK_PAD] = k_rope
    vb_ref[...] = kv[:, b_heads * B_NOPE_DIM:].astype(BF16)

    cu_ref[...] = u[:, o_cu:o_cu + c_w]


def _proj(h, mod, j0, norm_g, w_in_p, a_q_g, a_k_g, b_q_g, b_kv_g, w_uq_p, w_ukv_p, tables,
          *, mod_row, table_block, dims, tm):
    n, d = h.shape
    a_q, a_kv, b_q, b_kv, b_heads, c_w = dims
    assert n % tm == 0
    ncols = w_in_p.shape[1]
    mod_spec = lambda j: pl.BlockSpec((MOD_ROWS, d), lambda i: (0, j))
    tab_spec = pl.BlockSpec((tm, V7X_LANES), lambda i: (table_block(i), 0))
    row_spec = lambda w: pl.BlockSpec((tm, w), lambda i: (i, 0))
    out_widths = [(a_q, BF16), (a_kv, BF16), (a_kv, BF16), (b_heads * B_QK_PAD, BF16),
                  (b_heads * B_QK_PAD, BF16), (b_heads * B_V_DIM, BF16), (c_w, F32)]
    limit = _vmem_limit(
        [_nbytes((tm, d), F32)] + [_nbytes((tm, w), t) for w, t in out_widths]
        + [4 * _nbytes((tm, V7X_LANES), F32)],
        [_nbytes(w_in_p.shape, BF16), _nbytes(w_uq_p.shape, BF16), _nbytes(w_ukv_p.shape, BF16)],
        _nbytes((tm, ncols), F32) + _nbytes((tm, d), F32) + 2 * _nbytes((tm, 4 * B_QK_PAD), F32))
    return pl.pallas_call(
        functools.partial(_proj_kernel, mod_row=mod_row, dims=dims),
        out_shape=[jax.ShapeDtypeStruct((n, w), t) for w, t in out_widths],
        grid=(n // tm,),
        in_specs=[
            row_spec(d), mod_spec(j0), mod_spec(j0 + 1),
            _resident((1, d)), _resident(w_in_p.shape),
            _resident((1, A_HEAD_DIM)), _resident((1, A_HEAD_DIM)),
            _resident((1, b_q)), _resident((1, b_kv)),
            _resident(w_uq_p.shape), _resident(w_ukv_p.shape),
            tab_spec, tab_spec, tab_spec, tab_spec,
        ],
        out_specs=[row_spec(w) for w, _ in out_widths],
        compiler_params=pltpu.CompilerParams(
            dimension_semantics=("parallel",), vmem_limit_bytes=limit),
        name="mixer_in_proj",
    )(h, mod, mod, norm_g, w_in_p, a_q_g, a_k_g, b_q_g, b_kv_g, w_uq_p, w_ukv_p, *tables)


def _attn_kernel(*refs, n_stack, dk, dv, chunks, side_blocks):
    n_side, n_kv_refs = len(side_blocks), 2 * len(chunks)
    q_ref, kv_refs = refs[0], refs[1:1 + n_kv_refs]
    side_in = refs[1 + n_kv_refs:1 + n_kv_refs + n_side]
    o_ref, side_out = refs[1 + n_kv_refs + n_side], refs[2 + n_kv_refs + n_side:]
    tq = q_ref.shape[0]

    step_id = ((pl.program_id(0) * pl.num_programs(1) + pl.program_id(1)) * pl.num_programs(2)
               + pl.program_id(2))
    for w_ref, wo_ref, n_blocks in zip(side_in, side_out, side_blocks):
        def cast(w_ref=w_ref, wo_ref=wo_ref):
            wo_ref[...] = w_ref[...].astype(BF16)
        pl.when(step_id < n_blocks)(cast)

    q = jnp.concatenate([q_ref[:, h * dk:(h + 1) * dk] for h in range(n_stack)], axis=0)
    m_rows = n_stack * tq

    def step(k, v, carry):
        m, acc = carry
        s = lax.dot_general(q, k, (((1,), (1,)), ((), ())), preferred_element_type=F32)
        m_new = jnp.maximum(m, jnp.max(s, axis=-1, keepdims=True))
        p = jnp.exp2(s - m_new).astype(BF16)
        v_ones = jnp.concatenate([v, jnp.ones((v.shape[0], V7X_LANES), BF16)], axis=1)
        acc = jnp.exp2(m - m_new) * acc + jnp.dot(p, v_ones, preferred_element_type=F32)
        return m_new, acc

    carry = (jnp.full((m_rows, 1), -jnp.inf, F32), jnp.zeros((m_rows, dv + V7X_LANES), F32))
    for seg, chunk in enumerate(chunks):
        k_ref, v_ref = kv_refs[2 * seg], kv_refs[2 * seg + 1]
        n_chunks = k_ref.shape[0] // chunk
        for c in range(n_chunks):
            carry = step(k_ref[c * chunk:(c + 1) * chunk, :], v_ref[c * chunk:(c + 1) * chunk, :], carry)
    _, acc = carry
    out = acc[:, :dv] / acc[:, dv:dv + dv]
    for h in range(n_stack):
        o_ref[:, h * dv:(h + 1) * dv] = out[h * tq:(h + 1) * tq].astype(BF16)


def _attention(q, segments, side=(), *, batch, n_kv, n_stack, dk, dv, tq, chunk):
    nq = q.shape[0]
    lq = nq // batch
    assert lq % tq == 0 and dv == V7X_LANES
    tiles = lq // tq
    n_steps = batch * n_kv * tiles
    in_specs = [pl.BlockSpec((tq, n_stack * dk), lambda b, g, i: (b * tiles + i, g))]
    args, chunks, kv_bytes = [q], [], []
    for k, v in segments:
        lk = k.shape[0] // batch
        chunks.append(min(chunk, lk))
        assert lk % chunks[-1] == 0
        in_specs.append(pl.BlockSpec((lk, dk), lambda b, g, i: (b, g)))
        in_specs.append(pl.BlockSpec((lk, dv), lambda b, g, i: (b, g)))
        args += [k, v]
        kv_bytes += [_nbytes((lk, dk), BF16), _nbytes((lk, dv), BF16)]
    out_shape = [jax.ShapeDtypeStruct((nq, n_kv * n_stack * dv), BF16)]
    out_specs = [pl.BlockSpec((tq, n_stack * dv), lambda b, g, i: (b * tiles + i, g))]
    side_blocks, side_bytes = [], []
    for w, layer, rows in side:
        _, w_rows, w_cols = w.shape
        n_blocks = w_rows // rows
        assert w_rows % rows == 0 and rows % 16 == 0 and n_blocks <= n_steps

        def block(b, g, i, n_blocks=n_blocks):
            return jnp.minimum((b * n_kv + g) * tiles + i, n_blocks - 1)

        in_specs.append(pl.BlockSpec((None, rows, w_cols),
                                     lambda b, g, i, layer=layer, block=block: (layer, block(b, g, i), 0)))
        out_specs.append(pl.BlockSpec((rows, w_cols), lambda b, g, i, block=block: (block(b, g, i), 0)))
        out_shape.append(jax.ShapeDtypeStruct((w_rows, w_cols), BF16))
        args.append(w)
        side_blocks.append(n_blocks)
        side_bytes += [_nbytes((rows, w_cols), F32), _nbytes((rows, w_cols), BF16)]
    m_rows = n_stack * tq
    limit = _vmem_limit(
        [_nbytes((tq, n_stack * dk), BF16), _nbytes((tq, n_stack * dv), BF16)] + kv_bytes + side_bytes,
        [], 24 * _nbytes((m_rows, max(chunks)), F32) + 4 * _nbytes((m_rows, V7X_LANES), F32))
    outs = pl.pallas_call(
        functools.partial(_attn_kernel, n_stack=n_stack, dk=dk, dv=dv, chunks=tuple(chunks),
                          side_blocks=tuple(side_blocks)),
        out_shape=out_shape,
        grid=(batch, n_kv, tiles),
        in_specs=in_specs,
        out_specs=out_specs,
        compiler_params=pltpu.CompilerParams(
            dimension_semantics=("arbitrary", "arbitrary", "arbitrary"), vmem_limit_bytes=limit),
        name="attention",
    )(*args)
    return outs[0], outs[1:]


def _out_kernel(h_ref, gt_ref, ya_ref, yb_ref, cu_ref, cup_ref, cun_ref, wp_ref, cs_ref, wo_ref,
                o_ref, ext_ref, pooled_ref, *, mod_row, seg_len):
    i = pl.program_id(0)
    tm = h_ref.shape[0]
    halo = V7X_SUBLANES
    pos0 = lax.rem(i * tm, seg_len)
    ext_ref[0:halo, :] = jnp.where(pos0 > 0, cup_ref[...], 0.0)
    ext_ref[halo:halo + tm, :] = cu_ref[...]
    ext_ref[halo + tm:, :] = jnp.where(pos0 + tm < seg_len, cun_ref[...], 0.0)

    pos = pos0 + lax.broadcasted_iota(jnp.int32, (tm, 1), 0)
    a_w, b_w = ya_ref.shape[1], yb_ref.shape[1]
    gd = C_GROUP_DIM
    for c0 in range(0, tm, POOL_CHUNK_ROWS):
        c1 = c0 + POOL_CHUNK_ROWS
        for g, w in enumerate(C_WINDOWS):
            cols = slice(g * gd, (g + 1) * gd)
            tot = ext_ref[halo - w // 2 + c0:halo - w // 2 + c1, cols]
            for off in range(1 - w // 2, w // 2):
                tot = tot + ext_ref[halo + off + c0:halo + off + c1, cols]
            lo = jnp.clip(pos[c0:c1] - w // 2, 0, seg_len)
            hi = jnp.clip(pos[c0:c1] - w // 2 + w, 0, seg_len)
            pooled = tot / (hi - lo).astype(F32) - cu_ref[c0:c1, cols]
            pooled_ref[c0:c1, cols] = pooled.astype(BF16)
    y = jnp.dot(ya_ref[...], wo_ref[0:a_w, :], preferred_element_type=F32)
    y += jnp.dot(yb_ref[...], wo_ref[a_w:a_w + b_w, :], preferred_element_type=F32)
    yc = []
    for g in range(len(C_WINDOWS)):
        cols = slice(g * gd, (g + 1) * gd)
        mixed = jnp.dot(pooled_ref[:, cols], wp_ref[g], preferred_element_type=F32) * cs_ref[:, cols]
        yc.append(mixed.astype(BF16))
    y += jnp.dot(jnp.concatenate(yc, axis=1), wo_ref[a_w + b_w:, :], preferred_element_type=F32)
    o_ref[...] = h_ref[...] + gt_ref[pl.ds(mod_row(i), 1), :] * y


def _mix_out(h, mod, j_gate, ya, yb, cu, w_pool, c_scale, w_out, *, mod_row, seg_len, tm):
    n, d = h.shape
    c_w = cu.shape[1]
    assert n % tm == 0 and seg_len % tm == 0
    halo = V7X_SUBLANES
    per_tile = tm // halo
    last = n // halo - 1
    row_spec = lambda w: pl.BlockSpec((tm, w), lambda i: (i, 0))
    limit = _vmem_limit(
        [2 * _nbytes((tm, d), F32), _nbytes((tm, ya.shape[1]), BF16),
         _nbytes((tm, yb.shape[1]), BF16), _nbytes((tm, c_w), F32)],
        [_nbytes(w_out.shape, BF16), _nbytes(w_pool.shape, BF16), _nbytes((tm + 2 * halo, c_w), F32)],
        2 * _nbytes((tm, d), F32))
    return pl.pallas_call(
        functools.partial(_out_kernel, mod_row=mod_row, seg_len=seg_len),
        out_shape=jax.ShapeDtypeStruct((n, d), F32),
        grid=(n // tm,),
        in_specs=[
            row_spec(d),
            pl.BlockSpec((MOD_ROWS, d), lambda i: (0, j_gate)),
            row_spec(ya.shape[1]), row_spec(yb.shape[1]), row_spec(c_w),
            pl.BlockSpec((halo, c_w), lambda i: (jnp.maximum(i * per_tile - 1, 0), 0)),
            pl.BlockSpec((halo, c_w), lambda i: (jnp.minimum((i + 1) * per_tile, last), 0)),
            _resident(w_pool.shape), _resident((1, c_w)), _resident(w_out.shape),
        ],
        out_specs=row_spec(d),
        scratch_shapes=[pltpu.VMEM((tm + 2 * halo, c_w), F32), pltpu.VMEM((tm, c_w), BF16)],
        compiler_params=pltpu.CompilerParams(
            dimension_semantics=("parallel",), vmem_limit_bytes=limit),
        name="mixer_out_proj",
    )(h, mod, ya, yb, cu, cu, cu, w_pool, c_scale, w_out)


def _rope_tables(seq, dim):
    n = dim // 4
    t = jnp.arange(seq)
    rows = (t // GRID_W).astype(F32)
    cols = (t % GRID_W).astype(F32)
    inv = ROPE_THETA ** (-jnp.arange(n, dtype=F32) / n)
    ang = jnp.concatenate([rows[:, None] * inv[None, :], cols[:, None] * inv[None, :]], axis=-1)
    cos, sin = jnp.cos(ang), jnp.sin(ang)
    pad = jnp.zeros((seq, V7X_LANES - dim), F32)
    return (jnp.concatenate([cos, cos, pad], axis=-1), jnp.concatenate([-sin, sin, pad], axis=-1))


def _identity_tables(rows, dim):
    one = jnp.concatenate([jnp.ones((rows, dim), F32), jnp.zeros((rows, V7X_LANES - dim), F32)], -1)
    return one, jnp.zeros((rows, V7X_LANES), F32)


def _w_in_columns(a_q, a_kv, b_q, b_kv, c_w):
    o = a_q + 2 * a_kv + b_q + b_kv
    half = B_ROPE_DIM // 2
    k1, k2 = np.arange(o, o + half), np.arange(o + half, o + B_ROPE_DIM)
    return np.concatenate([np.arange(o), k1, k2, k2, k1, np.arange(o + B_ROPE_DIM, o + B_ROPE_DIM + c_w)])


def _w_uq_columns(heads):
    hd = B_NOPE_DIM + B_ROPE_DIM
    half = B_ROPE_DIM // 2
    out = []
    for h in range(heads):
        b = h * hd
        x1 = np.arange(b + B_NOPE_DIM, b + B_NOPE_DIM + half)
        x2 = x1 + half
        out += [np.arange(b, b + B_NOPE_DIM), x1, x2, x2, x1]
    return np.concatenate(out)


def _w_ukv_columns(heads):
    hd = B_NOPE_DIM + B_V_DIM
    ks = [np.arange(h * hd, h * hd + B_NOPE_DIM) for h in range(heads)]
    vs = [np.arange(h * hd + B_NOPE_DIM, (h + 1) * hd) for h in range(heads)]
    return np.concatenate(ks + vs)


def kernel(x, c, ctx, c_ctx, w_mod, b_mod, norm_g, ffn1_in, ffn1_out, w_in, a_q_g, a_k_g, b_q_g,
           b_kv_g, b_w_uq, b_w_ukv, c_w_pool, c_scale, w_out, ffn2_in, ffn2_out, final_g):
    batch, seq, d = x.shape
    ctx_len = ctx.shape[1]
    depth = w_mod.shape[0]
    assert batch + 1 <= MOD_ROWS
    b_q, b_kv = b_q_g.shape[1], b_kv_g.shape[1]
    b_heads = b_w_ukv.shape[2] // (B_NOPE_DIM + B_V_DIM)
    c_w = c_scale.shape[1]
    a_q = w_out.shape[1] - b_heads * B_V_DIM - c_w
    a_kv = a_q // A_GROUP
    dims = (a_q, a_kv, b_q, b_kv, b_heads, c_w)

    tm, tm_ctx = 512, 256
    tiles_per_seq = seq // tm
    lat_row = lambda i: lax.div(i, tiles_per_seq)
    ctx_row = lambda i: batch
    ffn_lat = dict(tm=1024, tf=512, slice_rows=128, mod_row=lambda t: lax.div(t, seq // 1024))
    ffn_ctx = dict(tm=512, tf=512, slice_rows=64, mod_row=ctx_row)

    c_rows = jnp.concatenate([c, c_ctx[None, :], jnp.zeros((MOD_ROWS - batch - 1, d), F32)], axis=0)
    mod = _modulation(c_rows, w_mod, b_mod)

    tab_lat = _rope_tables(seq, A_HEAD_DIM) + _rope_tables(seq, B_ROPE_DIM)
    tab_ctx = _identity_tables(tm_ctx, A_HEAD_DIM) + _identity_tables(tm_ctx, B_ROPE_DIM)
    cols_in = _w_in_columns(a_q, a_kv, b_q, b_kv, c_w)
    cols_uq, cols_ukv = _w_uq_columns(b_heads), _w_ukv_columns(b_heads)

    h = x.reshape(batch * seq, d)
    hc = ctx.reshape(batch * ctx_len, d)
    tq_a, tq_b = 256, 1024
    a_steps = batch * (a_kv // A_HEAD_DIM) * (seq // tq_a)
    b_steps = batch * b_heads * (seq // tq_b)
    rows_in, rows_out = ffn1_in.shape[1] // a_steps, 2 * ffn1_out.shape[1] // b_steps
    f1_in, f1_out = _layer_bf16(ffn1_in, 0), _layer_bf16(ffn1_out, 0)
    for i in range(depth):
        last = i == depth - 1
        ng = norm_g[i]
        w_in_p = w_in[i][:, cols_in].astype(BF16)
        w_uq_p = b_w_uq[i][:, cols_uq].astype(BF16)
        w_ukv_p = b_w_ukv[i][:, cols_ukv].astype(BF16)
        w_pool = c_w_pool[i].astype(BF16)
        side_a = [(ffn2_in, i, rows_in)] + ([] if last else [(ffn1_in, i + 1, rows_in)])
        side_b = [(ffn2_out, i, rows_out), (w_out, i, w_out.shape[1] // b_steps)]
        side_b += [] if last else [(ffn1_out, i + 1, rows_out)]
        gains = (a_q_g[i][None], a_k_g[i][None], b_q_g[i][None], b_kv_g[i][None])
        fg = final_g[None]

        ffn = functools.partial(_ffn, final_g=fg)
        h = ffn(h, mod[i], 0, ng[0:1], f1_in, f1_out, final_norm=False, **ffn_lat)
        hc = ffn(hc, mod[i], 0, ng[0:1], f1_in, f1_out, final_norm=False, **ffn_ctx)

        proj = functools.partial(_proj, norm_g=ng[1:2], w_in_p=w_in_p, a_q_g=gains[0], a_k_g=gains[1],
                                 b_q_g=gains[2], b_kv_g=gains[3], w_uq_p=w_uq_p, w_ukv_p=w_ukv_p,
                                 dims=dims)
        qa, ka, va, qb, kb, vb, cu = proj(h, mod[i], 3, tables=tab_lat, mod_row=lat_row,
                                          table_block=lambda t: lax.rem(t, tiles_per_seq), tm=tm)
        qac, kac, vac, qbc, kbc, vbc, cuc = proj(hc, mod[i], 3, tables=tab_ctx, mod_row=ctx_row,
                                                 table_block=lambda t: 0, tm=tm_ctx)

        attn_a = functools.partial(_attention, batch=batch, n_kv=a_kv // A_HEAD_DIM, n_stack=A_GROUP,
                                   dk=A_HEAD_DIM, dv=A_HEAD_DIM, tq=tq_a, chunk=256)
        attn_b = functools.partial(_attention, batch=batch, n_kv=b_heads, n_stack=1,
                                   dk=B_QK_PAD, dv=B_V_DIM, tq=tq_b, chunk=256)
        ya, cast_a = attn_a(qa, [(kac, vac), (ka, va)], side_a)
        yb, cast_b = attn_b(qb, [(kbc, vbc), (kb, vb)], side_b)
        f2_in, f2_out, w_o = cast_a[0], cast_b[0], cast_b[1]
        mix = functools.partial(_mix_out, w_pool=w_pool, c_scale=c_scale[i][None], w_out=w_o)
        h = mix(h, mod[i], 5, ya, yb, cu, mod_row=lat_row, seg_len=seq, tm=tm)
        h = ffn(h, mod[i], 6, ng[2:3], f2_in, f2_out, final_norm=last, **ffn_lat)
        if not last:
            yac, _ = attn_a(qac, [(kac, vac)], tq=ctx_len)
            ybc, _ = attn_b(qbc, [(kbc, vbc)], tq=ctx_len)
            hc = mix(hc, mod[i], 5, yac, ybc, cuc, mod_row=ctx_row, seg_len=ctx_len, tm=tm_ctx)
            hc = ffn(hc, mod[i], 6, ng[2:3], f2_in, f2_out, final_norm=False, **ffn_ctx)
            f1_in, f1_out = cast_a[1], cast_b[2]
    return h.reshape(batch, seq, d)
```

```python
import functools
import math

import jax
import jax.numpy as jnp
import numpy as np
from jax import lax
from jax.experimental import pallas as pl
from jax.experimental.pallas import tpu as pltpu

F32 = jnp.float32
BF16 = jnp.bfloat16

GRID_W = 64
ROPE_THETA = 10000.0
NORM_EPS = 1e-6
N_MOD = 9
A_HEAD_DIM = 128
A_GROUP = 4
B_NOPE_DIM = 128
B_ROPE_DIM = 64
B_V_DIM = 128
B_QK_PAD = 256
C_WINDOWS = (2, 4, 8, 16)
C_GROUP_DIM = 128
POOL_CHUNK_ROWS = 64
LOG2E = 1.4426950408889634

V7X_LANES = 128
V7X_SUBLANES = 8
V7X_VMEM_BYTES = 64 * 1024 * 1024
V7X_VMEM_RESERVE_BYTES = 8 * 1024 * 1024
MOD_ROWS = V7X_SUBLANES


def _nbytes(shape, dtype):
    return int(np.prod(shape)) * jnp.dtype(dtype).itemsize


def _vmem_limit(pipelined, resident, temps):
    need = 2 * sum(pipelined) + sum(resident) + temps
    return int(min(V7X_VMEM_BYTES - V7X_VMEM_RESERVE_BYTES, need + need // 4))


def _rms(x, eps=NORM_EPS):
    return x * lax.rsqrt(jnp.mean(x * x, axis=-1, keepdims=True) + eps)


def _resident(shape):
    nd = len(shape)
    return pl.BlockSpec(shape, lambda *_: (0,) * nd, pipeline_mode=pl.Buffered(1))


CAST_BLOCK_BYTES = 8 * 1024 * 1024


def _cast_kernel(w_ref, o_ref):
    o_ref[...] = w_ref[...].astype(BF16)


def _layer_bf16(w, layer):
    _, rows, cols = w.shape
    n_blocks = next(nb for nb in range(1, rows + 1)
                    if rows % nb == 0 and (rows // nb) % 16 == 0
                    and _nbytes((rows // nb, cols), F32) <= CAST_BLOCK_BYTES)
    tr = rows // n_blocks
    limit = _vmem_limit([_nbytes((tr, cols), F32), _nbytes((tr, cols), BF16)], [], 0)
    return pl.pallas_call(
        _cast_kernel,
        out_shape=jax.ShapeDtypeStruct((rows, cols), BF16),
        grid=(n_blocks,),
        in_specs=[pl.BlockSpec((None, tr, cols), lambda j: (layer, j, 0))],
        out_specs=pl.BlockSpec((tr, cols), lambda j: (j, 0)),
        compiler_params=pltpu.CompilerParams(
            dimension_semantics=("parallel",), vmem_limit_bytes=limit),
        name="weight_to_bf16",
    )(w)


def _mod_kernel(c_ref, w_ref, b_ref, o_ref):
    c = c_ref[...]
    act = (c * jax.nn.sigmoid(c)).astype(BF16)
    o_ref[...] = jnp.dot(act, w_ref[...].astype(BF16), preferred_element_type=F32) + b_ref[...]


def _modulation(c_rows, w_mod, b_mod, *, tn=1024):
    depth, d, nm = w_mod.shape
    assert nm % tn == 0
    limit = _vmem_limit([_nbytes((d, tn), F32), _nbytes((MOD_ROWS, tn), F32)],
                        [_nbytes((MOD_ROWS, d), F32)], _nbytes((d, tn), BF16))
    return pl.pallas_call(
        _mod_kernel,
        out_shape=jax.ShapeDtypeStruct((depth, MOD_ROWS, nm), F32),
        grid=(depth, nm // tn),
        in_specs=[
            pl.BlockSpec((MOD_ROWS, d), lambda l, j: (0, 0)),
            pl.BlockSpec((None, d, tn), lambda l, j: (l, 0, j)),
            pl.BlockSpec((None, 1, tn), lambda l, j: (l, 0, j)),
        ],
        out_specs=pl.BlockSpec((None, MOD_ROWS, tn), lambda l, j: (l, 0, j)),
        compiler_params=pltpu.CompilerParams(
            dimension_semantics=("parallel", "parallel"), vmem_limit_bytes=limit),
        name="modulation",
    )(c_rows, w_mod, b_mod.reshape(depth, 1, nm))


def _ffn_kernel(hn_ref, sh_ref, sc_ref, gt_ref, ng_ref, wa_ref, wb_ref, wo_ref, fg_ref,
                o_ref, xn0_ref, xn1_ref, hs_ref, *, mod_row, final_norm, n_slices):
    r, k = pl.program_id(0), pl.program_id(1)
    n_tiles, nk = pl.num_programs(0) - 1, pl.num_programs(1)
    slice_rows = hn_ref.shape[0]

    def stage_next_slice(nxt_ref):
        s = jnp.clip(k - 1, 0, n_slices - 1)
        rows = pl.ds(pl.multiple_of(s * slice_rows, 16), slice_rows)
        row = mod_row(jnp.minimum(r, n_tiles - 1))
        x = hn_ref[...]
        hs_ref[rows, :] = x
        y = _rms(x) * ng_ref[...]
        y = y * (1.0 + sc_ref[pl.ds(row, 1), :]) + sh_ref[pl.ds(row, 1), :]
        nxt_ref[rows, :] = y.astype(BF16)

    def step(first, cur_ref, nxt_ref):
        if not first:
            stage_next_slice(nxt_ref)
        xn = cur_ref[...]
        a = jnp.dot(xn, wa_ref[...], preferred_element_type=F32)
        b = jnp.dot(xn, wb_ref[...], preferred_element_type=F32)
        g = (a * jax.nn.sigmoid(a) * b).astype(BF16)
        part = jnp.dot(g, wo_ref[...], preferred_element_type=F32)
        part = part * (0.5 * gt_ref[pl.ds(mod_row(r - 1), 1), :])
        o_ref[...] = (hs_ref[...] if first else o_ref[...]) + part

    pl.when((r == 0) & (k > 0))(functools.partial(stage_next_slice, xn0_ref))
    for first in (True, False):
        for par, (nxt_ref, cur_ref) in enumerate(((xn0_ref, xn1_ref), (xn1_ref, xn0_ref))):
            cond = ((k == 0) if first else (k > 0)) & (r > 0) & (lax.rem(r, 2) == par)
            pl.when(cond)(functools.partial(step, first, cur_ref, nxt_ref))

    if final_norm:
        @pl.when((k == nk - 1) & (r > 0))
        def _():
            o_ref[...] = _rms(o_ref[...]) * fg_ref[...]


def _ffn(h, mod, j0, norm_g, w_in, w_out, final_g, *, mod_row, final_norm, tm, tf, slice_rows):
    n, d = h.shape
    dff = w_out.shape[0]
    assert n % tm == 0 and dff % tf == 0 and tm % slice_rows == 0
    nk, n_tiles, n_slices = dff // tf, n // tm, tm // slice_rows
    assert n_slices <= nk - 1
    mod_spec = lambda j: pl.BlockSpec((MOD_ROWS, d), lambda r, k: (0, j))
    limit = _vmem_limit(
        [_nbytes((tm, d), F32), _nbytes((slice_rows, d), F32),
         _nbytes((d, tf), BF16) * 2, _nbytes((tf, d), BF16)],
        [2 * _nbytes((tm, d), BF16), _nbytes((tm, d), F32)],
        4 * _nbytes((tm, tf), F32) + _nbytes((tm, d), F32))

    def next_slice(r, k):
        tile = jnp.minimum(r, n_tiles - 1)
        return (tile * n_slices + jnp.clip(k - 1, 0, n_slices - 1), 0)

    chunk = lambda r, k: jnp.where(r == 0, 0, k)
    return pl.pallas_call(
        functools.partial(_ffn_kernel, mod_row=mod_row, final_norm=final_norm, n_slices=n_slices),
        out_shape=jax.ShapeDtypeStruct((n, d), F32),
        grid=(n_tiles + 1, nk),
        in_specs=[
            pl.BlockSpec((slice_rows, d), next_slice),
            mod_spec(j0), mod_spec(j0 + 1), mod_spec(j0 + 2),
            pl.BlockSpec((1, d), lambda r, k: (0, 0)),
            pl.BlockSpec((d, tf), lambda r, k: (0, chunk(r, k))),
            pl.BlockSpec((d, tf), lambda r, k: (0, chunk(r, k) + nk)),
            pl.BlockSpec((tf, d), lambda r, k: (chunk(r, k), 0)),
            pl.BlockSpec((1, d), lambda r, k: (0, 0)),
        ],
        out_specs=pl.BlockSpec((tm, d), lambda r, k: (jnp.maximum(r - 1, 0), 0)),
        scratch_shapes=[pltpu.VMEM((tm, d), BF16), pltpu.VMEM((tm, d), BF16), pltpu.VMEM((tm, d), F32)],
        compiler_params=pltpu.CompilerParams(
            dimension_semantics=("arbitrary", "arbitrary"), vmem_limit_bytes=limit),
        name="swiglu_half_step",
    )(h, mod, mod, mod, norm_g, w_in, w_in, w_out, final_g)


def _rope(x, cos, sin_signed):
    return x * cos + pltpu.roll(x, V7X_LANES // 2, 1) * sin_signed


def _proj_kernel(h_ref, sh_ref, sc_ref, ng_ref, win_ref, aqg_ref, akg_ref, bqg_ref, bkvg_ref,
                 wuq_ref, wukv_ref, ca_ref, sa_ref, cb_ref, sb_ref,
                 qa_ref, ka_ref, va_ref, qb_ref, kb_ref, vb_ref, cu_ref, *, mod_row, dims):
    a_q, a_kv, b_q, b_kv, b_heads, c_w = dims
    row = mod_row(pl.program_id(0))
    y = _rms(h_ref[...]) * ng_ref[...]
    y = y * (1.0 + sc_ref[pl.ds(row, 1), :]) + sh_ref[pl.ds(row, 1), :]
    u = jnp.dot(y.astype(BF16), win_ref[...], preferred_element_type=F32)

    ca, sa, cb, sb = ca_ref[...], sa_ref[...], cb_ref[...], sb_ref[...]
    hd = A_HEAD_DIM
    o_ak, o_av, o_bq = a_q, a_q + a_kv, a_q + 2 * a_kv
    o_bkv, o_bkr = o_bq + b_q, o_bq + b_q + b_kv
    o_cu = o_bkr + V7X_LANES

    qg = aqg_ref[...] * (hd ** -0.5 * LOG2E)
    for h in range(a_q // hd):
        x = _rms(u[:, h * hd:(h + 1) * hd]) * qg
        qa_ref[:, h * hd:(h + 1) * hd] = _rope(x, ca, sa).astype(BF16)
    kg = akg_ref[...]
    for h in range(a_kv // hd):
        x = _rms(u[:, o_ak + h * hd:o_ak + (h + 1) * hd]) * kg
        ka_ref[:, h * hd:(h + 1) * hd] = _rope(x, ca, sa).astype(BF16)
    va_ref[...] = u[:, o_av:o_av + a_kv].astype(BF16)

    xq = (_rms(u[:, o_bq:o_bq + b_q]) * bqg_ref[...]).astype(BF16)
    q = jnp.dot(xq, wuq_ref[...], preferred_element_type=F32)
    q_scale = (B_NOPE_DIM + B_ROPE_DIM) ** -0.5 * LOG2E
    xkv = (_rms(u[:, o_bkv:o_bkv + b_kv]) * bkvg_ref[...]).astype(BF16)
    kv = jnp.dot(xkv, wukv_ref[...], preferred_element_type=F32)
    k_rope = _rope(u[:, o_bkr:o_bkr + V7X_LANES], cb, sb).astype(BF16)
    for h in range(b_heads):
        o = h * B_QK_PAD
        qb_ref[:, o:o + B_NOPE_DIM] = (q[:, o:o + B_NOPE_DIM] * q_scale).astype(BF16)
        qr = _rope(q[:, o + B_NOPE_DIM:o + B_QK_PAD], cb, sb)
        qb_ref[:, o + B_NOPE_DIM:o + B_QK_PAD] = (qr * q_scale).astype(BF16)
        kb_ref[:, o:o + B_NOPE_DIM] = kv[:, h * B_NOPE_DIM:(h + 1) * B_NOPE_DIM].astype(BF16)
        kb_ref[:, o + B_NOPE_DIM:o + B_QK_PAD] = k_rope
    vb_ref[...] = kv[:, b_heads * B_NOPE_DIM:].astype(BF16)

    cu_ref[...] = u[:, o_cu:o_cu + c_w]


def _proj(h, mod, j0, norm_g, w_in_p, a_q_g, a_k_g, b_q_g, b_kv_g, w_uq_p, w_ukv_p, tables,
          *, mod_row, table_block, dims, tm):
    n, d = h.shape
    a_q, a_kv, b_q, b_kv, b_heads, c_w = dims
    assert n % tm == 0
    ncols = w_in_p.shape[1]
    mod_spec = lambda j: pl.BlockSpec((MOD_ROWS, d), lambda i: (0, j))
    tab_spec = pl.BlockSpec((tm, V7X_LANES), lambda i: (table_block(i), 0))
    row_spec = lambda w: pl.BlockSpec((tm, w), lambda i: (i, 0))
    out_widths = [(a_q, BF16), (a_kv, BF16), (a_kv, BF16), (b_heads * B_QK_PAD, BF16),
                  (b_heads * B_QK_PAD, BF16), (b_heads * B_V_DIM, BF16), (c_w, F32)]
    limit = _vmem_limit(
        [_nbytes((tm, d), F32)] + [_nbytes((tm, w), t) for w, t in out_widths]
        + [4 * _nbytes((tm, V7X_LANES), F32)],
        [_nbytes(w_in_p.shape, BF16), _nbytes(w_uq_p.shape, BF16), _nbytes(w_ukv_p.shape, BF16)],
        _nbytes((tm, ncols), F32) + _nbytes((tm, d), F32) + 2 * _nbytes((tm, 4 * B_QK_PAD), F32))
    return pl.pallas_call(
        functools.partial(_proj_kernel, mod_row=mod_row, dims=dims),
        out_shape=[jax.ShapeDtypeStruct((n, w), t) for w, t in out_widths],
        grid=(n // tm,),
        in_specs=[
            row_spec(d), mod_spec(j0), mod_spec(j0 + 1),
            _resident((1, d)), _resident(w_in_p.shape),
            _resident((1, A_HEAD_DIM)), _resident((1, A_HEAD_DIM)),
            _resident((1, b_q)), _resident((1, b_kv)),
            _resident(w_uq_p.shape), _resident(w_ukv_p.shape),
            tab_spec, tab_spec, tab_spec, tab_spec,
        ],
        out_specs=[row_spec(w) for w, _ in out_widths],
        compiler_params=pltpu.CompilerParams(
            dimension_semantics=("parallel",), vmem_limit_bytes=limit),
        name="mixer_in_proj",
    )(h, mod, mod, norm_g, w_in_p, a_q_g, a_k_g, b_q_g, b_kv_g, w_uq_p, w_ukv_p, *tables)


def _attn_kernel(*refs, n_stack, dk, dv, chunks, side_blocks):
    n_side, n_kv_refs = len(side_blocks), 2 * len(chunks)
    q_ref, kv_refs = refs[0], refs[1:1 + n_kv_refs]
    side_in = refs[1 + n_kv_refs:1 + n_kv_refs + n_side]
    o_ref, side_out = refs[1 + n_kv_refs + n_side], refs[2 + n_kv_refs + n_side:]
    tq = q_ref.shape[0]

    step_id = ((pl.program_id(0) * pl.num_programs(1) + pl.program_id(1)) * pl.num_programs(2)
               + pl.program_id(2))
    for w_ref, wo_ref, n_blocks in zip(side_in, side_out, side_blocks):
        def cast(w_ref=w_ref, wo_ref=wo_ref):
            wo_ref[...] = w_ref[...].astype(BF16)
        pl.when(step_id < n_blocks)(cast)

    q = jnp.concatenate([q_ref[:, h * dk:(h + 1) * dk] for h in range(n_stack)], axis=0)
    m_rows = n_stack * tq

    def step(k, v, carry):
        m, acc = carry
        s = lax.dot_general(q, k, (((1,), (1,)), ((), ())), preferred_element_type=F32)
        m_new = jnp.maximum(m, jnp.max(s, axis=-1, keepdims=True))
        p = jnp.exp2(s - m_new).astype(BF16)
        v_ones = jnp.concatenate([v, jnp.ones((v.shape[0], V7X_LANES), BF16)], axis=1)
        acc = jnp.exp2(m - m_new) * acc + jnp.dot(p, v_ones, preferred_element_type=F32)
        return m_new, acc

    carry = (jnp.full((m_rows, 1), -jnp.inf, F32), jnp.zeros((m_rows, dv + V7X_LANES), F32))
    for seg, chunk in enumerate(chunks):
        k_ref, v_ref = kv_refs[2 * seg], kv_refs[2 * seg + 1]
        n_chunks = k_ref.shape[0] // chunk
        for c in range(n_chunks):
            carry = step(k_ref[c * chunk:(c + 1) * chunk, :], v_ref[c * chunk:(c + 1) * chunk, :], carry)
    _, acc = carry
    out = acc[:, :dv] / acc[:, dv:dv + dv]
    for h in range(n_stack):
        o_ref[:, h * dv:(h + 1) * dv] = out[h * tq:(h + 1) * tq].astype(BF16)


def _attention(q, segments, side=(), *, batch, n_kv, n_stack, dk, dv, tq, chunk):
    nq = q.shape[0]
    lq = nq // batch
    assert lq % tq == 0 and dv == V7X_LANES
    tiles = lq // tq
    n_steps = batch * n_kv * tiles
    in_specs = [pl.BlockSpec((tq, n_stack * dk), lambda b, g, i: (b * tiles + i, g))]
    args, chunks, kv_bytes = [q], [], []
    for k, v in segments:
        lk = k.shape[0] // batch
        chunks.append(min(chunk, lk))
        assert lk % chunks[-1] == 0
        in_specs.append(pl.BlockSpec((lk, dk), lambda b, g, i: (b, g)))
        in_specs.append(pl.BlockSpec((lk, dv), lambda b, g, i: (b, g)))
        args += [k, v]
        kv_bytes += [_nbytes((lk, dk), BF16), _nbytes((lk, dv), BF16)]
    out_shape = [jax.ShapeDtypeStruct((nq, n_kv * n_stack * dv), BF16)]
    out_specs = [pl.BlockSpec((tq, n_stack * dv), lambda b, g, i: (b * tiles + i, g))]
    side_blocks, side_bytes = [], []
    for w, layer, rows in side:
        _, w_rows, w_cols = w.shape
        n_blocks = w_rows // rows
        assert w_rows % rows == 0 and rows % 16 == 0 and n_blocks <= n_steps

        def block(b, g, i, n_blocks=n_blocks):
            return jnp.minimum((b * n_kv + g) * tiles + i, n_blocks - 1)

        in_specs.append(pl.BlockSpec((None, rows, w_cols),
                                     lambda b, g, i, layer=layer, block=block: (layer, block(b, g, i), 0)))
        out_specs.append(pl.BlockSpec((rows, w_cols), lambda b, g, i, block=block: (block(b, g, i), 0)))
        out_shape.append(jax.ShapeDtypeStruct((w_rows, w_cols), BF16))
        args.append(w)
        side_blocks.append(n_blocks)
        side_bytes += [_nbytes((rows, w_cols), F32), _nbytes((rows, w_cols), BF16)]
    m_rows = n_stack * tq
    limit = _vmem_limit(
        [_nbytes((tq, n_stack * dk), BF16), _nbytes((tq, n_stack * dv), BF16)] + kv_bytes + side_bytes,
        [], 24 * _nbytes((m_rows, max(chunks)), F32) + 4 * _nbytes((m_rows, V7X_LANES), F32))
    outs = pl.pallas_call(
        functools.partial(_attn_kernel, n_stack=n_stack, dk=dk, dv=dv, chunks=tuple(chunks),
                          side_blocks=tuple(side_blocks)),
        out_shape=out_shape,
        grid=(batch, n_kv, tiles),
        in_specs=in_specs,
        out_specs=out_specs,
        compiler_params=pltpu.CompilerParams(
            dimension_semantics=("arbitrary", "arbitrary", "arbitrary"), vmem_limit_bytes=limit),
        name="attention",
    )(*args)
    return outs[0], outs[1:]


def _out_kernel(h_ref, gt_ref, ya_ref, yb_ref, cu_ref, cup_ref, cun_ref, wp_ref, cs_ref, wo_ref,
                o_ref, ext_ref, pooled_ref, *, mod_row, seg_len):
    i = pl.program_id(0)
    tm = h_ref.shape[0]
    halo = V7X_SUBLANES
    pos0 = lax.rem(i * tm, seg_len)
    ext_ref[0:halo, :] = jnp.where(pos0 > 0, cup_ref[...], 0.0)
    ext_ref[halo:halo + tm, :] = cu_ref[...]
    ext_ref[halo + tm:, :] = jnp.where(pos0 + tm < seg_len, cun_ref[...], 0.0)

    pos = pos0 + lax.broadcasted_iota(jnp.int32, (tm, 1), 0)
    a_w, b_w = ya_ref.shape[1], yb_ref.shape[1]
    gd = C_GROUP_DIM
    for c0 in range(0, tm, POOL_CHUNK_ROWS):
        c1 = c0 + POOL_CHUNK_ROWS
        for g, w in enumerate(C_WINDOWS):
            cols = slice(g * gd, (g + 1) * gd)
            tot = ext_ref[halo - w // 2 + c0:halo - w // 2 + c1, cols]
            for off in range(1 - w // 2, w // 2):
                tot = tot + ext_ref[halo + off + c0:halo + off + c1, cols]
            lo = jnp.clip(pos[c0:c1] - w // 2, 0, seg_len)
            hi = jnp.clip(pos[c0:c1] - w // 2 + w, 0, seg_len)
            pooled = tot / (hi - lo).astype(F32) - cu_ref[c0:c1, cols]
            pooled_ref[c0:c1, cols] = pooled.astype(BF16)
    y = jnp.dot(ya_ref[...], wo_ref[0:a_w, :], preferred_element_type=F32)
    y += jnp.dot(yb_ref[...], wo_ref[a_w:a_w + b_w, :], preferred_element_type=F32)
    yc = []
    for g in range(len(C_WINDOWS)):
        cols = slice(g * gd, (g + 1) * gd)
        mixed = jnp.dot(pooled_ref[:, cols], wp_ref[g], preferred_element_type=F32) * cs_ref[:, cols]
        yc.append(mixed.astype(BF16))
    y += jnp.dot(jnp.concatenate(yc, axis=1), wo_ref[a_w + b_w:, :], preferred_element_type=F32)
    o_ref[...] = h_ref[...] + gt_ref[pl.ds(mod_row(i), 1), :] * y


def _mix_out(h, mod, j_gate, ya, yb, cu, w_pool, c_scale, w_out, *, mod_row, seg_len, tm):
    n, d = h.shape
    c_w = cu.shape[1]
    assert n % tm == 0 and seg_len % tm == 0
    halo = V7X_SUBLANES
    per_tile = tm // halo
    last = n // halo - 1
    row_spec = lambda w: pl.BlockSpec((tm, w), lambda i: (i, 0))
    limit = _vmem_limit(
        [2 * _nbytes((tm, d), F32), _nbytes((tm, ya.shape[1]), BF16),
         _nbytes((tm, yb.shape[1]), BF16), _nbytes((tm, c_w), F32)],
        [_nbytes(w_out.shape, BF16), _nbytes(w_pool.shape, BF16), _nbytes((tm + 2 * halo, c_w), F32)],
        2 * _nbytes((tm, d), F32))
    return pl.pallas_call(
        functools.partial(_out_kernel, mod_row=mod_row, seg_len=seg_len),
        out_shape=jax.ShapeDtypeStruct((n, d), F32),
        grid=(n // tm,),
        in_specs=[
            row_spec(d),
            pl.BlockSpec((MOD_ROWS, d), lambda i: (0, j_gate)),
            row_spec(ya.shape[1]), row_spec(yb.shape[1]), row_spec(c_w),
            pl.BlockSpec((halo, c_w), lambda i: (jnp.maximum(i * per_tile - 1, 0), 0)),
            pl.BlockSpec((halo, c_w), lambda i: (jnp.minimum((i + 1) * per_tile, last), 0)),
            _resident(w_pool.shape), _resident((1, c_w)), _resident(w_out.shape),
        ],
        out_specs=row_spec(d),
        scratch_shapes=[pltpu.VMEM((tm + 2 * halo, c_w), F32), pltpu.VMEM((tm, c_w), BF16)],
        compiler_params=pltpu.CompilerParams(
            dimension_semantics=("parallel",), vmem_limit_bytes=limit),
        name="mixer_out_proj",
    )(h, mod, ya, yb, cu, cu, cu, w_pool, c_scale, w_out)


def _rope_tables(seq, dim):
    n = dim // 4
    t = jnp.arange(seq)
    rows = (t // GRID_W).astype(F32)
    cols = (t % GRID_W).astype(F32)
    inv = ROPE_THETA ** (-jnp.arange(n, dtype=F32) / n)
    ang = jnp.concatenate([rows[:, None] * inv[None, :], cols[:, None] * inv[None, :]], axis=-1)
    cos, sin = jnp.cos(ang), jnp.sin(ang)
    pad = jnp.zeros((seq, V7X_LANES - dim), F32)
    return (jnp.concatenate([cos, cos, pad], axis=-1), jnp.concatenate([-sin, sin, pad], axis=-1))


def _identity_tables(rows, dim):
    one = jnp.concatenate([jnp.ones((rows, dim), F32), jnp.zeros((rows, V7X_LANES - dim), F32)], -1)
    return one, jnp.zeros((rows, V7X_LANES), F32)


def _w_in_columns(a_q, a_kv, b_q, b_kv, c_w):
    o = a_q + 2 * a_kv + b_q + b_kv
    half = B_ROPE_DIM // 2
    k1, k2 = np.arange(o, o + half), np.arange(o + half, o + B_ROPE_DIM)
    return np.concatenate([np.arange(o), k1, k2, k2, k1, np.arange(o + B_ROPE_DIM, o + B_ROPE_DIM + c_w)])


def _w_uq_columns(heads):
    hd = B_NOPE_DIM + B_ROPE_DIM
    half = B_ROPE_DIM // 2
    out = []
    for h in range(heads):
        b = h * hd
        x1 = np.arange(b + B_NOPE_DIM, b + B_NOPE_DIM + half)
        x2 = x1 + half
        out += [np.arange(b, b + B_NOPE_DIM), x1, x2, x2, x1]
    return np.concatenate(out)


def _w_ukv_columns(heads):
    hd = B_NOPE_DIM + B_V_DIM
    ks = [np.arange(h * hd, h * hd + B_NOPE_DIM) for h in range(heads)]
    vs = [np.arange(h * hd + B_NOPE_DIM, (h + 1) * hd) for h in range(heads)]
    return np.concatenate(ks + vs)


def kernel(x, c, ctx, c_ctx, w_mod, b_mod, norm_g, ffn1_in, ffn1_out, w_in, a_q_g, a_k_g, b_q_g,
           b_kv_g, b_w_uq, b_w_ukv, c_w_pool, c_scale, w_out, ffn2_in, ffn2_out, final_g):
    batch, seq, d = x.shape
    ctx_len = ctx.shape[1]
    depth = w_mod.shape[0]
    assert batch + 1 <= MOD_ROWS
    b_q, b_kv = b_q_g.shape[1], b_kv_g.shape[1]
    b_heads = b_w_ukv.shape[2] // (B_NOPE_DIM + B_V_DIM)
    c_w = c_scale.shape[1]
    a_q = w_out.shape[1] - b_heads * B_V_DIM - c_w
    a_kv = a_q // A_GROUP
    dims = (a_q, a_kv, b_q, b_kv, b_heads, c_w)

    tm, tm_ctx = 512, 256
    tiles_per_seq = seq // tm
    lat_row = lambda i: lax.div(i, tiles_per_seq)
    ctx_row = lambda i: batch
    ffn_lat = dict(tm=1024, tf=512, slice_rows=128, mod_row=lambda t: lax.div(t, seq // 1024))
    ffn_ctx = dict(tm=512, tf=512, slice_rows=64, mod_row=ctx_row)

    c_rows = jnp.concatenate([c, c_ctx[None, :], jnp.zeros((MOD_ROWS - batch - 1, d), F32)], axis=0)
    mod = _modulation(c_rows, w_mod, b_mod)

    tab_lat = _rope_tables(seq, A_HEAD_DIM) + _rope_tables(seq, B_ROPE_DIM)
    tab_ctx = _identity_tables(tm_ctx, A_HEAD_DIM) + _identity_tables(tm_ctx, B_ROPE_DIM)
    cols_in = _w_in_columns(a_q, a_kv, b_q, b_kv, c_w)
    cols_uq, cols_ukv = _w_uq_columns(b_heads), _w_ukv_columns(b_heads)

    h = x.reshape(batch * seq, d)
    hc = ctx.reshape(batch * ctx_len, d)
    tq_a, tq_b = 512, 2048
    a_steps = batch * (a_kv // A_HEAD_DIM) * (seq // tq_a)
    b_steps = batch * b_heads * (seq // tq_b)
    rows_in, rows_out = ffn1_in.shape[1] // a_steps, 2 * ffn1_out.shape[1] // b_steps
    f1_in, f1_out = _layer_bf16(ffn1_in, 0), _layer_bf16(ffn1_out, 0)
    for i in range(depth):
        last = i == depth - 1
        ng = norm_g[i]
        w_in_p = w_in[i][:, cols_in].astype(BF16)
        w_uq_p = b_w_uq[i][:, cols_uq].astype(BF16)
        w_ukv_p = b_w_ukv[i][:, cols_ukv].astype(BF16)
        w_pool = c_w_pool[i].astype(BF16)
        side_a = [(ffn2_in, i, rows_in)] + ([] if last else [(ffn1_in, i + 1, rows_in)])
        side_b = [(ffn2_out, i, rows_out), (w_out, i, w_out.shape[1] // b_steps)]
        side_b += [] if last else [(ffn1_out, i + 1, rows_out)]
        gains = (a_q_g[i][None], a_k_g[i][None], b_q_g[i][None], b_kv_g[i][None])
        fg = final_g[None]

        ffn = functools.partial(_ffn, final_g=fg)
        h = ffn(h, mod[i], 0, ng[0:1], f1_in, f1_out, final_norm=False, **ffn_lat)
        hc = ffn(hc, mod[i], 0, ng[0:1], f1_in, f1_out, final_norm=False, **ffn_ctx)

        proj = functools.partial(_proj, norm_g=ng[1:2], w_in_p=w_in_p, a_q_g=gains[0], a_k_g=gains[1],
                                 b_q_g=gains[2], b_kv_g=gains[3], w_uq_p=w_uq_p, w_ukv_p=w_ukv_p,
                                 dims=dims)
        qa, ka, va, qb, kb, vb, cu = proj(h, mod[i], 3, tables=tab_lat, mod_row=lat_row,
                                          table_block=lambda t: lax.rem(t, tiles_per_seq), tm=tm)
        qac, kac, vac, qbc, kbc, vbc, cuc = proj(hc, mod[i], 3, tables=tab_ctx, mod_row=ctx_row,
                                                 table_block=lambda t: 0, tm=tm_ctx)

        attn_a = functools.partial(_attention, batch=batch, n_kv=a_kv // A_HEAD_DIM, n_stack=A_GROUP,
                                   dk=A_HEAD_DIM, dv=A_HEAD_DIM, tq=tq_a, chunk=256)
        attn_b = functools.partial(_attention, batch=batch, n_kv=b_heads, n_stack=1,
                                   dk=B_QK_PAD, dv=B_V_DIM, tq=tq_b, chunk=256)
        ya, cast_a = attn_a(qa, [(kac, vac), (ka, va)], side_a)
        yb, cast_b = attn_b(qb, [(kbc, vbc), (kb, vb)], side_b)
        f2_in, f2_out, w_o = cast_a[0], cast_b[0], cast_b[1]
        mix = functools.partial(_mix_out, w_pool=w_pool, c_scale=c_scale[i][None], w_out=w_o)
        h = mix(h, mod[i], 5, ya, yb, cu, mod_row=lat_row, seg_len=seq, tm=tm)
        h = ffn(h, mod[i], 6, ng[2:3], f2_in, f2_out, final_norm=last, **ffn_lat)
        if not last:
            yac, _ = attn_a(qac, [(kac, vac)], tq=ctx_len)
            ybc, _ = attn_b(qbc, [(kbc, vbc)], tq=ctx_len)
            hc = mix(hc, mod[i], 5, yac, ybc, cuc, mod_row=ctx_row, seg_len=ctx_len, tm=tm_ctx)
            hc = ffn(hc, mod[i], 6, ng[2:3], f2_in, f2_out, final_norm=False, **ffn_ctx)
            f1_in, f1_out = cast_a[1], cast_b[2]
    return h.reshape(batch, seq, d)
```

```python
import functools
import math

import jax
import jax.numpy as jnp
import numpy as np
from jax import lax
from jax.experimental import pallas as pl
from jax.experimental.pallas import tpu as pltpu

F32 = jnp.float32
BF16 = jnp.bfloat16

GRID_W = 64
ROPE_THETA = 10000.0
NORM_EPS = 1e-6
N_MOD = 9
A_HEAD_DIM = 128
A_GROUP = 4
B_NOPE_DIM = 128
B_ROPE_DIM = 64
B_V_DIM = 128
B_QK_PAD = 256
C_WINDOWS = (2, 4, 8, 16)
C_GROUP_DIM = 128
POOL_CHUNK_ROWS = 64
OUT_COL_BLOCK = 256
LOG2E = 1.4426950408889634

V7X_LANES = 128
V7X_SUBLANES = 8
V7X_VMEM_BYTES = 64 * 1024 * 1024
V7X_VMEM_RESERVE_BYTES = 8 * 1024 * 1024
MOD_ROWS = V7X_SUBLANES


def _nbytes(shape, dtype):
    return int(np.prod(shape)) * jnp.dtype(dtype).itemsize


def _vmem_limit(pipelined, resident, temps):
    need = 2 * sum(pipelined) + sum(resident) + temps
    return int(min(V7X_VMEM_BYTES - V7X_VMEM_RESERVE_BYTES, need + need // 4))


def _rms(x, eps=NORM_EPS):
    return x * lax.rsqrt(jnp.mean(x * x, axis=-1, keepdims=True) + eps)


def _resident(shape):
    nd = len(shape)
    return pl.BlockSpec(shape, lambda *_: (0,) * nd, pipeline_mode=pl.Buffered(1))


CAST_BLOCK_BYTES = 8 * 1024 * 1024


def _cast_kernel(w_ref, o_ref):
    o_ref[...] = w_ref[...].astype(BF16)


def _layer_bf16(w, layer):
    _, rows, cols = w.shape
    n_blocks = next(nb for nb in range(1, rows + 1)
                    if rows % nb == 0 and (rows // nb) % 16 == 0
                    and _nbytes((rows // nb, cols), F32) <= CAST_BLOCK_BYTES)
    tr = rows // n_blocks
    limit = _vmem_limit([_nbytes((tr, cols), F32), _nbytes((tr, cols), BF16)], [], 0)
    return pl.pallas_call(
        _cast_kernel,
        out_shape=jax.ShapeDtypeStruct((rows, cols), BF16),
        grid=(n_blocks,),
        in_specs=[pl.BlockSpec((None, tr, cols), lambda j: (layer, j, 0))],
        out_specs=pl.BlockSpec((tr, cols), lambda j: (j, 0)),
        compiler_params=pltpu.CompilerParams(
            dimension_semantics=("parallel",), vmem_limit_bytes=limit),
        name="weight_to_bf16",
    )(w)


def _mod_kernel(c_ref, w_ref, b_ref, o_ref):
    c = c_ref[...]
    act = (c * jax.nn.sigmoid(c)).astype(BF16)
    o_ref[...] = jnp.dot(act, w_ref[...].astype(BF16), preferred_element_type=F32) + b_ref[...]


def _modulation(c_rows, w_mod, b_mod, *, tn=1024):
    depth, d, nm = w_mod.shape
    assert nm % tn == 0
    limit = _vmem_limit([_nbytes((d, tn), F32), _nbytes((MOD_ROWS, tn), F32)],
                        [_nbytes((MOD_ROWS, d), F32)], _nbytes((d, tn), BF16))
    return pl.pallas_call(
        _mod_kernel,
        out_shape=jax.ShapeDtypeStruct((depth, MOD_ROWS, nm), F32),
        grid=(depth, nm // tn),
        in_specs=[
            pl.BlockSpec((MOD_ROWS, d), lambda l, j: (0, 0)),
            pl.BlockSpec((None, d, tn), lambda l, j: (l, 0, j)),
            pl.BlockSpec((None, 1, tn), lambda l, j: (l, 0, j)),
        ],
        out_specs=pl.BlockSpec((None, MOD_ROWS, tn), lambda l, j: (l, 0, j)),
        compiler_params=pltpu.CompilerParams(
            dimension_semantics=("parallel", "parallel"), vmem_limit_bytes=limit),
        name="modulation",
    )(c_rows, w_mod, b_mod.reshape(depth, 1, nm))


def _ffn_kernel(hn_ref, sh_ref, sc_ref, gt_ref, ng_ref, wa_ref, wb_ref, wo_ref, fg_ref,
                o_ref, xn0_ref, xn1_ref, hs_ref, *, mod_row, final_norm, n_slices):
    r, k = pl.program_id(0), pl.program_id(1)
    n_tiles, nk = pl.num_programs(0) - 1, pl.num_programs(1)
    slice_rows = hn_ref.shape[0]

    def stage_next_slice(nxt_ref):
        s = jnp.clip(k - 1, 0, n_slices - 1)
        rows = pl.ds(pl.multiple_of(s * slice_rows, 16), slice_rows)
        row = mod_row(jnp.minimum(r, n_tiles - 1))
        x = hn_ref[...]
        hs_ref[rows, :] = x
        y = _rms(x) * ng_ref[...]
        y = y * (1.0 + sc_ref[pl.ds(row, 1), :]) + sh_ref[pl.ds(row, 1), :]
        nxt_ref[rows, :] = y.astype(BF16)

    def step(first, cur_ref, nxt_ref):
        if not first:
            stage_next_slice(nxt_ref)
        xn = cur_ref[...]
        a = jnp.dot(xn, wa_ref[...], preferred_element_type=F32)
        b = jnp.dot(xn, wb_ref[...], preferred_element_type=F32)
        g = (a * jax.nn.sigmoid(a) * b).astype(BF16)
        part = jnp.dot(g, wo_ref[...], preferred_element_type=F32)
        part = part * (0.5 * gt_ref[pl.ds(mod_row(r - 1), 1), :])
        o_ref[...] = (hs_ref[...] if first else o_ref[...]) + part

    pl.when((r == 0) & (k > 0))(functools.partial(stage_next_slice, xn0_ref))
    for first in (True, False):
        for par, (nxt_ref, cur_ref) in enumerate(((xn0_ref, xn1_ref), (xn1_ref, xn0_ref))):
            cond = ((k == 0) if first else (k > 0)) & (r > 0) & (lax.rem(r, 2) == par)
            pl.when(cond)(functools.partial(step, first, cur_ref, nxt_ref))

    if final_norm:
        @pl.when((k == nk - 1) & (r > 0))
        def _():
            o_ref[...] = _rms(o_ref[...]) * fg_ref[...]


def _ffn(h, mod, j0, norm_g, w_in, w_out, final_g, *, mod_row, final_norm, tm, tf, slice_rows):
    n, d = h.shape
    dff = w_out.shape[0]
    assert n % tm == 0 and dff % tf == 0 and tm % slice_rows == 0
    nk, n_tiles, n_slices = dff // tf, n // tm, tm // slice_rows
    assert n_slices <= nk - 1
    mod_spec = lambda j: pl.BlockSpec((MOD_ROWS, d), lambda r, k: (0, j))
    limit = _vmem_limit(
        [_nbytes((tm, d), F32), _nbytes((slice_rows, d), F32),
         _nbytes((d, tf), BF16) * 2, _nbytes((tf, d), BF16)],
        [2 * _nbytes((tm, d), BF16), _nbytes((tm, d), F32)],
        4 * _nbytes((tm, tf), F32) + _nbytes((tm, d), F32))

    def next_slice(r, k):
        tile = jnp.minimum(r, n_tiles - 1)
        return (tile * n_slices + jnp.clip(k - 1, 0, n_slices - 1), 0)

    chunk = lambda r, k: jnp.where(r == 0, 0, k)
    return pl.pallas_call(
        functools.partial(_ffn_kernel, mod_row=mod_row, final_norm=final_norm, n_slices=n_slices),
        out_shape=jax.ShapeDtypeStruct((n, d), F32),
        grid=(n_tiles + 1, nk),
        in_specs=[
            pl.BlockSpec((slice_rows, d), next_slice),
            mod_spec(j0), mod_spec(j0 + 1), mod_spec(j0 + 2),
            pl.BlockSpec((1, d), lambda r, k: (0, 0)),
            pl.BlockSpec((d, tf), lambda r, k: (0, chunk(r, k))),
            pl.BlockSpec((d, tf), lambda r, k: (0, chunk(r, k) + nk)),
            pl.BlockSpec((tf, d), lambda r, k: (chunk(r, k), 0)),
            pl.BlockSpec((1, d), lambda r, k: (0, 0)),
        ],
        out_specs=pl.BlockSpec((tm, d), lambda r, k: (jnp.maximum(r - 1, 0), 0)),
        scratch_shapes=[pltpu.VMEM((tm, d), BF16), pltpu.VMEM((tm, d), BF16), pltpu.VMEM((tm, d), F32)],
        compiler_params=pltpu.CompilerParams(
            dimension_semantics=("arbitrary", "arbitrary"), vmem_limit_bytes=limit),
        name="swiglu_half_step",
    )(h, mod, mod, mod, norm_g, w_in, w_in, w_out, final_g)


def _rope(x, cos, sin_signed):
    return x * cos + pltpu.roll(x, V7X_LANES // 2, 1) * sin_signed


def _proj_kernel(h_ref, sh_ref, sc_ref, ng_ref, win_ref, aqg_ref, akg_ref, bqg_ref, bkvg_ref,
                 wuq_ref, wukv_ref, ca_ref, sa_ref, cb_ref, sb_ref,
                 qa_ref, ka_ref, va_ref, qb_ref, kb_ref, vb_ref, cu_ref, *, mod_row, dims):
    a_q, a_kv, b_q, b_kv, b_heads, c_w = dims
    row = mod_row(pl.program_id(0))
    y = _rms(h_ref[...]) * ng_ref[...]
    y = y * (1.0 + sc_ref[pl.ds(row, 1), :]) + sh_ref[pl.ds(row, 1), :]
    u = jnp.dot(y.astype(BF16), win_ref[...], preferred_element_type=F32)

    ca, sa, cb, sb = ca_ref[...], sa_ref[...], cb_ref[...], sb_ref[...]
    hd = A_HEAD_DIM
    o_ak, o_av, o_bq = a_q, a_q + a_kv, a_q + 2 * a_kv
    o_bkv, o_bkr = o_bq + b_q, o_bq + b_q + b_kv
    o_cu = o_bkr + V7X_LANES

    qg = aqg_ref[...] * (hd ** -0.5 * LOG2E)
    for h in range(a_q // hd):
        x = _rms(u[:, h * hd:(h + 1) * hd]) * qg
        qa_ref[:, h * hd:(h + 1) * hd] = _rope(x, ca, sa).astype(BF16)
    kg = akg_ref[...]
    for h in range(a_kv // hd):
        x = _rms(u[:, o_ak + h * hd:o_ak + (h + 1) * hd]) * kg
        ka_ref[:, h * hd:(h + 1) * hd] = _rope(x, ca, sa).astype(BF16)
    va_ref[...] = u[:, o_av:o_av + a_kv].astype(BF16)

    xq = (_rms(u[:, o_bq:o_bq + b_q]) * bqg_ref[...]).astype(BF16)
    q = jnp.dot(xq, wuq_ref[...], preferred_element_type=F32)
    q_scale = (B_NOPE_DIM + B_ROPE_DIM) ** -0.5 * LOG2E
    xkv = (_rms(u[:, o_bkv:o_bkv + b_kv]) * bkvg_ref[...]).astype(BF16)
    kv = jnp.dot(xkv, wukv_ref[...], preferred_element_type=F32)
    k_rope = _rope(u[:, o_bkr:o_bkr + V7X_LANES], cb, sb).astype(BF16)
    for h in range(b_heads):
        o = h * B_QK_PAD
        qb_ref[:, o:o + B_NOPE_DIM] = (q[:, o:o + B_NOPE_DIM] * q_scale).astype(BF16)
        qr = _rope(q[:, o + B_NOPE_DIM:o + B_QK_PAD], cb, sb)
        qb_ref[:, o + B_NOPE_DIM:o + B_QK_PAD] = (qr * q_scale).astype(BF16)
        kb_ref[:, o:o + B_NOPE_DIM] = kv[:, h * B_NOPE_DIM:(h + 1) * B_NOPE_DIM].astype(BF16)
        kb_ref[:, o + B_NOPE_DIM:o + B_QK_PAD] = k_rope
    vb_ref[...] = kv[:, b_heads * B_NOPE_DIM:].astype(BF16)

    cu_ref[...] = u[:, o_cu:o_cu + c_w]


def _proj(h, mod, j0, norm_g, w_in_p, a_q_g, a_k_g, b_q_g, b_kv_g, w_uq_p, w_ukv_p, tables,
          *, mod_row, table_block, dims, tm):
    n, d = h.shape
    a_q, a_kv, b_q, b_kv, b_heads, c_w = dims
    assert n % tm == 0
    ncols = w_in_p.shape[1]
    mod_spec = lambda j: pl.BlockSpec((MOD_ROWS, d), lambda i: (0, j))
    tab_spec = pl.BlockSpec((tm, V7X_LANES), lambda i: (table_block(i), 0))
    row_spec = lambda w: pl.BlockSpec((tm, w), lambda i: (i, 0))
    out_widths = [(a_q, BF16), (a_kv, BF16), (a_kv, BF16), (b_heads * B_QK_PAD, BF16),
                  (b_heads * B_QK_PAD, BF16), (b_heads * B_V_DIM, BF16), (c_w, F32)]
    limit = _vmem_limit(
        [_nbytes((tm, d), F32)] + [_nbytes((tm, w), t) for w, t in out_widths]
        + [4 * _nbytes((tm, V7X_LANES), F32)],
        [_nbytes(w_in_p.shape, BF16), _nbytes(w_uq_p.shape, BF16), _nbytes(w_ukv_p.shape, BF16)],
        _nbytes((tm, ncols), F32) + _nbytes((tm, d), F32) + 2 * _nbytes((tm, 4 * B_QK_PAD), F32))
    return pl.pallas_call(
        functools.partial(_proj_kernel, mod_row=mod_row, dims=dims),
        out_shape=[jax.ShapeDtypeStruct((n, w), t) for w, t in out_widths],
        grid=(n // tm,),
        in_specs=[
            row_spec(d), mod_spec(j0), mod_spec(j0 + 1),
            _resident((1, d)), _resident(w_in_p.shape),
            _resident((1, A_HEAD_DIM)), _resident((1, A_HEAD_DIM)),
            _resident((1, b_q)), _resident((1, b_kv)),
            _resident(w_uq_p.shape), _resident(w_ukv_p.shape),
            tab_spec, tab_spec, tab_spec, tab_spec,
        ],
        out_specs=[row_spec(w) for w, _ in out_widths],
        compiler_params=pltpu.CompilerParams(
            dimension_semantics=("parallel",), vmem_limit_bytes=limit),
        name="mixer_in_proj",
    )(h, mod, mod, norm_g, w_in_p, a_q_g, a_k_g, b_q_g, b_kv_g, w_uq_p, w_ukv_p, *tables)


def _attn_kernel(*refs, n_stack, dk, dv, chunks, side_blocks):
    n_side, n_kv_refs = len(side_blocks), 2 * len(chunks)
    q_ref, kv_refs = refs[0], refs[1:1 + n_kv_refs]
    side_in = refs[1 + n_kv_refs:1 + n_kv_refs + n_side]
    o_ref, side_out = refs[1 + n_kv_refs + n_side], refs[2 + n_kv_refs + n_side:]
    tq = q_ref.shape[0]

    step_id = ((pl.program_id(0) * pl.num_programs(1) + pl.program_id(1)) * pl.num_programs(2)
               + pl.program_id(2))
    for w_ref, wo_ref, n_blocks in zip(side_in, side_out, side_blocks):
        def cast(w_ref=w_ref, wo_ref=wo_ref):
            wo_ref[...] = w_ref[...].astype(BF16)
        pl.when(step_id < n_blocks)(cast)

    q = jnp.concatenate([q_ref[:, h * dk:(h + 1) * dk] for h in range(n_stack)], axis=0)
    m_rows = n_stack * tq

    def step(k, v, carry):
        m, acc = carry
        s = lax.dot_general(q, k, (((1,), (1,)), ((), ())), preferred_element_type=F32)
        m_new = jnp.maximum(m, jnp.max(s, axis=-1, keepdims=True))
        p = jnp.exp2(s - m_new).astype(BF16)
        v_ones = jnp.concatenate([v, jnp.ones((v.shape[0], V7X_LANES), BF16)], axis=1)
        acc = jnp.exp2(m - m_new) * acc + jnp.dot(p, v_ones, preferred_element_type=F32)
        return m_new, acc

    carry = (jnp.full((m_rows, 1), -jnp.inf, F32), jnp.zeros((m_rows, dv + V7X_LANES), F32))
    for seg, chunk in enumerate(chunks):
        k_ref, v_ref = kv_refs[2 * seg], kv_refs[2 * seg + 1]
        n_chunks = k_ref.shape[0] // chunk
        for c in range(n_chunks):
            carry = step(k_ref[c * chunk:(c + 1) * chunk, :], v_ref[c * chunk:(c + 1) * chunk, :], carry)
    _, acc = carry
    out = acc[:, :dv] / acc[:, dv:dv + dv]
    for h in range(n_stack):
        o_ref[:, h * dv:(h + 1) * dv] = out[h * tq:(h + 1) * tq].astype(BF16)


def _attention(q, segments, side=(), *, batch, n_kv, n_stack, dk, dv, tq, chunk):
    nq = q.shape[0]
    lq = nq // batch
    assert lq % tq == 0 and dv == V7X_LANES
    tiles = lq // tq
    n_steps = batch * n_kv * tiles
    in_specs = [pl.BlockSpec((tq, n_stack * dk), lambda b, g, i: (b * tiles + i, g))]
    args, chunks, kv_bytes = [q], [], []
    for k, v in segments:
        lk = k.shape[0] // batch
        chunks.append(min(chunk, lk))
        assert lk % chunks[-1] == 0
        in_specs.append(pl.BlockSpec((lk, dk), lambda b, g, i: (b, g)))
        in_specs.append(pl.BlockSpec((lk, dv), lambda b, g, i: (b, g)))
        args += [k, v]
        kv_bytes += [_nbytes((lk, dk), BF16), _nbytes((lk, dv), BF16)]
    out_shape = [jax.ShapeDtypeStruct((nq, n_kv * n_stack * dv), BF16)]
    out_specs = [pl.BlockSpec((tq, n_stack * dv), lambda b, g, i: (b * tiles + i, g))]
    side_blocks, side_bytes = [], []
    for w, layer, rows in side:
        _, w_rows, w_cols = w.shape
        n_blocks = w_rows // rows
        assert w_rows % rows == 0 and rows % 16 == 0 and n_blocks <= n_steps

        def block(b, g, i, n_blocks=n_blocks):
            return jnp.minimum((b * n_kv + g) * tiles + i, n_blocks - 1)

        in_specs.append(pl.BlockSpec((None, rows, w_cols),
                                     lambda b, g, i, layer=layer, block=block: (layer, block(b, g, i), 0)))
        out_specs.append(pl.BlockSpec((rows, w_cols), lambda b, g, i, block=block: (block(b, g, i), 0)))
        out_shape.append(jax.ShapeDtypeStruct((w_rows, w_cols), BF16))
        args.append(w)
        side_blocks.append(n_blocks)
        side_bytes += [_nbytes((rows, w_cols), F32), _nbytes((rows, w_cols), BF16)]
    m_rows = n_stack * tq
    limit = _vmem_limit(
        [_nbytes((tq, n_stack * dk), BF16), _nbytes((tq, n_stack * dv), BF16)] + kv_bytes + side_bytes,
        [], 24 * _nbytes((m_rows, max(chunks)), F32) + 4 * _nbytes((m_rows, V7X_LANES), F32))
    outs = pl.pallas_call(
        functools.partial(_attn_kernel, n_stack=n_stack, dk=dk, dv=dv, chunks=tuple(chunks),
                          side_blocks=tuple(side_blocks)),
        out_shape=out_shape,
        grid=(batch, n_kv, tiles),
        in_specs=in_specs,
        out_specs=out_specs,
        compiler_params=pltpu.CompilerParams(
            dimension_semantics=("arbitrary", "arbitrary", "arbitrary"), vmem_limit_bytes=limit),
        name="attention",
    )(*args)
    return outs[0], outs[1:]


def _out_kernel(h_ref, gt_ref, ya_ref, yb_ref, cu_ref, cup_ref, cun_ref, wp_ref, cs_ref, wo_ref,
                o_ref, ext_ref, pooled0_ref, pooled1_ref, *, mod_row, seg_len):
    i = pl.program_id(0)
    n_tiles = pl.num_programs(0) - 1
    tm = h_ref.shape[0]
    halo = V7X_SUBLANES
    gd = C_GROUP_DIM
    n_rows = tm + 2 * halo
    pitch = n_rows // V7X_SUBLANES

    def stage(pooled_ref):
        pos0 = lax.rem(jnp.minimum(i, n_tiles - 1) * tm, seg_len)
        top_ok, bot_ok = pos0 > 0, pos0 + tm < seg_len
        sub = lax.broadcasted_iota(jnp.int32, (V7X_SUBLANES, V7X_LANES), 0)
        top_ind = jnp.where((sub == 0) & jnp.logical_not(top_ok), 0.0, 1.0)
        bot_ind = jnp.where((sub == V7X_SUBLANES - 1) & jnp.logical_not(bot_ok), 0.0, 1.0)

        def extended(vregs):
            ext = {v: vregs[v] for v in range(pitch)}
            for v in range(-halo, 0):
                x = vregs[v + pitch]
                ext[v] = x if isinstance(x, float) else pltpu.roll(x, 1, 0)
            for v in range(pitch, pitch + halo):
                x = vregs[v - pitch]
                ext[v] = x if isinstance(x, float) else pltpu.roll(x, V7X_SUBLANES - 1, 0)
            return ext

        def window_sums(ext):
            c2 = {v: ext[v - 1] + ext[v] for v in range(-halo + 1, pitch + halo)}
            c4 = {v: c2[v - 1] + c2[v + 1] for v in range(-halo + 2, pitch + halo - 1)}
            c8 = {v: c4[v - 2] + c4[v + 2] for v in range(-halo + 4, pitch + halo - 3)}
            c16 = {v: c8[v - 4] + c8[v + 4] for v in range(0, pitch)}
            return dict(zip(C_WINDOWS, (c2, c4, c8, c16)))

        ind = [top_ind if v < halo else bot_ind if v >= pitch - halo else 1.0 for v in range(pitch)]
        counts = window_sums(extended(ind))
        for g, w in enumerate(C_WINDOWS):
            cols = slice(g * gd, (g + 1) * gd)
            ext_ref[g, 0:halo, :] = jnp.where(top_ok, cup_ref[:, cols], 0.0)
            ext_ref[g, halo:halo + tm, :] = cu_ref[:, cols]
            ext_ref[g, halo + tm:, :] = jnp.where(bot_ok, cun_ref[:, cols], 0.0)
            ext_g, pooled_g = ext_ref.at[g], pooled_ref.at[g]
            rows = [ext_g[pl.ds(v, V7X_SUBLANES, stride=pitch), :] for v in range(pitch)]
            sums = window_sums(extended(rows))[w]
            for v in range(pitch):
                cnt = counts[w][v]
                inv = 1.0 / cnt
                pooled_g[pl.ds(v, V7X_SUBLANES, stride=pitch), :] = sums[v] * inv - rows[v]

    def stage_and_mix(nxt_ref, cur_ref):
        a_w, b_w = ya_ref.shape[1], yb_ref.shape[1]
        stage(nxt_ref)
        yc = []
        for g in range(len(C_WINDOWS)):
            cols = slice(g * gd, (g + 1) * gd)
            pooled = cur_ref[g, halo:halo + tm, :].astype(BF16)
            mixed = jnp.dot(pooled, wp_ref[g], preferred_element_type=F32) * cs_ref[:, cols]
            yc.append(mixed.astype(BF16))
        y = jnp.dot(ya_ref[...], wo_ref[0:a_w, :], preferred_element_type=F32)
        y += jnp.dot(yb_ref[...], wo_ref[a_w:a_w + b_w, :], preferred_element_type=F32)
        y += jnp.dot(jnp.concatenate(yc, axis=1), wo_ref[a_w + b_w:, :], preferred_element_type=F32)
        o_ref[...] = h_ref[...] + gt_ref[pl.ds(mod_row(i - 1), 1), :] * y

    pl.when(i == 0)(functools.partial(stage, pooled0_ref))
    for par, (nxt_ref, cur_ref) in enumerate(((pooled0_ref, pooled1_ref), (pooled1_ref, pooled0_ref))):
        pl.when((i > 0) & (lax.rem(i, 2) == par))(functools.partial(stage_and_mix, nxt_ref, cur_ref))


def _mix_out(h, mod, j_gate, ya, yb, cu, w_pool, c_scale, w_out, *, mod_row, seg_len, tm):
    n, d = h.shape
    c_w = cu.shape[1]
    assert n % tm == 0 and seg_len % tm == 0
    halo = V7X_SUBLANES
    per_tile = tm // halo
    n_tiles, last = n // tm, n // halo - 1
    prev_spec = lambda w: pl.BlockSpec((tm, w), lambda i: (jnp.maximum(i - 1, 0), 0))
    staged = lambda i: jnp.minimum(i, n_tiles - 1)
    limit = _vmem_limit(
        [2 * _nbytes((tm, d), F32), _nbytes((tm, ya.shape[1]), BF16),
         _nbytes((tm, yb.shape[1]), BF16), _nbytes((tm, c_w), F32)],
        [_nbytes(w_out.shape, BF16), _nbytes(w_pool.shape, BF16), 3 * _nbytes((tm + 2 * halo, c_w), F32)],
        2 * _nbytes((tm, d), F32))
    group_rows = pltpu.VMEM((c_w // C_GROUP_DIM, tm + 2 * halo, C_GROUP_DIM), F32)
    return pl.pallas_call(
        functools.partial(_out_kernel, mod_row=mod_row, seg_len=seg_len),
        out_shape=jax.ShapeDtypeStruct((n, d), F32),
        grid=(n_tiles + 1,),
        in_specs=[
            prev_spec(d),
            pl.BlockSpec((MOD_ROWS, d), lambda i: (0, j_gate)),
            prev_spec(ya.shape[1]), prev_spec(yb.shape[1]),
            pl.BlockSpec((tm, c_w), lambda i: (staged(i), 0)),
            pl.BlockSpec((halo, c_w), lambda i: (jnp.maximum(staged(i) * per_tile - 1, 0), 0)),
            pl.BlockSpec((halo, c_w), lambda i: (jnp.minimum((staged(i) + 1) * per_tile, last), 0)),
            _resident(w_pool.shape), _resident((1, c_w)), _resident(w_out.shape),
        ],
        out_specs=prev_spec(d),
        scratch_shapes=[group_rows, group_rows, group_rows],
        compiler_params=pltpu.CompilerParams(
            dimension_semantics=("arbitrary",), vmem_limit_bytes=limit),
        name="mixer_out_proj",
    )(h, mod, ya, yb, cu, cu, cu, w_pool, c_scale, w_out)


def _rope_tables(seq, dim):
    n = dim // 4
    t = np.arange(seq)
    rows = (t // GRID_W).astype(np.float32)
    cols = (t % GRID_W).astype(np.float32)
    inv = (np.float32(ROPE_THETA) ** (-np.arange(n, dtype=np.float32) / np.float32(n))).astype(np.float32)
    ang = np.concatenate([rows[:, None] * inv[None, :], cols[:, None] * inv[None, :]], axis=-1)
    cos, sin = np.cos(ang).astype(np.float32), np.sin(ang).astype(np.float32)
    pad = np.zeros((seq, V7X_LANES - dim), np.float32)
    return (np.concatenate([cos, cos, pad], axis=-1), np.concatenate([-sin, sin, pad], axis=-1))


def _identity_tables(rows, dim):
    one = np.concatenate([np.ones((rows, dim), np.float32),
                          np.zeros((rows, V7X_LANES - dim), np.float32)], -1)
    return one, np.zeros((rows, V7X_LANES), np.float32)


def _w_in_columns(a_q, a_kv, b_q, b_kv, c_w):
    o = a_q + 2 * a_kv + b_q + b_kv
    half = B_ROPE_DIM // 2
    k1, k2 = np.arange(o, o + half), np.arange(o + half, o + B_ROPE_DIM)
    return np.concatenate([np.arange(o), k1, k2, k2, k1, np.arange(o + B_ROPE_DIM, o + B_ROPE_DIM + c_w)])


def _w_uq_columns(heads):
    hd = B_NOPE_DIM + B_ROPE_DIM
    half = B_ROPE_DIM // 2
    out = []
    for h in range(heads):
        b = h * hd
        x1 = np.arange(b + B_NOPE_DIM, b + B_NOPE_DIM + half)
        x2 = x1 + half
        out += [np.arange(b, b + B_NOPE_DIM), x1, x2, x2, x1]
    return np.concatenate(out)


def _w_ukv_columns(heads):
    hd = B_NOPE_DIM + B_V_DIM
    ks = [np.arange(h * hd, h * hd + B_NOPE_DIM) for h in range(heads)]
    vs = [np.arange(h * hd + B_NOPE_DIM, (h + 1) * hd) for h in range(heads)]
    return np.concatenate(ks + vs)


def kernel(x, c, ctx, c_ctx, w_mod, b_mod, norm_g, ffn1_in, ffn1_out, w_in, a_q_g, a_k_g, b_q_g,
           b_kv_g, b_w_uq, b_w_ukv, c_w_pool, c_scale, w_out, ffn2_in, ffn2_out, final_g):
    batch, seq, d = x.shape
    ctx_len = ctx.shape[1]
    depth = w_mod.shape[0]
    assert batch + 1 <= MOD_ROWS
    b_q, b_kv = b_q_g.shape[1], b_kv_g.shape[1]
    b_heads = b_w_ukv.shape[2] // (B_NOPE_DIM + B_V_DIM)
    c_w = c_scale.shape[1]
    a_q = w_out.shape[1] - b_heads * B_V_DIM - c_w
    a_kv = a_q // A_GROUP
    dims = (a_q, a_kv, b_q, b_kv, b_heads, c_w)

    tm, tm_ctx = 512, 256
    tiles_per_seq = seq // tm
    lat_row = lambda i: lax.div(i, tiles_per_seq)
    ctx_row = lambda i: batch
    ffn_lat = dict(tm=1024, tf=512, slice_rows=128, mod_row=lambda t: lax.div(t, seq // 1024))
    ffn_ctx = dict(tm=batch * ctx_len, tf=512, slice_rows=128, mod_row=ctx_row)

    c_rows = jnp.concatenate([c, c_ctx[None, :], jnp.zeros((MOD_ROWS - batch - 1, d), F32)], axis=0)
    mod = _modulation(c_rows, w_mod, b_mod)

    tab_lat = _rope_tables(seq, A_HEAD_DIM) + _rope_tables(seq, B_ROPE_DIM)
    tab_ctx = _identity_tables(tm_ctx, A_HEAD_DIM) + _identity_tables(tm_ctx, B_ROPE_DIM)
    cols_in = _w_in_columns(a_q, a_kv, b_q, b_kv, c_w)
    cols_uq, cols_ukv = _w_uq_columns(b_heads), _w_ukv_columns(b_heads)

    h = x.reshape(batch * seq, d)
    hc = ctx.reshape(batch * ctx_len, d)
    tq_a, tq_b = 512, 2048
    a_steps = batch * (a_kv // A_HEAD_DIM) * (seq // tq_a)
    b_steps = batch * b_heads * (seq // tq_b)
    rows_in, rows_out = ffn1_in.shape[1] // a_steps, 2 * ffn1_out.shape[1] // b_steps
    f1_in, f1_out = _layer_bf16(ffn1_in, 0), _layer_bf16(ffn1_out, 0)
    for i in range(depth):
        last = i == depth - 1
        ng = norm_g[i]
        w_in_p = w_in[i][:, cols_in].astype(BF16)
        w_uq_p = b_w_uq[i][:, cols_uq].astype(BF16)
        w_ukv_p = b_w_ukv[i][:, cols_ukv].astype(BF16)
        w_pool = c_w_pool[i].astype(BF16)
        side_a = [(ffn2_in, i, rows_in)] + ([] if last else [(ffn1_in, i + 1, rows_in)])
        side_b = [(ffn2_out, i, rows_out), (w_out, i, w_out.shape[1] // b_steps)]
        side_b += [] if last else [(ffn1_out, i + 1, rows_out)]
        gains = (a_q_g[i][None], a_k_g[i][None], b_q_g[i][None], b_kv_g[i][None])
        fg = final_g[None]

        ffn = functools.partial(_ffn, final_g=fg)
        h = ffn(h, mod[i], 0, ng[0:1], f1_in, f1_out, final_norm=False, **ffn_lat)
        hc = ffn(hc, mod[i], 0, ng[0:1], f1_in, f1_out, final_norm=False, **ffn_ctx)

        proj = functools.partial(_proj, norm_g=ng[1:2], w_in_p=w_in_p, a_q_g=gains[0], a_k_g=gains[1],
                                 b_q_g=gains[2], b_kv_g=gains[3], w_uq_p=w_uq_p, w_ukv_p=w_ukv_p,
                                 dims=dims)
        qa, ka, va, qb, kb, vb, cu = proj(h, mod[i], 3, tables=tab_lat, mod_row=lat_row,
                                          table_block=lambda t: lax.rem(t, tiles_per_seq), tm=tm)
        qac, kac, vac, qbc, kbc, vbc, cuc = proj(hc, mod[i], 3, tables=tab_ctx, mod_row=ctx_row,
                                                 table_block=lambda t: 0, tm=tm_ctx)

        attn_a = functools.partial(_attention, batch=batch, n_kv=a_kv // A_HEAD_DIM, n_stack=A_GROUP,
                                   dk=A_HEAD_DIM, dv=A_HEAD_DIM, tq=tq_a, chunk=256)
        attn_b = functools.partial(_attention, batch=batch, n_kv=b_heads, n_stack=1,
                                   dk=B_QK_PAD, dv=B_V_DIM, tq=tq_b, chunk=256)
        ya, cast_a = attn_a(qa, [(kac, vac), (ka, va)], side_a)
        yb, cast_b = attn_b(qb, [(kbc, vbc), (kb, vb)], side_b)
        f2_in, f2_out, w_o = cast_a[0], cast_b[0], cast_b[1]
        mix = functools.partial(_mix_out, w_pool=w_pool, c_scale=c_scale[i][None], w_out=w_o)
        h = mix(h, mod[i], 5, ya, yb, cu, mod_row=lat_row, seg_len=seq, tm=tm)
        h = ffn(h, mod[i], 6, ng[2:3], f2_in, f2_out, final_norm=last, **ffn_lat)
        if not last:
            yac, _ = attn_a(qac, [(kac, vac)], tq=ctx_len)
            ybc, _ = attn_b(qbc, [(kbc, vbc)], tq=ctx_len)
            hc = mix(hc, mod[i], 5, yac, ybc, cuc, mod_row=ctx_row, seg_len=ctx_len, tm=tm_ctx)
            hc = ffn(hc, mod[i], 6, ng[2:3], f2_in, f2_out, final_norm=False, **ffn_ctx)
            f1_in, f1_out = cast_a[1], cast_b[2]
    return h.reshape(batch, seq, d)
```

```python
import functools
import math

import jax
import jax.numpy as jnp
import numpy as np
from jax import lax
from jax.experimental import pallas as pl
from jax.experimental.pallas import tpu as pltpu

F32 = jnp.float32
BF16 = jnp.bfloat16

GRID_W = 64
ROPE_THETA = 10000.0
NORM_EPS = 1e-6
N_MOD = 9
A_HEAD_DIM = 128
A_GROUP = 4
B_NOPE_DIM = 128
B_ROPE_DIM = 64
B_V_DIM = 128
B_QK_PAD = 256
C_WINDOWS = (2, 4, 8, 16)
C_GROUP_DIM = 128
POOL_CHUNK_ROWS = 64
OUT_COL_BLOCK = 256
LOG2E = 1.4426950408889634

V7X_LANES = 128
V7X_SUBLANES = 8
V7X_VMEM_BYTES = 64 * 1024 * 1024
V7X_VMEM_RESERVE_BYTES = 8 * 1024 * 1024
MOD_ROWS = V7X_SUBLANES


def _nbytes(shape, dtype):
    return int(np.prod(shape)) * jnp.dtype(dtype).itemsize


def _vmem_limit(pipelined, resident, temps):
    need = 2 * sum(pipelined) + sum(resident) + temps
    return int(min(V7X_VMEM_BYTES - V7X_VMEM_RESERVE_BYTES, need + need // 4))


def _rms(x, eps=NORM_EPS):
    return x * lax.rsqrt(jnp.mean(x * x, axis=-1, keepdims=True) + eps)


def _resident(shape):
    nd = len(shape)
    return pl.BlockSpec(shape, lambda *_: (0,) * nd, pipeline_mode=pl.Buffered(1))


CAST_BLOCK_BYTES = 8 * 1024 * 1024


def _cast_kernel(w_ref, o_ref):
    o_ref[...] = w_ref[...].astype(BF16)


def _layer_bf16(w, layer):
    _, rows, cols = w.shape
    n_blocks = next(nb for nb in range(1, rows + 1)
                    if rows % nb == 0 and (rows // nb) % 16 == 0
                    and _nbytes((rows // nb, cols), F32) <= CAST_BLOCK_BYTES)
    tr = rows // n_blocks
    limit = _vmem_limit([_nbytes((tr, cols), F32), _nbytes((tr, cols), BF16)], [], 0)
    return pl.pallas_call(
        _cast_kernel,
        out_shape=jax.ShapeDtypeStruct((rows, cols), BF16),
        grid=(n_blocks,),
        in_specs=[pl.BlockSpec((None, tr, cols), lambda j: (layer, j, 0))],
        out_specs=pl.BlockSpec((tr, cols), lambda j: (j, 0)),
        compiler_params=pltpu.CompilerParams(
            dimension_semantics=("parallel",), vmem_limit_bytes=limit),
        name="weight_to_bf16",
    )(w)


def _mod_kernel(c_ref, w_ref, b_ref, o_ref):
    c = c_ref[...]
    act = (c * jax.nn.sigmoid(c)).astype(BF16)
    o_ref[...] = jnp.dot(act, w_ref[...].astype(BF16), preferred_element_type=F32) + b_ref[...]


def _modulation(c_rows, w_mod, b_mod, *, tn=1024):
    depth, d, nm = w_mod.shape
    assert nm % tn == 0
    limit = _vmem_limit([_nbytes((d, tn), F32), _nbytes((MOD_ROWS, tn), F32)],
                        [_nbytes((MOD_ROWS, d), F32)], _nbytes((d, tn), BF16))
    return pl.pallas_call(
        _mod_kernel,
        out_shape=jax.ShapeDtypeStruct((depth, MOD_ROWS, nm), F32),
        grid=(depth, nm // tn),
        in_specs=[
            pl.BlockSpec((MOD_ROWS, d), lambda l, j: (0, 0)),
            pl.BlockSpec((None, d, tn), lambda l, j: (l, 0, j)),
            pl.BlockSpec((None, 1, tn), lambda l, j: (l, 0, j)),
        ],
        out_specs=pl.BlockSpec((None, MOD_ROWS, tn), lambda l, j: (l, 0, j)),
        compiler_params=pltpu.CompilerParams(
            dimension_semantics=("parallel", "parallel"), vmem_limit_bytes=limit),
        name="modulation",
    )(c_rows, w_mod, b_mod.reshape(depth, 1, nm))


def _ffn_kernel(hn_ref, sh_ref, sc_ref, gt_ref, ng_ref, wa_ref, wb_ref, wo_ref, fg_ref,
                o_ref, xn0_ref, xn1_ref, hs_ref, *, mod_row, final_norm, n_slices):
    r, k = pl.program_id(0), pl.program_id(1)
    n_tiles, nk = pl.num_programs(0) - 1, pl.num_programs(1)
    slice_rows = hn_ref.shape[0]

    def stage_next_slice(nxt_ref):
        s = jnp.clip(k - 1, 0, n_slices - 1)
        rows = pl.ds(pl.multiple_of(s * slice_rows, 16), slice_rows)
        row = mod_row(jnp.minimum(r, n_tiles - 1))
        x = hn_ref[...]
        hs_ref[rows, :] = x
        y = _rms(x) * ng_ref[...]
        y = y * (1.0 + sc_ref[pl.ds(row, 1), :]) + sh_ref[pl.ds(row, 1), :]
        nxt_ref[rows, :] = y.astype(BF16)

    def step(first, cur_ref, nxt_ref):
        if not first:
            stage_next_slice(nxt_ref)
        xn = cur_ref[...]
        a = jnp.dot(xn, wa_ref[...], preferred_element_type=F32)
        b = jnp.dot(xn, wb_ref[...], preferred_element_type=F32)
        g = (a * jax.nn.sigmoid(a) * b).astype(BF16)
        part = jnp.dot(g, wo_ref[...], preferred_element_type=F32)
        part = part * (0.5 * gt_ref[pl.ds(mod_row(r - 1), 1), :])
        o_ref[...] = (hs_ref[...] if first else o_ref[...]) + part

    pl.when((r == 0) & (k > 0))(functools.partial(stage_next_slice, xn0_ref))
    for first in (True, False):
        for par, (nxt_ref, cur_ref) in enumerate(((xn0_ref, xn1_ref), (xn1_ref, xn0_ref))):
            cond = ((k == 0) if first else (k > 0)) & (r > 0) & (lax.rem(r, 2) == par)
            pl.when(cond)(functools.partial(step, first, cur_ref, nxt_ref))

    if final_norm:
        @pl.when((k == nk - 1) & (r > 0))
        def _():
            o_ref[...] = _rms(o_ref[...]) * fg_ref[...]


def _ffn(h, mod, j0, norm_g, w_in, w_out, final_g, *, mod_row, final_norm, tm, tf, slice_rows):
    n, d = h.shape
    dff = w_out.shape[0]
    assert n % tm == 0 and dff % tf == 0 and tm % slice_rows == 0
    nk, n_tiles, n_slices = dff // tf, n // tm, tm // slice_rows
    assert n_slices <= nk - 1
    mod_spec = lambda j: pl.BlockSpec((MOD_ROWS, d), lambda r, k: (0, j))
    limit = _vmem_limit(
        [_nbytes((tm, d), F32), _nbytes((slice_rows, d), F32),
         _nbytes((d, tf), BF16) * 2, _nbytes((tf, d), BF16)],
        [2 * _nbytes((tm, d), BF16), _nbytes((tm, d), F32)],
        4 * _nbytes((tm, tf), F32) + _nbytes((tm, d), F32))

    def next_slice(r, k):
        tile = jnp.minimum(r, n_tiles - 1)
        return (tile * n_slices + jnp.clip(k - 1, 0, n_slices - 1), 0)

    chunk = lambda r, k: jnp.where(r == 0, 0, k)
    return pl.pallas_call(
        functools.partial(_ffn_kernel, mod_row=mod_row, final_norm=final_norm, n_slices=n_slices),
        out_shape=jax.ShapeDtypeStruct((n, d), F32),
        grid=(n_tiles + 1, nk),
        in_specs=[
            pl.BlockSpec((slice_rows, d), next_slice),
            mod_spec(j0), mod_spec(j0 + 1), mod_spec(j0 + 2),
            pl.BlockSpec((1, d), lambda r, k: (0, 0)),
            pl.BlockSpec((d, tf), lambda r, k: (0, chunk(r, k))),
            pl.BlockSpec((d, tf), lambda r, k: (0, chunk(r, k) + nk)),
            pl.BlockSpec((tf, d), lambda r, k: (chunk(r, k), 0)),
            pl.BlockSpec((1, d), lambda r, k: (0, 0)),
        ],
        out_specs=pl.BlockSpec((tm, d), lambda r, k: (jnp.maximum(r - 1, 0), 0)),
        scratch_shapes=[pltpu.VMEM((tm, d), BF16), pltpu.VMEM((tm, d), BF16), pltpu.VMEM((tm, d), F32)],
        compiler_params=pltpu.CompilerParams(
            dimension_semantics=("arbitrary", "arbitrary"), vmem_limit_bytes=limit),
        name="swiglu_half_step",
    )(h, mod, mod, mod, norm_g, w_in, w_in, w_out, final_g)


def _rope(x, cos, sin_signed):
    return x * cos + pltpu.roll(x, V7X_LANES // 2, 1) * sin_signed


def _proj_kernel(h_ref, sh_ref, sc_ref, ng_ref, win_ref, aqg_ref, akg_ref, bqg_ref, bkvg_ref,
                 wuq_ref, wukv_ref, ca_ref, sa_ref, cb_ref, sb_ref,
                 qa_ref, ka_ref, va_ref, qb_ref, kb_ref, vb_ref, cu_ref, *, mod_row, dims):
    a_q, a_kv, b_q, b_kv, b_heads, c_w = dims
    row = mod_row(pl.program_id(0))
    y = _rms(h_ref[...]) * ng_ref[...]
    y = y * (1.0 + sc_ref[pl.ds(row, 1), :]) + sh_ref[pl.ds(row, 1), :]
    u = jnp.dot(y.astype(BF16), win_ref[...], preferred_element_type=F32)

    ca, sa, cb, sb = ca_ref[...], sa_ref[...], cb_ref[...], sb_ref[...]
    hd = A_HEAD_DIM
    o_ak, o_av, o_bq = a_q, a_q + a_kv, a_q + 2 * a_kv
    o_bkv, o_bkr = o_bq + b_q, o_bq + b_q + b_kv
    o_cu = o_bkr + B_ROPE_DIM
    half = B_ROPE_DIM // 2

    qg = aqg_ref[...] * (hd ** -0.5 * LOG2E)
    for h in range(a_q // hd):
        x = _rms(u[:, h * hd:(h + 1) * hd]) * qg
        qa_ref[:, h * hd:(h + 1) * hd] = _rope(x, ca, sa).astype(BF16)
    kg = akg_ref[...]
    for h in range(a_kv // hd):
        x = _rms(u[:, o_ak + h * hd:o_ak + (h + 1) * hd]) * kg
        ka_ref[:, h * hd:(h + 1) * hd] = _rope(x, ca, sa).astype(BF16)
    va_ref[...] = u[:, o_av:o_av + a_kv].astype(BF16)

    xq = (_rms(u[:, o_bq:o_bq + b_q]) * bqg_ref[...]).astype(BF16)
    q = jnp.dot(xq, wuq_ref[...], preferred_element_type=F32)
    q_scale = (B_NOPE_DIM + B_ROPE_DIM) ** -0.5 * LOG2E
    xkv = (_rms(u[:, o_bkv:o_bkv + b_kv]) * bkvg_ref[...]).astype(BF16)
    kv = jnp.dot(xkv, wukv_ref[...], preferred_element_type=F32)
    kr = u[:, o_bkr:o_bkr + B_ROPE_DIM]
    kr = jnp.concatenate([kr, kr[:, half:], kr[:, :half]], axis=1)
    k_rope = _rope(kr, cb, sb).astype(BF16)
    for h in range(b_heads):
        o = h * B_QK_PAD
        qb_ref[:, o:o + B_NOPE_DIM] = (q[:, o:o + B_NOPE_DIM] * q_scale).astype(BF16)
        qr = _rope(q[:, o + B_NOPE_DIM:o + B_QK_PAD], cb, sb)
        qb_ref[:, o + B_NOPE_DIM:o + B_QK_PAD] = (qr * q_scale).astype(BF16)
        kb_ref[:, o:o + B_NOPE_DIM] = kv[:, h * B_NOPE_DIM:(h + 1) * B_NOPE_DIM].astype(BF16)
        kb_ref[:, o + B_NOPE_DIM:o + B_QK_PAD] = k_rope
    vb_ref[...] = kv[:, b_heads * B_NOPE_DIM:].astype(BF16)

    cu_ref[...] = u[:, o_cu:o_cu + c_w]


def _proj(h, mod, j0, norm_g, w_in_p, a_q_g, a_k_g, b_q_g, b_kv_g, w_uq_p, w_ukv_p, tables,
          *, mod_row, table_block, dims, tm):
    n, d = h.shape
    a_q, a_kv, b_q, b_kv, b_heads, c_w = dims
    assert n % tm == 0
    ncols = w_in_p.shape[1]
    mod_spec = lambda j: pl.BlockSpec((MOD_ROWS, d), lambda i: (0, j))
    tab_spec = pl.BlockSpec((tm, V7X_LANES), lambda i: (table_block(i), 0))
    row_spec = lambda w: pl.BlockSpec((tm, w), lambda i: (i, 0))
    out_widths = [(a_q, BF16), (a_kv, BF16), (a_kv, BF16), (b_heads * B_QK_PAD, BF16),
                  (b_heads * B_QK_PAD, BF16), (b_heads * B_V_DIM, BF16), (c_w, F32)]
    limit = _vmem_limit(
        [_nbytes((tm, d), F32)] + [_nbytes((tm, w), t) for w, t in out_widths]
        + [4 * _nbytes((tm, V7X_LANES), F32)],
        [_nbytes(w_in_p.shape, BF16), _nbytes(w_uq_p.shape, BF16), _nbytes(w_ukv_p.shape, BF16)],
        _nbytes((tm, ncols), F32) + _nbytes((tm, d), F32) + 2 * _nbytes((tm, 4 * B_QK_PAD), F32))
    return pl.pallas_call(
        functools.partial(_proj_kernel, mod_row=mod_row, dims=dims),
        out_shape=[jax.ShapeDtypeStruct((n, w), t) for w, t in out_widths],
        grid=(n // tm,),
        in_specs=[
            row_spec(d), mod_spec(j0), mod_spec(j0 + 1),
            _resident((1, d)), _resident(w_in_p.shape),
            _resident((1, A_HEAD_DIM)), _resident((1, A_HEAD_DIM)),
            _resident((1, b_q)), _resident((1, b_kv)),
            _resident(w_uq_p.shape), _resident(w_ukv_p.shape),
            tab_spec, tab_spec, tab_spec, tab_spec,
        ],
        out_specs=[row_spec(w) for w, _ in out_widths],
        compiler_params=pltpu.CompilerParams(
            dimension_semantics=("parallel",), vmem_limit_bytes=limit),
        name="mixer_in_proj",
    )(h, mod, mod, norm_g, w_in_p, a_q_g, a_k_g, b_q_g, b_kv_g, w_uq_p, w_ukv_p, *tables)


def _attn_kernel(*refs, n_stack, dk, dv, chunks, side_blocks):
    n_side, n_kv_refs = len(side_blocks), 2 * len(chunks)
    q_ref, kv_refs = refs[0], refs[1:1 + n_kv_refs]
    side_in = refs[1 + n_kv_refs:1 + n_kv_refs + n_side]
    o_ref, side_out = refs[1 + n_kv_refs + n_side], refs[2 + n_kv_refs + n_side:]
    tq = q_ref.shape[0]

    step_id = ((pl.program_id(0) * pl.num_programs(1) + pl.program_id(1)) * pl.num_programs(2)
               + pl.program_id(2))
    for w_ref, wo_ref, n_blocks in zip(side_in, side_out, side_blocks):
        def cast(w_ref=w_ref, wo_ref=wo_ref):
            wo_ref[...] = w_ref[...].astype(BF16)
        pl.when(step_id < n_blocks)(cast)

    q = jnp.concatenate([q_ref[:, h * dk:(h + 1) * dk] for h in range(n_stack)], axis=0)
    m_rows = n_stack * tq

    def step(k, v, carry):
        m, acc = carry
        s = lax.dot_general(q, k, (((1,), (1,)), ((), ())), preferred_element_type=F32)
        m_new = jnp.maximum(m, jnp.max(s, axis=-1, keepdims=True))
        p = jnp.exp2(s - m_new).astype(BF16)
        v_ones = jnp.concatenate([v, jnp.ones((v.shape[0], V7X_LANES), BF16)], axis=1)
        acc = jnp.exp2(m - m_new) * acc + jnp.dot(p, v_ones, preferred_element_type=F32)
        return m_new, acc

    carry = (jnp.full((m_rows, 1), -jnp.inf, F32), jnp.zeros((m_rows, dv + V7X_LANES), F32))
    for seg, chunk in enumerate(chunks):
        k_ref, v_ref = kv_refs[2 * seg], kv_refs[2 * seg + 1]
        n_chunks = k_ref.shape[0] // chunk
        for c in range(n_chunks):
            carry = step(k_ref[c * chunk:(c + 1) * chunk, :], v_ref[c * chunk:(c + 1) * chunk, :], carry)
    _, acc = carry
    out = acc[:, :dv] / acc[:, dv:dv + dv]
    for h in range(n_stack):
        o_ref[:, h * dv:(h + 1) * dv] = out[h * tq:(h + 1) * tq].astype(BF16)


def _attention(q, segments, side=(), *, batch, n_kv, n_stack, dk, dv, tq, chunk):
    nq = q.shape[0]
    lq = nq // batch
    assert lq % tq == 0 and dv == V7X_LANES
    tiles = lq // tq
    n_steps = batch * n_kv * tiles
    in_specs = [pl.BlockSpec((tq, n_stack * dk), lambda b, g, i: (b * tiles + i, g))]
    args, chunks, kv_bytes = [q], [], []
    for k, v in segments:
        lk = k.shape[0] // batch
        chunks.append(min(chunk, lk))
        assert lk % chunks[-1] == 0
        in_specs.append(pl.BlockSpec((lk, dk), lambda b, g, i: (b, g)))
        in_specs.append(pl.BlockSpec((lk, dv), lambda b, g, i: (b, g)))
        args += [k, v]
        kv_bytes += [_nbytes((lk, dk), BF16), _nbytes((lk, dv), BF16)]
    out_shape = [jax.ShapeDtypeStruct((nq, n_kv * n_stack * dv), BF16)]
    out_specs = [pl.BlockSpec((tq, n_stack * dv), lambda b, g, i: (b * tiles + i, g))]
    side_blocks, side_bytes = [], []
    for w, layer, rows in side:
        _, w_rows, w_cols = w.shape
        n_blocks = w_rows // rows
        assert w_rows % rows == 0 and rows % 16 == 0 and n_blocks <= n_steps

        def block(b, g, i, n_blocks=n_blocks):
            return jnp.minimum((b * n_kv + g) * tiles + i, n_blocks - 1)

        in_specs.append(pl.BlockSpec((None, rows, w_cols),
                                     lambda b, g, i, layer=layer, block=block: (layer, block(b, g, i), 0)))
        out_specs.append(pl.BlockSpec((rows, w_cols), lambda b, g, i, block=block: (block(b, g, i), 0)))
        out_shape.append(jax.ShapeDtypeStruct((w_rows, w_cols), BF16))
        args.append(w)
        side_blocks.append(n_blocks)
        side_bytes += [_nbytes((rows, w_cols), F32), _nbytes((rows, w_cols), BF16)]
    m_rows = n_stack * tq
    limit = _vmem_limit(
        [_nbytes((tq, n_stack * dk), BF16), _nbytes((tq, n_stack * dv), BF16)] + kv_bytes + side_bytes,
        [], 24 * _nbytes((m_rows, max(chunks)), F32) + 4 * _nbytes((m_rows, V7X_LANES), F32))
    outs = pl.pallas_call(
        functools.partial(_attn_kernel, n_stack=n_stack, dk=dk, dv=dv, chunks=tuple(chunks),
                          side_blocks=tuple(side_blocks)),
        out_shape=out_shape,
        grid=(batch, n_kv, tiles),
        in_specs=in_specs,
        out_specs=out_specs,
        compiler_params=pltpu.CompilerParams(
            dimension_semantics=("arbitrary", "arbitrary", "arbitrary"), vmem_limit_bytes=limit),
        name="attention",
    )(*args)
    return outs[0], outs[1:]


def _out_kernel(h_ref, gt_ref, ya_ref, yb_ref, cu_ref, cup_ref, cun_ref, wp_ref, cs_ref, wo_ref,
                o_ref, ext_ref, pooled0_ref, pooled1_ref, *, mod_row, seg_len):
    i = pl.program_id(0)
    n_tiles = pl.num_programs(0) - 1
    tm = h_ref.shape[0]
    halo = V7X_SUBLANES
    gd = C_GROUP_DIM
    n_rows = tm + 2 * halo
    pitch = n_rows // V7X_SUBLANES

    def stage(pooled_ref):
        pos0 = lax.rem(jnp.minimum(i, n_tiles - 1) * tm, seg_len)
        top_ok, bot_ok = pos0 > 0, pos0 + tm < seg_len
        sub = lax.broadcasted_iota(jnp.int32, (V7X_SUBLANES, V7X_LANES), 0)
        top_ind = jnp.where((sub == 0) & jnp.logical_not(top_ok), 0.0, 1.0)
        bot_ind = jnp.where((sub == V7X_SUBLANES - 1) & jnp.logical_not(bot_ok), 0.0, 1.0)

        def extended(vregs):
            ext = {v: vregs[v] for v in range(pitch)}
            for v in range(-halo, 0):
                x = vregs[v + pitch]
                ext[v] = x if isinstance(x, float) else pltpu.roll(x, 1, 0)
            for v in range(pitch, pitch + halo):
                x = vregs[v - pitch]
                ext[v] = x if isinstance(x, float) else pltpu.roll(x, V7X_SUBLANES - 1, 0)
            return ext

        def window_sums(ext):
            c2 = {v: ext[v - 1] + ext[v] for v in range(-halo + 1, pitch + halo)}
            c4 = {v: c2[v - 1] + c2[v + 1] for v in range(-halo + 2, pitch + halo - 1)}
            c8 = {v: c4[v - 2] + c4[v + 2] for v in range(-halo + 4, pitch + halo - 3)}
            c16 = {v: c8[v - 4] + c8[v + 4] for v in range(0, pitch)}
            return dict(zip(C_WINDOWS, (c2, c4, c8, c16)))

        ind = [top_ind if v < halo else bot_ind if v >= pitch - halo else 1.0 for v in range(pitch)]
        counts = window_sums(extended(ind))
        for g, w in enumerate(C_WINDOWS):
            cols = slice(g * gd, (g + 1) * gd)
            ext_ref[g, 0:halo, :] = jnp.where(top_ok, cup_ref[:, cols], 0.0)
            ext_ref[g, halo:halo + tm, :] = cu_ref[:, cols]
            ext_ref[g, halo + tm:, :] = jnp.where(bot_ok, cun_ref[:, cols], 0.0)
            ext_g, pooled_g = ext_ref.at[g], pooled_ref.at[g]
            rows = [ext_g[pl.ds(v, V7X_SUBLANES, stride=pitch), :] for v in range(pitch)]
            sums = window_sums(extended(rows))[w]
            for v in range(pitch):
                cnt = counts[w][v]
                inv = 1.0 / cnt
                pooled_g[pl.ds(v, V7X_SUBLANES, stride=pitch), :] = sums[v] * inv - rows[v]

    def stage_and_mix(nxt_ref, cur_ref):
        a_w, b_w = ya_ref.shape[1], yb_ref.shape[1]
        stage(nxt_ref)
        yc = []
        for g in range(len(C_WINDOWS)):
            cols = slice(g * gd, (g + 1) * gd)
            pooled = cur_ref[g, halo:halo + tm, :].astype(BF16)
            mixed = jnp.dot(pooled, wp_ref[g], preferred_element_type=F32) * cs_ref[:, cols]
            yc.append(mixed.astype(BF16))
        y = jnp.dot(ya_ref[...], wo_ref[0:a_w, :], preferred_element_type=F32)
        y += jnp.dot(yb_ref[...], wo_ref[a_w:a_w + b_w, :], preferred_element_type=F32)
        y += jnp.dot(jnp.concatenate(yc, axis=1), wo_ref[a_w + b_w:, :], preferred_element_type=F32)
        o_ref[...] = h_ref[...] + gt_ref[pl.ds(mod_row(i - 1), 1), :] * y

    pl.when(i == 0)(functools.partial(stage, pooled0_ref))
    for par, (nxt_ref, cur_ref) in enumerate(((pooled0_ref, pooled1_ref), (pooled1_ref, pooled0_ref))):
        pl.when((i > 0) & (lax.rem(i, 2) == par))(functools.partial(stage_and_mix, nxt_ref, cur_ref))


def _mix_out(h, mod, j_gate, ya, yb, cu, w_pool, c_scale, w_out, *, mod_row, seg_len, tm):
    n, d = h.shape
    c_w = cu.shape[1]
    assert n % tm == 0 and seg_len % tm == 0
    halo = V7X_SUBLANES
    per_tile = tm // halo
    n_tiles, last = n // tm, n // halo - 1
    prev_spec = lambda w: pl.BlockSpec((tm, w), lambda i: (jnp.maximum(i - 1, 0), 0))
    staged = lambda i: jnp.minimum(i, n_tiles - 1)
    limit = _vmem_limit(
        [2 * _nbytes((tm, d), F32), _nbytes((tm, ya.shape[1]), BF16),
         _nbytes((tm, yb.shape[1]), BF16), _nbytes((tm, c_w), F32)],
        [_nbytes(w_out.shape, BF16), _nbytes(w_pool.shape, BF16), 3 * _nbytes((tm + 2 * halo, c_w), F32)],
        2 * _nbytes((tm, d), F32))
    group_rows = pltpu.VMEM((c_w // C_GROUP_DIM, tm + 2 * halo, C_GROUP_DIM), F32)
    return pl.pallas_call(
        functools.partial(_out_kernel, mod_row=mod_row, seg_len=seg_len),
        out_shape=jax.ShapeDtypeStruct((n, d), F32),
        grid=(n_tiles + 1,),
        in_specs=[
            prev_spec(d),
            pl.BlockSpec((MOD_ROWS, d), lambda i: (0, j_gate)),
            prev_spec(ya.shape[1]), prev_spec(yb.shape[1]),
            pl.BlockSpec((tm, c_w), lambda i: (staged(i), 0)),
            pl.BlockSpec((halo, c_w), lambda i: (jnp.maximum(staged(i) * per_tile - 1, 0), 0)),
            pl.BlockSpec((halo, c_w), lambda i: (jnp.minimum((staged(i) + 1) * per_tile, last), 0)),
            _resident(w_pool.shape), _resident((1, c_w)), _resident(w_out.shape),
        ],
        out_specs=prev_spec(d),
        scratch_shapes=[group_rows, group_rows, group_rows],
        compiler_params=pltpu.CompilerParams(
            dimension_semantics=("arbitrary",), vmem_limit_bytes=limit),
        name="mixer_out_proj",
    )(h, mod, ya, yb, cu, cu, cu, w_pool, c_scale, w_out)


def _rope_tables(seq, dim):
    n = dim // 4
    t = np.arange(seq)
    rows = (t // GRID_W).astype(np.float32)
    cols = (t % GRID_W).astype(np.float32)
    inv = (np.float32(ROPE_THETA) ** (-np.arange(n, dtype=np.float32) / np.float32(n))).astype(np.float32)
    ang = np.concatenate([rows[:, None] * inv[None, :], cols[:, None] * inv[None, :]], axis=-1)
    cos, sin = np.cos(ang).astype(np.float32), np.sin(ang).astype(np.float32)
    pad = np.zeros((seq, V7X_LANES - dim), np.float32)
    return (np.concatenate([cos, cos, pad], axis=-1), np.concatenate([-sin, sin, pad], axis=-1))


def _identity_tables(rows, dim):
    one = np.concatenate([np.ones((rows, dim), np.float32),
                          np.zeros((rows, V7X_LANES - dim), np.float32)], -1)
    return one, np.zeros((rows, V7X_LANES), np.float32)


def _w_uq_columns(heads):
    hd = B_NOPE_DIM + B_ROPE_DIM
    half = B_ROPE_DIM // 2
    out = []
    for h in range(heads):
        b = h * hd
        x1 = np.arange(b + B_NOPE_DIM, b + B_NOPE_DIM + half)
        x2 = x1 + half
        out += [np.arange(b, b + B_NOPE_DIM), x1, x2, x2, x1]
    return np.concatenate(out)


def _w_ukv_columns(heads):
    hd = B_NOPE_DIM + B_V_DIM
    ks = [np.arange(h * hd, h * hd + B_NOPE_DIM) for h in range(heads)]
    vs = [np.arange(h * hd + B_NOPE_DIM, (h + 1) * hd) for h in range(heads)]
    return np.concatenate(ks + vs)


def kernel(x, c, ctx, c_ctx, w_mod, b_mod, norm_g, ffn1_in, ffn1_out, w_in, a_q_g, a_k_g, b_q_g,
           b_kv_g, b_w_uq, b_w_ukv, c_w_pool, c_scale, w_out, ffn2_in, ffn2_out, final_g):
    batch, seq, d = x.shape
    ctx_len = ctx.shape[1]
    depth = w_mod.shape[0]
    assert batch + 1 <= MOD_ROWS
    b_q, b_kv = b_q_g.shape[1], b_kv_g.shape[1]
    b_heads = b_w_ukv.shape[2] // (B_NOPE_DIM + B_V_DIM)
    c_w = c_scale.shape[1]
    a_q = w_out.shape[1] - b_heads * B_V_DIM - c_w
    a_kv = a_q // A_GROUP
    dims = (a_q, a_kv, b_q, b_kv, b_heads, c_w)

    tm, tm_ctx = 512, 256
    tiles_per_seq = seq // tm
    lat_row = lambda i: lax.div(i, tiles_per_seq)
    ctx_row = lambda i: batch
    ffn_lat = dict(tm=1024, tf=512, slice_rows=128, mod_row=lambda t: lax.div(t, seq // 1024))
    ffn_ctx = dict(tm=batch * ctx_len, tf=512, slice_rows=128, mod_row=ctx_row)

    c_rows = jnp.concatenate([c, c_ctx[None, :], jnp.zeros((MOD_ROWS - batch - 1, d), F32)], axis=0)
    mod = _modulation(c_rows, w_mod, b_mod)

    tab_lat = _rope_tables(seq, A_HEAD_DIM) + _rope_tables(seq, B_ROPE_DIM)
    tab_ctx = _identity_tables(tm_ctx, A_HEAD_DIM) + _identity_tables(tm_ctx, B_ROPE_DIM)
    cols_uq, cols_ukv = _w_uq_columns(b_heads), _w_ukv_columns(b_heads)

    h = x.reshape(batch * seq, d)
    hc = ctx.reshape(batch * ctx_len, d)
    tq_a, tq_b = 512, 2048
    a_steps = batch * (a_kv // A_HEAD_DIM) * (seq // tq_a)
    b_steps = batch * b_heads * (seq // tq_b)
    rows_in, rows_out = ffn1_in.shape[1] // a_steps, 2 * ffn1_out.shape[1] // b_steps
    f1_in, f1_out, w_in_p = _layer_bf16(ffn1_in, 0), _layer_bf16(ffn1_out, 0), _layer_bf16(w_in, 0)
    for i in range(depth):
        last = i == depth - 1
        ng = norm_g[i]
        w_uq_p = b_w_uq[i][:, cols_uq].astype(BF16)
        w_ukv_p = b_w_ukv[i][:, cols_ukv].astype(BF16)
        w_pool = c_w_pool[i].astype(BF16)
        side_a = [(ffn2_in, i, rows_in)] + ([] if last else [(ffn1_in, i + 1, rows_in)])
        side_b = [(ffn2_out, i, rows_out), (w_out, i, w_out.shape[1] // b_steps)]
        side_b += [] if last else [(ffn1_out, i + 1, rows_out), (w_in, i + 1, w_in.shape[1] // b_steps)]
        gains = (a_q_g[i][None], a_k_g[i][None], b_q_g[i][None], b_kv_g[i][None])
        fg = final_g[None]

        ffn = functools.partial(_ffn, final_g=fg)
        h = ffn(h, mod[i], 0, ng[0:1], f1_in, f1_out, final_norm=False, **ffn_lat)
        hc = ffn(hc, mod[i], 0, ng[0:1], f1_in, f1_out, final_norm=False, **ffn_ctx)

        proj = functools.partial(_proj, norm_g=ng[1:2], w_in_p=w_in_p, a_q_g=gains[0], a_k_g=gains[1],
                                 b_q_g=gains[2], b_kv_g=gains[3], w_uq_p=w_uq_p, w_ukv_p=w_ukv_p,
                                 dims=dims)
        qa, ka, va, qb, kb, vb, cu = proj(h, mod[i], 3, tables=tab_lat, mod_row=lat_row,
                                          table_block=lambda t: lax.rem(t, tiles_per_seq), tm=tm)
        qac, kac, vac, qbc, kbc, vbc, cuc = proj(hc, mod[i], 3, tables=tab_ctx, mod_row=ctx_row,
                                                 table_block=lambda t: 0, tm=tm_ctx)

        attn_a = functools.partial(_attention, batch=batch, n_kv=a_kv // A_HEAD_DIM, n_stack=A_GROUP,
                                   dk=A_HEAD_DIM, dv=A_HEAD_DIM, tq=tq_a, chunk=256)
        attn_b = functools.partial(_attention, batch=batch, n_kv=b_heads, n_stack=1,
                                   dk=B_QK_PAD, dv=B_V_DIM, tq=tq_b, chunk=256)
        ya, cast_a = attn_a(qa, [(kac, vac), (ka, va)], side_a)
        yb, cast_b = attn_b(qb, [(kbc, vbc), (kb, vb)], side_b)
        f2_in, f2_out, w_o = cast_a[0], cast_b[0], cast_b[1]
        mix = functools.partial(_mix_out, w_pool=w_pool, c_scale=c_scale[i][None], w_out=w_o)
        h = mix(h, mod[i], 5, ya, yb, cu, mod_row=lat_row, seg_len=seq, tm=tm)
        h = ffn(h, mod[i], 6, ng[2:3], f2_in, f2_out, final_norm=last, **ffn_lat)
        if not last:
            yac, _ = attn_a(qac, [(kac, vac)], tq=ctx_len)
            ybc, _ = attn_b(qbc, [(kbc, vbc)], tq=ctx_len)
            hc = mix(hc, mod[i], 5, yac, ybc, cuc, mod_row=ctx_row, seg_len=ctx_len, tm=tm_ctx)
            hc = ffn(hc, mod[i], 6, ng[2:3], f2_in, f2_out, final_norm=False, **ffn_ctx)
            f1_in, f1_out, w_in_p = cast_a[1], cast_b[2], cast_b[3]
    return h.reshape(batch, seq, d)
```

```python
import functools
import math

import jax
import jax.numpy as jnp
import numpy as np
from jax import lax
from jax.experimental import pallas as pl
from jax.experimental.pallas import tpu as pltpu

F32 = jnp.float32
BF16 = jnp.bfloat16

GRID_W = 64
ROPE_THETA = 10000.0
NORM_EPS = 1e-6
N_MOD = 9
A_HEAD_DIM = 128
A_GROUP = 4
B_NOPE_DIM = 128
B_ROPE_DIM = 64
B_V_DIM = 128
B_QK_PAD = 256
C_WINDOWS = (2, 4, 8, 16)
C_GROUP_DIM = 128
POOL_CHUNK_ROWS = 64
OUT_COL_BLOCK = 256
LOG2E = 1.4426950408889634

V7X_LANES = 128
V7X_SUBLANES = 8
V7X_VMEM_BYTES = 64 * 1024 * 1024
V7X_VMEM_RESERVE_BYTES = 8 * 1024 * 1024
MOD_ROWS = V7X_SUBLANES


def _nbytes(shape, dtype):
    return int(np.prod(shape)) * jnp.dtype(dtype).itemsize


def _vmem_limit(pipelined, resident, temps):
    need = 2 * sum(pipelined) + sum(resident) + temps
    return int(min(V7X_VMEM_BYTES - V7X_VMEM_RESERVE_BYTES, need + need // 4))


def _rms(x, eps=NORM_EPS):
    return x * lax.rsqrt(jnp.mean(x * x, axis=-1, keepdims=True) + eps)


def _resident(shape):
    nd = len(shape)
    return pl.BlockSpec(shape, lambda *_: (0,) * nd, pipeline_mode=pl.Buffered(1))


CAST_BLOCK_BYTES = 8 * 1024 * 1024


def _cast_kernel(w_ref, o_ref):
    o_ref[...] = w_ref[...].astype(BF16)


def _layer_bf16(w, layer):
    _, rows, cols = w.shape
    n_blocks = next(nb for nb in range(1, rows + 1)
                    if rows % nb == 0 and (rows // nb) % 16 == 0
                    and _nbytes((rows // nb, cols), F32) <= CAST_BLOCK_BYTES)
    tr = rows // n_blocks
    limit = _vmem_limit([_nbytes((tr, cols), F32), _nbytes((tr, cols), BF16)], [], 0)
    return pl.pallas_call(
        _cast_kernel,
        out_shape=jax.ShapeDtypeStruct((rows, cols), BF16),
        grid=(n_blocks,),
        in_specs=[pl.BlockSpec((None, tr, cols), lambda j: (layer, j, 0))],
        out_specs=pl.BlockSpec((tr, cols), lambda j: (j, 0)),
        compiler_params=pltpu.CompilerParams(
            dimension_semantics=("parallel",), vmem_limit_bytes=limit),
        name="weight_to_bf16",
    )(w)


def _mod_kernel(c_ref, w_ref, b_ref, o_ref):
    c = c_ref[...]
    act = (c * jax.nn.sigmoid(c)).astype(BF16)
    o_ref[...] = jnp.dot(act, w_ref[...].astype(BF16), preferred_element_type=F32) + b_ref[...]


def _modulation(c_rows, w_mod, b_mod, *, tn=1024):
    depth, d, nm = w_mod.shape
    assert nm % tn == 0
    limit = _vmem_limit([_nbytes((d, tn), F32), _nbytes((MOD_ROWS, tn), F32)],
                        [_nbytes((MOD_ROWS, d), F32)], _nbytes((d, tn), BF16))
    return pl.pallas_call(
        _mod_kernel,
        out_shape=jax.ShapeDtypeStruct((depth, MOD_ROWS, nm), F32),
        grid=(depth, nm // tn),
        in_specs=[
            pl.BlockSpec((MOD_ROWS, d), lambda l, j: (0, 0)),
            pl.BlockSpec((None, d, tn), lambda l, j: (l, 0, j)),
            pl.BlockSpec((None, 1, tn), lambda l, j: (l, 0, j)),
        ],
        out_specs=pl.BlockSpec((None, MOD_ROWS, tn), lambda l, j: (l, 0, j)),
        compiler_params=pltpu.CompilerParams(
            dimension_semantics=("parallel", "parallel"), vmem_limit_bytes=limit),
        name="modulation",
    )(c_rows, w_mod, b_mod.reshape(depth, 1, nm))


def _ffn_kernel(hn_ref, sh_ref, sc_ref, gt_ref, ng_ref, wa_ref, wb_ref, wo_ref, fg_ref,
                o_ref, xn0_ref, xn1_ref, hs_ref, *, mod_row, final_norm, n_slices):
    r, k = pl.program_id(0), pl.program_id(1)
    n_tiles, nk = pl.num_programs(0) - 1, pl.num_programs(1)
    slice_rows = hn_ref.shape[0]

    def stage_next_slice(nxt_ref):
        s = jnp.clip(k - 1, 0, n_slices - 1)
        rows = pl.ds(pl.multiple_of(s * slice_rows, 16), slice_rows)
        row = mod_row(jnp.minimum(r, n_tiles - 1))
        x = hn_ref[...]
        hs_ref[rows, :] = x
        y = _rms(x) * ng_ref[...]
        y = y * (1.0 + sc_ref[pl.ds(row, 1), :]) + sh_ref[pl.ds(row, 1), :]
        nxt_ref[rows, :] = y.astype(BF16)

    def step(first, cur_ref, nxt_ref):
        if not first:
            stage_next_slice(nxt_ref)
        xn = cur_ref[...]
        a = jnp.dot(xn, wa_ref[...], preferred_element_type=F32)
        b = jnp.dot(xn, wb_ref[...], preferred_element_type=F32)
        g = (a * jax.nn.sigmoid(a) * b).astype(BF16)
        part = jnp.dot(g, wo_ref[...], preferred_element_type=F32)
        part = part * (0.5 * gt_ref[pl.ds(mod_row(r - 1), 1), :])
        o_ref[...] = (hs_ref[...] if first else o_ref[...]) + part

    pl.when((r == 0) & (k > 0))(functools.partial(stage_next_slice, xn0_ref))
    for first in (True, False):
        for par, (nxt_ref, cur_ref) in enumerate(((xn0_ref, xn1_ref), (xn1_ref, xn0_ref))):
            cond = ((k == 0) if first else (k > 0)) & (r > 0) & (lax.rem(r, 2) == par)
            pl.when(cond)(functools.partial(step, first, cur_ref, nxt_ref))

    if final_norm:
        @pl.when((k == nk - 1) & (r > 0))
        def _():
            o_ref[...] = _rms(o_ref[...]) * fg_ref[...]


def _ffn(h, mod, j0, norm_g, w_in, w_out, final_g, *, mod_row, final_norm, tm, tf, slice_rows):
    n, d = h.shape
    dff = w_out.shape[0]
    assert n % tm == 0 and dff % tf == 0 and tm % slice_rows == 0
    nk, n_tiles, n_slices = dff // tf, n // tm, tm // slice_rows
    assert n_slices <= nk - 1
    mod_spec = lambda j: pl.BlockSpec((MOD_ROWS, d), lambda r, k: (0, j))
    limit = _vmem_limit(
        [_nbytes((tm, d), F32), _nbytes((slice_rows, d), F32),
         _nbytes((d, tf), BF16) * 2, _nbytes((tf, d), BF16)],
        [2 * _nbytes((tm, d), BF16), _nbytes((tm, d), F32)],
        4 * _nbytes((tm, tf), F32) + _nbytes((tm, d), F32))

    def next_slice(r, k):
        tile = jnp.minimum(r, n_tiles - 1)
        return (tile * n_slices + jnp.clip(k - 1, 0, n_slices - 1), 0)

    chunk = lambda r, k: jnp.where(r == 0, 0, k)
    return pl.pallas_call(
        functools.partial(_ffn_kernel, mod_row=mod_row, final_norm=final_norm, n_slices=n_slices),
        out_shape=jax.ShapeDtypeStruct((n, d), F32),
        grid=(n_tiles + 1, nk),
        in_specs=[
            pl.BlockSpec((slice_rows, d), next_slice),
            mod_spec(j0), mod_spec(j0 + 1), mod_spec(j0 + 2),
            pl.BlockSpec((1, d), lambda r, k: (0, 0)),
            pl.BlockSpec((d, tf), lambda r, k: (0, chunk(r, k))),
            pl.BlockSpec((d, tf), lambda r, k: (0, chunk(r, k) + nk)),
            pl.BlockSpec((tf, d), lambda r, k: (chunk(r, k), 0)),
            pl.BlockSpec((1, d), lambda r, k: (0, 0)),
        ],
        out_specs=pl.BlockSpec((tm, d), lambda r, k: (jnp.maximum(r - 1, 0), 0)),
        scratch_shapes=[pltpu.VMEM((tm, d), BF16), pltpu.VMEM((tm, d), BF16), pltpu.VMEM((tm, d), F32)],
        compiler_params=pltpu.CompilerParams(
            dimension_semantics=("arbitrary", "arbitrary"), vmem_limit_bytes=limit),
        name="swiglu_half_step",
    )(h, mod, mod, mod, norm_g, w_in, w_in, w_out, final_g)


def _rope(x, cos, sin_signed):
    return x * cos + pltpu.roll(x, V7X_LANES // 2, 1) * sin_signed


def _proj_kernel(h_ref, sh_ref, sc_ref, ng_ref, win_ref, aqg_ref, akg_ref, bqg_ref, bkvg_ref,
                 wuq_ref, wukv_ref, ca_ref, sa_ref, cb_ref, sb_ref,
                 qa_ref, ka_ref, va_ref, qb_ref, kb_ref, vb_ref, cu_ref, *, mod_row, dims):
    a_q, a_kv, b_q, b_kv, b_heads, c_w = dims
    row = mod_row(pl.program_id(0))
    y = _rms(h_ref[...]) * ng_ref[...]
    y = y * (1.0 + sc_ref[pl.ds(row, 1), :]) + sh_ref[pl.ds(row, 1), :]
    u = jnp.dot(y.astype(BF16), win_ref[...], preferred_element_type=F32)

    ca, sa, cb, sb = ca_ref[...], sa_ref[...], cb_ref[...], sb_ref[...]
    hd = A_HEAD_DIM
    o_ak, o_av, o_bq = a_q, a_q + a_kv, a_q + 2 * a_kv
    o_bkv, o_bkr = o_bq + b_q, o_bq + b_q + b_kv
    o_cu = o_bkr + B_ROPE_DIM
    half = B_ROPE_DIM // 2

    qg = aqg_ref[...] * (hd ** -0.5 * LOG2E)
    for h in range(a_q // hd):
        x = _rms(u[:, h * hd:(h + 1) * hd]) * qg
        qa_ref[:, h * hd:(h + 1) * hd] = _rope(x, ca, sa).astype(BF16)
    kg = akg_ref[...]
    for h in range(a_kv // hd):
        x = _rms(u[:, o_ak + h * hd:o_ak + (h + 1) * hd]) * kg
        ka_ref[:, h * hd:(h + 1) * hd] = _rope(x, ca, sa).astype(BF16)
    va_ref[...] = u[:, o_av:o_av + a_kv].astype(BF16)

    xq = (_rms(u[:, o_bq:o_bq + b_q]) * bqg_ref[...]).astype(BF16)
    q = jnp.dot(xq, wuq_ref[...], preferred_element_type=F32)
    q_scale = (B_NOPE_DIM + B_ROPE_DIM) ** -0.5 * LOG2E
    xkv = (_rms(u[:, o_bkv:o_bkv + b_kv]) * bkvg_ref[...]).astype(BF16)
    kv = jnp.dot(xkv, wukv_ref[...], preferred_element_type=F32)
    kr = u[:, o_bkr:o_bkr + B_ROPE_DIM]
    kr = jnp.concatenate([kr, kr[:, half:], kr[:, :half]], axis=1)
    k_rope = _rope(kr, cb, sb).astype(BF16)
    for h in range(b_heads):
        o = h * B_QK_PAD
        qb_ref[:, o:o + B_NOPE_DIM] = (q[:, o:o + B_NOPE_DIM] * q_scale).astype(BF16)
        qr = _rope(q[:, o + B_NOPE_DIM:o + B_QK_PAD], cb, sb)
        qb_ref[:, o + B_NOPE_DIM:o + B_QK_PAD] = (qr * q_scale).astype(BF16)
        kb_ref[:, o:o + B_NOPE_DIM] = kv[:, h * B_NOPE_DIM:(h + 1) * B_NOPE_DIM].astype(BF16)
        kb_ref[:, o + B_NOPE_DIM:o + B_QK_PAD] = k_rope
    vb_ref[...] = kv[:, b_heads * B_NOPE_DIM:].astype(BF16)

    cu_ref[...] = u[:, o_cu:o_cu + c_w]


def _proj(h, mod, j0, norm_g, w_in_all, layer, a_q_g, a_k_g, b_q_g, b_kv_g, w_uq_p, w_ukv_p, tables,
          *, mod_row, table_block, dims, tm):
    n, d = h.shape
    a_q, a_kv, b_q, b_kv, b_heads, c_w = dims
    assert n % tm == 0
    ncols = w_in_all.shape[2]
    w_in_shape = w_in_all.shape[1:]
    mod_spec = lambda j: pl.BlockSpec((MOD_ROWS, d), lambda i: (0, j))
    tab_spec = pl.BlockSpec((tm, V7X_LANES), lambda i: (table_block(i), 0))
    row_spec = lambda w: pl.BlockSpec((tm, w), lambda i: (i, 0))
    out_widths = [(a_q, BF16), (a_kv, BF16), (a_kv, BF16), (b_heads * B_QK_PAD, BF16),
                  (b_heads * B_QK_PAD, BF16), (b_heads * B_V_DIM, BF16), (c_w, F32)]
    limit = _vmem_limit(
        [_nbytes((tm, d), F32)] + [_nbytes((tm, w), t) for w, t in out_widths]
        + [4 * _nbytes((tm, V7X_LANES), F32)],
        [_nbytes(w_in_shape, BF16), _nbytes(w_uq_p.shape, BF16), _nbytes(w_ukv_p.shape, BF16)],
        _nbytes((tm, ncols), F32) + _nbytes((tm, d), F32) + 2 * _nbytes((tm, 4 * B_QK_PAD), F32))
    return pl.pallas_call(
        functools.partial(_proj_kernel, mod_row=mod_row, dims=dims),
        out_shape=[jax.ShapeDtypeStruct((n, w), t) for w, t in out_widths],
        grid=(n // tm,),
        in_specs=[
            row_spec(d), mod_spec(j0), mod_spec(j0 + 1),
            _resident((1, d)),
            pl.BlockSpec((None,) + w_in_shape, lambda i: (layer, 0, 0), pipeline_mode=pl.Buffered(1)),
            _resident((1, A_HEAD_DIM)), _resident((1, A_HEAD_DIM)),
            _resident((1, b_q)), _resident((1, b_kv)),
            _resident(w_uq_p.shape), _resident(w_ukv_p.shape),
            tab_spec, tab_spec, tab_spec, tab_spec,
        ],
        out_specs=[row_spec(w) for w, _ in out_widths],
        compiler_params=pltpu.CompilerParams(
            dimension_semantics=("parallel",), vmem_limit_bytes=limit),
        name="mixer_in_proj",
    )(h, mod, mod, norm_g, w_in_all, a_q_g, a_k_g, b_q_g, b_kv_g, w_uq_p, w_ukv_p, *tables)


def _attn_kernel(*refs, n_stack, dk, dv, chunks, side_blocks):
    n_side, n_kv_refs = len(side_blocks), 2 * len(chunks)
    q_ref, kv_refs = refs[0], refs[1:1 + n_kv_refs]
    side_in = refs[1 + n_kv_refs:1 + n_kv_refs + n_side]
    o_ref, side_out = refs[1 + n_kv_refs + n_side], refs[2 + n_kv_refs + n_side:]
    tq = q_ref.shape[0]

    step_id = ((pl.program_id(0) * pl.num_programs(1) + pl.program_id(1)) * pl.num_programs(2)
               + pl.program_id(2))
    for w_ref, wo_ref, n_blocks in zip(side_in, side_out, side_blocks):
        def cast(w_ref=w_ref, wo_ref=wo_ref):
            wo_ref[...] = w_ref[...].astype(BF16)
        pl.when(step_id < n_blocks)(cast)

    q = jnp.concatenate([q_ref[:, h * dk:(h + 1) * dk] for h in range(n_stack)], axis=0)
    m_rows = n_stack * tq

    def step(k, v, carry):
        m, acc = carry
        s = lax.dot_general(q, k, (((1,), (1,)), ((), ())), preferred_element_type=F32)
        m_new = jnp.maximum(m, jnp.max(s, axis=-1, keepdims=True))
        p = jnp.exp2(s - m_new).astype(BF16)
        v_ones = jnp.concatenate([v, jnp.ones((v.shape[0], V7X_LANES), BF16)], axis=1)
        acc = jnp.exp2(m - m_new) * acc + jnp.dot(p, v_ones, preferred_element_type=F32)
        return m_new, acc

    carry = (jnp.full((m_rows, 1), -jnp.inf, F32), jnp.zeros((m_rows, dv + V7X_LANES), F32))
    for seg, chunk in enumerate(chunks):
        k_ref, v_ref = kv_refs[2 * seg], kv_refs[2 * seg + 1]
        n_chunks = k_ref.shape[0] // chunk
        for c in range(n_chunks):
            carry = step(k_ref[c * chunk:(c + 1) * chunk, :], v_ref[c * chunk:(c + 1) * chunk, :], carry)
    _, acc = carry
    out = acc[:, :dv] / acc[:, dv:dv + dv]
    for h in range(n_stack):
        o_ref[:, h * dv:(h + 1) * dv] = out[h * tq:(h + 1) * tq].astype(BF16)


def _attention(q, segments, side=(), *, batch, n_kv, n_stack, dk, dv, tq, chunk):
    nq = q.shape[0]
    lq = nq // batch
    assert lq % tq == 0 and dv == V7X_LANES
    tiles = lq // tq
    n_steps = batch * n_kv * tiles
    in_specs = [pl.BlockSpec((tq, n_stack * dk), lambda b, g, i: (b * tiles + i, g))]
    args, chunks, kv_bytes = [q], [], []
    for k, v in segments:
        lk = k.shape[0] // batch
        chunks.append(min(chunk, lk))
        assert lk % chunks[-1] == 0
        in_specs.append(pl.BlockSpec((lk, dk), lambda b, g, i: (b, g)))
        in_specs.append(pl.BlockSpec((lk, dv), lambda b, g, i: (b, g)))
        args += [k, v]
        kv_bytes += [_nbytes((lk, dk), BF16), _nbytes((lk, dv), BF16)]
    out_shape = [jax.ShapeDtypeStruct((nq, n_kv * n_stack * dv), BF16)]
    out_specs = [pl.BlockSpec((tq, n_stack * dv), lambda b, g, i: (b * tiles + i, g))]
    side_blocks, side_bytes = [], []
    for w, layer, rows in side:
        _, w_rows, w_cols = w.shape
        n_blocks = w_rows // rows
        assert w_rows % rows == 0 and rows % 16 == 0 and n_blocks <= n_steps

        def block(b, g, i, n_blocks=n_blocks):
            return jnp.minimum((b * n_kv + g) * tiles + i, n_blocks - 1)

        in_specs.append(pl.BlockSpec((None, rows, w_cols),
                                     lambda b, g, i, layer=layer, block=block: (layer, block(b, g, i), 0)))
        out_specs.append(pl.BlockSpec((rows, w_cols), lambda b, g, i, block=block: (block(b, g, i), 0)))
        out_shape.append(jax.ShapeDtypeStruct((w_rows, w_cols), BF16))
        args.append(w)
        side_blocks.append(n_blocks)
        side_bytes += [_nbytes((rows, w_cols), F32), _nbytes((rows, w_cols), BF16)]
    m_rows = n_stack * tq
    limit = _vmem_limit(
        [_nbytes((tq, n_stack * dk), BF16), _nbytes((tq, n_stack * dv), BF16)] + kv_bytes + side_bytes,
        [], 24 * _nbytes((m_rows, max(chunks)), F32) + 4 * _nbytes((m_rows, V7X_LANES), F32))
    outs = pl.pallas_call(
        functools.partial(_attn_kernel, n_stack=n_stack, dk=dk, dv=dv, chunks=tuple(chunks),
                          side_blocks=tuple(side_blocks)),
        out_shape=out_shape,
        grid=(batch, n_kv, tiles),
        in_specs=in_specs,
        out_specs=out_specs,
        compiler_params=pltpu.CompilerParams(
            dimension_semantics=("arbitrary", "arbitrary", "arbitrary"), vmem_limit_bytes=limit),
        name="attention",
    )(*args)
    return outs[0], outs[1:]


def _out_kernel(h_ref, gt_ref, ya_ref, yb_ref, cu_ref, cup_ref, cun_ref, wp_ref, cs_ref, wo_ref,
                o_ref, ext_ref, pooled0_ref, pooled1_ref, *, mod_row, seg_len):
    i = pl.program_id(0)
    n_tiles = pl.num_programs(0) - 1
    tm = h_ref.shape[0]
    halo = V7X_SUBLANES
    gd = C_GROUP_DIM
    n_rows = tm + 2 * halo
    pitch = n_rows // V7X_SUBLANES

    def stage(pooled_ref):
        pos0 = lax.rem(jnp.minimum(i, n_tiles - 1) * tm, seg_len)
        top_ok, bot_ok = pos0 > 0, pos0 + tm < seg_len
        sub = lax.broadcasted_iota(jnp.int32, (V7X_SUBLANES, V7X_LANES), 0)
        top_ind = jnp.where((sub == 0) & jnp.logical_not(top_ok), 0.0, 1.0)
        bot_ind = jnp.where((sub == V7X_SUBLANES - 1) & jnp.logical_not(bot_ok), 0.0, 1.0)

        def extended(vregs):
            ext = {v: vregs[v] for v in range(pitch)}
            for v in range(-halo, 0):
                x = vregs[v + pitch]
                ext[v] = x if isinstance(x, float) else pltpu.roll(x, 1, 0)
            for v in range(pitch, pitch + halo):
                x = vregs[v - pitch]
                ext[v] = x if isinstance(x, float) else pltpu.roll(x, V7X_SUBLANES - 1, 0)
            return ext

        def window_sums(ext):
            c2 = {v: ext[v - 1] + ext[v] for v in range(-halo + 1, pitch + halo)}
            c4 = {v: c2[v - 1] + c2[v + 1] for v in range(-halo + 2, pitch + halo - 1)}
            c8 = {v: c4[v - 2] + c4[v + 2] for v in range(-halo + 4, pitch + halo - 3)}
            c16 = {v: c8[v - 4] + c8[v + 4] for v in range(0, pitch)}
            return dict(zip(C_WINDOWS, (c2, c4, c8, c16)))

        ind = [top_ind if v < halo else bot_ind if v >= pitch - halo else 1.0 for v in range(pitch)]
        counts = window_sums(extended(ind))
        for g, w in enumerate(C_WINDOWS):
            cols = slice(g * gd, (g + 1) * gd)
            ext_ref[g, 0:halo, :] = jnp.where(top_ok, cup_ref[:, cols], 0.0)
            ext_ref[g, halo:halo + tm, :] = cu_ref[:, cols]
            ext_ref[g, halo + tm:, :] = jnp.where(bot_ok, cun_ref[:, cols], 0.0)
            ext_g, pooled_g = ext_ref.at[g], pooled_ref.at[g]
            rows = [ext_g[pl.ds(v, V7X_SUBLANES, stride=pitch), :] for v in range(pitch)]
            sums = window_sums(extended(rows))[w]
            for v in range(pitch):
                cnt = counts[w][v]
                inv = 1.0 / cnt
                pooled_g[pl.ds(v, V7X_SUBLANES, stride=pitch), :] = sums[v] * inv - rows[v]

    def stage_and_mix(nxt_ref, cur_ref):
        a_w, b_w = ya_ref.shape[1], yb_ref.shape[1]
        stage(nxt_ref)
        yc = []
        for g in range(len(C_WINDOWS)):
            cols = slice(g * gd, (g + 1) * gd)
            pooled = cur_ref[g, halo:halo + tm, :].astype(BF16)
            mixed = jnp.dot(pooled, wp_ref[g], preferred_element_type=F32) * cs_ref[:, cols]
            yc.append(mixed.astype(BF16))
        y = jnp.dot(ya_ref[...], wo_ref[0:a_w, :], preferred_element_type=F32)
        y += jnp.dot(yb_ref[...], wo_ref[a_w:a_w + b_w, :], preferred_element_type=F32)
        y += jnp.dot(jnp.concatenate(yc, axis=1), wo_ref[a_w + b_w:, :], preferred_element_type=F32)
        o_ref[...] = h_ref[...] + gt_ref[pl.ds(mod_row(i - 1), 1), :] * y

    pl.when(i == 0)(functools.partial(stage, pooled0_ref))
    for par, (nxt_ref, cur_ref) in enumerate(((pooled0_ref, pooled1_ref), (pooled1_ref, pooled0_ref))):
        pl.when((i > 0) & (lax.rem(i, 2) == par))(functools.partial(stage_and_mix, nxt_ref, cur_ref))


def _mix_out(h, mod, j_gate, ya, yb, cu, w_pool, c_scale, w_out, *, mod_row, seg_len, tm):
    n, d = h.shape
    c_w = cu.shape[1]
    assert n % tm == 0 and seg_len % tm == 0
    halo = V7X_SUBLANES
    per_tile = tm // halo
    n_tiles, last = n // tm, n // halo - 1
    prev_spec = lambda w: pl.BlockSpec((tm, w), lambda i: (jnp.maximum(i - 1, 0), 0))
    staged = lambda i: jnp.minimum(i, n_tiles - 1)
    limit = _vmem_limit(
        [2 * _nbytes((tm, d), F32), _nbytes((tm, ya.shape[1]), BF16),
         _nbytes((tm, yb.shape[1]), BF16), _nbytes((tm, c_w), F32)],
        [_nbytes(w_out.shape, BF16), _nbytes(w_pool.shape, BF16), 3 * _nbytes((tm + 2 * halo, c_w), F32)],
        2 * _nbytes((tm, d), F32))
    group_rows = pltpu.VMEM((c_w // C_GROUP_DIM, tm + 2 * halo, C_GROUP_DIM), F32)
    return pl.pallas_call(
        functools.partial(_out_kernel, mod_row=mod_row, seg_len=seg_len),
        out_shape=jax.ShapeDtypeStruct((n, d), F32),
        grid=(n_tiles + 1,),
        in_specs=[
            prev_spec(d),
            pl.BlockSpec((MOD_ROWS, d), lambda i: (0, j_gate)),
            prev_spec(ya.shape[1]), prev_spec(yb.shape[1]),
            pl.BlockSpec((tm, c_w), lambda i: (staged(i), 0)),
            pl.BlockSpec((halo, c_w), lambda i: (jnp.maximum(staged(i) * per_tile - 1, 0), 0)),
            pl.BlockSpec((halo, c_w), lambda i: (jnp.minimum((staged(i) + 1) * per_tile, last), 0)),
            _resident(w_pool.shape), _resident((1, c_w)), _resident(w_out.shape),
        ],
        out_specs=prev_spec(d),
        scratch_shapes=[group_rows, group_rows, group_rows],
        compiler_params=pltpu.CompilerParams(
            dimension_semantics=("arbitrary",), vmem_limit_bytes=limit),
        name="mixer_out_proj",
    )(h, mod, ya, yb, cu, cu, cu, w_pool, c_scale, w_out)


def _rope_tables(seq, dim):
    n = dim // 4
    t = np.arange(seq)
    rows = (t // GRID_W).astype(np.float32)
    cols = (t % GRID_W).astype(np.float32)
    inv = (np.float32(ROPE_THETA) ** (-np.arange(n, dtype=np.float32) / np.float32(n))).astype(np.float32)
    ang = np.concatenate([rows[:, None] * inv[None, :], cols[:, None] * inv[None, :]], axis=-1)
    cos, sin = np.cos(ang).astype(np.float32), np.sin(ang).astype(np.float32)
    pad = np.zeros((seq, V7X_LANES - dim), np.float32)
    return (np.concatenate([cos, cos, pad], axis=-1), np.concatenate([-sin, sin, pad], axis=-1))


def _identity_tables(rows, dim):
    one = np.concatenate([np.ones((rows, dim), np.float32),
                          np.zeros((rows, V7X_LANES - dim), np.float32)], -1)
    return one, np.zeros((rows, V7X_LANES), np.float32)


def _w_uq_columns(heads):
    hd = B_NOPE_DIM + B_ROPE_DIM
    half = B_ROPE_DIM // 2
    out = []
    for h in range(heads):
        b = h * hd
        x1 = np.arange(b + B_NOPE_DIM, b + B_NOPE_DIM + half)
        x2 = x1 + half
        out += [np.arange(b, b + B_NOPE_DIM), x1, x2, x2, x1]
    return np.concatenate(out)


def _w_ukv_columns(heads):
    hd = B_NOPE_DIM + B_V_DIM
    ks = [np.arange(h * hd, h * hd + B_NOPE_DIM) for h in range(heads)]
    vs = [np.arange(h * hd + B_NOPE_DIM, (h + 1) * hd) for h in range(heads)]
    return np.concatenate(ks + vs)


def kernel(x, c, ctx, c_ctx, w_mod, b_mod, norm_g, ffn1_in, ffn1_out, w_in, a_q_g, a_k_g, b_q_g,
           b_kv_g, b_w_uq, b_w_ukv, c_w_pool, c_scale, w_out, ffn2_in, ffn2_out, final_g):
    batch, seq, d = x.shape
    ctx_len = ctx.shape[1]
    depth = w_mod.shape[0]
    assert batch + 1 <= MOD_ROWS
    b_q, b_kv = b_q_g.shape[1], b_kv_g.shape[1]
    b_heads = b_w_ukv.shape[2] // (B_NOPE_DIM + B_V_DIM)
    c_w = c_scale.shape[1]
    a_q = w_out.shape[1] - b_heads * B_V_DIM - c_w
    a_kv = a_q // A_GROUP
    dims = (a_q, a_kv, b_q, b_kv, b_heads, c_w)

    tm, tm_ctx = 512, 256
    tiles_per_seq = seq // tm
    lat_row = lambda i: lax.div(i, tiles_per_seq)
    ctx_row = lambda i: batch
    ffn_lat = dict(tm=1024, tf=512, slice_rows=128, mod_row=lambda t: lax.div(t, seq // 1024))
    ffn_ctx = dict(tm=batch * ctx_len, tf=512, slice_rows=128, mod_row=ctx_row)

    c_rows = jnp.concatenate([c, c_ctx[None, :], jnp.zeros((MOD_ROWS - batch - 1, d), F32)], axis=0)
    mod = _modulation(c_rows, w_mod, b_mod)

    tab_lat = _rope_tables(seq, A_HEAD_DIM) + _rope_tables(seq, B_ROPE_DIM)
    tab_ctx = _identity_tables(tm_ctx, A_HEAD_DIM) + _identity_tables(tm_ctx, B_ROPE_DIM)
    cols_uq, cols_ukv = _w_uq_columns(b_heads), _w_ukv_columns(b_heads)

    h = x.reshape(batch * seq, d)
    hc = ctx.reshape(batch * ctx_len, d)
    tq_a, tq_b = 512, 2048
    a_steps = batch * (a_kv // A_HEAD_DIM) * (seq // tq_a)
    b_steps = batch * b_heads * (seq // tq_b)
    rows_in, rows_out = ffn1_in.shape[1] // a_steps, 2 * ffn1_out.shape[1] // b_steps
    f1_in, f1_out = _layer_bf16(ffn1_in, 0), _layer_bf16(ffn1_out, 0)
    w_in_bf = w_in.astype(BF16)
    for i in range(depth):
        last = i == depth - 1
        ng = norm_g[i]
        w_uq_p = b_w_uq[i][:, cols_uq].astype(BF16)
        w_ukv_p = b_w_ukv[i][:, cols_ukv].astype(BF16)
        w_pool = c_w_pool[i].astype(BF16)
        side_a = [(ffn2_in, i, rows_in)] + ([] if last else [(ffn1_in, i + 1, rows_in)])
        side_b = [(ffn2_out, i, rows_out), (w_out, i, w_out.shape[1] // b_steps)]
        side_b += [] if last else [(ffn1_out, i + 1, rows_out)]
        gains = (a_q_g[i][None], a_k_g[i][None], b_q_g[i][None], b_kv_g[i][None])
        fg = final_g[None]

        ffn = functools.partial(_ffn, final_g=fg)
        h = ffn(h, mod[i], 0, ng[0:1], f1_in, f1_out, final_norm=False, **ffn_lat)
        hc = ffn(hc, mod[i], 0, ng[0:1], f1_in, f1_out, final_norm=False, **ffn_ctx)

        proj = functools.partial(_proj, norm_g=ng[1:2], w_in_all=w_in_bf, layer=i, a_q_g=gains[0], a_k_g=gains[1],
                                 b_q_g=gains[2], b_kv_g=gains[3], w_uq_p=w_uq_p, w_ukv_p=w_ukv_p,
                                 dims=dims)
        qa, ka, va, qb, kb, vb, cu = proj(h, mod[i], 3, tables=tab_lat, mod_row=lat_row,
                                          table_block=lambda t: lax.rem(t, tiles_per_seq), tm=tm)
        qac, kac, vac, qbc, kbc, vbc, cuc = proj(hc, mod[i], 3, tables=tab_ctx, mod_row=ctx_row,
                                                 table_block=lambda t: 0, tm=tm_ctx)

        attn_a = functools.partial(_attention, batch=batch, n_kv=a_kv // A_HEAD_DIM, n_stack=A_GROUP,
                                   dk=A_HEAD_DIM, dv=A_HEAD_DIM, tq=tq_a, chunk=256)
        attn_b = functools.partial(_attention, batch=batch, n_kv=b_heads, n_stack=1,
                                   dk=B_QK_PAD, dv=B_V_DIM, tq=tq_b, chunk=256)
        ya, cast_a = attn_a(qa, [(kac, vac), (ka, va)], side_a)
        yb, cast_b = attn_b(qb, [(kbc, vbc), (kb, vb)], side_b)
        f2_in, f2_out, w_o = cast_a[0], cast_b[0], cast_b[1]
        mix = functools.partial(_mix_out, w_pool=w_pool, c_scale=c_scale[i][None], w_out=w_o)
        h = mix(h, mod[i], 5, ya, yb, cu, mod_row=lat_row, seg_len=seq, tm=tm)
        h = ffn(h, mod[i], 6, ng[2:3], f2_in, f2_out, final_norm=last, **ffn_lat)
        if not last:
            yac, _ = attn_a(qac, [(kac, vac)], tq=ctx_len)
            ybc, _ = attn_b(qbc, [(kbc, vbc)], tq=ctx_len)
            hc = mix(hc, mod[i], 5, yac, ybc, cuc, mod_row=ctx_row, seg_len=ctx_len, tm=tm_ctx)
            hc = ffn(hc, mod[i], 6, ng[2:3], f2_in, f2_out, final_norm=False, **ffn_ctx)
            f1_in, f1_out = cast_a[1], cast_b[2]
    return h.reshape(batch, seq, d)
```

```python
import functools
import math

import jax
import jax.numpy as jnp
import numpy as np
from jax import lax
from jax.experimental import pallas as pl
from jax.experimental.pallas import tpu as pltpu

F32 = jnp.float32
BF16 = jnp.bfloat16

GRID_W = 64
ROPE_THETA = 10000.0
NORM_EPS = 1e-6
N_MOD = 9
A_HEAD_DIM = 128
A_GROUP = 4
B_NOPE_DIM = 128
B_ROPE_DIM = 64
B_V_DIM = 128
B_QK_PAD = 256
C_WINDOWS = (2, 4, 8, 16)
C_GROUP_DIM = 128
POOL_CHUNK_ROWS = 64
OUT_COL_BLOCK = 256
LOG2E = 1.4426950408889634

V7X_LANES = 128
V7X_SUBLANES = 8
V7X_VMEM_BYTES = 64 * 1024 * 1024
V7X_VMEM_RESERVE_BYTES = 8 * 1024 * 1024
MOD_ROWS = V7X_SUBLANES


def _nbytes(shape, dtype):
    return int(np.prod(shape)) * jnp.dtype(dtype).itemsize


def _vmem_limit(pipelined, resident, temps):
    need = 2 * sum(pipelined) + sum(resident) + temps
    return int(min(V7X_VMEM_BYTES - V7X_VMEM_RESERVE_BYTES, need + need // 4))


def _rms(x, eps=NORM_EPS):
    return x * lax.rsqrt(jnp.mean(x * x, axis=-1, keepdims=True) + eps)


def _resident(shape):
    nd = len(shape)
    return pl.BlockSpec(shape, lambda *_: (0,) * nd, pipeline_mode=pl.Buffered(1))


CAST_BLOCK_BYTES = 8 * 1024 * 1024


def _cast_kernel(w_ref, o_ref):
    o_ref[...] = w_ref[...].astype(BF16)


def _layer_bf16(w, layer):
    _, rows, cols = w.shape
    n_blocks = next(nb for nb in range(1, rows + 1)
                    if rows % nb == 0 and (rows // nb) % 16 == 0
                    and _nbytes((rows // nb, cols), F32) <= CAST_BLOCK_BYTES)
    tr = rows // n_blocks
    limit = _vmem_limit([_nbytes((tr, cols), F32), _nbytes((tr, cols), BF16)], [], 0)
    return pl.pallas_call(
        _cast_kernel,
        out_shape=jax.ShapeDtypeStruct((rows, cols), BF16),
        grid=(n_blocks,),
        in_specs=[pl.BlockSpec((None, tr, cols), lambda j: (layer, j, 0))],
        out_specs=pl.BlockSpec((tr, cols), lambda j: (j, 0)),
        compiler_params=pltpu.CompilerParams(
            dimension_semantics=("parallel",), vmem_limit_bytes=limit),
        name="weight_to_bf16",
    )(w)


def _mod_kernel(c_ref, w_ref, b_ref, o_ref):
    c = c_ref[...]
    act = (c * jax.nn.sigmoid(c)).astype(BF16)
    o_ref[...] = jnp.dot(act, w_ref[...].astype(BF16), preferred_element_type=F32) + b_ref[...]


def _modulation(c_rows, w_mod, b_mod, *, tn=1024):
    depth, d, nm = w_mod.shape
    assert nm % tn == 0
    limit = _vmem_limit([_nbytes((d, tn), F32), _nbytes((MOD_ROWS, tn), F32)],
                        [_nbytes((MOD_ROWS, d), F32)], _nbytes((d, tn), BF16))
    return pl.pallas_call(
        _mod_kernel,
        out_shape=jax.ShapeDtypeStruct((depth, MOD_ROWS, nm), F32),
        grid=(depth, nm // tn),
        in_specs=[
            pl.BlockSpec((MOD_ROWS, d), lambda l, j: (0, 0)),
            pl.BlockSpec((None, d, tn), lambda l, j: (l, 0, j)),
            pl.BlockSpec((None, 1, tn), lambda l, j: (l, 0, j)),
        ],
        out_specs=pl.BlockSpec((None, MOD_ROWS, tn), lambda l, j: (l, 0, j)),
        compiler_params=pltpu.CompilerParams(
            dimension_semantics=("parallel", "parallel"), vmem_limit_bytes=limit),
        name="modulation",
    )(c_rows, w_mod, b_mod.reshape(depth, 1, nm))


def _ffn_kernel(hn_ref, sh_ref, sc_ref, gt_ref, ng_ref, wa_ref, wb_ref, wo_ref, fg_ref,
                o_ref, xn0_ref, xn1_ref, hs_ref, *, mod_row, final_norm, n_slices):
    r, k = pl.program_id(0), pl.program_id(1)
    n_tiles, nk = pl.num_programs(0) - 1, pl.num_programs(1)
    slice_rows = hn_ref.shape[0]

    def stage_next_slice(nxt_ref):
        s = jnp.clip(k - 1, 0, n_slices - 1)
        rows = pl.ds(pl.multiple_of(s * slice_rows, 16), slice_rows)
        row = mod_row(jnp.minimum(r, n_tiles - 1))
        x = hn_ref[...]
        hs_ref[rows, :] = x
        y = _rms(x) * ng_ref[...]
        y = y * (1.0 + sc_ref[pl.ds(row, 1), :]) + sh_ref[pl.ds(row, 1), :]
        nxt_ref[rows, :] = y.astype(BF16)

    def step(first, cur_ref, nxt_ref):
        if not first:
            stage_next_slice(nxt_ref)
        xn = cur_ref[...]
        a = jnp.dot(xn, wa_ref[...], preferred_element_type=F32)
        b = jnp.dot(xn, wb_ref[...], preferred_element_type=F32)
        g = (a * jax.nn.sigmoid(a) * b).astype(BF16)
        part = jnp.dot(g, wo_ref[...], preferred_element_type=F32)
        part = part * (0.5 * gt_ref[pl.ds(mod_row(r - 1), 1), :])
        o_ref[...] = (hs_ref[...] if first else o_ref[...]) + part

    pl.when((r == 0) & (k > 0))(functools.partial(stage_next_slice, xn0_ref))
    for first in (True, False):
        for par, (nxt_ref, cur_ref) in enumerate(((xn0_ref, xn1_ref), (xn1_ref, xn0_ref))):
            cond = ((k == 0) if first else (k > 0)) & (r > 0) & (lax.rem(r, 2) == par)
            pl.when(cond)(functools.partial(step, first, cur_ref, nxt_ref))

    if final_norm:
        @pl.when((k == nk - 1) & (r > 0))
        def _():
            o_ref[...] = _rms(o_ref[...]) * fg_ref[...]


def _ffn(h, mod, j0, norm_g, w_in, w_out, final_g, *, mod_row, final_norm, tm, tf, slice_rows):
    n, d = h.shape
    dff = w_out.shape[0]
    assert n % tm == 0 and dff % tf == 0 and tm % slice_rows == 0
    nk, n_tiles, n_slices = dff // tf, n // tm, tm // slice_rows
    assert n_slices <= nk - 1
    mod_spec = lambda j: pl.BlockSpec((MOD_ROWS, d), lambda r, k: (0, j))
    limit = _vmem_limit(
        [_nbytes((tm, d), F32), _nbytes((slice_rows, d), F32),
         _nbytes((d, tf), BF16) * 2, _nbytes((tf, d), BF16)],
        [2 * _nbytes((tm, d), BF16), _nbytes((tm, d), F32)],
        4 * _nbytes((tm, tf), F32) + _nbytes((tm, d), F32))

    def next_slice(r, k):
        tile = jnp.minimum(r, n_tiles - 1)
        return (tile * n_slices + jnp.clip(k - 1, 0, n_slices - 1), 0)

    chunk = lambda r, k: jnp.where(r == 0, 0, k)
    return pl.pallas_call(
        functools.partial(_ffn_kernel, mod_row=mod_row, final_norm=final_norm, n_slices=n_slices),
        out_shape=jax.ShapeDtypeStruct((n, d), F32),
        grid=(n_tiles + 1, nk),
        in_specs=[
            pl.BlockSpec((slice_rows, d), next_slice),
            mod_spec(j0), mod_spec(j0 + 1), mod_spec(j0 + 2),
            pl.BlockSpec((1, d), lambda r, k: (0, 0)),
            pl.BlockSpec((d, tf), lambda r, k: (0, chunk(r, k))),
            pl.BlockSpec((d, tf), lambda r, k: (0, chunk(r, k) + nk)),
            pl.BlockSpec((tf, d), lambda r, k: (chunk(r, k), 0)),
            pl.BlockSpec((1, d), lambda r, k: (0, 0)),
        ],
        out_specs=pl.BlockSpec((tm, d), lambda r, k: (jnp.maximum(r - 1, 0), 0)),
        scratch_shapes=[pltpu.VMEM((tm, d), BF16), pltpu.VMEM((tm, d), BF16), pltpu.VMEM((tm, d), F32)],
        compiler_params=pltpu.CompilerParams(
            dimension_semantics=("arbitrary", "arbitrary"), vmem_limit_bytes=limit),
        name="swiglu_half_step",
    )(h, mod, mod, mod, norm_g, w_in, w_in, w_out, final_g)


def _rope(x, cos, sin_signed):
    return x * cos + pltpu.roll(x, V7X_LANES // 2, 1) * sin_signed


def _proj_kernel(h_ref, sh_ref, sc_ref, ng_ref, win_ref, aqg_ref, akg_ref, bqg_ref, bkvg_ref,
                 wuq_ref, wukv_ref, ca_ref, sa_ref, cb_ref, sb_ref,
                 qa_ref, ka_ref, va_ref, qb_ref, kb_ref, vb_ref, cu_ref, *, mod_row, dims):
    a_q, a_kv, b_q, b_kv, b_heads, c_w = dims
    row = mod_row(pl.program_id(0))
    y = _rms(h_ref[...]) * ng_ref[...]
    y = y * (1.0 + sc_ref[pl.ds(row, 1), :]) + sh_ref[pl.ds(row, 1), :]
    u = jnp.dot(y.astype(BF16), win_ref[...], preferred_element_type=F32)

    ca, sa, cb, sb = ca_ref[...], sa_ref[...], cb_ref[...], sb_ref[...]
    hd = A_HEAD_DIM
    o_ak, o_av, o_bq = a_q, a_q + a_kv, a_q + 2 * a_kv
    o_bkv, o_bkr = o_bq + b_q, o_bq + b_q + b_kv
    o_cu = o_bkr + B_ROPE_DIM
    half = B_ROPE_DIM // 2

    qg = aqg_ref[...] * (hd ** -0.5 * LOG2E)
    for h in range(a_q // hd):
        x = _rms(u[:, h * hd:(h + 1) * hd]) * qg
        qa_ref[:, h * hd:(h + 1) * hd] = _rope(x, ca, sa).astype(BF16)
    kg = akg_ref[...]
    for h in range(a_kv // hd):
        x = _rms(u[:, o_ak + h * hd:o_ak + (h + 1) * hd]) * kg
        ka_ref[:, h * hd:(h + 1) * hd] = _rope(x, ca, sa).astype(BF16)
    va_ref[...] = u[:, o_av:o_av + a_kv].astype(BF16)

    xq = (_rms(u[:, o_bq:o_bq + b_q]) * bqg_ref[...]).astype(BF16)
    q = jnp.dot(xq, wuq_ref[...], preferred_element_type=F32)
    q_scale = (B_NOPE_DIM + B_ROPE_DIM) ** -0.5 * LOG2E
    xkv = (_rms(u[:, o_bkv:o_bkv + b_kv]) * bkvg_ref[...]).astype(BF16)
    kv = jnp.dot(xkv, wukv_ref[...], preferred_element_type=F32)
    kr = u[:, o_bkr:o_bkr + B_ROPE_DIM]
    kr = jnp.concatenate([kr, kr[:, half:], kr[:, :half]], axis=1)
    k_rope = _rope(kr, cb, sb).astype(BF16)
    for h in range(b_heads):
        o = h * B_QK_PAD
        qb_ref[:, o:o + B_NOPE_DIM] = (q[:, o:o + B_NOPE_DIM] * q_scale).astype(BF16)
        qr = _rope(q[:, o + B_NOPE_DIM:o + B_QK_PAD], cb, sb)
        qb_ref[:, o + B_NOPE_DIM:o + B_QK_PAD] = (qr * q_scale).astype(BF16)
        kb_ref[:, o:o + B_NOPE_DIM] = kv[:, h * B_NOPE_DIM:(h + 1) * B_NOPE_DIM].astype(BF16)
        kb_ref[:, o + B_NOPE_DIM:o + B_QK_PAD] = k_rope
    vb_ref[...] = kv[:, b_heads * B_NOPE_DIM:].astype(BF16)

    cu_ref[...] = u[:, o_cu:o_cu + c_w]


def _proj(h, mod, j0, norm_g, w_in_all, layer, a_q_g, a_k_g, b_q_g, b_kv_g, w_uq_p, w_ukv_p, tables,
          *, mod_row, table_block, dims, tm):
    n, d = h.shape
    a_q, a_kv, b_q, b_kv, b_heads, c_w = dims
    assert n % tm == 0
    ncols = w_in_all.shape[2]
    w_in_shape = w_in_all.shape[1:]
    mod_spec = lambda j: pl.BlockSpec((MOD_ROWS, d), lambda i: (0, j))
    tab_spec = pl.BlockSpec((tm, V7X_LANES), lambda i: (table_block(i), 0))
    row_spec = lambda w: pl.BlockSpec((tm, w), lambda i: (i, 0))
    out_widths = [(a_q, BF16), (a_kv, BF16), (a_kv, BF16), (b_heads * B_QK_PAD, BF16),
                  (b_heads * B_QK_PAD, BF16), (b_heads * B_V_DIM, BF16), (c_w, F32)]
    limit = _vmem_limit(
        [_nbytes((tm, d), F32)] + [_nbytes((tm, w), t) for w, t in out_widths]
        + [4 * _nbytes((tm, V7X_LANES), F32)],
        [_nbytes(w_in_shape, BF16), _nbytes(w_uq_p.shape, BF16), _nbytes(w_ukv_p.shape, BF16)],
        _nbytes((tm, ncols), F32) + _nbytes((tm, d), F32) + 2 * _nbytes((tm, 4 * B_QK_PAD), F32))
    return pl.pallas_call(
        functools.partial(_proj_kernel, mod_row=mod_row, dims=dims),
        out_shape=[jax.ShapeDtypeStruct((n, w), t) for w, t in out_widths],
        grid=(n // tm,),
        in_specs=[
            row_spec(d), mod_spec(j0), mod_spec(j0 + 1),
            _resident((1, d)),
            pl.BlockSpec((None,) + w_in_shape, lambda i: (layer, 0, 0), pipeline_mode=pl.Buffered(1)),
            _resident((1, A_HEAD_DIM)), _resident((1, A_HEAD_DIM)),
            _resident((1, b_q)), _resident((1, b_kv)),
            _resident(w_uq_p.shape), _resident(w_ukv_p.shape),
            tab_spec, tab_spec, tab_spec, tab_spec,
        ],
        out_specs=[row_spec(w) for w, _ in out_widths],
        compiler_params=pltpu.CompilerParams(
            dimension_semantics=("parallel",), vmem_limit_bytes=limit),
        name="mixer_in_proj",
    )(h, mod, mod, norm_g, w_in_all, a_q_g, a_k_g, b_q_g, b_kv_g, w_uq_p, w_ukv_p, *tables)


def _attn_kernel(*refs, n_stack, dk, dv, chunks, side_blocks, sub_tiles):
    n_side, n_kv_refs = len(side_blocks), 2 * len(chunks)
    q_ref, kv_refs = refs[0], refs[1:1 + n_kv_refs]
    side_in = refs[1 + n_kv_refs:1 + n_kv_refs + n_side]
    o_ref, side_out = refs[1 + n_kv_refs + n_side], refs[2 + n_kv_refs + n_side:]
    tq = q_ref.shape[0] // sub_tiles

    step_id = ((pl.program_id(0) * pl.num_programs(1) + pl.program_id(1)) * pl.num_programs(2)
               + pl.program_id(2))
    for w_ref, wo_ref, n_blocks in zip(side_in, side_out, side_blocks):
        def cast(w_ref=w_ref, wo_ref=wo_ref):
            wo_ref[...] = w_ref[...].astype(BF16)
        pl.when(step_id < n_blocks)(cast)

    m_rows = n_stack * tq

    def step(q, k, v, carry):
        m, acc = carry
        s = lax.dot_general(q, k, (((1,), (1,)), ((), ())), preferred_element_type=F32)
        m_new = jnp.maximum(m, jnp.max(s, axis=-1, keepdims=True))
        p = jnp.exp2(s - m_new).astype(BF16)
        v_ones = jnp.concatenate([v, jnp.ones((v.shape[0], V7X_LANES), BF16)], axis=1)
        acc = jnp.exp2(m - m_new) * acc + jnp.dot(p, v_ones, preferred_element_type=F32)
        return m_new, acc

    for t in range(sub_tiles):
        rows = slice(t * tq, (t + 1) * tq)
        q = jnp.concatenate([q_ref[rows, h * dk:(h + 1) * dk] for h in range(n_stack)], axis=0)
        carry = (jnp.full((m_rows, 1), -jnp.inf, F32), jnp.zeros((m_rows, dv + V7X_LANES), F32))
        for seg, chunk in enumerate(chunks):
            k_ref, v_ref = kv_refs[2 * seg], kv_refs[2 * seg + 1]
            for c in range(k_ref.shape[0] // chunk):
                keys = slice(c * chunk, (c + 1) * chunk)
                carry = step(q, k_ref[keys, :], v_ref[keys, :], carry)
        _, acc = carry
        out = acc[:, :dv] / acc[:, dv:dv + dv]
        for h in range(n_stack):
            o_ref[rows, h * dv:(h + 1) * dv] = out[h * tq:(h + 1) * tq].astype(BF16)


def _attention(q, segments, side=(), *, batch, n_kv, n_stack, dk, dv, tq, chunk, sub_tiles=1):
    nq = q.shape[0]
    lq = nq // batch
    tq_blk = tq * sub_tiles
    assert lq % tq_blk == 0 and dv == V7X_LANES
    tiles = lq // tq_blk
    n_steps = batch * n_kv * tiles
    in_specs = [pl.BlockSpec((tq_blk, n_stack * dk), lambda b, g, i: (b * tiles + i, g))]
    args, chunks, kv_bytes = [q], [], []
    for k, v in segments:
        lk = k.shape[0] // batch
        chunks.append(min(chunk, lk))
        assert lk % chunks[-1] == 0
        in_specs.append(pl.BlockSpec((lk, dk), lambda b, g, i: (b, g)))
        in_specs.append(pl.BlockSpec((lk, dv), lambda b, g, i: (b, g)))
        args += [k, v]
        kv_bytes += [_nbytes((lk, dk), BF16), _nbytes((lk, dv), BF16)]
    out_shape = [jax.ShapeDtypeStruct((nq, n_kv * n_stack * dv), BF16)]
    out_specs = [pl.BlockSpec((tq_blk, n_stack * dv), lambda b, g, i: (b * tiles + i, g))]
    side_blocks, side_bytes = [], []
    for w, layer, rows in side:
        _, w_rows, w_cols = w.shape
        n_blocks = w_rows // rows
        assert w_rows % rows == 0 and rows % 16 == 0 and n_blocks <= n_steps

        def block(b, g, i, n_blocks=n_blocks):
            return jnp.minimum((b * n_kv + g) * tiles + i, n_blocks - 1)

        in_specs.append(pl.BlockSpec((None, rows, w_cols),
                                     lambda b, g, i, layer=layer, block=block: (layer, block(b, g, i), 0)))
        out_specs.append(pl.BlockSpec((rows, w_cols), lambda b, g, i, block=block: (block(b, g, i), 0)))
        out_shape.append(jax.ShapeDtypeStruct((w_rows, w_cols), BF16))
        args.append(w)
        side_blocks.append(n_blocks)
        side_bytes += [_nbytes((rows, w_cols), F32), _nbytes((rows, w_cols), BF16)]
    m_rows = n_stack * tq
    limit = _vmem_limit(
        [_nbytes((tq_blk, n_stack * dk), BF16), _nbytes((tq_blk, n_stack * dv), BF16)] + kv_bytes
        + side_bytes,
        [], 24 * _nbytes((m_rows, max(chunks)), F32) + 4 * _nbytes((m_rows, V7X_LANES), F32))
    outs = pl.pallas_call(
        functools.partial(_attn_kernel, n_stack=n_stack, dk=dk, dv=dv, chunks=tuple(chunks),
                          side_blocks=tuple(side_blocks), sub_tiles=sub_tiles),
        out_shape=out_shape,
        grid=(batch, n_kv, tiles),
        in_specs=in_specs,
        out_specs=out_specs,
        compiler_params=pltpu.CompilerParams(
            dimension_semantics=("arbitrary", "arbitrary", "arbitrary"), vmem_limit_bytes=limit),
        name="attention",
    )(*args)
    return outs[0], outs[1:]


def _out_kernel(h_ref, gt_ref, ya_ref, yb_ref, cu_ref, cup_ref, cun_ref, wp_ref, cs_ref, wo_ref,
                o_ref, ext_ref, pooled0_ref, pooled1_ref, *, mod_row, seg_len):
    i = pl.program_id(0)
    n_tiles = pl.num_programs(0) - 1
    tm = h_ref.shape[0]
    halo = V7X_SUBLANES
    gd = C_GROUP_DIM
    n_rows = tm + 2 * halo
    pitch = n_rows // V7X_SUBLANES

    def stage(pooled_ref):
        pos0 = lax.rem(jnp.minimum(i, n_tiles - 1) * tm, seg_len)
        top_ok, bot_ok = pos0 > 0, pos0 + tm < seg_len
        sub = lax.broadcasted_iota(jnp.int32, (V7X_SUBLANES, V7X_LANES), 0)
        top_ind = jnp.where((sub == 0) & jnp.logical_not(top_ok), 0.0, 1.0)
        bot_ind = jnp.where((sub == V7X_SUBLANES - 1) & jnp.logical_not(bot_ok), 0.0, 1.0)

        def extended(vregs):
            ext = {v: vregs[v] for v in range(pitch)}
            for v in range(-halo, 0):
                x = vregs[v + pitch]
                ext[v] = x if isinstance(x, float) else pltpu.roll(x, 1, 0)
            for v in range(pitch, pitch + halo):
                x = vregs[v - pitch]
                ext[v] = x if isinstance(x, float) else pltpu.roll(x, V7X_SUBLANES - 1, 0)
            return ext

        def window_sums(ext):
            c2 = {v: ext[v - 1] + ext[v] for v in range(-halo + 1, pitch + halo)}
            c4 = {v: c2[v - 1] + c2[v + 1] for v in range(-halo + 2, pitch + halo - 1)}
            c8 = {v: c4[v - 2] + c4[v + 2] for v in range(-halo + 4, pitch + halo - 3)}
            c16 = {v: c8[v - 4] + c8[v + 4] for v in range(0, pitch)}
            return dict(zip(C_WINDOWS, (c2, c4, c8, c16)))

        ind = [top_ind if v < halo else bot_ind if v >= pitch - halo else 1.0 for v in range(pitch)]
        counts = window_sums(extended(ind))
        for g, w in enumerate(C_WINDOWS):
            cols = slice(g * gd, (g + 1) * gd)
            ext_ref[g, 0:halo, :] = jnp.where(top_ok, cup_ref[:, cols], 0.0)
            ext_ref[g, halo:halo + tm, :] = cu_ref[:, cols]
            ext_ref[g, halo + tm:, :] = jnp.where(bot_ok, cun_ref[:, cols], 0.0)
            ext_g, pooled_g = ext_ref.at[g], pooled_ref.at[g]
            rows = [ext_g[pl.ds(v, V7X_SUBLANES, stride=pitch), :] for v in range(pitch)]
            sums = window_sums(extended(rows))[w]
            for v in range(pitch):
                cnt = counts[w][v]
                inv = 1.0 / cnt
                pooled_g[pl.ds(v, V7X_SUBLANES, stride=pitch), :] = sums[v] * inv - rows[v]

    def stage_and_mix(nxt_ref, cur_ref):
        a_w, b_w = ya_ref.shape[1], yb_ref.shape[1]
        stage(nxt_ref)
        yc = []
        for g in range(len(C_WINDOWS)):
            cols = slice(g * gd, (g + 1) * gd)
            pooled = cur_ref[g, halo:halo + tm, :].astype(BF16)
            mixed = jnp.dot(pooled, wp_ref[g], preferred_element_type=F32) * cs_ref[:, cols]
            yc.append(mixed.astype(BF16))
        y = jnp.dot(ya_ref[...], wo_ref[0:a_w, :], preferred_element_type=F32)
        y += jnp.dot(yb_ref[...], wo_ref[a_w:a_w + b_w, :], preferred_element_type=F32)
        y += jnp.dot(jnp.concatenate(yc, axis=1), wo_ref[a_w + b_w:, :], preferred_element_type=F32)
        o_ref[...] = h_ref[...] + gt_ref[pl.ds(mod_row(i - 1), 1), :] * y

    pl.when(i == 0)(functools.partial(stage, pooled0_ref))
    for par, (nxt_ref, cur_ref) in enumerate(((pooled0_ref, pooled1_ref), (pooled1_ref, pooled0_ref))):
        pl.when((i > 0) & (lax.rem(i, 2) == par))(functools.partial(stage_and_mix, nxt_ref, cur_ref))


def _mix_out(h, mod, j_gate, ya, yb, cu, w_pool, c_scale, w_out, *, mod_row, seg_len, tm):
    n, d = h.shape
    c_w = cu.shape[1]
    assert n % tm == 0 and seg_len % tm == 0
    halo = V7X_SUBLANES
    per_tile = tm // halo
    n_tiles, last = n // tm, n // halo - 1
    prev_spec = lambda w: pl.BlockSpec((tm, w), lambda i: (jnp.maximum(i - 1, 0), 0))
    staged = lambda i: jnp.minimum(i, n_tiles - 1)
    limit = _vmem_limit(
        [2 * _nbytes((tm, d), F32), _nbytes((tm, ya.shape[1]), BF16),
         _nbytes((tm, yb.shape[1]), BF16), _nbytes((tm, c_w), F32)],
        [_nbytes(w_out.shape, BF16), _nbytes(w_pool.shape, BF16), 3 * _nbytes((tm + 2 * halo, c_w), F32)],
        2 * _nbytes((tm, d), F32))
    group_rows = pltpu.VMEM((c_w // C_GROUP_DIM, tm + 2 * halo, C_GROUP_DIM), F32)
    return pl.pallas_call(
        functools.partial(_out_kernel, mod_row=mod_row, seg_len=seg_len),
        out_shape=jax.ShapeDtypeStruct((n, d), F32),
        grid=(n_tiles + 1,),
        in_specs=[
            prev_spec(d),
            pl.BlockSpec((MOD_ROWS, d), lambda i: (0, j_gate)),
            prev_spec(ya.shape[1]), prev_spec(yb.shape[1]),
            pl.BlockSpec((tm, c_w), lambda i: (staged(i), 0)),
            pl.BlockSpec((halo, c_w), lambda i: (jnp.maximum(staged(i) * per_tile - 1, 0), 0)),
            pl.BlockSpec((halo, c_w), lambda i: (jnp.minimum((staged(i) + 1) * per_tile, last), 0)),
            _resident(w_pool.shape), _resident((1, c_w)), _resident(w_out.shape),
        ],
        out_specs=prev_spec(d),
        scratch_shapes=[group_rows, group_rows, group_rows],
        compiler_params=pltpu.CompilerParams(
            dimension_semantics=("arbitrary",), vmem_limit_bytes=limit),
        name="mixer_out_proj",
    )(h, mod, ya, yb, cu, cu, cu, w_pool, c_scale, w_out)


def _rope_tables(seq, dim):
    n = dim // 4
    t = np.arange(seq)
    rows = (t // GRID_W).astype(np.float32)
    cols = (t % GRID_W).astype(np.float32)
    inv = (np.float32(ROPE_THETA) ** (-np.arange(n, dtype=np.float32) / np.float32(n))).astype(np.float32)
    ang = np.concatenate([rows[:, None] * inv[None, :], cols[:, None] * inv[None, :]], axis=-1)
    cos, sin = np.cos(ang).astype(np.float32), np.sin(ang).astype(np.float32)
    pad = np.zeros((seq, V7X_LANES - dim), np.float32)
    return (np.concatenate([cos, cos, pad], axis=-1), np.concatenate([-sin, sin, pad], axis=-1))


def _identity_tables(rows, dim):
    one = np.concatenate([np.ones((rows, dim), np.float32),
                          np.zeros((rows, V7X_LANES - dim), np.float32)], -1)
    return one, np.zeros((rows, V7X_LANES), np.float32)


def _w_uq_columns(heads):
    hd = B_NOPE_DIM + B_ROPE_DIM
    half = B_ROPE_DIM // 2
    out = []
    for h in range(heads):
        b = h * hd
        x1 = np.arange(b + B_NOPE_DIM, b + B_NOPE_DIM + half)
        x2 = x1 + half
        out += [np.arange(b, b + B_NOPE_DIM), x1, x2, x2, x1]
    return np.concatenate(out)


def _w_ukv_columns(heads):
    hd = B_NOPE_DIM + B_V_DIM
    ks = [np.arange(h * hd, h * hd + B_NOPE_DIM) for h in range(heads)]
    vs = [np.arange(h * hd + B_NOPE_DIM, (h + 1) * hd) for h in range(heads)]
    return np.concatenate(ks + vs)


def kernel(x, c, ctx, c_ctx, w_mod, b_mod, norm_g, ffn1_in, ffn1_out, w_in, a_q_g, a_k_g, b_q_g,
           b_kv_g, b_w_uq, b_w_ukv, c_w_pool, c_scale, w_out, ffn2_in, ffn2_out, final_g):
    batch, seq, d = x.shape
    ctx_len = ctx.shape[1]
    depth = w_mod.shape[0]
    assert batch + 1 <= MOD_ROWS
    b_q, b_kv = b_q_g.shape[1], b_kv_g.shape[1]
    b_heads = b_w_ukv.shape[2] // (B_NOPE_DIM + B_V_DIM)
    c_w = c_scale.shape[1]
    a_q = w_out.shape[1] - b_heads * B_V_DIM - c_w
    a_kv = a_q // A_GROUP
    dims = (a_q, a_kv, b_q, b_kv, b_heads, c_w)

    tm, tm_ctx = 512, 256
    tiles_per_seq = seq // tm
    lat_row = lambda i: lax.div(i, tiles_per_seq)
    ctx_row = lambda i: batch
    ffn_lat = dict(tm=1024, tf=512, slice_rows=128, mod_row=lambda t: lax.div(t, seq // 1024))
    ffn_ctx = dict(tm=batch * ctx_len, tf=512, slice_rows=128, mod_row=ctx_row)

    c_rows = jnp.concatenate([c, c_ctx[None, :], jnp.zeros((MOD_ROWS - batch - 1, d), F32)], axis=0)
    mod = _modulation(c_rows, w_mod, b_mod)

    tab_lat = _rope_tables(seq, A_HEAD_DIM) + _rope_tables(seq, B_ROPE_DIM)
    tab_ctx = _identity_tables(tm_ctx, A_HEAD_DIM) + _identity_tables(tm_ctx, B_ROPE_DIM)
    cols_uq, cols_ukv = _w_uq_columns(b_heads), _w_ukv_columns(b_heads)

    h = x.reshape(batch * seq, d)
    hc = ctx.reshape(batch * ctx_len, d)
    tq_a, tq_b, sub = 512, 2048, 2
    a_steps = batch * (a_kv // A_HEAD_DIM) * (seq // (tq_a * sub))
    b_steps = batch * b_heads * (seq // (tq_b * sub))
    rows_in, rows_out = ffn1_in.shape[1] // a_steps, 2 * ffn1_out.shape[1] // b_steps
    f1_in, f1_out = _layer_bf16(ffn1_in, 0), _layer_bf16(ffn1_out, 0)
    w_in_bf = w_in.astype(BF16)
    for i in range(depth):
        last = i == depth - 1
        ng = norm_g[i]
        w_uq_p = b_w_uq[i][:, cols_uq].astype(BF16)
        w_ukv_p = b_w_ukv[i][:, cols_ukv].astype(BF16)
        w_pool = c_w_pool[i].astype(BF16)
        side_a = [(ffn2_in, i, rows_in)] + ([] if last else [(ffn1_in, i + 1, rows_in)])
        side_b = [(ffn2_out, i, rows_out), (w_out, i, w_out.shape[1] // b_steps)]
        side_b += [] if last else [(ffn1_out, i + 1, rows_out)]
        gains = (a_q_g[i][None], a_k_g[i][None], b_q_g[i][None], b_kv_g[i][None])
        fg = final_g[None]

        ffn = functools.partial(_ffn, final_g=fg)
        h = ffn(h, mod[i], 0, ng[0:1], f1_in, f1_out, final_norm=False, **ffn_lat)
        hc = ffn(hc, mod[i], 0, ng[0:1], f1_in, f1_out, final_norm=False, **ffn_ctx)

        proj = functools.partial(_proj, norm_g=ng[1:2], w_in_all=w_in_bf, layer=i, a_q_g=gains[0], a_k_g=gains[1],
                                 b_q_g=gains[2], b_kv_g=gains[3], w_uq_p=w_uq_p, w_ukv_p=w_ukv_p,
                                 dims=dims)
        qa, ka, va, qb, kb, vb, cu = proj(h, mod[i], 3, tables=tab_lat, mod_row=lat_row,
                                          table_block=lambda t: lax.rem(t, tiles_per_seq), tm=tm)
        qac, kac, vac, qbc, kbc, vbc, cuc = proj(hc, mod[i], 3, tables=tab_ctx, mod_row=ctx_row,
                                                 table_block=lambda t: 0, tm=tm_ctx)

        attn_a = functools.partial(_attention, batch=batch, n_kv=a_kv // A_HEAD_DIM, n_stack=A_GROUP,
                                   dk=A_HEAD_DIM, dv=A_HEAD_DIM, tq=tq_a, chunk=256)
        attn_b = functools.partial(_attention, batch=batch, n_kv=b_heads, n_stack=1,
                                   dk=B_QK_PAD, dv=B_V_DIM, tq=tq_b, chunk=256)
        ya, cast_a = attn_a(qa, [(kac, vac), (ka, va)], side_a, sub_tiles=sub)
        yb, cast_b = attn_b(qb, [(kbc, vbc), (kb, vb)], side_b, sub_tiles=sub)
        f2_in, f2_out, w_o = cast_a[0], cast_b[0], cast_b[1]
        mix = functools.partial(_mix_out, w_pool=w_pool, c_scale=c_scale[i][None], w_out=w_o)
        h = mix(h, mod[i], 5, ya, yb, cu, mod_row=lat_row, seg_len=seq, tm=tm)
        h = ffn(h, mod[i], 6, ng[2:3], f2_in, f2_out, final_norm=last, **ffn_lat)
        if not last:
            yac, _ = attn_a(qac, [(kac, vac)], tq=ctx_len)
            ybc, _ = attn_b(qbc, [(kbc, vbc)], tq=ctx_len)
            hc = mix(hc, mod[i], 5, yac, ybc, cuc, mod_row=ctx_row, seg_len=ctx_len, tm=tm_ctx)
            hc = ffn(hc, mod[i], 6, ng[2:3], f2_in, f2_out, final_norm=False, **ffn_ctx)
            f1_in, f1_out = cast_a[1], cast_b[2]
    return h.reshape(batch, seq, d)
```

```python
import functools
import math

import jax
import jax.numpy as jnp
import numpy as np
from jax import lax
from jax.experimental import pallas as pl
from jax.experimental.pallas import tpu as pltpu

F32 = jnp.float32
BF16 = jnp.bfloat16

GRID_W = 64
ROPE_THETA = 10000.0
NORM_EPS = 1e-6
N_MOD = 9
A_HEAD_DIM = 128
A_GROUP = 4
B_NOPE_DIM = 128
B_ROPE_DIM = 64
B_V_DIM = 128
C_WINDOWS = (2, 4, 8, 16)
C_GROUP_DIM = 128
LOG2E = 1.4426950408889634

V7X_LANES = 128
V7X_SUBLANES = 8
V7X_MXU_DIM = 256
V7X_VMEM_BYTES = 64 * 1024 * 1024
B_QK_PAD = V7X_MXU_DIM
ATTN_KEY_CHUNK = V7X_MXU_DIM
ATTN_LIVE_TILES = 24
V7X_VMEM_RESERVE_BYTES = 8 * 1024 * 1024
MOD_ROWS = V7X_SUBLANES


def _nbytes(shape, dtype):
    return int(np.prod(shape)) * jnp.dtype(dtype).itemsize


def _vmem_limit(pipelined, resident, temps):
    need = 2 * sum(pipelined) + sum(resident) + temps
    return int(min(V7X_VMEM_BYTES - V7X_VMEM_RESERVE_BYTES, need + need // 4))


def _rms(x, eps=NORM_EPS):
    return x * lax.rsqrt(jnp.mean(x * x, axis=-1, keepdims=True) + eps)


def _resident(shape):
    nd = len(shape)
    return pl.BlockSpec(shape, lambda *_: (0,) * nd, pipeline_mode=pl.Buffered(1))


CAST_BLOCK_BYTES = 8 * 1024 * 1024


def _cast_kernel(w_ref, o_ref):
    o_ref[...] = w_ref[...].astype(BF16)


def _layer_bf16(w, layer):
    _, rows, cols = w.shape
    n_blocks = next(nb for nb in range(1, rows + 1)
                    if rows % nb == 0 and (rows // nb) % 16 == 0
                    and _nbytes((rows // nb, cols), F32) <= CAST_BLOCK_BYTES)
    tr = rows // n_blocks
    limit = _vmem_limit([_nbytes((tr, cols), F32), _nbytes((tr, cols), BF16)], [], 0)
    return pl.pallas_call(
        _cast_kernel,
        out_shape=jax.ShapeDtypeStruct((rows, cols), BF16),
        grid=(n_blocks,),
        in_specs=[pl.BlockSpec((None, tr, cols), lambda j: (layer, j, 0))],
        out_specs=pl.BlockSpec((tr, cols), lambda j: (j, 0)),
        compiler_params=pltpu.CompilerParams(
            dimension_semantics=("parallel",), vmem_limit_bytes=limit),
        name="weight_to_bf16",
    )(w)


def _mod_kernel(c_ref, w_ref, b_ref, o_ref):
    c = c_ref[...]
    act = (c * jax.nn.sigmoid(c)).astype(BF16)
    o_ref[...] = jnp.dot(act, w_ref[...].astype(BF16), preferred_element_type=F32) + b_ref[...]


def _modulation(c_rows, w_mod, b_mod, *, tn=1024):
    depth, d, nm = w_mod.shape
    assert nm % tn == 0
    limit = _vmem_limit([_nbytes((d, tn), F32), _nbytes((MOD_ROWS, tn), F32)],
                        [_nbytes((MOD_ROWS, d), F32)], _nbytes((d, tn), BF16))
    return pl.pallas_call(
        _mod_kernel,
        out_shape=jax.ShapeDtypeStruct((depth, MOD_ROWS, nm), F32),
        grid=(depth, nm // tn),
        in_specs=[
            pl.BlockSpec((MOD_ROWS, d), lambda l, j: (0, 0)),
            pl.BlockSpec((None, d, tn), lambda l, j: (l, 0, j)),
            pl.BlockSpec((None, 1, tn), lambda l, j: (l, 0, j)),
        ],
        out_specs=pl.BlockSpec((None, MOD_ROWS, tn), lambda l, j: (l, 0, j)),
        compiler_params=pltpu.CompilerParams(
            dimension_semantics=("parallel", "parallel"), vmem_limit_bytes=limit),
        name="modulation",
    )(c_rows, w_mod, b_mod.reshape(depth, 1, nm))


def _ffn_kernel(hn_ref, sh_ref, sc_ref, gt_ref, ng_ref, wa_ref, wb_ref, wo_ref, fg_ref,
                o_ref, xn0_ref, xn1_ref, hs_ref, *, mod_row, final_norm, n_slices):
    r, k = pl.program_id(0), pl.program_id(1)
    n_tiles, nk = pl.num_programs(0) - 1, pl.num_programs(1)
    slice_rows = hn_ref.shape[0]

    def stage_next_slice(nxt_ref):
        s = jnp.clip(k - 1, 0, n_slices - 1)
        rows = pl.ds(pl.multiple_of(s * slice_rows, 16), slice_rows)
        row = mod_row(jnp.minimum(r, n_tiles - 1))
        x = hn_ref[...]
        hs_ref[rows, :] = x
        y = _rms(x) * ng_ref[...]
        y = y * (1.0 + sc_ref[pl.ds(row, 1), :]) + sh_ref[pl.ds(row, 1), :]
        nxt_ref[rows, :] = y.astype(BF16)

    def step(first, cur_ref, nxt_ref):
        if not first:
            stage_next_slice(nxt_ref)
        xn = cur_ref[...]
        a = jnp.dot(xn, wa_ref[...], preferred_element_type=F32)
        b = jnp.dot(xn, wb_ref[...], preferred_element_type=F32)
        g = (a * jax.nn.sigmoid(a) * b).astype(BF16)
        part = jnp.dot(g, wo_ref[...], preferred_element_type=F32)
        part = part * (0.5 * gt_ref[pl.ds(mod_row(r - 1), 1), :])
        o_ref[...] = (hs_ref[...] if first else o_ref[...]) + part

    pl.when((r == 0) & (k > 0))(functools.partial(stage_next_slice, xn0_ref))
    for first in (True, False):
        for par, (nxt_ref, cur_ref) in enumerate(((xn0_ref, xn1_ref), (xn1_ref, xn0_ref))):
            cond = ((k == 0) if first else (k > 0)) & (r > 0) & (lax.rem(r, 2) == par)
            pl.when(cond)(functools.partial(step, first, cur_ref, nxt_ref))

    if final_norm:
        @pl.when((k == nk - 1) & (r > 0))
        def _():
            o_ref[...] = _rms(o_ref[...]) * fg_ref[...]


def _ffn(h, mod, j0, norm_g, w_in, w_out, final_g, *, mod_row, final_norm, tm, tf, slice_rows):
    n, d = h.shape
    dff = w_out.shape[0]
    assert n % tm == 0 and dff % tf == 0 and tm % slice_rows == 0
    nk, n_tiles, n_slices = dff // tf, n // tm, tm // slice_rows
    assert n_slices <= nk - 1
    mod_spec = lambda j: pl.BlockSpec((MOD_ROWS, d), lambda r, k: (0, j))
    limit = _vmem_limit(
        [_nbytes((tm, d), F32), _nbytes((slice_rows, d), F32),
         _nbytes((d, tf), BF16) * 2, _nbytes((tf, d), BF16)],
        [2 * _nbytes((tm, d), BF16), _nbytes((tm, d), F32)],
        4 * _nbytes((tm, tf), F32) + _nbytes((tm, d), F32))

    def next_slice(r, k):
        tile = jnp.minimum(r, n_tiles - 1)
        return (tile * n_slices + jnp.clip(k - 1, 0, n_slices - 1), 0)

    chunk = lambda r, k: jnp.where(r == 0, 0, k)
    return pl.pallas_call(
        functools.partial(_ffn_kernel, mod_row=mod_row, final_norm=final_norm, n_slices=n_slices),
        out_shape=jax.ShapeDtypeStruct((n, d), F32),
        grid=(n_tiles + 1, nk),
        in_specs=[
            pl.BlockSpec((slice_rows, d), next_slice),
            mod_spec(j0), mod_spec(j0 + 1), mod_spec(j0 + 2),
            pl.BlockSpec((1, d), lambda r, k: (0, 0)),
            pl.BlockSpec((d, tf), lambda r, k: (0, chunk(r, k))),
            pl.BlockSpec((d, tf), lambda r, k: (0, chunk(r, k) + nk)),
            pl.BlockSpec((tf, d), lambda r, k: (chunk(r, k), 0)),
            pl.BlockSpec((1, d), lambda r, k: (0, 0)),
        ],
        out_specs=pl.BlockSpec((tm, d), lambda r, k: (jnp.maximum(r - 1, 0), 0)),
        scratch_shapes=[pltpu.VMEM((tm, d), BF16), pltpu.VMEM((tm, d), BF16), pltpu.VMEM((tm, d), F32)],
        compiler_params=pltpu.CompilerParams(
            dimension_semantics=("arbitrary", "arbitrary"), vmem_limit_bytes=limit),
        name="swiglu_half_step",
    )(h, mod, mod, mod, norm_g, w_in, w_in, w_out, final_g)


def _rope(x, cos, sin_signed):
    return x * cos + pltpu.roll(x, V7X_LANES // 2, 1) * sin_signed


def _proj_kernel(h_ref, sh_ref, sc_ref, ng_ref, win_ref, aqg_ref, akg_ref, bqg_ref, bkvg_ref,
                 wuq_ref, wukv_ref, ca_ref, sa_ref, cb_ref, sb_ref,
                 qa_ref, ka_ref, va_ref, qb_ref, kb_ref, vb_ref, cu_ref, *, mod_row, dims):
    a_q, a_kv, b_q, b_kv, b_heads, c_w = dims
    row = mod_row(pl.program_id(0))
    y = _rms(h_ref[...]) * ng_ref[...]
    y = y * (1.0 + sc_ref[pl.ds(row, 1), :]) + sh_ref[pl.ds(row, 1), :]
    u = jnp.dot(y.astype(BF16), win_ref[...], preferred_element_type=F32)

    ca, sa, cb, sb = ca_ref[...], sa_ref[...], cb_ref[...], sb_ref[...]
    hd = A_HEAD_DIM
    o_ak, o_av, o_bq = a_q, a_q + a_kv, a_q + 2 * a_kv
    o_bkv, o_bkr = o_bq + b_q, o_bq + b_q + b_kv
    o_cu = o_bkr + B_ROPE_DIM
    half = B_ROPE_DIM // 2

    qg = aqg_ref[...] * (hd ** -0.5 * LOG2E)
    for h in range(a_q // hd):
        x = _rms(u[:, h * hd:(h + 1) * hd]) * qg
        qa_ref[:, h * hd:(h + 1) * hd] = _rope(x, ca, sa).astype(BF16)
    kg = akg_ref[...]
    for h in range(a_kv // hd):
        x = _rms(u[:, o_ak + h * hd:o_ak + (h + 1) * hd]) * kg
        ka_ref[:, h * hd:(h + 1) * hd] = _rope(x, ca, sa).astype(BF16)
    va_ref[...] = u[:, o_av:o_av + a_kv].astype(BF16)

    xq = (_rms(u[:, o_bq:o_bq + b_q]) * bqg_ref[...]).astype(BF16)
    q = jnp.dot(xq, wuq_ref[...], preferred_element_type=F32)
    q_scale = (B_NOPE_DIM + B_ROPE_DIM) ** -0.5 * LOG2E
    xkv = (_rms(u[:, o_bkv:o_bkv + b_kv]) * bkvg_ref[...]).astype(BF16)
    kv = jnp.dot(xkv, wukv_ref[...], preferred_element_type=F32)
    kr = u[:, o_bkr:o_bkr + B_ROPE_DIM]
    kr = jnp.concatenate([kr, kr[:, half:], kr[:, :half]], axis=1)
    k_rope = _rope(kr, cb, sb).astype(BF16)
    for h in range(b_heads):
        o = h * B_QK_PAD
        qb_ref[:, o:o + B_NOPE_DIM] = (q[:, o:o + B_NOPE_DIM] * q_scale).astype(BF16)
        qr = _rope(q[:, o + B_NOPE_DIM:o + B_QK_PAD], cb, sb)
        qb_ref[:, o + B_NOPE_DIM:o + B_QK_PAD] = (qr * q_scale).astype(BF16)
        kb_ref[:, o:o + B_NOPE_DIM] = kv[:, h * B_NOPE_DIM:(h + 1) * B_NOPE_DIM].astype(BF16)
        kb_ref[:, o + B_NOPE_DIM:o + B_QK_PAD] = k_rope
    vb_ref[...] = kv[:, b_heads * B_NOPE_DIM:].astype(BF16)

    cu_ref[...] = u[:, o_cu:o_cu + c_w]


def _proj(h, mod, j0, norm_g, w_in_all, layer, a_q_g, a_k_g, b_q_g, b_kv_g, w_uq_p, w_ukv_p, tables,
          *, mod_row, table_block, dims, tm):
    n, d = h.shape
    a_q, a_kv, b_q, b_kv, b_heads, c_w = dims
    assert n % tm == 0
    ncols = w_in_all.shape[2]
    w_in_shape = w_in_all.shape[1:]
    mod_spec = lambda j: pl.BlockSpec((MOD_ROWS, d), lambda i: (0, j))
    tab_spec = pl.BlockSpec((tm, V7X_LANES), lambda i: (table_block(i), 0))
    row_spec = lambda w: pl.BlockSpec((tm, w), lambda i: (i, 0))
    out_widths = [(a_q, BF16), (a_kv, BF16), (a_kv, BF16), (b_heads * B_QK_PAD, BF16),
                  (b_heads * B_QK_PAD, BF16), (b_heads * B_V_DIM, BF16), (c_w, F32)]
    limit = _vmem_limit(
        [_nbytes((tm, d), F32)] + [_nbytes((tm, w), t) for w, t in out_widths]
        + [4 * _nbytes((tm, V7X_LANES), F32)],
        [_nbytes(w_in_shape, BF16), _nbytes(w_uq_p.shape, BF16), _nbytes(w_ukv_p.shape, BF16)],
        _nbytes((tm, ncols), F32) + _nbytes((tm, d), F32) + 2 * _nbytes((tm, 4 * B_QK_PAD), F32))
    return pl.pallas_call(
        functools.partial(_proj_kernel, mod_row=mod_row, dims=dims),
        out_shape=[jax.ShapeDtypeStruct((n, w), t) for w, t in out_widths],
        grid=(n // tm,),
        in_specs=[
            row_spec(d), mod_spec(j0), mod_spec(j0 + 1),
            _resident((1, d)),
            pl.BlockSpec((None,) + w_in_shape, lambda i: (layer, 0, 0), pipeline_mode=pl.Buffered(1)),
            _resident((1, A_HEAD_DIM)), _resident((1, A_HEAD_DIM)),
            _resident((1, b_q)), _resident((1, b_kv)),
            _resident(w_uq_p.shape), _resident(w_ukv_p.shape),
            tab_spec, tab_spec, tab_spec, tab_spec,
        ],
        out_specs=[row_spec(w) for w, _ in out_widths],
        compiler_params=pltpu.CompilerParams(
            dimension_semantics=("parallel",), vmem_limit_bytes=limit),
        name="mixer_in_proj",
    )(h, mod, mod, norm_g, w_in_all, a_q_g, a_k_g, b_q_g, b_kv_g, w_uq_p, w_ukv_p, *tables)


def _attn_kernel(*refs, n_stack, dk, dv, chunks, side_blocks, sub_tiles):
    n_side, n_kv_refs = len(side_blocks), 2 * len(chunks)
    q_ref, kv_refs = refs[0], refs[1:1 + n_kv_refs]
    side_in = refs[1 + n_kv_refs:1 + n_kv_refs + n_side]
    o_ref, side_out = refs[1 + n_kv_refs + n_side], refs[2 + n_kv_refs + n_side:]
    tq = q_ref.shape[0] // sub_tiles

    step_id = ((pl.program_id(0) * pl.num_programs(1) + pl.program_id(1)) * pl.num_programs(2)
               + pl.program_id(2))
    for w_ref, wo_ref, n_blocks in zip(side_in, side_out, side_blocks):
        def cast(w_ref=w_ref, wo_ref=wo_ref):
            wo_ref[...] = w_ref[...].astype(BF16)
        pl.when(step_id < n_blocks)(cast)

    m_rows = n_stack * tq

    def step(q, k, v, carry):
        m, acc = carry
        s = lax.dot_general(q, k, (((1,), (1,)), ((), ())), preferred_element_type=F32)
        m_new = jnp.maximum(m, jnp.max(s, axis=-1, keepdims=True))
        p = jnp.exp2(s - m_new).astype(BF16)
        v_ones = jnp.concatenate([v, jnp.ones((v.shape[0], V7X_LANES), BF16)], axis=1)
        acc = jnp.exp2(m - m_new) * acc + jnp.dot(p, v_ones, preferred_element_type=F32)
        return m_new, acc

    for t in range(sub_tiles):
        rows = slice(t * tq, (t + 1) * tq)
        q = jnp.concatenate([q_ref[rows, h * dk:(h + 1) * dk] for h in range(n_stack)], axis=0)
        carry = (jnp.full((m_rows, 1), -jnp.inf, F32), jnp.zeros((m_rows, dv + V7X_LANES), F32))
        for seg, chunk in enumerate(chunks):
            k_ref, v_ref = kv_refs[2 * seg], kv_refs[2 * seg + 1]
            for c in range(k_ref.shape[0] // chunk):
                keys = slice(c * chunk, (c + 1) * chunk)
                carry = step(q, k_ref[keys, :], v_ref[keys, :], carry)
        _, acc = carry
        out = acc[:, :dv] / acc[:, dv:dv + dv]
        for h in range(n_stack):
            o_ref[rows, h * dv:(h + 1) * dv] = out[h * tq:(h + 1) * tq].astype(BF16)


def _attention(q, segments, side=(), *, batch, n_kv, n_stack, dk, dv, tq, chunk, sub_tiles=1):
    nq = q.shape[0]
    lq = nq // batch
    tq_blk = tq * sub_tiles
    assert lq % tq_blk == 0 and dv == V7X_LANES
    tiles = lq // tq_blk
    n_steps = batch * n_kv * tiles
    in_specs = [pl.BlockSpec((tq_blk, n_stack * dk), lambda b, g, i: (b * tiles + i, g))]
    args, chunks, kv_bytes = [q], [], []
    for k, v in segments:
        lk = k.shape[0] // batch
        chunks.append(min(chunk, lk))
        assert lk % chunks[-1] == 0
        in_specs.append(pl.BlockSpec((lk, dk), lambda b, g, i: (b, g)))
        in_specs.append(pl.BlockSpec((lk, dv), lambda b, g, i: (b, g)))
        args += [k, v]
        kv_bytes += [_nbytes((lk, dk), BF16), _nbytes((lk, dv), BF16)]
    out_shape = [jax.ShapeDtypeStruct((nq, n_kv * n_stack * dv), BF16)]
    out_specs = [pl.BlockSpec((tq_blk, n_stack * dv), lambda b, g, i: (b * tiles + i, g))]
    side_blocks, side_bytes = [], []
    for w, layer, rows in side:
        _, w_rows, w_cols = w.shape
        n_blocks = w_rows // rows
        assert w_rows % rows == 0 and rows % 16 == 0 and n_blocks <= n_steps

        def block(b, g, i, n_blocks=n_blocks):
            return jnp.minimum((b * n_kv + g) * tiles + i, n_blocks - 1)

        in_specs.append(pl.BlockSpec((None, rows, w_cols),
                                     lambda b, g, i, layer=layer, block=block: (layer, block(b, g, i), 0)))
        out_specs.append(pl.BlockSpec((rows, w_cols), lambda b, g, i, block=block: (block(b, g, i), 0)))
        out_shape.append(jax.ShapeDtypeStruct((w_rows, w_cols), BF16))
        args.append(w)
        side_blocks.append(n_blocks)
        side_bytes += [_nbytes((rows, w_cols), F32), _nbytes((rows, w_cols), BF16)]
    m_rows = n_stack * tq
    limit = _vmem_limit(
        [_nbytes((tq_blk, n_stack * dk), BF16), _nbytes((tq_blk, n_stack * dv), BF16)] + kv_bytes
        + side_bytes,
        [], ATTN_LIVE_TILES * _nbytes((m_rows, max(chunks)), F32) + 4 * _nbytes((m_rows, V7X_LANES), F32))
    outs = pl.pallas_call(
        functools.partial(_attn_kernel, n_stack=n_stack, dk=dk, dv=dv, chunks=tuple(chunks),
                          side_blocks=tuple(side_blocks), sub_tiles=sub_tiles),
        out_shape=out_shape,
        grid=(batch, n_kv, tiles),
        in_specs=in_specs,
        out_specs=out_specs,
        compiler_params=pltpu.CompilerParams(
            dimension_semantics=("arbitrary", "arbitrary", "arbitrary"), vmem_limit_bytes=limit),
        name="attention",
    )(*args)
    return outs[0], outs[1:]


def _out_kernel(h_ref, gt_ref, ya_ref, yb_ref, cu_ref, cup_ref, cun_ref, wp_ref, cs_ref, wo_ref,
                o_ref, ext_ref, pooled0_ref, pooled1_ref, *, mod_row, seg_len):
    i = pl.program_id(0)
    n_tiles = pl.num_programs(0) - 1
    tm = h_ref.shape[0]
    halo = V7X_SUBLANES
    gd = C_GROUP_DIM
    n_rows = tm + 2 * halo
    pitch = n_rows // V7X_SUBLANES

    def stage(pooled_ref):
        pos0 = lax.rem(jnp.minimum(i, n_tiles - 1) * tm, seg_len)
        top_ok, bot_ok = pos0 > 0, pos0 + tm < seg_len
        sub = lax.broadcasted_iota(jnp.int32, (V7X_SUBLANES, V7X_LANES), 0)
        top_ind = jnp.where((sub == 0) & jnp.logical_not(top_ok), 0.0, 1.0)
        bot_ind = jnp.where((sub == V7X_SUBLANES - 1) & jnp.logical_not(bot_ok), 0.0, 1.0)

        def extended(vregs):
            ext = {v: vregs[v] for v in range(pitch)}
            for v in range(-halo, 0):
                x = vregs[v + pitch]
                ext[v] = x if isinstance(x, float) else pltpu.roll(x, 1, 0)
            for v in range(pitch, pitch + halo):
                x = vregs[v - pitch]
                ext[v] = x if isinstance(x, float) else pltpu.roll(x, V7X_SUBLANES - 1, 0)
            return ext

        def window_sums(ext):
            c2 = {v: ext[v - 1] + ext[v] for v in range(-halo + 1, pitch + halo)}
            c4 = {v: c2[v - 1] + c2[v + 1] for v in range(-halo + 2, pitch + halo - 1)}
            c8 = {v: c4[v - 2] + c4[v + 2] for v in range(-halo + 4, pitch + halo - 3)}
            c16 = {v: c8[v - 4] + c8[v + 4] for v in range(0, pitch)}
            return dict(zip(C_WINDOWS, (c2, c4, c8, c16)))

        ind = [top_ind if v < halo else bot_ind if v >= pitch - halo else 1.0 for v in range(pitch)]
        counts = window_sums(extended(ind))
        for g, w in enumerate(C_WINDOWS):
            cols = slice(g * gd, (g + 1) * gd)
            ext_ref[g, 0:halo, :] = jnp.where(top_ok, cup_ref[:, cols], 0.0)
            ext_ref[g, halo:halo + tm, :] = cu_ref[:, cols]
            ext_ref[g, halo + tm:, :] = jnp.where(bot_ok, cun_ref[:, cols], 0.0)
            ext_g, pooled_g = ext_ref.at[g], pooled_ref.at[g]
            rows = [ext_g[pl.ds(v, V7X_SUBLANES, stride=pitch), :] for v in range(pitch)]
            sums = window_sums(extended(rows))[w]
            for v in range(pitch):
                cnt = counts[w][v]
                inv = 1.0 / cnt
                pooled_g[pl.ds(v, V7X_SUBLANES, stride=pitch), :] = sums[v] * inv - rows[v]

    def stage_and_mix(nxt_ref, cur_ref):
        a_w, b_w = ya_ref.shape[1], yb_ref.shape[1]
        stage(nxt_ref)
        yc = []
        for g in range(len(C_WINDOWS)):
            cols = slice(g * gd, (g + 1) * gd)
            pooled = cur_ref[g, halo:halo + tm, :].astype(BF16)
            mixed = jnp.dot(pooled, wp_ref[g], preferred_element_type=F32) * cs_ref[:, cols]
            yc.append(mixed.astype(BF16))
        y = jnp.dot(ya_ref[...], wo_ref[0:a_w, :], preferred_element_type=F32)
        y += jnp.dot(yb_ref[...], wo_ref[a_w:a_w + b_w, :], preferred_element_type=F32)
        y += jnp.dot(jnp.concatenate(yc, axis=1), wo_ref[a_w + b_w:, :], preferred_element_type=F32)
        o_ref[...] = h_ref[...] + gt_ref[pl.ds(mod_row(i - 1), 1), :] * y

    pl.when(i == 0)(functools.partial(stage, pooled0_ref))
    for par, (nxt_ref, cur_ref) in enumerate(((pooled0_ref, pooled1_ref), (pooled1_ref, pooled0_ref))):
        pl.when((i > 0) & (lax.rem(i, 2) == par))(functools.partial(stage_and_mix, nxt_ref, cur_ref))


def _mix_out(h, mod, j_gate, ya, yb, cu, w_pool, c_scale, w_out, *, mod_row, seg_len, tm):
    n, d = h.shape
    c_w = cu.shape[1]
    assert n % tm == 0 and seg_len % tm == 0
    halo = V7X_SUBLANES
    per_tile = tm // halo
    n_tiles, last = n // tm, n // halo - 1
    prev_spec = lambda w: pl.BlockSpec((tm, w), lambda i: (jnp.maximum(i - 1, 0), 0))
    staged = lambda i: jnp.minimum(i, n_tiles - 1)
    limit = _vmem_limit(
        [2 * _nbytes((tm, d), F32), _nbytes((tm, ya.shape[1]), BF16),
         _nbytes((tm, yb.shape[1]), BF16), _nbytes((tm, c_w), F32)],
        [_nbytes(w_out.shape, BF16), _nbytes(w_pool.shape, BF16), 3 * _nbytes((tm + 2 * halo, c_w), F32)],
        2 * _nbytes((tm, d), F32))
    group_rows = pltpu.VMEM((c_w // C_GROUP_DIM, tm + 2 * halo, C_GROUP_DIM), F32)
    return pl.pallas_call(
        functools.partial(_out_kernel, mod_row=mod_row, seg_len=seg_len),
        out_shape=jax.ShapeDtypeStruct((n, d), F32),
        grid=(n_tiles + 1,),
        in_specs=[
            prev_spec(d),
            pl.BlockSpec((MOD_ROWS, d), lambda i: (0, j_gate)),
            prev_spec(ya.shape[1]), prev_spec(yb.shape[1]),
            pl.BlockSpec((tm, c_w), lambda i: (staged(i), 0)),
            pl.BlockSpec((halo, c_w), lambda i: (jnp.maximum(staged(i) * per_tile - 1, 0), 0)),
            pl.BlockSpec((halo, c_w), lambda i: (jnp.minimum((staged(i) + 1) * per_tile, last), 0)),
            _resident(w_pool.shape), _resident((1, c_w)), _resident(w_out.shape),
        ],
        out_specs=prev_spec(d),
        scratch_shapes=[group_rows, group_rows, group_rows],
        compiler_params=pltpu.CompilerParams(
            dimension_semantics=("arbitrary",), vmem_limit_bytes=limit),
        name="mixer_out_proj",
    )(h, mod, ya, yb, cu, cu, cu, w_pool, c_scale, w_out)


def _rope_tables(seq, dim):
    n = dim // 4
    t = np.arange(seq)
    rows = (t // GRID_W).astype(np.float32)
    cols = (t % GRID_W).astype(np.float32)
    inv = (np.float32(ROPE_THETA) ** (-np.arange(n, dtype=np.float32) / np.float32(n))).astype(np.float32)
    ang = np.concatenate([rows[:, None] * inv[None, :], cols[:, None] * inv[None, :]], axis=-1)
    cos, sin = np.cos(ang).astype(np.float32), np.sin(ang).astype(np.float32)
    pad = np.zeros((seq, V7X_LANES - dim), np.float32)
    return (np.concatenate([cos, cos, pad], axis=-1), np.concatenate([-sin, sin, pad], axis=-1))


def _identity_tables(rows, dim):
    one = np.concatenate([np.ones((rows, dim), np.float32),
                          np.zeros((rows, V7X_LANES - dim), np.float32)], -1)
    return one, np.zeros((rows, V7X_LANES), np.float32)


def _w_uq_columns(heads):
    hd = B_NOPE_DIM + B_ROPE_DIM
    half = B_ROPE_DIM // 2
    out = []
    for h in range(heads):
        b = h * hd
        x1 = np.arange(b + B_NOPE_DIM, b + B_NOPE_DIM + half)
        x2 = x1 + half
        out += [np.arange(b, b + B_NOPE_DIM), x1, x2, x2, x1]
    return np.concatenate(out)


def _w_ukv_columns(heads):
    hd = B_NOPE_DIM + B_V_DIM
    ks = [np.arange(h * hd, h * hd + B_NOPE_DIM) for h in range(heads)]
    vs = [np.arange(h * hd + B_NOPE_DIM, (h + 1) * hd) for h in range(heads)]
    return np.concatenate(ks + vs)


def kernel(x, c, ctx, c_ctx, w_mod, b_mod, norm_g, ffn1_in, ffn1_out, w_in, a_q_g, a_k_g, b_q_g,
           b_kv_g, b_w_uq, b_w_ukv, c_w_pool, c_scale, w_out, ffn2_in, ffn2_out, final_g):
    batch, seq, d = x.shape
    ctx_len = ctx.shape[1]
    depth = w_mod.shape[0]
    assert batch + 1 <= MOD_ROWS
    b_q, b_kv = b_q_g.shape[1], b_kv_g.shape[1]
    b_heads = b_w_ukv.shape[2] // (B_NOPE_DIM + B_V_DIM)
    c_w = c_scale.shape[1]
    a_q = w_out.shape[1] - b_heads * B_V_DIM - c_w
    a_kv = a_q // A_GROUP
    dims = (a_q, a_kv, b_q, b_kv, b_heads, c_w)

    tm, tm_ctx = 512, 256
    tiles_per_seq = seq // tm
    lat_row = lambda i: lax.div(i, tiles_per_seq)
    ctx_row = lambda i: batch
    ffn_lat = dict(tm=1024, tf=512, slice_rows=128, mod_row=lambda t: lax.div(t, seq // 1024))
    ffn_ctx = dict(tm=batch * ctx_len, tf=512, slice_rows=128, mod_row=ctx_row)

    c_rows = jnp.concatenate([c, c_ctx[None, :], jnp.zeros((MOD_ROWS - batch - 1, d), F32)], axis=0)
    mod = _modulation(c_rows, w_mod, b_mod)

    tab_lat = _rope_tables(seq, A_HEAD_DIM) + _rope_tables(seq, B_ROPE_DIM)
    tab_ctx = _identity_tables(tm_ctx, A_HEAD_DIM) + _identity_tables(tm_ctx, B_ROPE_DIM)
    cols_uq, cols_ukv = _w_uq_columns(b_heads), _w_ukv_columns(b_heads)

    h = x.reshape(batch * seq, d)
    hc = ctx.reshape(batch * ctx_len, d)
    tq_a, tq_b, sub = 512, 2048, 2
    a_steps = batch * (a_kv // A_HEAD_DIM) * (seq // (tq_a * sub))
    b_steps = batch * b_heads * (seq // (tq_b * sub))
    rows_in, rows_out = ffn1_in.shape[1] // a_steps, 2 * ffn1_out.shape[1] // b_steps
    f1_in, f1_out = _layer_bf16(ffn1_in, 0), _layer_bf16(ffn1_out, 0)
    w_in_bf = w_in.astype(BF16)
    for i in range(depth):
        last = i == depth - 1
        ng = norm_g[i]
        w_uq_p = b_w_uq[i][:, cols_uq].astype(BF16)
        w_ukv_p = b_w_ukv[i][:, cols_ukv].astype(BF16)
        w_pool = c_w_pool[i].astype(BF16)
        side_a = [(ffn2_in, i, rows_in)] + ([] if last else [(ffn1_in, i + 1, rows_in)])
        side_b = [(ffn2_out, i, rows_out), (w_out, i, w_out.shape[1] // b_steps)]
        side_b += [] if last else [(ffn1_out, i + 1, rows_out)]
        gains = (a_q_g[i][None], a_k_g[i][None], b_q_g[i][None], b_kv_g[i][None])
        fg = final_g[None]

        ffn = functools.partial(_ffn, final_g=fg)
        h = ffn(h, mod[i], 0, ng[0:1], f1_in, f1_out, final_norm=False, **ffn_lat)
        hc = ffn(hc, mod[i], 0, ng[0:1], f1_in, f1_out, final_norm=False, **ffn_ctx)

        proj = functools.partial(_proj, norm_g=ng[1:2], w_in_all=w_in_bf, layer=i, a_q_g=gains[0], a_k_g=gains[1],
                                 b_q_g=gains[2], b_kv_g=gains[3], w_uq_p=w_uq_p, w_ukv_p=w_ukv_p,
                                 dims=dims)
        qa, ka, va, qb, kb, vb, cu = proj(h, mod[i], 3, tables=tab_lat, mod_row=lat_row,
                                          table_block=lambda t: lax.rem(t, tiles_per_seq), tm=tm)
        qac, kac, vac, qbc, kbc, vbc, cuc = proj(hc, mod[i], 3, tables=tab_ctx, mod_row=ctx_row,
                                                 table_block=lambda t: 0, tm=tm_ctx)

        attn_a = functools.partial(_attention, batch=batch, n_kv=a_kv // A_HEAD_DIM, n_stack=A_GROUP,
                                   dk=A_HEAD_DIM, dv=A_HEAD_DIM, tq=tq_a, chunk=ATTN_KEY_CHUNK)
        attn_b = functools.partial(_attention, batch=batch, n_kv=b_heads, n_stack=1,
                                   dk=B_QK_PAD, dv=B_V_DIM, tq=tq_b, chunk=ATTN_KEY_CHUNK)
        ya, cast_a = attn_a(qa, [(kac, vac), (ka, va)], side_a, sub_tiles=sub)
        yb, cast_b = attn_b(qb, [(kbc, vbc), (kb, vb)], side_b, sub_tiles=sub)
        f2_in, f2_out, w_o = cast_a[0], cast_b[0], cast_b[1]
        mix = functools.partial(_mix_out, w_pool=w_pool, c_scale=c_scale[i][None], w_out=w_o)
        h = mix(h, mod[i], 5, ya, yb, cu, mod_row=lat_row, seg_len=seq, tm=tm)
        h = ffn(h, mod[i], 6, ng[2:3], f2_in, f2_out, final_norm=last, **ffn_lat)
        if not last:
            yac, _ = attn_a(qac, [(kac, vac)], tq=ctx_len)
            ybc, _ = attn_b(qbc, [(kbc, vbc)], tq=ctx_len)
            hc = mix(hc, mod[i], 5, yac, ybc, cuc, mod_row=ctx_row, seg_len=ctx_len, tm=tm_ctx)
            hc = ffn(hc, mod[i], 6, ng[2:3], f2_in, f2_out, final_norm=False, **ffn_ctx)
            f1_in, f1_out = cast_a[1], cast_b[2]
    return h.reshape(batch, seq, d)
```

```python
import functools
import math

import jax
import jax.numpy as jnp
import numpy as np
from jax import lax
from jax.experimental import pallas as pl
from jax.experimental.pallas import tpu as pltpu

F32 = jnp.float32
BF16 = jnp.bfloat16

GRID_W = 64
ROPE_THETA = 10000.0
NORM_EPS = 1e-6
N_MOD = 9
A_HEAD_DIM = 128
A_GROUP = 4
B_NOPE_DIM = 128
B_ROPE_DIM = 64
B_V_DIM = 128
C_WINDOWS = (2, 4, 8, 16)
C_GROUP_DIM = 128
LOG2E = 1.4426950408889634

V7X_LANES = 128
V7X_SUBLANES = 8
V7X_MXU_DIM = 256
V7X_VMEM_BYTES = 64 * 1024 * 1024
B_QK_PAD = V7X_MXU_DIM
ATTN_KEY_CHUNK = V7X_MXU_DIM
ATTN_LIVE_TILES = 24
V7X_VMEM_RESERVE_BYTES = 8 * 1024 * 1024
MOD_ROWS = V7X_SUBLANES


def _nbytes(shape, dtype):
    return int(np.prod(shape)) * jnp.dtype(dtype).itemsize


def _vmem_limit(pipelined, resident, temps):
    need = 2 * sum(pipelined) + sum(resident) + temps
    return int(min(V7X_VMEM_BYTES - V7X_VMEM_RESERVE_BYTES, need + need // 4))


def _rms(x, eps=NORM_EPS):
    return x * lax.rsqrt(jnp.mean(x * x, axis=-1, keepdims=True) + eps)


def _resident(shape):
    nd = len(shape)
    return pl.BlockSpec(shape, lambda *_: (0,) * nd, pipeline_mode=pl.Buffered(1))


CAST_BLOCK_BYTES = 8 * 1024 * 1024


def _cast_kernel(w_ref, o_ref):
    o_ref[...] = w_ref[...].astype(BF16)


def _layer_bf16(w, layer):
    _, rows, cols = w.shape
    n_blocks = next(nb for nb in range(1, rows + 1)
                    if rows % nb == 0 and (rows // nb) % 16 == 0
                    and _nbytes((rows // nb, cols), F32) <= CAST_BLOCK_BYTES)
    tr = rows // n_blocks
    limit = _vmem_limit([_nbytes((tr, cols), F32), _nbytes((tr, cols), BF16)], [], 0)
    return pl.pallas_call(
        _cast_kernel,
        out_shape=jax.ShapeDtypeStruct((rows, cols), BF16),
        grid=(n_blocks,),
        in_specs=[pl.BlockSpec((None, tr, cols), lambda j: (layer, j, 0))],
        out_specs=pl.BlockSpec((tr, cols), lambda j: (j, 0)),
        compiler_params=pltpu.CompilerParams(
            dimension_semantics=("parallel",), vmem_limit_bytes=limit),
        name="weight_to_bf16",
    )(w)


def _mod_kernel(c_ref, w_ref, b_ref, o_ref):
    c = c_ref[...]
    act = (c * jax.nn.sigmoid(c)).astype(BF16)
    o_ref[...] = jnp.dot(act, w_ref[...].astype(BF16), preferred_element_type=F32) + b_ref[...]


def _modulation(c_rows, w_mod, b_mod, *, tn=1024):
    depth, d, nm = w_mod.shape
    assert nm % tn == 0
    limit = _vmem_limit([_nbytes((d, tn), F32), _nbytes((MOD_ROWS, tn), F32)],
                        [_nbytes((MOD_ROWS, d), F32)], _nbytes((d, tn), BF16))
    return pl.pallas_call(
        _mod_kernel,
        out_shape=jax.ShapeDtypeStruct((depth, MOD_ROWS, nm), F32),
        grid=(depth, nm // tn),
        in_specs=[
            pl.BlockSpec((MOD_ROWS, d), lambda l, j: (0, 0)),
            pl.BlockSpec((None, d, tn), lambda l, j: (l, 0, j)),
            pl.BlockSpec((None, 1, tn), lambda l, j: (l, 0, j)),
        ],
        out_specs=pl.BlockSpec((None, MOD_ROWS, tn), lambda l, j: (l, 0, j)),
        compiler_params=pltpu.CompilerParams(
            dimension_semantics=("parallel", "parallel"), vmem_limit_bytes=limit),
        name="modulation",
    )(c_rows, w_mod, b_mod.reshape(depth, 1, nm))


def _ffn_kernel(hn_ref, sh_ref, sc_ref, gt_ref, ng_ref, wa_ref, wb_ref, wo_ref, fg_ref,
                o_ref, xn0_ref, xn1_ref, hs_ref, *, mod_row, final_norm, n_slices):
    r, k = pl.program_id(0), pl.program_id(1)
    n_tiles, nk = pl.num_programs(0) - 1, pl.num_programs(1)
    slice_rows = hn_ref.shape[0]

    def stage_next_slice(nxt_ref):
        s = jnp.clip(k - 1, 0, n_slices - 1)
        rows = pl.ds(pl.multiple_of(s * slice_rows, 16), slice_rows)
        row = mod_row(jnp.minimum(r, n_tiles - 1))
        x = hn_ref[...]
        hs_ref[rows, :] = x
        y = _rms(x) * ng_ref[...]
        y = y * (1.0 + sc_ref[pl.ds(row, 1), :]) + sh_ref[pl.ds(row, 1), :]
        nxt_ref[rows, :] = y.astype(BF16)

    def step(first, cur_ref, nxt_ref):
        if not first:
            stage_next_slice(nxt_ref)
        xn = cur_ref[...]
        a = jnp.dot(xn, wa_ref[...], preferred_element_type=F32)
        b = jnp.dot(xn, wb_ref[...], preferred_element_type=F32)
        g = (a * jax.nn.sigmoid(a) * b).astype(BF16)
        part = jnp.dot(g, wo_ref[...], preferred_element_type=F32)
        part = part * (0.5 * gt_ref[pl.ds(mod_row(r - 1), 1), :])
        o_ref[...] = (hs_ref[...] if first else o_ref[...]) + part

    pl.when((r == 0) & (k > 0))(functools.partial(stage_next_slice, xn0_ref))
    for first in (True, False):
        for par, (nxt_ref, cur_ref) in enumerate(((xn0_ref, xn1_ref), (xn1_ref, xn0_ref))):
            cond = ((k == 0) if first else (k > 0)) & (r > 0) & (lax.rem(r, 2) == par)
            pl.when(cond)(functools.partial(step, first, cur_ref, nxt_ref))

    if final_norm:
        @pl.when((k == nk - 1) & (r > 0))
        def _():
            o_ref[...] = _rms(o_ref[...]) * fg_ref[...]


def _ffn(h, mod, j0, norm_g, w_in, w_out, final_g, *, mod_row, final_norm, tm, tf, slice_rows):
    n, d = h.shape
    dff = w_out.shape[0]
    assert n % tm == 0 and dff % tf == 0 and tm % slice_rows == 0
    nk, n_tiles, n_slices = dff // tf, n // tm, tm // slice_rows
    assert n_slices <= nk - 1
    mod_spec = lambda j: pl.BlockSpec((MOD_ROWS, d), lambda r, k: (0, j))
    limit = _vmem_limit(
        [_nbytes((tm, d), F32), _nbytes((slice_rows, d), F32),
         _nbytes((d, tf), BF16) * 2, _nbytes((tf, d), BF16)],
        [2 * _nbytes((tm, d), BF16), _nbytes((tm, d), F32)],
        4 * _nbytes((tm, tf), F32) + _nbytes((tm, d), F32))

    def next_slice(r, k):
        tile = jnp.minimum(r, n_tiles - 1)
        return (tile * n_slices + jnp.clip(k - 1, 0, n_slices - 1), 0)

    chunk = lambda r, k: jnp.where(r == 0, 0, k)
    return pl.pallas_call(
        functools.partial(_ffn_kernel, mod_row=mod_row, final_norm=final_norm, n_slices=n_slices),
        out_shape=jax.ShapeDtypeStruct((n, d), F32),
        grid=(n_tiles + 1, nk),
        in_specs=[
            pl.BlockSpec((slice_rows, d), next_slice),
            mod_spec(j0), mod_spec(j0 + 1), mod_spec(j0 + 2),
            pl.BlockSpec((1, d), lambda r, k: (0, 0)),
            pl.BlockSpec((d, tf), lambda r, k: (0, chunk(r, k))),
            pl.BlockSpec((d, tf), lambda r, k: (0, chunk(r, k) + nk)),
            pl.BlockSpec((tf, d), lambda r, k: (chunk(r, k), 0)),
            pl.BlockSpec((1, d), lambda r, k: (0, 0)),
        ],
        out_specs=pl.BlockSpec((tm, d), lambda r, k: (jnp.maximum(r - 1, 0), 0)),
        scratch_shapes=[pltpu.VMEM((tm, d), BF16), pltpu.VMEM((tm, d), BF16), pltpu.VMEM((tm, d), F32)],
        compiler_params=pltpu.CompilerParams(
            dimension_semantics=("arbitrary", "arbitrary"), vmem_limit_bytes=limit),
        name="swiglu_half_step",
    )(h, mod, mod, mod, norm_g, w_in, w_in, w_out, final_g)


def _rope(x, cos, sin_signed):
    return x * cos + pltpu.roll(x, V7X_LANES // 2, 1) * sin_signed


def _proj_kernel(h_ref, sh_ref, sc_ref, ng_ref, win_ref, aqg_ref, akg_ref, bqg_ref, bkvg_ref,
                 wuq_ref, wukv_ref, ca_ref, sa_ref, cb_ref, sb_ref,
                 qa_ref, ka_ref, va_ref, qb_ref, kb_ref, vb_ref, cu_ref, *, mod_row, dims):
    a_q, a_kv, b_q, b_kv, b_heads, c_w = dims
    row = mod_row(pl.program_id(0))
    y = _rms(h_ref[...]) * ng_ref[...]
    y = y * (1.0 + sc_ref[pl.ds(row, 1), :]) + sh_ref[pl.ds(row, 1), :]
    u = jnp.dot(y.astype(BF16), win_ref[...], preferred_element_type=F32)

    ca, sa, cb, sb = ca_ref[...], sa_ref[...], cb_ref[...], sb_ref[...]
    hd = A_HEAD_DIM
    o_ak, o_av, o_bq = a_q, a_q + a_kv, a_q + 2 * a_kv
    o_bkv, o_bkr = o_bq + b_q, o_bq + b_q + b_kv
    o_cu = o_bkr + B_ROPE_DIM
    half = B_ROPE_DIM // 2

    qg = aqg_ref[...] * (hd ** -0.5 * LOG2E)
    for h in range(a_q // hd):
        x = _rms(u[:, h * hd:(h + 1) * hd]) * qg
        qa_ref[:, h * hd:(h + 1) * hd] = _rope(x, ca, sa).astype(BF16)
    kg = akg_ref[...]
    for h in range(a_kv // hd):
        x = _rms(u[:, o_ak + h * hd:o_ak + (h + 1) * hd]) * kg
        ka_ref[:, h * hd:(h + 1) * hd] = _rope(x, ca, sa).astype(BF16)
    va_ref[...] = u[:, o_av:o_av + a_kv].astype(BF16)

    xq = (_rms(u[:, o_bq:o_bq + b_q]) * bqg_ref[...]).astype(BF16)
    q = jnp.dot(xq, wuq_ref[...], preferred_element_type=F32)
    q_scale = (B_NOPE_DIM + B_ROPE_DIM) ** -0.5 * LOG2E
    xkv = (_rms(u[:, o_bkv:o_bkv + b_kv]) * bkvg_ref[...]).astype(BF16)
    kv = jnp.dot(xkv, wukv_ref[...], preferred_element_type=F32)
    kr = u[:, o_bkr:o_bkr + B_ROPE_DIM]
    kr = jnp.concatenate([kr, kr[:, half:], kr[:, :half]], axis=1)
    k_rope = _rope(kr, cb, sb).astype(BF16)
    for h in range(b_heads):
        o = h * B_QK_PAD
        qb_ref[:, o:o + B_NOPE_DIM] = (q[:, o:o + B_NOPE_DIM] * q_scale).astype(BF16)
        qr = _rope(q[:, o + B_NOPE_DIM:o + B_QK_PAD], cb, sb)
        qb_ref[:, o + B_NOPE_DIM:o + B_QK_PAD] = (qr * q_scale).astype(BF16)
        kb_ref[:, o:o + B_NOPE_DIM] = kv[:, h * B_NOPE_DIM:(h + 1) * B_NOPE_DIM].astype(BF16)
        kb_ref[:, o + B_NOPE_DIM:o + B_QK_PAD] = k_rope
    vb_ref[...] = kv[:, b_heads * B_NOPE_DIM:].astype(BF16)

    cu_ref[...] = u[:, o_cu:o_cu + c_w]


def _proj(h, mod, j0, norm_g, w_in_all, layer, a_q_g, a_k_g, b_q_g, b_kv_g, w_uq_p, w_ukv_p, tables,
          *, mod_row, table_block, dims, tm):
    n, d = h.shape
    a_q, a_kv, b_q, b_kv, b_heads, c_w = dims
    assert n % tm == 0
    ncols = w_in_all.shape[2]
    w_in_shape = w_in_all.shape[1:]
    mod_spec = lambda j: pl.BlockSpec((MOD_ROWS, d), lambda i: (0, j))
    tab_spec = pl.BlockSpec((tm, V7X_LANES), lambda i: (table_block(i), 0))
    row_spec = lambda w: pl.BlockSpec((tm, w), lambda i: (i, 0))
    out_widths = [(a_q, BF16), (a_kv, BF16), (a_kv, BF16), (b_heads * B_QK_PAD, BF16),
                  (b_heads * B_QK_PAD, BF16), (b_heads * B_V_DIM, BF16), (c_w, F32)]
    limit = _vmem_limit(
        [_nbytes((tm, d), F32)] + [_nbytes((tm, w), t) for w, t in out_widths]
        + [4 * _nbytes((tm, V7X_LANES), F32)],
        [_nbytes(w_in_shape, BF16), _nbytes(w_uq_p.shape, BF16), _nbytes(w_ukv_p.shape, BF16)],
        _nbytes((tm, ncols), F32) + _nbytes((tm, d), F32) + 2 * _nbytes((tm, 4 * B_QK_PAD), F32))
    return pl.pallas_call(
        functools.partial(_proj_kernel, mod_row=mod_row, dims=dims),
        out_shape=[jax.ShapeDtypeStruct((n, w), t) for w, t in out_widths],
        grid=(n // tm,),
        in_specs=[
            row_spec(d), mod_spec(j0), mod_spec(j0 + 1),
            _resident((1, d)),
            pl.BlockSpec((None,) + w_in_shape, lambda i: (layer, 0, 0), pipeline_mode=pl.Buffered(1)),
            _resident((1, A_HEAD_DIM)), _resident((1, A_HEAD_DIM)),
            _resident((1, b_q)), _resident((1, b_kv)),
            _resident(w_uq_p.shape), _resident(w_ukv_p.shape),
            tab_spec, tab_spec, tab_spec, tab_spec,
        ],
        out_specs=[row_spec(w) for w, _ in out_widths],
        compiler_params=pltpu.CompilerParams(
            dimension_semantics=("parallel",), vmem_limit_bytes=limit),
        name="mixer_in_proj",
    )(h, mod, mod, norm_g, w_in_all, a_q_g, a_k_g, b_q_g, b_kv_g, w_uq_p, w_ukv_p, *tables)


def _attn_kernel(*refs, n_stack, dk, dv, chunks, side_blocks, sub_tiles):
    n_side, n_kv_refs = len(side_blocks), 2 * len(chunks)
    q_ref, kv_refs = refs[0], refs[1:1 + n_kv_refs]
    side_in = refs[1 + n_kv_refs:1 + n_kv_refs + n_side]
    o_ref, side_out = refs[1 + n_kv_refs + n_side], refs[2 + n_kv_refs + n_side:]
    tq = q_ref.shape[0] // sub_tiles

    step_id = ((pl.program_id(0) * pl.num_programs(1) + pl.program_id(1)) * pl.num_programs(2)
               + pl.program_id(2))
    for w_ref, wo_ref, n_blocks in zip(side_in, side_out, side_blocks):
        def cast(w_ref=w_ref, wo_ref=wo_ref):
            wo_ref[...] = w_ref[...].astype(BF16)
        pl.when(step_id < n_blocks)(cast)

    m_rows = n_stack * tq

    def step(q, k, v, carry):
        m, acc = carry
        s = lax.dot_general(q, k, (((1,), (1,)), ((), ())), preferred_element_type=F32)
        m_new = jnp.maximum(m, jnp.max(s, axis=-1, keepdims=True))
        p = jnp.exp2(s - m_new).astype(BF16)
        v_ones = jnp.concatenate([v, jnp.ones((v.shape[0], V7X_LANES), BF16)], axis=1)
        acc = jnp.exp2(m - m_new) * acc + jnp.dot(p, v_ones, preferred_element_type=F32)
        return m_new, acc

    for t in range(sub_tiles):
        rows = slice(t * tq, (t + 1) * tq)
        q = jnp.concatenate([q_ref[rows, h * dk:(h + 1) * dk] for h in range(n_stack)], axis=0)
        carry = (jnp.full((m_rows, 1), -jnp.inf, F32), jnp.zeros((m_rows, dv + V7X_LANES), F32))
        for seg, chunk in enumerate(chunks):
            k_ref, v_ref = kv_refs[2 * seg], kv_refs[2 * seg + 1]
            for c in range(k_ref.shape[0] // chunk):
                keys = slice(c * chunk, (c + 1) * chunk)
                carry = step(q, k_ref[keys, :], v_ref[keys, :], carry)
        _, acc = carry
        out = acc[:, :dv] / acc[:, dv:dv + dv]
        for h in range(n_stack):
            o_ref[rows, h * dv:(h + 1) * dv] = out[h * tq:(h + 1) * tq].astype(BF16)


def _attention(q, segments, side=(), *, batch, n_kv, n_stack, dk, dv, tq, chunk, sub_tiles=1):
    nq = q.shape[0]
    lq = nq // batch
    tq_blk = tq * sub_tiles
    assert lq % tq_blk == 0 and dv == V7X_LANES
    tiles = lq // tq_blk
    n_steps = batch * n_kv * tiles
    in_specs = [pl.BlockSpec((tq_blk, n_stack * dk), lambda b, g, i: (b * tiles + i, g))]
    args, chunks, kv_bytes = [q], [], []
    for k, v in segments:
        lk = k.shape[0] // batch
        chunks.append(min(chunk, lk))
        assert lk % chunks[-1] == 0
        in_specs.append(pl.BlockSpec((lk, dk), lambda b, g, i: (b, g)))
        in_specs.append(pl.BlockSpec((lk, dv), lambda b, g, i: (b, g)))
        args += [k, v]
        kv_bytes += [_nbytes((lk, dk), BF16), _nbytes((lk, dv), BF16)]
    out_shape = [jax.ShapeDtypeStruct((nq, n_kv * n_stack * dv), BF16)]
    out_specs = [pl.BlockSpec((tq_blk, n_stack * dv), lambda b, g, i: (b * tiles + i, g))]
    side_blocks, side_bytes = [], []
    for w, layer, rows in side:
        _, w_rows, w_cols = w.shape
        n_blocks = w_rows // rows
        assert w_rows % rows == 0 and rows % 16 == 0 and n_blocks <= n_steps

        def block(b, g, i, n_blocks=n_blocks):
            return jnp.minimum((b * n_kv + g) * tiles + i, n_blocks - 1)

        in_specs.append(pl.BlockSpec((None, rows, w_cols),
                                     lambda b, g, i, layer=layer, block=block: (layer, block(b, g, i), 0)))
        out_specs.append(pl.BlockSpec((rows, w_cols), lambda b, g, i, block=block: (block(b, g, i), 0)))
        out_shape.append(jax.ShapeDtypeStruct((w_rows, w_cols), BF16))
        args.append(w)
        side_blocks.append(n_blocks)
        side_bytes += [_nbytes((rows, w_cols), F32), _nbytes((rows, w_cols), BF16)]
    m_rows = n_stack * tq
    limit = _vmem_limit(
        [_nbytes((tq_blk, n_stack * dk), BF16), _nbytes((tq_blk, n_stack * dv), BF16)] + kv_bytes
        + side_bytes,
        [], ATTN_LIVE_TILES * _nbytes((m_rows, max(chunks)), F32) + 4 * _nbytes((m_rows, V7X_LANES), F32))
    outs = pl.pallas_call(
        functools.partial(_attn_kernel, n_stack=n_stack, dk=dk, dv=dv, chunks=tuple(chunks),
                          side_blocks=tuple(side_blocks), sub_tiles=sub_tiles),
        out_shape=out_shape,
        grid=(batch, n_kv, tiles),
        in_specs=in_specs,
        out_specs=out_specs,
        compiler_params=pltpu.CompilerParams(
            dimension_semantics=("arbitrary", "arbitrary", "arbitrary"), vmem_limit_bytes=limit),
        name="attention",
    )(*args)
    return outs[0], outs[1:]


def _out_kernel(h_ref, gt_ref, ya_ref, yb_ref, cu_ref, cup_ref, cun_ref, wp_ref, cs_ref, wo_ref,
                o_ref, ext_ref, pooled0_ref, pooled1_ref, *, mod_row, seg_len):
    i = pl.program_id(0)
    n_tiles = pl.num_programs(0) - 1
    tm = h_ref.shape[0]
    halo = V7X_SUBLANES
    gd = C_GROUP_DIM
    n_rows = tm + 2 * halo
    pitch = n_rows // V7X_SUBLANES

    def stage(pooled_ref):
        pos0 = lax.rem(jnp.minimum(i, n_tiles - 1) * tm, seg_len)
        top_ok, bot_ok = pos0 > 0, pos0 + tm < seg_len
        sub = lax.broadcasted_iota(jnp.int32, (V7X_SUBLANES, V7X_LANES), 0)
        top_ind = jnp.where((sub == 0) & jnp.logical_not(top_ok), 0.0, 1.0)
        bot_ind = jnp.where((sub == V7X_SUBLANES - 1) & jnp.logical_not(bot_ok), 0.0, 1.0)

        def extended(vregs):
            ext = {v: vregs[v] for v in range(pitch)}
            for v in range(-halo, 0):
                x = vregs[v + pitch]
                ext[v] = x if isinstance(x, float) else pltpu.roll(x, 1, 0)
            for v in range(pitch, pitch + halo):
                x = vregs[v - pitch]
                ext[v] = x if isinstance(x, float) else pltpu.roll(x, V7X_SUBLANES - 1, 0)
            return ext

        def window_sums(ext):
            c2 = {v: ext[v - 1] + ext[v] for v in range(-halo + 1, pitch + halo)}
            c4 = {v: c2[v - 1] + c2[v + 1] for v in range(-halo + 2, pitch + halo - 1)}
            c8 = {v: c4[v - 2] + c4[v + 2] for v in range(-halo + 4, pitch + halo - 3)}
            c16 = {v: c8[v - 4] + c8[v + 4] for v in range(0, pitch)}
            return dict(zip(C_WINDOWS, (c2, c4, c8, c16)))

        ind = [top_ind if v < halo else bot_ind if v >= pitch - halo else 1.0 for v in range(pitch)]
        counts = window_sums(extended(ind))
        for g, w in enumerate(C_WINDOWS):
            cols = slice(g * gd, (g + 1) * gd)
            ext_ref[g, 0:halo, :] = jnp.where(top_ok, cup_ref[:, cols], 0.0)
            ext_ref[g, halo:halo + tm, :] = cu_ref[:, cols]
            ext_ref[g, halo + tm:, :] = jnp.where(bot_ok, cun_ref[:, cols], 0.0)
            ext_g, pooled_g = ext_ref.at[g], pooled_ref.at[g]
            rows = [ext_g[pl.ds(v, V7X_SUBLANES, stride=pitch), :] for v in range(pitch)]
            sums = window_sums(extended(rows))[w]
            for v in range(pitch):
                cnt = counts[w][v]
                inv = 1.0 / cnt
                pooled_g[pl.ds(v, V7X_SUBLANES, stride=pitch), :] = sums[v] * inv - rows[v]

    def stage_and_mix(nxt_ref, cur_ref):
        a_w, b_w = ya_ref.shape[1], yb_ref.shape[1]
        stage(nxt_ref)
        yc = []
        for g in range(len(C_WINDOWS)):
            cols = slice(g * gd, (g + 1) * gd)
            pooled = cur_ref[g, halo:halo + tm, :].astype(BF16)
            mixed = jnp.dot(pooled, wp_ref[g], preferred_element_type=F32) * cs_ref[:, cols]
            yc.append(mixed.astype(BF16))
        y = jnp.dot(ya_ref[...], wo_ref[0:a_w, :], preferred_element_type=F32)
        y += jnp.dot(yb_ref[...], wo_ref[a_w:a_w + b_w, :], preferred_element_type=F32)
        y += jnp.dot(jnp.concatenate(yc, axis=1), wo_ref[a_w + b_w:, :], preferred_element_type=F32)
        o_ref[...] = h_ref[...] + gt_ref[pl.ds(mod_row(i - 1), 1), :] * y

    pl.when(i == 0)(functools.partial(stage, pooled0_ref))
    for par, (nxt_ref, cur_ref) in enumerate(((pooled0_ref, pooled1_ref), (pooled1_ref, pooled0_ref))):
        pl.when((i > 0) & (lax.rem(i, 2) == par))(functools.partial(stage_and_mix, nxt_ref, cur_ref))


def _mix_out(h, mod, j_gate, ya, yb, cu, w_pool, c_scale, w_out, *, mod_row, seg_len, tm):
    n, d = h.shape
    c_w = cu.shape[1]
    assert n % tm == 0 and seg_len % tm == 0
    halo = V7X_SUBLANES
    per_tile = tm // halo
    n_tiles, last = n // tm, n // halo - 1
    prev_spec = lambda w: pl.BlockSpec((tm, w), lambda i: (jnp.maximum(i - 1, 0), 0))
    staged = lambda i: jnp.minimum(i, n_tiles - 1)
    limit = _vmem_limit(
        [2 * _nbytes((tm, d), F32), _nbytes((tm, ya.shape[1]), BF16),
         _nbytes((tm, yb.shape[1]), BF16), _nbytes((tm, c_w), F32)],
        [_nbytes(w_out.shape, BF16), _nbytes(w_pool.shape, BF16), 3 * _nbytes((tm + 2 * halo, c_w), F32)],
        2 * _nbytes((tm, d), F32))
    group_rows = pltpu.VMEM((c_w // C_GROUP_DIM, tm + 2 * halo, C_GROUP_DIM), F32)
    return pl.pallas_call(
        functools.partial(_out_kernel, mod_row=mod_row, seg_len=seg_len),
        out_shape=jax.ShapeDtypeStruct((n, d), F32),
        grid=(n_tiles + 1,),
        in_specs=[
            prev_spec(d),
            pl.BlockSpec((MOD_ROWS, d), lambda i: (0, j_gate)),
            prev_spec(ya.shape[1]), prev_spec(yb.shape[1]),
            pl.BlockSpec((tm, c_w), lambda i: (staged(i), 0)),
            pl.BlockSpec((halo, c_w), lambda i: (jnp.maximum(staged(i) * per_tile - 1, 0), 0)),
            pl.BlockSpec((halo, c_w), lambda i: (jnp.minimum((staged(i) + 1) * per_tile, last), 0)),
            _resident(w_pool.shape), _resident((1, c_w)), _resident(w_out.shape),
        ],
        out_specs=prev_spec(d),
        scratch_shapes=[group_rows, group_rows, group_rows],
        compiler_params=pltpu.CompilerParams(
            dimension_semantics=("arbitrary",), vmem_limit_bytes=limit),
        name="mixer_out_proj",
    )(h, mod, ya, yb, cu, cu, cu, w_pool, c_scale, w_out)


def _rope_tables(seq, dim):
    n = dim // 4
    t = np.arange(seq)
    rows = (t // GRID_W).astype(np.float32)
    cols = (t % GRID_W).astype(np.float32)
    inv = (np.float32(ROPE_THETA) ** (-np.arange(n, dtype=np.float32) / np.float32(n))).astype(np.float32)
    ang = np.concatenate([rows[:, None] * inv[None, :], cols[:, None] * inv[None, :]], axis=-1)
    cos, sin = np.cos(ang).astype(np.float32), np.sin(ang).astype(np.float32)
    pad = np.zeros((seq, V7X_LANES - dim), np.float32)
    return (np.concatenate([cos, cos, pad], axis=-1), np.concatenate([-sin, sin, pad], axis=-1))


def _identity_tables(rows, dim):
    one = np.concatenate([np.ones((rows, dim), np.float32),
                          np.zeros((rows, V7X_LANES - dim), np.float32)], -1)
    return one, np.zeros((rows, V7X_LANES), np.float32)


def _w_uq_columns(heads):
    hd = B_NOPE_DIM + B_ROPE_DIM
    half = B_ROPE_DIM // 2
    out = []
    for h in range(heads):
        b = h * hd
        x1 = np.arange(b + B_NOPE_DIM, b + B_NOPE_DIM + half)
        x2 = x1 + half
        out += [np.arange(b, b + B_NOPE_DIM), x1, x2, x2, x1]
    return np.concatenate(out)


def _w_ukv_columns(heads):
    hd = B_NOPE_DIM + B_V_DIM
    ks = [np.arange(h * hd, h * hd + B_NOPE_DIM) for h in range(heads)]
    vs = [np.arange(h * hd + B_NOPE_DIM, (h + 1) * hd) for h in range(heads)]
    return np.concatenate(ks + vs)


def kernel(x, c, ctx, c_ctx, w_mod, b_mod, norm_g, ffn1_in, ffn1_out, w_in, a_q_g, a_k_g, b_q_g,
           b_kv_g, b_w_uq, b_w_ukv, c_w_pool, c_scale, w_out, ffn2_in, ffn2_out, final_g):
    batch, seq, d = x.shape
    ctx_len = ctx.shape[1]
    depth = w_mod.shape[0]
    assert batch + 1 <= MOD_ROWS
    b_q, b_kv = b_q_g.shape[1], b_kv_g.shape[1]
    b_heads = b_w_ukv.shape[2] // (B_NOPE_DIM + B_V_DIM)
    c_w = c_scale.shape[1]
    a_q = w_out.shape[1] - b_heads * B_V_DIM - c_w
    a_kv = a_q // A_GROUP
    dims = (a_q, a_kv, b_q, b_kv, b_heads, c_w)

    tm, tm_ctx = 512, ctx_len
    assert seq % 1024 == 0 and seq % GRID_W == 0 and ctx_len % 64 == 0
    tiles_per_seq = seq // tm
    lat_row = lambda i: lax.div(i, tiles_per_seq)
    ctx_row = lambda i: batch
    ffn_lat = dict(tm=1024, tf=512, slice_rows=128, mod_row=lambda t: lax.div(t, seq // 1024))
    ffn_ctx = dict(tm=batch * ctx_len, tf=512, slice_rows=128, mod_row=ctx_row)

    c_rows = jnp.concatenate([c, c_ctx[None, :], jnp.zeros((MOD_ROWS - batch - 1, d), F32)], axis=0)
    mod = _modulation(c_rows, w_mod, b_mod)

    tab_lat = _rope_tables(seq, A_HEAD_DIM) + _rope_tables(seq, B_ROPE_DIM)
    tab_ctx = _identity_tables(tm_ctx, A_HEAD_DIM) + _identity_tables(tm_ctx, B_ROPE_DIM)
    cols_uq, cols_ukv = _w_uq_columns(b_heads), _w_ukv_columns(b_heads)

    h = x.reshape(batch * seq, d)
    hc = ctx.reshape(batch * ctx_len, d)
    tq_a, tq_b, sub = 512, 2048, 2
    a_steps = batch * (a_kv // A_HEAD_DIM) * (seq // (tq_a * sub))
    b_steps = batch * b_heads * (seq // (tq_b * sub))
    rows_in, rows_out = ffn1_in.shape[1] // a_steps, 2 * ffn1_out.shape[1] // b_steps
    f1_in, f1_out = _layer_bf16(ffn1_in, 0), _layer_bf16(ffn1_out, 0)
    w_in_bf = w_in.astype(BF16)
    for i in range(depth):
        last = i == depth - 1
        ng = norm_g[i]
        w_uq_p = b_w_uq[i][:, cols_uq].astype(BF16)
        w_ukv_p = b_w_ukv[i][:, cols_ukv].astype(BF16)
        w_pool = c_w_pool[i].astype(BF16)
        side_a = [(ffn2_in, i, rows_in)] + ([] if last else [(ffn1_in, i + 1, rows_in)])
        side_b = [(ffn2_out, i, rows_out), (w_out, i, w_out.shape[1] // b_steps)]
        side_b += [] if last else [(ffn1_out, i + 1, rows_out)]
        gains = (a_q_g[i][None], a_k_g[i][None], b_q_g[i][None], b_kv_g[i][None])
        fg = final_g[None]

        ffn = functools.partial(_ffn, final_g=fg)
        h = ffn(h, mod[i], 0, ng[0:1], f1_in, f1_out, final_norm=False, **ffn_lat)
        hc = ffn(hc, mod[i], 0, ng[0:1], f1_in, f1_out, final_norm=False, **ffn_ctx)

        proj = functools.partial(_proj, norm_g=ng[1:2], w_in_all=w_in_bf, layer=i, a_q_g=gains[0], a_k_g=gains[1],
                                 b_q_g=gains[2], b_kv_g=gains[3], w_uq_p=w_uq_p, w_ukv_p=w_ukv_p,
                                 dims=dims)
        qa, ka, va, qb, kb, vb, cu = proj(h, mod[i], 3, tables=tab_lat, mod_row=lat_row,
                                          table_block=lambda t: lax.rem(t, tiles_per_seq), tm=tm)
        qac, kac, vac, qbc, kbc, vbc, cuc = proj(hc, mod[i], 3, tables=tab_ctx, mod_row=ctx_row,
                                                 table_block=lambda t: 0, tm=tm_ctx)

        attn_a = functools.partial(_attention, batch=batch, n_kv=a_kv // A_HEAD_DIM, n_stack=A_GROUP,
                                   dk=A_HEAD_DIM, dv=A_HEAD_DIM, tq=tq_a, chunk=ATTN_KEY_CHUNK)
        attn_b = functools.partial(_attention, batch=batch, n_kv=b_heads, n_stack=1,
                                   dk=B_QK_PAD, dv=B_V_DIM, tq=tq_b, chunk=ATTN_KEY_CHUNK)
        ya, cast_a = attn_a(qa, [(kac, vac), (ka, va)], side_a, sub_tiles=sub)
        yb, cast_b = attn_b(qb, [(kbc, vbc), (kb, vb)], side_b, sub_tiles=sub)
        f2_in, f2_out, w_o = cast_a[0], cast_b[0], cast_b[1]
        mix = functools.partial(_mix_out, w_pool=w_pool, c_scale=c_scale[i][None], w_out=w_o)
        h = mix(h, mod[i], 5, ya, yb, cu, mod_row=lat_row, seg_len=seq, tm=tm)
        h = ffn(h, mod[i], 6, ng[2:3], f2_in, f2_out, final_norm=last, **ffn_lat)
        if not last:
            yac, _ = attn_a(qac, [(kac, vac)], tq=ctx_len)
            ybc, _ = attn_b(qbc, [(kbc, vbc)], tq=ctx_len)
            hc = mix(hc, mod[i], 5, yac, ybc, cuc, mod_row=ctx_row, seg_len=ctx_len, tm=tm_ctx)
            hc = ffn(hc, mod[i], 6, ng[2:3], f2_in, f2_out, final_norm=False, **ffn_ctx)
            f1_in, f1_out = cast_a[1], cast_b[2]
    return h.reshape(batch, seq, d)
```

```python
import functools

import jax
import jax.numpy as jnp
import numpy as np
from jax import lax
from jax.experimental import pallas as pl
from jax.experimental.pallas import tpu as pltpu

F32 = jnp.float32
BF16 = jnp.bfloat16

GRID_W = 64
ROPE_THETA = 10000.0
NORM_EPS = 1e-6
A_HEAD_DIM = 128
A_GROUP = 4
B_NOPE_DIM = 128
B_ROPE_DIM = 64
B_V_DIM = 128
C_WINDOWS = (2, 4, 8, 16)
C_GROUP_DIM = 128
LOG2E = 1.4426950408889634

V7X_LANES = 128
V7X_SUBLANES = 8
V7X_MXU_DIM = 256
V7X_VMEM_BYTES = 64 * 1024 * 1024
V7X_VMEM_RESERVE_BYTES = 8 * 1024 * 1024

MOD_ROWS = V7X_SUBLANES
B_QK_PAD = V7X_MXU_DIM
ATTN_KEY_CHUNK = V7X_MXU_DIM
ATTN_LIVE_TILES = 24


def _nbytes(shape, dtype):
    return int(np.prod(shape)) * jnp.dtype(dtype).itemsize


def _vmem_limit(pipelined, resident, temps):
    need = 2 * sum(pipelined) + sum(resident) + temps
    return int(min(V7X_VMEM_BYTES - V7X_VMEM_RESERVE_BYTES, need + need // 4))


def _rms(x, eps=NORM_EPS):
    return x * lax.rsqrt(jnp.mean(x * x, axis=-1, keepdims=True) + eps)


def _resident(shape):
    nd = len(shape)
    return pl.BlockSpec(shape, lambda *_: (0,) * nd, pipeline_mode=pl.Buffered(1))


CAST_BLOCK_BYTES = 8 * 1024 * 1024


def _cast_kernel(w_ref, o_ref):
    o_ref[...] = w_ref[...].astype(BF16)


def _layer_bf16(w, layer):
    _, rows, cols = w.shape
    n_blocks = next(nb for nb in range(1, rows + 1)
                    if rows % nb == 0 and (rows // nb) % 16 == 0
                    and _nbytes((rows // nb, cols), F32) <= CAST_BLOCK_BYTES)
    tr = rows // n_blocks
    limit = _vmem_limit([_nbytes((tr, cols), F32), _nbytes((tr, cols), BF16)], [], 0)
    return pl.pallas_call(
        _cast_kernel,
        out_shape=jax.ShapeDtypeStruct((rows, cols), BF16),
        grid=(n_blocks,),
        in_specs=[pl.BlockSpec((None, tr, cols), lambda j: (layer, j, 0))],
        out_specs=pl.BlockSpec((tr, cols), lambda j: (j, 0)),
        compiler_params=pltpu.CompilerParams(
            dimension_semantics=("parallel",), vmem_limit_bytes=limit),
        name="weight_to_bf16",
    )(w)


def _mod_kernel(c_ref, w_ref, b_ref, o_ref):
    c = c_ref[...]
    act = (c * jax.nn.sigmoid(c)).astype(BF16)
    o_ref[...] = jnp.dot(act, w_ref[...].astype(BF16), preferred_element_type=F32) + b_ref[...]


def _modulation(c_rows, w_mod, b_mod, *, tn=1024):
    depth, d, nm = w_mod.shape
    assert nm % tn == 0
    limit = _vmem_limit([_nbytes((d, tn), F32), _nbytes((MOD_ROWS, tn), F32)],
                        [_nbytes((MOD_ROWS, d), F32)], _nbytes((d, tn), BF16))
    return pl.pallas_call(
        _mod_kernel,
        out_shape=jax.ShapeDtypeStruct((depth, MOD_ROWS, nm), F32),
        grid=(depth, nm // tn),
        in_specs=[
            pl.BlockSpec((MOD_ROWS, d), lambda l, j: (0, 0)),
            pl.BlockSpec((None, d, tn), lambda l, j: (l, 0, j)),
            pl.BlockSpec((None, 1, tn), lambda l, j: (l, 0, j)),
        ],
        out_specs=pl.BlockSpec((None, MOD_ROWS, tn), lambda l, j: (l, 0, j)),
        compiler_params=pltpu.CompilerParams(
            dimension_semantics=("parallel", "parallel"), vmem_limit_bytes=limit),
        name="modulation",
    )(c_rows, w_mod, b_mod.reshape(depth, 1, nm))


def _ffn_kernel(hn_ref, sh_ref, sc_ref, gt_ref, ng_ref, wa_ref, wb_ref, wo_ref, fg_ref,
                o_ref, xn0_ref, xn1_ref, hs_ref, *, mod_row, final_norm, n_slices):
    r, k = pl.program_id(0), pl.program_id(1)
    n_tiles, nk = pl.num_programs(0) - 1, pl.num_programs(1)
    slice_rows = hn_ref.shape[0]

    def stage_next_slice(nxt_ref):
        s = jnp.clip(k - 1, 0, n_slices - 1)
        rows = pl.ds(pl.multiple_of(s * slice_rows, 16), slice_rows)
        row = mod_row(jnp.minimum(r, n_tiles - 1))
        x = hn_ref[...]
        hs_ref[rows, :] = x
        y = _rms(x) * ng_ref[...]
        y = y * (1.0 + sc_ref[pl.ds(row, 1), :]) + sh_ref[pl.ds(row, 1), :]
        nxt_ref[rows, :] = y.astype(BF16)

    def step(first, cur_ref, nxt_ref):
        if not first:
            stage_next_slice(nxt_ref)
        xn = cur_ref[...]
        a = jnp.dot(xn, wa_ref[...], preferred_element_type=F32)
        b = jnp.dot(xn, wb_ref[...], preferred_element_type=F32)
        g = (a * jax.nn.sigmoid(a) * b).astype(BF16)
        part = jnp.dot(g, wo_ref[...], preferred_element_type=F32)
        part = part * (0.5 * gt_ref[pl.ds(mod_row(r - 1), 1), :])
        o_ref[...] = (hs_ref[...] if first else o_ref[...]) + part

    pl.when((r == 0) & (k > 0))(functools.partial(stage_next_slice, xn0_ref))
    for first in (True, False):
        for par, (nxt_ref, cur_ref) in enumerate(((xn0_ref, xn1_ref), (xn1_ref, xn0_ref))):
            cond = ((k == 0) if first else (k > 0)) & (r > 0) & (lax.rem(r, 2) == par)
            pl.when(cond)(functools.partial(step, first, cur_ref, nxt_ref))

    if final_norm:
        @pl.when((k == nk - 1) & (r > 0))
        def _():
            o_ref[...] = _rms(o_ref[...]) * fg_ref[...]


def _ffn(h, mod, j0, norm_g, w_in, w_out, final_g, *, mod_row, final_norm, tm, tf, slice_rows):
    n, d = h.shape
    dff = w_out.shape[0]
    assert n % tm == 0 and dff % tf == 0 and tm % slice_rows == 0
    nk, n_tiles, n_slices = dff // tf, n // tm, tm // slice_rows
    assert n_slices <= nk - 1
    mod_spec = lambda j: pl.BlockSpec((MOD_ROWS, d), lambda r, k: (0, j))
    limit = _vmem_limit(
        [_nbytes((tm, d), F32), _nbytes((slice_rows, d), F32),
         _nbytes((d, tf), BF16) * 2, _nbytes((tf, d), BF16)],
        [2 * _nbytes((tm, d), BF16), _nbytes((tm, d), F32)],
        4 * _nbytes((tm, tf), F32) + _nbytes((tm, d), F32))

    def next_slice(r, k):
        tile = jnp.minimum(r, n_tiles - 1)
        return (tile * n_slices + jnp.clip(k - 1, 0, n_slices - 1), 0)

    chunk = lambda r, k: jnp.where(r == 0, 0, k)
    return pl.pallas_call(
        functools.partial(_ffn_kernel, mod_row=mod_row, final_norm=final_norm, n_slices=n_slices),
        out_shape=jax.ShapeDtypeStruct((n, d), F32),
        grid=(n_tiles + 1, nk),
        in_specs=[
            pl.BlockSpec((slice_rows, d), next_slice),
            mod_spec(j0), mod_spec(j0 + 1), mod_spec(j0 + 2),
            pl.BlockSpec((1, d), lambda r, k: (0, 0)),
            pl.BlockSpec((d, tf), lambda r, k: (0, chunk(r, k))),
            pl.BlockSpec((d, tf), lambda r, k: (0, chunk(r, k) + nk)),
            pl.BlockSpec((tf, d), lambda r, k: (chunk(r, k), 0)),
            pl.BlockSpec((1, d), lambda r, k: (0, 0)),
        ],
        out_specs=pl.BlockSpec((tm, d), lambda r, k: (jnp.maximum(r - 1, 0), 0)),
        scratch_shapes=[pltpu.VMEM((tm, d), BF16), pltpu.VMEM((tm, d), BF16), pltpu.VMEM((tm, d), F32)],
        compiler_params=pltpu.CompilerParams(
            dimension_semantics=("arbitrary", "arbitrary"), vmem_limit_bytes=limit),
        name="swiglu_half_step",
    )(h, mod, mod, mod, norm_g, w_in, w_in, w_out, final_g)


def _rope(x, cos, sin_signed):
    return x * cos + pltpu.roll(x, V7X_LANES // 2, 1) * sin_signed


def _proj_kernel(h_ref, sh_ref, sc_ref, ng_ref, win_ref, aqg_ref, akg_ref, bqg_ref, bkvg_ref,
                 wuq_ref, wukv_ref, ca_ref, sa_ref, cb_ref, sb_ref,
                 qa_ref, ka_ref, va_ref, qb_ref, kb_ref, vb_ref, cu_ref, *, mod_row, dims):
    a_q, a_kv, b_q, b_kv, b_heads, c_w = dims
    row = mod_row(pl.program_id(0))
    y = _rms(h_ref[...]) * ng_ref[...]
    y = y * (1.0 + sc_ref[pl.ds(row, 1), :]) + sh_ref[pl.ds(row, 1), :]
    u = jnp.dot(y.astype(BF16), win_ref[...], preferred_element_type=F32)

    ca, sa, cb, sb = ca_ref[...], sa_ref[...], cb_ref[...], sb_ref[...]
    hd = A_HEAD_DIM
    o_ak, o_av, o_bq = a_q, a_q + a_kv, a_q + 2 * a_kv
    o_bkv, o_bkr = o_bq + b_q, o_bq + b_q + b_kv
    o_cu = o_bkr + B_ROPE_DIM
    half = B_ROPE_DIM // 2

    qg = aqg_ref[...] * (hd ** -0.5 * LOG2E)
    for h in range(a_q // hd):
        x = _rms(u[:, h * hd:(h + 1) * hd]) * qg
        qa_ref[:, h * hd:(h + 1) * hd] = _rope(x, ca, sa).astype(BF16)
    kg = akg_ref[...]
    for h in range(a_kv // hd):
        x = _rms(u[:, o_ak + h * hd:o_ak + (h + 1) * hd]) * kg
        ka_ref[:, h * hd:(h + 1) * hd] = _rope(x, ca, sa).astype(BF16)
    va_ref[...] = u[:, o_av:o_av + a_kv].astype(BF16)

    xq = (_rms(u[:, o_bq:o_bq + b_q]) * bqg_ref[...]).astype(BF16)
    q = jnp.dot(xq, wuq_ref[...], preferred_element_type=F32)
    q_scale = (B_NOPE_DIM + B_ROPE_DIM) ** -0.5 * LOG2E
    xkv = (_rms(u[:, o_bkv:o_bkv + b_kv]) * bkvg_ref[...]).astype(BF16)
    kv = jnp.dot(xkv, wukv_ref[...], preferred_element_type=F32)
    kr = u[:, o_bkr:o_bkr + B_ROPE_DIM]
    kr = jnp.concatenate([kr, kr[:, half:], kr[:, :half]], axis=1)
    k_rope = _rope(kr, cb, sb).astype(BF16)
    for h in range(b_heads):
        o = h * B_QK_PAD
        qb_ref[:, o:o + B_NOPE_DIM] = (q[:, o:o + B_NOPE_DIM] * q_scale).astype(BF16)
        qr = _rope(q[:, o + B_NOPE_DIM:o + B_QK_PAD], cb, sb)
        qb_ref[:, o + B_NOPE_DIM:o + B_QK_PAD] = (qr * q_scale).astype(BF16)
        kb_ref[:, o:o + B_NOPE_DIM] = kv[:, h * B_NOPE_DIM:(h + 1) * B_NOPE_DIM].astype(BF16)
        kb_ref[:, o + B_NOPE_DIM:o + B_QK_PAD] = k_rope
    vb_ref[...] = kv[:, b_heads * B_NOPE_DIM:].astype(BF16)

    cu_ref[...] = u[:, o_cu:o_cu + c_w]


def _proj(h, mod, j0, norm_g, w_in_all, layer, a_q_g, a_k_g, b_q_g, b_kv_g, w_uq_p, w_ukv_p, tables,
          *, mod_row, table_block, dims, tm):
    n, d = h.shape
    a_q, a_kv, b_q, b_kv, b_heads, c_w = dims
    assert n % tm == 0
    ncols = w_in_all.shape[2]
    w_in_shape = w_in_all.shape[1:]
    mod_spec = lambda j: pl.BlockSpec((MOD_ROWS, d), lambda i: (0, j))
    tab_spec = pl.BlockSpec((tm, V7X_LANES), lambda i: (table_block(i), 0))
    row_spec = lambda w: pl.BlockSpec((tm, w), lambda i: (i, 0))
    out_widths = [(a_q, BF16), (a_kv, BF16), (a_kv, BF16), (b_heads * B_QK_PAD, BF16),
                  (b_heads * B_QK_PAD, BF16), (b_heads * B_V_DIM, BF16), (c_w, F32)]
    limit = _vmem_limit(
        [_nbytes((tm, d), F32)] + [_nbytes((tm, w), t) for w, t in out_widths]
        + [4 * _nbytes((tm, V7X_LANES), F32)],
        [_nbytes(w_in_shape, BF16), _nbytes(w_uq_p.shape, BF16), _nbytes(w_ukv_p.shape, BF16)],
        _nbytes((tm, ncols), F32) + _nbytes((tm, d), F32) + 2 * _nbytes((tm, 4 * B_QK_PAD), F32))
    return pl.pallas_call(
        functools.partial(_proj_kernel, mod_row=mod_row, dims=dims),
        out_shape=[jax.ShapeDtypeStruct((n, w), t) for w, t in out_widths],
        grid=(n // tm,),
        in_specs=[
            row_spec(d), mod_spec(j0), mod_spec(j0 + 1),
            _resident((1, d)),
            pl.BlockSpec((None,) + w_in_shape, lambda i: (layer, 0, 0), pipeline_mode=pl.Buffered(1)),
            _resident((1, A_HEAD_DIM)), _resident((1, A_HEAD_DIM)),
            _resident((1, b_q)), _resident((1, b_kv)),
            _resident(w_uq_p.shape), _resident(w_ukv_p.shape),
            tab_spec, tab_spec, tab_spec, tab_spec,
        ],
        out_specs=[row_spec(w) for w, _ in out_widths],
        compiler_params=pltpu.CompilerParams(
            dimension_semantics=("parallel",), vmem_limit_bytes=limit),
        name="mixer_in_proj",
    )(h, mod, mod, norm_g, w_in_all, a_q_g, a_k_g, b_q_g, b_kv_g, w_uq_p, w_ukv_p, *tables)


def _attn_kernel(*refs, n_stack, dk, dv, chunks, side_blocks, sub_tiles):
    n_side, n_kv_refs = len(side_blocks), 2 * len(chunks)
    q_ref, kv_refs = refs[0], refs[1:1 + n_kv_refs]
    side_in = refs[1 + n_kv_refs:1 + n_kv_refs + n_side]
    o_ref, side_out = refs[1 + n_kv_refs + n_side], refs[2 + n_kv_refs + n_side:]
    tq = q_ref.shape[0] // sub_tiles

    step_id = ((pl.program_id(0) * pl.num_programs(1) + pl.program_id(1)) * pl.num_programs(2)
               + pl.program_id(2))
    for w_ref, wo_ref, n_blocks in zip(side_in, side_out, side_blocks):
        def cast(w_ref=w_ref, wo_ref=wo_ref):
            wo_ref[...] = w_ref[...].astype(BF16)
        pl.when(step_id < n_blocks)(cast)

    m_rows = n_stack * tq

    def step(q, k, v, carry):
        m, acc = carry
        s = lax.dot_general(q, k, (((1,), (1,)), ((), ())), preferred_element_type=F32)
        m_new = jnp.maximum(m, jnp.max(s, axis=-1, keepdims=True))
        p = jnp.exp2(s - m_new).astype(BF16)
        v_ones = jnp.concatenate([v, jnp.ones((v.shape[0], V7X_LANES), BF16)], axis=1)
        acc = jnp.exp2(m - m_new) * acc + jnp.dot(p, v_ones, preferred_element_type=F32)
        return m_new, acc

    for t in range(sub_tiles):
        rows = slice(t * tq, (t + 1) * tq)
        q = jnp.concatenate([q_ref[rows, h * dk:(h + 1) * dk] for h in range(n_stack)], axis=0)
        carry = (jnp.full((m_rows, 1), -jnp.inf, F32), jnp.zeros((m_rows, dv + V7X_LANES), F32))
        for seg, chunk in enumerate(chunks):
            k_ref, v_ref = kv_refs[2 * seg], kv_refs[2 * seg + 1]
            for c in range(k_ref.shape[0] // chunk):
                keys = slice(c * chunk, (c + 1) * chunk)
                carry = step(q, k_ref[keys, :], v_ref[keys, :], carry)
        _, acc = carry
        out = acc[:, :dv] / acc[:, dv:dv + dv]
        for h in range(n_stack):
            o_ref[rows, h * dv:(h + 1) * dv] = out[h * tq:(h + 1) * tq].astype(BF16)


def _attention(q, segments, side=(), *, batch, n_kv, n_stack, dk, dv, tq, chunk, sub_tiles=1):
    nq = q.shape[0]
    lq = nq // batch
    tq_blk = tq * sub_tiles
    assert lq % tq_blk == 0 and dv == V7X_LANES
    tiles = lq // tq_blk
    n_steps = batch * n_kv * tiles
    in_specs = [pl.BlockSpec((tq_blk, n_stack * dk), lambda b, g, i: (b * tiles + i, g))]
    args, chunks, kv_bytes = [q], [], []
    for k, v in segments:
        lk = k.shape[0] // batch
        chunks.append(min(chunk, lk))
        assert lk % chunks[-1] == 0
        in_specs.append(pl.BlockSpec((lk, dk), lambda b, g, i: (b, g)))
        in_specs.append(pl.BlockSpec((lk, dv), lambda b, g, i: (b, g)))
        args += [k, v]
        kv_bytes += [_nbytes((lk, dk), BF16), _nbytes((lk, dv), BF16)]
    out_shape = [jax.ShapeDtypeStruct((nq, n_kv * n_stack * dv), BF16)]
    out_specs = [pl.BlockSpec((tq_blk, n_stack * dv), lambda b, g, i: (b * tiles + i, g))]
    side_blocks, side_bytes = [], []
    for w, layer, rows in side:
        _, w_rows, w_cols = w.shape
        n_blocks = w_rows // rows
        assert w_rows % rows == 0 and rows % 16 == 0 and n_blocks <= n_steps

        def block(b, g, i, n_blocks=n_blocks):
            return jnp.minimum((b * n_kv + g) * tiles + i, n_blocks - 1)

        in_specs.append(pl.BlockSpec((None, rows, w_cols),
                                     lambda b, g, i, layer=layer, block=block: (layer, block(b, g, i), 0)))
        out_specs.append(pl.BlockSpec((rows, w_cols), lambda b, g, i, block=block: (block(b, g, i), 0)))
        out_shape.append(jax.ShapeDtypeStruct((w_rows, w_cols), BF16))
        args.append(w)
        side_blocks.append(n_blocks)
        side_bytes += [_nbytes((rows, w_cols), F32), _nbytes((rows, w_cols), BF16)]
    m_rows = n_stack * tq
    limit = _vmem_limit(
        [_nbytes((tq_blk, n_stack * dk), BF16), _nbytes((tq_blk, n_stack * dv), BF16)] + kv_bytes
        + side_bytes,
        [], ATTN_LIVE_TILES * _nbytes((m_rows, max(chunks)), F32) + 4 * _nbytes((m_rows, V7X_LANES), F32))
    outs = pl.pallas_call(
        functools.partial(_attn_kernel, n_stack=n_stack, dk=dk, dv=dv, chunks=tuple(chunks),
                          side_blocks=tuple(side_blocks), sub_tiles=sub_tiles),
        out_shape=out_shape,
        grid=(batch, n_kv, tiles),
        in_specs=in_specs,
        out_specs=out_specs,
        compiler_params=pltpu.CompilerParams(
            dimension_semantics=("arbitrary", "arbitrary", "arbitrary"), vmem_limit_bytes=limit),
        name="attention",
    )(*args)
    return outs[0], outs[1:]


def _out_kernel(h_ref, gt_ref, ya_ref, yb_ref, cu_ref, cup_ref, cun_ref, wp_ref, cs_ref, wo_ref,
                o_ref, ext_ref, pooled0_ref, pooled1_ref, *, mod_row, seg_len):
    i = pl.program_id(0)
    n_tiles = pl.num_programs(0) - 1
    tm = h_ref.shape[0]
    halo = V7X_SUBLANES
    gd = C_GROUP_DIM
    n_rows = tm + 2 * halo
    pitch = n_rows // V7X_SUBLANES

    def stage(pooled_ref):
        pos0 = lax.rem(jnp.minimum(i, n_tiles - 1) * tm, seg_len)
        top_ok, bot_ok = pos0 > 0, pos0 + tm < seg_len
        sub = lax.broadcasted_iota(jnp.int32, (V7X_SUBLANES, V7X_LANES), 0)
        top_ind = jnp.where((sub == 0) & jnp.logical_not(top_ok), 0.0, 1.0)
        bot_ind = jnp.where((sub == V7X_SUBLANES - 1) & jnp.logical_not(bot_ok), 0.0, 1.0)

        def extended(vregs):
            ext = {v: vregs[v] for v in range(pitch)}
            for v in range(-halo, 0):
                x = vregs[v + pitch]
                ext[v] = x if isinstance(x, float) else pltpu.roll(x, 1, 0)
            for v in range(pitch, pitch + halo):
                x = vregs[v - pitch]
                ext[v] = x if isinstance(x, float) else pltpu.roll(x, V7X_SUBLANES - 1, 0)
            return ext

        def window_sums(ext):
            c2 = {v: ext[v - 1] + ext[v] for v in range(-halo + 1, pitch + halo)}
            c4 = {v: c2[v - 1] + c2[v + 1] for v in range(-halo + 2, pitch + halo - 1)}
            c8 = {v: c4[v - 2] + c4[v + 2] for v in range(-halo + 4, pitch + halo - 3)}
            c16 = {v: c8[v - 4] + c8[v + 4] for v in range(0, pitch)}
            return dict(zip(C_WINDOWS, (c2, c4, c8, c16)))

        ind = [top_ind if v < halo else bot_ind if v >= pitch - halo else 1.0 for v in range(pitch)]
        counts = window_sums(extended(ind))
        for g, w in enumerate(C_WINDOWS):
            cols = slice(g * gd, (g + 1) * gd)
            ext_ref[g, 0:halo, :] = jnp.where(top_ok, cup_ref[:, cols], 0.0)
            ext_ref[g, halo:halo + tm, :] = cu_ref[:, cols]
            ext_ref[g, halo + tm:, :] = jnp.where(bot_ok, cun_ref[:, cols], 0.0)
            ext_g, pooled_g = ext_ref.at[g], pooled_ref.at[g]
            rows = [ext_g[pl.ds(v, V7X_SUBLANES, stride=pitch), :] for v in range(pitch)]
            sums = window_sums(extended(rows))[w]
            for v in range(pitch):
                cnt = counts[w][v]
                inv = 1.0 / cnt
                pooled_g[pl.ds(v, V7X_SUBLANES, stride=pitch), :] = sums[v] * inv - rows[v]

    def stage_and_mix(nxt_ref, cur_ref):
        a_w, b_w = ya_ref.shape[1], yb_ref.shape[1]
        stage(nxt_ref)
        yc = []
        for g in range(len(C_WINDOWS)):
            cols = slice(g * gd, (g + 1) * gd)
            pooled = cur_ref[g, halo:halo + tm, :].astype(BF16)
            mixed = jnp.dot(pooled, wp_ref[g], preferred_element_type=F32) * cs_ref[:, cols]
            yc.append(mixed.astype(BF16))
        y = jnp.dot(ya_ref[...], wo_ref[0:a_w, :], preferred_element_type=F32)
        y += jnp.dot(yb_ref[...], wo_ref[a_w:a_w + b_w, :], preferred_element_type=F32)
        y += jnp.dot(jnp.concatenate(yc, axis=1), wo_ref[a_w + b_w:, :], preferred_element_type=F32)
        o_ref[...] = h_ref[...] + gt_ref[pl.ds(mod_row(i - 1), 1), :] * y

    pl.when(i == 0)(functools.partial(stage, pooled0_ref))
    for par, (nxt_ref, cur_ref) in enumerate(((pooled0_ref, pooled1_ref), (pooled1_ref, pooled0_ref))):
        pl.when((i > 0) & (lax.rem(i, 2) == par))(functools.partial(stage_and_mix, nxt_ref, cur_ref))


def _mix_out(h, mod, j_gate, ya, yb, cu, w_pool, c_scale, w_out, *, mod_row, seg_len, tm):
    n, d = h.shape
    c_w = cu.shape[1]
    assert n % tm == 0 and seg_len % tm == 0
    halo = V7X_SUBLANES
    per_tile = tm // halo
    n_tiles, last = n // tm, n // halo - 1
    prev_spec = lambda w: pl.BlockSpec((tm, w), lambda i: (jnp.maximum(i - 1, 0), 0))
    staged = lambda i: jnp.minimum(i, n_tiles - 1)
    limit = _vmem_limit(
        [2 * _nbytes((tm, d), F32), _nbytes((tm, ya.shape[1]), BF16),
         _nbytes((tm, yb.shape[1]), BF16), _nbytes((tm, c_w), F32)],
        [_nbytes(w_out.shape, BF16), _nbytes(w_pool.shape, BF16), 3 * _nbytes((tm + 2 * halo, c_w), F32)],
        2 * _nbytes((tm, d), F32))
    group_rows = pltpu.VMEM((c_w // C_GROUP_DIM, tm + 2 * halo, C_GROUP_DIM), F32)
    return pl.pallas_call(
        functools.partial(_out_kernel, mod_row=mod_row, seg_len=seg_len),
        out_shape=jax.ShapeDtypeStruct((n, d), F32),
        grid=(n_tiles + 1,),
        in_specs=[
            prev_spec(d),
            pl.BlockSpec((MOD_ROWS, d), lambda i: (0, j_gate)),
            prev_spec(ya.shape[1]), prev_spec(yb.shape[1]),
            pl.BlockSpec((tm, c_w), lambda i: (staged(i), 0)),
            pl.BlockSpec((halo, c_w), lambda i: (jnp.maximum(staged(i) * per_tile - 1, 0), 0)),
            pl.BlockSpec((halo, c_w), lambda i: (jnp.minimum((staged(i) + 1) * per_tile, last), 0)),
            _resident(w_pool.shape), _resident((1, c_w)), _resident(w_out.shape),
        ],
        out_specs=prev_spec(d),
        scratch_shapes=[group_rows, group_rows, group_rows],
        compiler_params=pltpu.CompilerParams(
            dimension_semantics=("arbitrary",), vmem_limit_bytes=limit),
        name="mixer_out_proj",
    )(h, mod, ya, yb, cu, cu, cu, w_pool, c_scale, w_out)


def _rope_tables(seq, dim):
    n = dim // 4
    t = np.arange(seq)
    rows = (t // GRID_W).astype(np.float32)
    cols = (t % GRID_W).astype(np.float32)
    inv = (np.float32(ROPE_THETA) ** (-np.arange(n, dtype=np.float32) / np.float32(n))).astype(np.float32)
    ang = np.concatenate([rows[:, None] * inv[None, :], cols[:, None] * inv[None, :]], axis=-1)
    cos, sin = np.cos(ang).astype(np.float32), np.sin(ang).astype(np.float32)
    pad = np.zeros((seq, V7X_LANES - dim), np.float32)
    return (np.concatenate([cos, cos, pad], axis=-1), np.concatenate([-sin, sin, pad], axis=-1))


def _identity_tables(rows, dim):
    one = np.concatenate([np.ones((rows, dim), np.float32),
                          np.zeros((rows, V7X_LANES - dim), np.float32)], -1)
    return one, np.zeros((rows, V7X_LANES), np.float32)


def _w_uq_columns(heads):
    hd = B_NOPE_DIM + B_ROPE_DIM
    half = B_ROPE_DIM // 2
    out = []
    for h in range(heads):
        b = h * hd
        x1 = np.arange(b + B_NOPE_DIM, b + B_NOPE_DIM + half)
        x2 = x1 + half
        out += [np.arange(b, b + B_NOPE_DIM), x1, x2, x2, x1]
    return np.concatenate(out)


def _w_ukv_columns(heads):
    hd = B_NOPE_DIM + B_V_DIM
    ks = [np.arange(h * hd, h * hd + B_NOPE_DIM) for h in range(heads)]
    vs = [np.arange(h * hd + B_NOPE_DIM, (h + 1) * hd) for h in range(heads)]
    return np.concatenate(ks + vs)


def kernel(x, c, ctx, c_ctx, w_mod, b_mod, norm_g, ffn1_in, ffn1_out, w_in, a_q_g, a_k_g, b_q_g,
           b_kv_g, b_w_uq, b_w_ukv, c_w_pool, c_scale, w_out, ffn2_in, ffn2_out, final_g):
    batch, seq, d = x.shape
    ctx_len = ctx.shape[1]
    depth = w_mod.shape[0]
    assert batch + 1 <= MOD_ROWS
    b_q, b_kv = b_q_g.shape[1], b_kv_g.shape[1]
    b_heads = b_w_ukv.shape[2] // (B_NOPE_DIM + B_V_DIM)
    c_w = c_scale.shape[1]
    a_q = w_out.shape[1] - b_heads * B_V_DIM - c_w
    a_kv = a_q // A_GROUP
    dims = (a_q, a_kv, b_q, b_kv, b_heads, c_w)

    tm, tm_ctx = 512, ctx_len
    assert seq % 1024 == 0 and seq % GRID_W == 0 and ctx_len % 64 == 0
    tiles_per_seq = seq // tm
    lat_row = lambda i: lax.div(i, tiles_per_seq)
    ctx_row = lambda i: batch
    ffn_lat = dict(tm=1024, tf=512, slice_rows=128, mod_row=lambda t: lax.div(t, seq // 1024))
    ffn_ctx = dict(tm=batch * ctx_len, tf=512, slice_rows=128, mod_row=ctx_row)

    c_rows = jnp.concatenate([c, c_ctx[None, :], jnp.zeros((MOD_ROWS - batch - 1, d), F32)], axis=0)
    mod = _modulation(c_rows, w_mod, b_mod)

    tab_lat = _rope_tables(seq, A_HEAD_DIM) + _rope_tables(seq, B_ROPE_DIM)
    tab_ctx = _identity_tables(tm_ctx, A_HEAD_DIM) + _identity_tables(tm_ctx, B_ROPE_DIM)
    cols_uq, cols_ukv = _w_uq_columns(b_heads), _w_ukv_columns(b_heads)

    h = x.reshape(batch * seq, d)
    hc = ctx.reshape(batch * ctx_len, d)
    tq_a, tq_b, sub = 512, 2048, 2
    a_steps = batch * (a_kv // A_HEAD_DIM) * (seq // (tq_a * sub))
    b_steps = batch * b_heads * (seq // (tq_b * sub))
    rows_in, rows_out = ffn1_in.shape[1] // a_steps, 2 * ffn1_out.shape[1] // b_steps
    f1_in, f1_out = _layer_bf16(ffn1_in, 0), _layer_bf16(ffn1_out, 0)
    w_in_bf = w_in.astype(BF16)
    for i in range(depth):
        last = i == depth - 1
        ng = norm_g[i]
        w_uq_p = b_w_uq[i][:, cols_uq].astype(BF16)
        w_ukv_p = b_w_ukv[i][:, cols_ukv].astype(BF16)
        w_pool = c_w_pool[i].astype(BF16)
        side_a = [(ffn2_in, i, rows_in)] + ([] if last else [(ffn1_in, i + 1, rows_in)])
        side_b = [(ffn2_out, i, rows_out), (w_out, i, w_out.shape[1] // b_steps)]
        side_b += [] if last else [(ffn1_out, i + 1, rows_out)]
        gains = (a_q_g[i][None], a_k_g[i][None], b_q_g[i][None], b_kv_g[i][None])
        fg = final_g[None]

        ffn = functools.partial(_ffn, final_g=fg)
        h = ffn(h, mod[i], 0, ng[0:1], f1_in, f1_out, final_norm=False, **ffn_lat)
        hc = ffn(hc, mod[i], 0, ng[0:1], f1_in, f1_out, final_norm=False, **ffn_ctx)

        proj = functools.partial(_proj, norm_g=ng[1:2], w_in_all=w_in_bf, layer=i, a_q_g=gains[0], a_k_g=gains[1],
                                 b_q_g=gains[2], b_kv_g=gains[3], w_uq_p=w_uq_p, w_ukv_p=w_ukv_p,
                                 dims=dims)
        qa, ka, va, qb, kb, vb, cu = proj(h, mod[i], 3, tables=tab_lat, mod_row=lat_row,
                                          table_block=lambda t: lax.rem(t, tiles_per_seq), tm=tm)
        qac, kac, vac, qbc, kbc, vbc, cuc = proj(hc, mod[i], 3, tables=tab_ctx, mod_row=ctx_row,
                                                 table_block=lambda t: 0, tm=tm_ctx)

        attn_a = functools.partial(_attention, batch=batch, n_kv=a_kv // A_HEAD_DIM, n_stack=A_GROUP,
                                   dk=A_HEAD_DIM, dv=A_HEAD_DIM, tq=tq_a, chunk=ATTN_KEY_CHUNK)
        attn_b = functools.partial(_attention, batch=batch, n_kv=b_heads, n_stack=1,
                                   dk=B_QK_PAD, dv=B_V_DIM, tq=tq_b, chunk=ATTN_KEY_CHUNK)
        ya, cast_a = attn_a(qa, [(kac, vac), (ka, va)], side_a, sub_tiles=sub)
        yb, cast_b = attn_b(qb, [(kbc, vbc), (kb, vb)], side_b, sub_tiles=sub)
        f2_in, f2_out, w_o = cast_a[0], cast_b[0], cast_b[1]
        mix = functools.partial(_mix_out, w_pool=w_pool, c_scale=c_scale[i][None], w_out=w_o)
        h = mix(h, mod[i], 5, ya, yb, cu, mod_row=lat_row, seg_len=seq, tm=tm)
        h = ffn(h, mod[i], 6, ng[2:3], f2_in, f2_out, final_norm=last, **ffn_lat)
        if not last:
            yac, _ = attn_a(qac, [(kac, vac)], tq=ctx_len)
            ybc, _ = attn_b(qbc, [(kbc, vbc)], tq=ctx_len)
            hc = mix(hc, mod[i], 5, yac, ybc, cuc, mod_row=ctx_row, seg_len=ctx_len, tm=tm_ctx)
            hc = ffn(hc, mod[i], 6, ng[2:3], f2_in, f2_out, final_norm=False, **ffn_ctx)
            f1_in, f1_out = cast_a[1], cast_b[2]
    return h.reshape(batch, seq, d)
```

```python
import functools

import jax
import jax.numpy as jnp
import numpy as np
from jax import lax
from jax.experimental import pallas as pl
from jax.experimental.pallas import tpu as pltpu

F32 = jnp.float32
BF16 = jnp.bfloat16

GRID_W = 64
ROPE_THETA = 10000.0
NORM_EPS = 1e-6
A_HEAD_DIM = 128
A_GROUP = 4
B_NOPE_DIM = 128
B_ROPE_DIM = 64
B_V_DIM = 128
C_WINDOWS = (2, 4, 8, 16)
C_GROUP_DIM = 128
LOG2E = 1.4426950408889634

V7X_LANES = 128
V7X_SUBLANES = 8
V7X_MXU_DIM = 256
V7X_VMEM_BYTES = 64 * 1024 * 1024
V7X_VMEM_RESERVE_BYTES = 8 * 1024 * 1024

MOD_ROWS = V7X_SUBLANES
B_QK_PAD = V7X_MXU_DIM
ATTN_KEY_CHUNK = V7X_MXU_DIM
ATTN_LIVE_TILES = 24


def _nbytes(shape, dtype):
    return int(np.prod(shape)) * jnp.dtype(dtype).itemsize


def _vmem_limit(pipelined, resident, temps):
    need = 2 * sum(pipelined) + sum(resident) + temps
    return int(min(V7X_VMEM_BYTES - V7X_VMEM_RESERVE_BYTES, need + need // 4))


def _rms(x, eps=NORM_EPS):
    return x * lax.rsqrt(jnp.mean(x * x, axis=-1, keepdims=True) + eps)


def _resident(shape):
    nd = len(shape)
    return pl.BlockSpec(shape, lambda *_: (0,) * nd, pipeline_mode=pl.Buffered(1))


CAST_BLOCK_BYTES = 8 * 1024 * 1024


def _cast_kernel(w_ref, o_ref):
    o_ref[...] = w_ref[...].astype(BF16)


def _layer_bf16(w, layer):
    _, rows, cols = w.shape
    n_blocks = next(nb for nb in range(1, rows + 1)
                    if rows % nb == 0 and (rows // nb) % 16 == 0
                    and _nbytes((rows // nb, cols), F32) <= CAST_BLOCK_BYTES)
    tr = rows // n_blocks
    limit = _vmem_limit([_nbytes((tr, cols), F32), _nbytes((tr, cols), BF16)], [], 0)
    return pl.pallas_call(
        _cast_kernel,
        out_shape=jax.ShapeDtypeStruct((rows, cols), BF16),
        grid=(n_blocks,),
        in_specs=[pl.BlockSpec((None, tr, cols), lambda j: (layer, j, 0))],
        out_specs=pl.BlockSpec((tr, cols), lambda j: (j, 0)),
        compiler_params=pltpu.CompilerParams(
            dimension_semantics=("parallel",), vmem_limit_bytes=limit),
        name="weight_to_bf16",
    )(w)


def _mod_kernel(c_ref, w_ref, b_ref, o_ref):
    c = c_ref[...]
    act = (c * jax.nn.sigmoid(c)).astype(BF16)
    o_ref[...] = jnp.dot(act, w_ref[...].astype(BF16), preferred_element_type=F32) + b_ref[...]


def _modulation(c_rows, w_mod, b_mod, *, tn=1024):
    depth, d, nm = w_mod.shape
    assert nm % tn == 0
    limit = _vmem_limit([_nbytes((d, tn), F32), _nbytes((MOD_ROWS, tn), F32)],
                        [_nbytes((MOD_ROWS, d), F32)], _nbytes((d, tn), BF16))
    return pl.pallas_call(
        _mod_kernel,
        out_shape=jax.ShapeDtypeStruct((depth, MOD_ROWS, nm), F32),
        grid=(depth, nm // tn),
        in_specs=[
            pl.BlockSpec((MOD_ROWS, d), lambda l, j: (0, 0)),
            pl.BlockSpec((None, d, tn), lambda l, j: (l, 0, j)),
            pl.BlockSpec((None, 1, tn), lambda l, j: (l, 0, j)),
        ],
        out_specs=pl.BlockSpec((None, MOD_ROWS, tn), lambda l, j: (l, 0, j)),
        compiler_params=pltpu.CompilerParams(
            dimension_semantics=("parallel", "parallel"), vmem_limit_bytes=limit),
        name="modulation",
    )(c_rows, w_mod, b_mod.reshape(depth, 1, nm))


def _ffn_kernel(hn_ref, sh_ref, sc_ref, gt_ref, ng_ref, wa_ref, wb_ref, wo_ref, fg_ref,
                o_ref, xn0_ref, xn1_ref, hs_ref, *, mod_row, final_norm, n_slices):
    r, k = pl.program_id(0), pl.program_id(1)
    n_tiles, nk = pl.num_programs(0) - 1, pl.num_programs(1)
    slice_rows = hn_ref.shape[0]

    def stage_next_slice(nxt_ref):
        s = jnp.clip(k - 1, 0, n_slices - 1)
        rows = pl.ds(pl.multiple_of(s * slice_rows, 16), slice_rows)
        row = mod_row(jnp.minimum(r, n_tiles - 1))
        x = hn_ref[...]
        hs_ref[rows, :] = x
        y = _rms(x) * ng_ref[...]
        y = y * (1.0 + sc_ref[pl.ds(row, 1), :]) + sh_ref[pl.ds(row, 1), :]
        nxt_ref[rows, :] = y.astype(BF16)

    def step(first, cur_ref, nxt_ref):
        if not first:
            stage_next_slice(nxt_ref)
        xn = cur_ref[...]
        a = jnp.dot(xn, wa_ref[...], preferred_element_type=F32)
        b = jnp.dot(xn, wb_ref[...], preferred_element_type=F32)
        g = (a * jax.nn.sigmoid(a) * b).astype(BF16)
        part = jnp.dot(g, wo_ref[...], preferred_element_type=F32)
        part = part * (0.5 * gt_ref[pl.ds(mod_row(r - 1), 1), :])
        o_ref[...] = (hs_ref[...] if first else o_ref[...]) + part

    pl.when((r == 0) & (k > 0))(functools.partial(stage_next_slice, xn0_ref))
    for first in (True, False):
        for par, (nxt_ref, cur_ref) in enumerate(((xn0_ref, xn1_ref), (xn1_ref, xn0_ref))):
            cond = ((k == 0) if first else (k > 0)) & (r > 0) & (lax.rem(r, 2) == par)
            pl.when(cond)(functools.partial(step, first, cur_ref, nxt_ref))

    if final_norm:
        @pl.when((k == nk - 1) & (r > 0))
        def _():
            o_ref[...] = _rms(o_ref[...]) * fg_ref[...]


def _ffn(h, mod, j0, norm_g, w_in, w_out, final_g, *, mod_row, final_norm, tm, tf, slice_rows):
    n, d = h.shape
    dff = w_out.shape[0]
    assert n % tm == 0 and dff % tf == 0 and tm % slice_rows == 0
    nk, n_tiles, n_slices = dff // tf, n // tm, tm // slice_rows
    assert n_slices <= nk - 1
    mod_spec = lambda j: pl.BlockSpec((MOD_ROWS, d), lambda r, k: (0, j))
    limit = _vmem_limit(
        [_nbytes((tm, d), F32), _nbytes((slice_rows, d), F32),
         _nbytes((d, tf), BF16) * 2, _nbytes((tf, d), BF16)],
        [2 * _nbytes((tm, d), BF16), _nbytes((tm, d), F32)],
        4 * _nbytes((tm, tf), F32) + _nbytes((tm, d), F32))

    def next_slice(r, k):
        tile = jnp.minimum(r, n_tiles - 1)
        return (tile * n_slices + jnp.clip(k - 1, 0, n_slices - 1), 0)

    chunk = lambda r, k: jnp.where(r == 0, 0, k)
    return pl.pallas_call(
        functools.partial(_ffn_kernel, mod_row=mod_row, final_norm=final_norm, n_slices=n_slices),
        out_shape=jax.ShapeDtypeStruct((n, d), F32),
        grid=(n_tiles + 1, nk),
        in_specs=[
            pl.BlockSpec((slice_rows, d), next_slice),
            mod_spec(j0), mod_spec(j0 + 1), mod_spec(j0 + 2),
            pl.BlockSpec((1, d), lambda r, k: (0, 0)),
            pl.BlockSpec((d, tf), lambda r, k: (0, chunk(r, k))),
            pl.BlockSpec((d, tf), lambda r, k: (0, chunk(r, k) + nk)),
            pl.BlockSpec((tf, d), lambda r, k: (chunk(r, k), 0)),
            pl.BlockSpec((1, d), lambda r, k: (0, 0)),
        ],
        out_specs=pl.BlockSpec((tm, d), lambda r, k: (jnp.maximum(r - 1, 0), 0)),
        scratch_shapes=[pltpu.VMEM((tm, d), BF16), pltpu.VMEM((tm, d), BF16), pltpu.VMEM((tm, d), F32)],
        compiler_params=pltpu.CompilerParams(
            dimension_semantics=("arbitrary", "arbitrary"), vmem_limit_bytes=limit),
        name="swiglu_half_step",
    )(h, mod, mod, mod, norm_g, w_in, w_in, w_out, final_g)


def _rope(x, cos, sin_signed):
    return x * cos + pltpu.roll(x, V7X_LANES // 2, 1) * sin_signed


def _proj_kernel(h_ref, sh_ref, sc_ref, ng_ref, win_ref, aqg_ref, akg_ref, bqg_ref, bkvg_ref,
                 wuq_ref, wukv_ref, ca_ref, sa_ref, cb_ref, sb_ref,
                 qa_ref, ka_ref, va_ref, qb_ref, kb_ref, vb_ref, cu_ref, *, mod_row, dims):
    a_q, a_kv, b_q, b_kv, b_heads, c_w = dims
    row = mod_row(pl.program_id(0))
    y = _rms(h_ref[...]) * ng_ref[...]
    y = y * (1.0 + sc_ref[pl.ds(row, 1), :]) + sh_ref[pl.ds(row, 1), :]
    u = jnp.dot(y.astype(BF16), win_ref[...], preferred_element_type=F32)

    ca, sa, cb, sb = ca_ref[...], sa_ref[...], cb_ref[...], sb_ref[...]
    hd = A_HEAD_DIM
    o_ak, o_av, o_bq = a_q, a_q + a_kv, a_q + 2 * a_kv
    o_bkv, o_bkr = o_bq + b_q, o_bq + b_q + b_kv
    o_cu = o_bkr + B_ROPE_DIM
    half = B_ROPE_DIM // 2

    qg = aqg_ref[...] * (hd ** -0.5 * LOG2E)
    for h in range(a_q // hd):
        x = _rms(u[:, h * hd:(h + 1) * hd]) * qg
        qa_ref[:, h * hd:(h + 1) * hd] = _rope(x, ca, sa).astype(BF16)
    kg = akg_ref[...]
    for h in range(a_kv // hd):
        x = _rms(u[:, o_ak + h * hd:o_ak + (h + 1) * hd]) * kg
        ka_ref[:, h * hd:(h + 1) * hd] = _rope(x, ca, sa).astype(BF16)
    va_ref[...] = u[:, o_av:o_av + a_kv].astype(BF16)

    xq = (_rms(u[:, o_bq:o_bq + b_q]) * bqg_ref[...]).astype(BF16)
    q = jnp.dot(xq, wuq_ref[...], preferred_element_type=F32)
    q_scale = (B_NOPE_DIM + B_ROPE_DIM) ** -0.5 * LOG2E
    xkv = (_rms(u[:, o_bkv:o_bkv + b_kv]) * bkvg_ref[...]).astype(BF16)
    kv = jnp.dot(xkv, wukv_ref[...], preferred_element_type=F32)
    kr = u[:, o_bkr:o_bkr + B_ROPE_DIM]
    kr = jnp.concatenate([kr, kr[:, half:], kr[:, :half]], axis=1)
    k_rope = _rope(kr, cb, sb).astype(BF16)
    for h in range(b_heads):
        o = h * B_QK_PAD
        qb_ref[:, o:o + B_NOPE_DIM] = (q[:, o:o + B_NOPE_DIM] * q_scale).astype(BF16)
        qr = _rope(q[:, o + B_NOPE_DIM:o + B_QK_PAD], cb, sb)
        qb_ref[:, o + B_NOPE_DIM:o + B_QK_PAD] = (qr * q_scale).astype(BF16)
        kb_ref[:, o:o + B_NOPE_DIM] = kv[:, h * B_NOPE_DIM:(h + 1) * B_NOPE_DIM].astype(BF16)
        kb_ref[:, o + B_NOPE_DIM:o + B_QK_PAD] = k_rope
    vb_ref[...] = kv[:, b_heads * B_NOPE_DIM:].astype(BF16)

    cu_ref[...] = u[:, o_cu:o_cu + c_w]


def _proj(h, mod, j0, norm_g, w_in_all, layer, a_q_g, a_k_g, b_q_g, b_kv_g, w_uq_p, w_ukv_p, tables,
          *, mod_row, table_block, dims, tm):
    n, d = h.shape
    a_q, a_kv, b_q, b_kv, b_heads, c_w = dims
    assert n % tm == 0
    ncols = w_in_all.shape[2]
    w_in_shape = w_in_all.shape[1:]
    mod_spec = lambda j: pl.BlockSpec((MOD_ROWS, d), lambda i: (0, j))
    tab_spec = pl.BlockSpec((tm, V7X_LANES), lambda i: (table_block(i), 0))
    row_spec = lambda w: pl.BlockSpec((tm, w), lambda i: (i, 0))
    out_widths = [(a_q, BF16), (a_kv, BF16), (a_kv, BF16), (b_heads * B_QK_PAD, BF16),
                  (b_heads * B_QK_PAD, BF16), (b_heads * B_V_DIM, BF16), (c_w, F32)]
    limit = _vmem_limit(
        [_nbytes((tm, d), F32)] + [_nbytes((tm, w), t) for w, t in out_widths]
        + [4 * _nbytes((tm, V7X_LANES), F32)],
        [_nbytes(w_in_shape, BF16), _nbytes(w_uq_p.shape, BF16), _nbytes(w_ukv_p.shape, BF16)],
        _nbytes((tm, ncols), F32) + _nbytes((tm, d), F32) + 2 * _nbytes((tm, 4 * B_QK_PAD), F32))
    return pl.pallas_call(
        functools.partial(_proj_kernel, mod_row=mod_row, dims=dims),
        out_shape=[jax.ShapeDtypeStruct((n, w), t) for w, t in out_widths],
        grid=(n // tm,),
        in_specs=[
            row_spec(d), mod_spec(j0), mod_spec(j0 + 1),
            _resident((1, d)),
            pl.BlockSpec((None,) + w_in_shape, lambda i: (layer, 0, 0), pipeline_mode=pl.Buffered(1)),
            _resident((1, A_HEAD_DIM)), _resident((1, A_HEAD_DIM)),
            _resident((1, b_q)), _resident((1, b_kv)),
            _resident(w_uq_p.shape), _resident(w_ukv_p.shape),
            tab_spec, tab_spec, tab_spec, tab_spec,
        ],
        out_specs=[row_spec(w) for w, _ in out_widths],
        compiler_params=pltpu.CompilerParams(
            dimension_semantics=("parallel",), vmem_limit_bytes=limit),
        name="mixer_in_proj",
    )(h, mod, mod, norm_g, w_in_all, a_q_g, a_k_g, b_q_g, b_kv_g, w_uq_p, w_ukv_p, *tables)


def _attn_kernel(*refs, n_stack, dk, dv, chunks, side_blocks, sub_tiles, n_steps):
    n_side, n_kv_refs = len(side_blocks), 2 * len(chunks)
    q_ref, kv_refs = refs[0], refs[1:1 + n_kv_refs]
    side_in = refs[1 + n_kv_refs:1 + n_kv_refs + n_side]
    o_ref, side_out = refs[1 + n_kv_refs + n_side], refs[2 + n_kv_refs + n_side:]
    tq = q_ref.shape[0] // sub_tiles

    step_id = ((pl.program_id(0) * pl.num_programs(1) + pl.program_id(1)) * pl.num_programs(2)
               + pl.program_id(2))
    for w_ref, wo_ref, n_blocks in zip(side_in, side_out, side_blocks):
        def cast(w_ref=w_ref, wo_ref=wo_ref):
            wo_ref[...] = w_ref[...].astype(BF16)
        if n_blocks == n_steps:
            cast()
        else:
            pl.when(step_id < n_blocks)(cast)

    m_rows = n_stack * tq

    def step(q, k, v, carry):
        m, acc = carry
        s = lax.dot_general(q, k, (((1,), (1,)), ((), ())), preferred_element_type=F32)
        m_new = jnp.maximum(m, jnp.max(s, axis=-1, keepdims=True))
        p = jnp.exp2(s - m_new).astype(BF16)
        v_ones = jnp.concatenate([v, jnp.ones((v.shape[0], V7X_LANES), BF16)], axis=1)
        acc = jnp.exp2(m - m_new) * acc + jnp.dot(p, v_ones, preferred_element_type=F32)
        return m_new, acc

    for t in range(sub_tiles):
        rows = slice(t * tq, (t + 1) * tq)
        q = jnp.concatenate([q_ref[rows, h * dk:(h + 1) * dk] for h in range(n_stack)], axis=0)
        carry = (jnp.full((m_rows, 1), -jnp.inf, F32), jnp.zeros((m_rows, dv + V7X_LANES), F32))
        for seg, chunk in enumerate(chunks):
            k_ref, v_ref = kv_refs[2 * seg], kv_refs[2 * seg + 1]
            for c in range(k_ref.shape[0] // chunk):
                keys = slice(c * chunk, (c + 1) * chunk)
                carry = step(q, k_ref[keys, :], v_ref[keys, :], carry)
        _, acc = carry
        out = acc[:, :dv] / acc[:, dv:dv + dv]
        for h in range(n_stack):
            o_ref[rows, h * dv:(h + 1) * dv] = out[h * tq:(h + 1) * tq].astype(BF16)


def _attention(q, segments, side=(), *, batch, n_kv, n_stack, dk, dv, tq, chunk, sub_tiles=1):
    nq = q.shape[0]
    lq = nq // batch
    tq_blk = tq * sub_tiles
    assert lq % tq_blk == 0 and dv == V7X_LANES
    tiles = lq // tq_blk
    n_steps = batch * n_kv * tiles
    in_specs = [pl.BlockSpec((tq_blk, n_stack * dk), lambda b, g, i: (b * tiles + i, g))]
    args, chunks, kv_bytes = [q], [], []
    for k, v in segments:
        lk = k.shape[0] // batch
        chunks.append(min(chunk, lk))
        assert lk % chunks[-1] == 0
        in_specs.append(pl.BlockSpec((lk, dk), lambda b, g, i: (b, g)))
        in_specs.append(pl.BlockSpec((lk, dv), lambda b, g, i: (b, g)))
        args += [k, v]
        kv_bytes += [_nbytes((lk, dk), BF16), _nbytes((lk, dv), BF16)]
    out_shape = [jax.ShapeDtypeStruct((nq, n_kv * n_stack * dv), BF16)]
    out_specs = [pl.BlockSpec((tq_blk, n_stack * dv), lambda b, g, i: (b * tiles + i, g))]
    side_blocks, side_bytes = [], []
    for w, layer, rows in side:
        _, w_rows, w_cols = w.shape
        n_blocks = w_rows // rows
        assert w_rows % rows == 0 and rows % 16 == 0 and n_blocks <= n_steps

        def block(b, g, i, n_blocks=n_blocks):
            return jnp.minimum((b * n_kv + g) * tiles + i, n_blocks - 1)

        in_specs.append(pl.BlockSpec((None, rows, w_cols),
                                     lambda b, g, i, layer=layer, block=block: (layer, block(b, g, i), 0)))
        out_specs.append(pl.BlockSpec((rows, w_cols), lambda b, g, i, block=block: (block(b, g, i), 0)))
        out_shape.append(jax.ShapeDtypeStruct((w_rows, w_cols), BF16))
        args.append(w)
        side_blocks.append(n_blocks)
        side_bytes += [_nbytes((rows, w_cols), F32), _nbytes((rows, w_cols), BF16)]
    m_rows = n_stack * tq
    limit = _vmem_limit(
        [_nbytes((tq_blk, n_stack * dk), BF16), _nbytes((tq_blk, n_stack * dv), BF16)] + kv_bytes
        + side_bytes,
        [], ATTN_LIVE_TILES * _nbytes((m_rows, max(chunks)), F32) + 4 * _nbytes((m_rows, V7X_LANES), F32))
    outs = pl.pallas_call(
        functools.partial(_attn_kernel, n_stack=n_stack, dk=dk, dv=dv, chunks=tuple(chunks),
                          side_blocks=tuple(side_blocks), sub_tiles=sub_tiles, n_steps=n_steps),
        out_shape=out_shape,
        grid=(batch, n_kv, tiles),
        in_specs=in_specs,
        out_specs=out_specs,
        compiler_params=pltpu.CompilerParams(
            dimension_semantics=("arbitrary", "arbitrary", "arbitrary"), vmem_limit_bytes=limit),
        name="attention",
    )(*args)
    return outs[0], outs[1:]


def _out_kernel(h_ref, gt_ref, ya_ref, yb_ref, cu_ref, cup_ref, cun_ref, wp_ref, cs_ref, wo_ref,
                o_ref, ext_ref, pooled0_ref, pooled1_ref, *, mod_row, seg_len):
    i = pl.program_id(0)
    n_tiles = pl.num_programs(0) - 1
    tm = h_ref.shape[0]
    halo = V7X_SUBLANES
    gd = C_GROUP_DIM
    n_rows = tm + 2 * halo
    pitch = n_rows // V7X_SUBLANES

    def stage(pooled_ref):
        pos0 = lax.rem(jnp.minimum(i, n_tiles - 1) * tm, seg_len)
        top_ok, bot_ok = pos0 > 0, pos0 + tm < seg_len
        sub = lax.broadcasted_iota(jnp.int32, (V7X_SUBLANES, V7X_LANES), 0)
        top_ind = jnp.where((sub == 0) & jnp.logical_not(top_ok), 0.0, 1.0)
        bot_ind = jnp.where((sub == V7X_SUBLANES - 1) & jnp.logical_not(bot_ok), 0.0, 1.0)

        def extended(vregs):
            ext = {v: vregs[v] for v in range(pitch)}
            for v in range(-halo, 0):
                x = vregs[v + pitch]
                ext[v] = x if isinstance(x, float) else pltpu.roll(x, 1, 0)
            for v in range(pitch, pitch + halo):
                x = vregs[v - pitch]
                ext[v] = x if isinstance(x, float) else pltpu.roll(x, V7X_SUBLANES - 1, 0)
            return ext

        def window_sums(ext):
            c2 = {v: ext[v - 1] + ext[v] for v in range(-halo + 1, pitch + halo)}
            c4 = {v: c2[v - 1] + c2[v + 1] for v in range(-halo + 2, pitch + halo - 1)}
            c8 = {v: c4[v - 2] + c4[v + 2] for v in range(-halo + 4, pitch + halo - 3)}
            c16 = {v: c8[v - 4] + c8[v + 4] for v in range(0, pitch)}
            return dict(zip(C_WINDOWS, (c2, c4, c8, c16)))

        ind = [top_ind if v < halo else bot_ind if v >= pitch - halo else 1.0 for v in range(pitch)]
        counts = window_sums(extended(ind))
        for g, w in enumerate(C_WINDOWS):
            cols = slice(g * gd, (g + 1) * gd)
            ext_ref[g, 0:halo, :] = jnp.where(top_ok, cup_ref[:, cols], 0.0)
            ext_ref[g, halo:halo + tm, :] = cu_ref[:, cols]
            ext_ref[g, halo + tm:, :] = jnp.where(bot_ok, cun_ref[:, cols], 0.0)
            ext_g, pooled_g = ext_ref.at[g], pooled_ref.at[g]
            rows = [ext_g[pl.ds(v, V7X_SUBLANES, stride=pitch), :] for v in range(pitch)]
            sums = window_sums(extended(rows))[w]
            for v in range(pitch):
                cnt = counts[w][v]
                inv = 1.0 / cnt
                pooled_g[pl.ds(v, V7X_SUBLANES, stride=pitch), :] = sums[v] * inv - rows[v]

    def stage_and_mix(nxt_ref, cur_ref):
        a_w, b_w = ya_ref.shape[1], yb_ref.shape[1]
        stage(nxt_ref)
        yc = []
        for g in range(len(C_WINDOWS)):
            cols = slice(g * gd, (g + 1) * gd)
            pooled = cur_ref[g, halo:halo + tm, :].astype(BF16)
            mixed = jnp.dot(pooled, wp_ref[g], preferred_element_type=F32) * cs_ref[:, cols]
            yc.append(mixed.astype(BF16))
        y = jnp.dot(ya_ref[...], wo_ref[0:a_w, :], preferred_element_type=F32)
        y += jnp.dot(yb_ref[...], wo_ref[a_w:a_w + b_w, :], preferred_element_type=F32)
        y += jnp.dot(jnp.concatenate(yc, axis=1), wo_ref[a_w + b_w:, :], preferred_element_type=F32)
        o_ref[...] = h_ref[...] + gt_ref[pl.ds(mod_row(i - 1), 1), :] * y

    pl.when(i == 0)(functools.partial(stage, pooled0_ref))
    for par, (nxt_ref, cur_ref) in enumerate(((pooled0_ref, pooled1_ref), (pooled1_ref, pooled0_ref))):
        pl.when((i > 0) & (lax.rem(i, 2) == par))(functools.partial(stage_and_mix, nxt_ref, cur_ref))


def _mix_out(h, mod, j_gate, ya, yb, cu, w_pool, c_scale, w_out, *, mod_row, seg_len, tm):
    n, d = h.shape
    c_w = cu.shape[1]
    assert n % tm == 0 and seg_len % tm == 0
    halo = V7X_SUBLANES
    per_tile = tm // halo
    n_tiles, last = n // tm, n // halo - 1
    prev_spec = lambda w: pl.BlockSpec((tm, w), lambda i: (jnp.maximum(i - 1, 0), 0))
    staged = lambda i: jnp.minimum(i, n_tiles - 1)
    limit = _vmem_limit(
        [2 * _nbytes((tm, d), F32), _nbytes((tm, ya.shape[1]), BF16),
         _nbytes((tm, yb.shape[1]), BF16), _nbytes((tm, c_w), F32)],
        [_nbytes(w_out.shape, BF16), _nbytes(w_pool.shape, BF16), 3 * _nbytes((tm + 2 * halo, c_w), F32)],
        2 * _nbytes((tm, d), F32))
    group_rows = pltpu.VMEM((c_w // C_GROUP_DIM, tm + 2 * halo, C_GROUP_DIM), F32)
    return pl.pallas_call(
        functools.partial(_out_kernel, mod_row=mod_row, seg_len=seg_len),
        out_shape=jax.ShapeDtypeStruct((n, d), F32),
        grid=(n_tiles + 1,),
        in_specs=[
            prev_spec(d),
            pl.BlockSpec((MOD_ROWS, d), lambda i: (0, j_gate)),
            prev_spec(ya.shape[1]), prev_spec(yb.shape[1]),
            pl.BlockSpec((tm, c_w), lambda i: (staged(i), 0)),
            pl.BlockSpec((halo, c_w), lambda i: (jnp.maximum(staged(i) * per_tile - 1, 0), 0)),
            pl.BlockSpec((halo, c_w), lambda i: (jnp.minimum((staged(i) + 1) * per_tile, last), 0)),
            _resident(w_pool.shape), _resident((1, c_w)), _resident(w_out.shape),
        ],
        out_specs=prev_spec(d),
        scratch_shapes=[group_rows, group_rows, group_rows],
        compiler_params=pltpu.CompilerParams(
            dimension_semantics=("arbitrary",), vmem_limit_bytes=limit),
        name="mixer_out_proj",
    )(h, mod, ya, yb, cu, cu, cu, w_pool, c_scale, w_out)


def _rope_tables(seq, dim):
    n = dim // 4
    t = np.arange(seq)
    rows = (t // GRID_W).astype(np.float32)
    cols = (t % GRID_W).astype(np.float32)
    inv = (np.float32(ROPE_THETA) ** (-np.arange(n, dtype=np.float32) / np.float32(n))).astype(np.float32)
    ang = np.concatenate([rows[:, None] * inv[None, :], cols[:, None] * inv[None, :]], axis=-1)
    cos, sin = np.cos(ang).astype(np.float32), np.sin(ang).astype(np.float32)
    pad = np.zeros((seq, V7X_LANES - dim), np.float32)
    return (np.concatenate([cos, cos, pad], axis=-1), np.concatenate([-sin, sin, pad], axis=-1))


def _identity_tables(rows, dim):
    one = np.concatenate([np.ones((rows, dim), np.float32),
                          np.zeros((rows, V7X_LANES - dim), np.float32)], -1)
    return one, np.zeros((rows, V7X_LANES), np.float32)


def _w_uq_columns(heads):
    hd = B_NOPE_DIM + B_ROPE_DIM
    half = B_ROPE_DIM // 2
    out = []
    for h in range(heads):
        b = h * hd
        x1 = np.arange(b + B_NOPE_DIM, b + B_NOPE_DIM + half)
        x2 = x1 + half
        out += [np.arange(b, b + B_NOPE_DIM), x1, x2, x2, x1]
    return np.concatenate(out)


def _w_ukv_columns(heads):
    hd = B_NOPE_DIM + B_V_DIM
    ks = [np.arange(h * hd, h * hd + B_NOPE_DIM) for h in range(heads)]
    vs = [np.arange(h * hd + B_NOPE_DIM, (h + 1) * hd) for h in range(heads)]
    return np.concatenate(ks + vs)


def kernel(x, c, ctx, c_ctx, w_mod, b_mod, norm_g, ffn1_in, ffn1_out, w_in, a_q_g, a_k_g, b_q_g,
           b_kv_g, b_w_uq, b_w_ukv, c_w_pool, c_scale, w_out, ffn2_in, ffn2_out, final_g):
    batch, seq, d = x.shape
    ctx_len = ctx.shape[1]
    depth = w_mod.shape[0]
    assert batch + 1 <= MOD_ROWS
    b_q, b_kv = b_q_g.shape[1], b_kv_g.shape[1]
    b_heads = b_w_ukv.shape[2] // (B_NOPE_DIM + B_V_DIM)
    c_w = c_scale.shape[1]
    a_q = w_out.shape[1] - b_heads * B_V_DIM - c_w
    a_kv = a_q // A_GROUP
    dims = (a_q, a_kv, b_q, b_kv, b_heads, c_w)

    tm, tm_ctx = 512, ctx_len
    assert seq % 1024 == 0 and seq % GRID_W == 0 and ctx_len % 64 == 0
    tiles_per_seq = seq // tm
    lat_row = lambda i: lax.div(i, tiles_per_seq)
    ctx_row = lambda i: batch
    ffn_lat = dict(tm=1024, tf=512, slice_rows=128, mod_row=lambda t: lax.div(t, seq // 1024))
    ffn_ctx = dict(tm=batch * ctx_len, tf=512, slice_rows=128, mod_row=ctx_row)

    c_rows = jnp.concatenate([c, c_ctx[None, :], jnp.zeros((MOD_ROWS - batch - 1, d), F32)], axis=0)
    mod = _modulation(c_rows, w_mod, b_mod)

    tab_lat = _rope_tables(seq, A_HEAD_DIM) + _rope_tables(seq, B_ROPE_DIM)
    tab_ctx = _identity_tables(tm_ctx, A_HEAD_DIM) + _identity_tables(tm_ctx, B_ROPE_DIM)
    cols_uq, cols_ukv = _w_uq_columns(b_heads), _w_ukv_columns(b_heads)

    h = x.reshape(batch * seq, d)
    hc = ctx.reshape(batch * ctx_len, d)
    tq_a, tq_b, sub = 512, 2048, 2
    a_steps = batch * (a_kv // A_HEAD_DIM) * (seq // (tq_a * sub))
    b_steps = batch * b_heads * (seq // (tq_b * sub))
    rows_in, rows_out = ffn1_in.shape[1] // a_steps, ffn1_out.shape[1] // b_steps
    f1_in, f1_out = _layer_bf16(ffn1_in, 0), _layer_bf16(ffn1_out, 0)
    w_in_bf = w_in.astype(BF16)
    for i in range(depth):
        last = i == depth - 1
        ng = norm_g[i]
        w_uq_p = b_w_uq[i][:, cols_uq].astype(BF16)
        w_ukv_p = b_w_ukv[i][:, cols_ukv].astype(BF16)
        w_pool = c_w_pool[i].astype(BF16)
        side_a = [(ffn2_in, i, rows_in)] + ([] if last else [(ffn1_in, i + 1, rows_in)])
        side_b = [(ffn2_out, i, rows_out), (w_out, i, w_out.shape[1] // b_steps)]
        side_b += [] if last else [(ffn1_out, i + 1, rows_out)]
        gains = (a_q_g[i][None], a_k_g[i][None], b_q_g[i][None], b_kv_g[i][None])
        fg = final_g[None]

        ffn = functools.partial(_ffn, final_g=fg)
        h = ffn(h, mod[i], 0, ng[0:1], f1_in, f1_out, final_norm=False, **ffn_lat)
        hc = ffn(hc, mod[i], 0, ng[0:1], f1_in, f1_out, final_norm=False, **ffn_ctx)

        proj = functools.partial(_proj, norm_g=ng[1:2], w_in_all=w_in_bf, layer=i, a_q_g=gains[0], a_k_g=gains[1],
                                 b_q_g=gains[2], b_kv_g=gains[3], w_uq_p=w_uq_p, w_ukv_p=w_ukv_p,
                                 dims=dims)
        qa, ka, va, qb, kb, vb, cu = proj(h, mod[i], 3, tables=tab_lat, mod_row=lat_row,
                                          table_block=lambda t: lax.rem(t, tiles_per_seq), tm=tm)
        qac, kac, vac, qbc, kbc, vbc, cuc = proj(hc, mod[i], 3, tables=tab_ctx, mod_row=ctx_row,
                                                 table_block=lambda t: 0, tm=tm_ctx)

        attn_a = functools.partial(_attention, batch=batch, n_kv=a_kv // A_HEAD_DIM, n_stack=A_GROUP,
                                   dk=A_HEAD_DIM, dv=A_HEAD_DIM, tq=tq_a, chunk=ATTN_KEY_CHUNK)
        attn_b = functools.partial(_attention, batch=batch, n_kv=b_heads, n_stack=1,
                                   dk=B_QK_PAD, dv=B_V_DIM, tq=tq_b, chunk=ATTN_KEY_CHUNK)
        ya, cast_a = attn_a(qa, [(kac, vac), (ka, va)], side_a, sub_tiles=sub)
        yb, cast_b = attn_b(qb, [(kbc, vbc), (kb, vb)], side_b, sub_tiles=sub)
        f2_in, f2_out, w_o = cast_a[0], cast_b[0], cast_b[1]
        mix = functools.partial(_mix_out, w_pool=w_pool, c_scale=c_scale[i][None], w_out=w_o)
        h = mix(h, mod[i], 5, ya, yb, cu, mod_row=lat_row, seg_len=seq, tm=tm)
        h = ffn(h, mod[i], 6, ng[2:3], f2_in, f2_out, final_norm=last, **ffn_lat)
        if not last:
            yac, _ = attn_a(qac, [(kac, vac)], tq=ctx_len)
            ybc, _ = attn_b(qbc, [(kbc, vbc)], tq=ctx_len)
            hc = mix(hc, mod[i], 5, yac, ybc, cuc, mod_row=ctx_row, seg_len=ctx_len, tm=tm_ctx)
            hc = ffn(hc, mod[i], 6, ng[2:3], f2_in, f2_out, final_norm=False, **ffn_ctx)
            f1_in, f1_out = cast_a[1], cast_b[2]
    return h.reshape(batch, seq, d)
```

```python
import functools

import jax
import jax.numpy as jnp
import numpy as np
from jax import lax
from jax.experimental import pallas as pl
from jax.experimental.pallas import tpu as pltpu

F32 = jnp.float32
BF16 = jnp.bfloat16

GRID_W = 64
ROPE_THETA = 10000.0
NORM_EPS = 1e-6
A_HEAD_DIM = 128
A_GROUP = 4
B_NOPE_DIM = 128
B_ROPE_DIM = 64
B_V_DIM = 128
C_WINDOWS = (2, 4, 8, 16)
C_GROUP_DIM = 128
LOG2E = 1.4426950408889634

V7X_LANES = 128
V7X_SUBLANES = 8
V7X_MXU_DIM = 256
V7X_VMEM_BYTES = 64 * 1024 * 1024
V7X_VMEM_RESERVE_BYTES = 8 * 1024 * 1024

MOD_ROWS = V7X_SUBLANES
B_QK_PAD = V7X_MXU_DIM
ATTN_KEY_CHUNK = V7X_MXU_DIM
ATTN_LIVE_TILES = 24


def _nbytes(shape, dtype):
    return int(np.prod(shape)) * jnp.dtype(dtype).itemsize


def _vmem_limit(pipelined, resident, temps):
    need = 2 * sum(pipelined) + sum(resident) + temps
    return int(min(V7X_VMEM_BYTES - V7X_VMEM_RESERVE_BYTES, need + need // 4))


def _rms(x, eps=NORM_EPS):
    return x * lax.rsqrt(jnp.mean(x * x, axis=-1, keepdims=True) + eps)


def _resident(shape):
    nd = len(shape)
    return pl.BlockSpec(shape, lambda *_: (0,) * nd, pipeline_mode=pl.Buffered(1))


CAST_BLOCK_BYTES = 8 * 1024 * 1024


def _cast_kernel(w_ref, o_ref):
    o_ref[...] = w_ref[...].astype(BF16)


def _layer_bf16(w, layer):
    _, rows, cols = w.shape
    n_blocks = next(nb for nb in range(1, rows + 1)
                    if rows % nb == 0 and (rows // nb) % 16 == 0
                    and _nbytes((rows // nb, cols), F32) <= CAST_BLOCK_BYTES)
    tr = rows // n_blocks
    limit = _vmem_limit([_nbytes((tr, cols), F32), _nbytes((tr, cols), BF16)], [], 0)
    return pl.pallas_call(
        _cast_kernel,
        out_shape=jax.ShapeDtypeStruct((rows, cols), BF16),
        grid=(n_blocks,),
        in_specs=[pl.BlockSpec((None, tr, cols), lambda j: (layer, j, 0))],
        out_specs=pl.BlockSpec((tr, cols), lambda j: (j, 0)),
        compiler_params=pltpu.CompilerParams(
            dimension_semantics=("parallel",), vmem_limit_bytes=limit),
        name="weight_to_bf16",
    )(w)


def _mod_kernel(c_ref, w_ref, b_ref, o_ref):
    c = c_ref[...]
    act = (c * jax.nn.sigmoid(c)).astype(BF16)
    o_ref[...] = jnp.dot(act, w_ref[...].astype(BF16), preferred_element_type=F32) + b_ref[...]


def _modulation(c_rows, w_mod, b_mod, *, tn=1024):
    depth, d, nm = w_mod.shape
    assert nm % tn == 0
    limit = _vmem_limit([_nbytes((d, tn), F32), _nbytes((MOD_ROWS, tn), F32)],
                        [_nbytes((MOD_ROWS, d), F32)], _nbytes((d, tn), BF16))
    return pl.pallas_call(
        _mod_kernel,
        out_shape=jax.ShapeDtypeStruct((depth, MOD_ROWS, nm), F32),
        grid=(depth, nm // tn),
        in_specs=[
            pl.BlockSpec((MOD_ROWS, d), lambda l, j: (0, 0)),
            pl.BlockSpec((None, d, tn), lambda l, j: (l, 0, j)),
            pl.BlockSpec((None, 1, tn), lambda l, j: (l, 0, j)),
        ],
        out_specs=pl.BlockSpec((None, MOD_ROWS, tn), lambda l, j: (l, 0, j)),
        compiler_params=pltpu.CompilerParams(
            dimension_semantics=("parallel", "parallel"), vmem_limit_bytes=limit),
        name="modulation",
    )(c_rows, w_mod, b_mod.reshape(depth, 1, nm))


def _ffn_kernel(hn_ref, mod_ref, g_ref, wa_ref, wb_ref, wo_ref,
                o_ref, xn0_ref, xn1_ref, hs_ref, *, mod_row, final_norm, n_slices):
    r, k = pl.program_id(0), pl.program_id(1)
    n_tiles, nk = pl.num_programs(0) - 1, pl.num_programs(1)
    slice_rows, d = hn_ref.shape
    shift = lambda row: mod_ref[pl.ds(row, 1), 0:d]
    scale = lambda row: mod_ref[pl.ds(row, 1), d:2 * d]
    gate = lambda row: mod_ref[pl.ds(row, 1), 2 * d:3 * d]

    def stage_next_slice(nxt_ref):
        s = jnp.clip(k - 1, 0, n_slices - 1)
        rows = pl.ds(pl.multiple_of(s * slice_rows, 16), slice_rows)
        row = mod_row(jnp.minimum(r, n_tiles - 1))
        x = hn_ref[...]
        hs_ref[rows, :] = x
        y = _rms(x) * g_ref[0:1, :]
        y = y * (1.0 + scale(row)) + shift(row)
        nxt_ref[rows, :] = y.astype(BF16)

    def step(first, cur_ref, nxt_ref):
        if not first:
            stage_next_slice(nxt_ref)
        xn = cur_ref[...]
        a = jnp.dot(xn, wa_ref[...], preferred_element_type=F32)
        b = jnp.dot(xn, wb_ref[...], preferred_element_type=F32)
        g = (a * jax.nn.sigmoid(a) * b).astype(BF16)
        part = jnp.dot(g, wo_ref[...], preferred_element_type=F32)
        part = part * (0.5 * gate(mod_row(r - 1)))
        o_ref[...] = (hs_ref[...] if first else o_ref[...]) + part

    pl.when((r == 0) & (k > 0))(functools.partial(stage_next_slice, xn0_ref))
    for first in (True, False):
        for par, (nxt_ref, cur_ref) in enumerate(((xn0_ref, xn1_ref), (xn1_ref, xn0_ref))):
            cond = ((k == 0) if first else (k > 0)) & (r > 0) & (lax.rem(r, 2) == par)
            pl.when(cond)(functools.partial(step, first, cur_ref, nxt_ref))

    if final_norm:
        @pl.when((k == nk - 1) & (r > 0))
        def _():
            o_ref[...] = _rms(o_ref[...]) * g_ref[1:2, :]


def _ffn(h, mod, j0, norm_g, w_in, w_out, final_g, *, mod_row, final_norm, tm, tf, slice_rows):
    n, d = h.shape
    dff = w_out.shape[0]
    assert n % tm == 0 and dff % tf == 0 and tm % slice_rows == 0
    nk, n_tiles, n_slices = dff // tf, n // tm, tm // slice_rows
    assert n_slices <= nk - 1 and j0 % 3 == 0
    gains = jnp.concatenate([norm_g, final_g], axis=0)
    limit = _vmem_limit(
        [_nbytes((tm, d), F32), _nbytes((slice_rows, d), F32),
         _nbytes((d, tf), BF16) * 2, _nbytes((tf, d), BF16)],
        [2 * _nbytes((tm, d), BF16), _nbytes((tm, d), F32)],
        4 * _nbytes((tm, tf), F32) + _nbytes((tm, d), F32))

    def next_slice(r, k):
        tile = jnp.minimum(r, n_tiles - 1)
        return (tile * n_slices + jnp.clip(k - 1, 0, n_slices - 1), 0)

    chunk = lambda r, k: jnp.where(r == 0, 0, k)
    return pl.pallas_call(
        functools.partial(_ffn_kernel, mod_row=mod_row, final_norm=final_norm, n_slices=n_slices),
        out_shape=jax.ShapeDtypeStruct((n, d), F32),
        grid=(n_tiles + 1, nk),
        in_specs=[
            pl.BlockSpec((slice_rows, d), next_slice),
            pl.BlockSpec((MOD_ROWS, 3 * d), lambda r, k: (0, j0 // 3)),
            pl.BlockSpec((2, d), lambda r, k: (0, 0)),
            pl.BlockSpec((d, tf), lambda r, k: (0, chunk(r, k))),
            pl.BlockSpec((d, tf), lambda r, k: (0, chunk(r, k) + nk)),
            pl.BlockSpec((tf, d), lambda r, k: (chunk(r, k), 0)),
        ],
        out_specs=pl.BlockSpec((tm, d), lambda r, k: (jnp.maximum(r - 1, 0), 0)),
        scratch_shapes=[pltpu.VMEM((tm, d), BF16), pltpu.VMEM((tm, d), BF16), pltpu.VMEM((tm, d), F32)],
        compiler_params=pltpu.CompilerParams(
            dimension_semantics=("arbitrary", "arbitrary"), vmem_limit_bytes=limit),
        name="swiglu_half_step",
    )(h, mod, gains, w_in, w_in, w_out)


def _rope(x, cos, sin_signed):
    return x * cos + pltpu.roll(x, V7X_LANES // 2, 1) * sin_signed


def _proj_kernel(h_ref, sh_ref, sc_ref, ng_ref, win_ref, aqg_ref, akg_ref, bqg_ref, bkvg_ref,
                 wuq_ref, wukv_ref, ca_ref, sa_ref, cb_ref, sb_ref,
                 qa_ref, ka_ref, va_ref, qb_ref, kb_ref, vb_ref, cu_ref, *, mod_row, dims):
    a_q, a_kv, b_q, b_kv, b_heads, c_w = dims
    row = mod_row(pl.program_id(0))
    y = _rms(h_ref[...]) * ng_ref[...]
    y = y * (1.0 + sc_ref[pl.ds(row, 1), :]) + sh_ref[pl.ds(row, 1), :]
    u = jnp.dot(y.astype(BF16), win_ref[...], preferred_element_type=F32)

    ca, sa, cb, sb = ca_ref[...], sa_ref[...], cb_ref[...], sb_ref[...]
    hd = A_HEAD_DIM
    o_ak, o_av, o_bq = a_q, a_q + a_kv, a_q + 2 * a_kv
    o_bkv, o_bkr = o_bq + b_q, o_bq + b_q + b_kv
    o_cu = o_bkr + B_ROPE_DIM
    half = B_ROPE_DIM // 2

    qg = aqg_ref[...] * (hd ** -0.5 * LOG2E)
    for h in range(a_q // hd):
        x = _rms(u[:, h * hd:(h + 1) * hd]) * qg
        qa_ref[:, h * hd:(h + 1) * hd] = _rope(x, ca, sa).astype(BF16)
    kg = akg_ref[...]
    for h in range(a_kv // hd):
        x = _rms(u[:, o_ak + h * hd:o_ak + (h + 1) * hd]) * kg
        ka_ref[:, h * hd:(h + 1) * hd] = _rope(x, ca, sa).astype(BF16)
    va_ref[...] = u[:, o_av:o_av + a_kv].astype(BF16)

    xq = (_rms(u[:, o_bq:o_bq + b_q]) * bqg_ref[...]).astype(BF16)
    q = jnp.dot(xq, wuq_ref[...], preferred_element_type=F32)
    q_scale = (B_NOPE_DIM + B_ROPE_DIM) ** -0.5 * LOG2E
    xkv = (_rms(u[:, o_bkv:o_bkv + b_kv]) * bkvg_ref[...]).astype(BF16)
    kv = jnp.dot(xkv, wukv_ref[...], preferred_element_type=F32)
    kr = u[:, o_bkr:o_bkr + B_ROPE_DIM]
    kr = jnp.concatenate([kr, kr[:, half:], kr[:, :half]], axis=1)
    k_rope = _rope(kr, cb, sb).astype(BF16)
    for h in range(b_heads):
        o = h * B_QK_PAD
        qb_ref[:, o:o + B_NOPE_DIM] = (q[:, o:o + B_NOPE_DIM] * q_scale).astype(BF16)
        qr = _rope(q[:, o + B_NOPE_DIM:o + B_QK_PAD], cb, sb)
        qb_ref[:, o + B_NOPE_DIM:o + B_QK_PAD] = (qr * q_scale).astype(BF16)
        kb_ref[:, o:o + B_NOPE_DIM] = kv[:, h * B_NOPE_DIM:(h + 1) * B_NOPE_DIM].astype(BF16)
        kb_ref[:, o + B_NOPE_DIM:o + B_QK_PAD] = k_rope
    vb_ref[...] = kv[:, b_heads * B_NOPE_DIM:].astype(BF16)

    cu_ref[...] = u[:, o_cu:o_cu + c_w]


def _proj(h, mod, j0, norm_g, w_in_all, layer, a_q_g, a_k_g, b_q_g, b_kv_g, w_uq_p, w_ukv_p, tables,
          *, mod_row, table_block, dims, tm):
    n, d = h.shape
    a_q, a_kv, b_q, b_kv, b_heads, c_w = dims
    assert n % tm == 0
    ncols = w_in_all.shape[2]
    w_in_shape = w_in_all.shape[1:]
    mod_spec = lambda j: pl.BlockSpec((MOD_ROWS, d), lambda i: (0, j))
    tab_spec = pl.BlockSpec((tm, V7X_LANES), lambda i: (table_block(i), 0))
    row_spec = lambda w: pl.BlockSpec((tm, w), lambda i: (i, 0))
    out_widths = [(a_q, BF16), (a_kv, BF16), (a_kv, BF16), (b_heads * B_QK_PAD, BF16),
                  (b_heads * B_QK_PAD, BF16), (b_heads * B_V_DIM, BF16), (c_w, F32)]
    limit = _vmem_limit(
        [_nbytes((tm, d), F32)] + [_nbytes((tm, w), t) for w, t in out_widths]
        + [4 * _nbytes((tm, V7X_LANES), F32)],
        [_nbytes(w_in_shape, BF16), _nbytes(w_uq_p.shape, BF16), _nbytes(w_ukv_p.shape, BF16)],
        _nbytes((tm, ncols), F32) + _nbytes((tm, d), F32) + 2 * _nbytes((tm, 4 * B_QK_PAD), F32))
    return pl.pallas_call(
        functools.partial(_proj_kernel, mod_row=mod_row, dims=dims),
        out_shape=[jax.ShapeDtypeStruct((n, w), t) for w, t in out_widths],
        grid=(n // tm,),
        in_specs=[
            row_spec(d), mod_spec(j0), mod_spec(j0 + 1),
            _resident((1, d)),
            pl.BlockSpec((None,) + w_in_shape, lambda i: (layer, 0, 0), pipeline_mode=pl.Buffered(1)),
            _resident((1, A_HEAD_DIM)), _resident((1, A_HEAD_DIM)),
            _resident((1, b_q)), _resident((1, b_kv)),
            _resident(w_uq_p.shape), _resident(w_ukv_p.shape),
            tab_spec, tab_spec, tab_spec, tab_spec,
        ],
        out_specs=[row_spec(w) for w, _ in out_widths],
        compiler_params=pltpu.CompilerParams(
            dimension_semantics=("parallel",), vmem_limit_bytes=limit),
        name="mixer_in_proj",
    )(h, mod, mod, norm_g, w_in_all, a_q_g, a_k_g, b_q_g, b_kv_g, w_uq_p, w_ukv_p, *tables)


def _attn_kernel(*refs, n_stack, dk, dv, chunks, side_blocks, sub_tiles):
    n_side, n_kv_refs = len(side_blocks), 2 * len(chunks)
    q_ref, kv_refs = refs[0], refs[1:1 + n_kv_refs]
    side_in = refs[1 + n_kv_refs:1 + n_kv_refs + n_side]
    o_ref, side_out = refs[1 + n_kv_refs + n_side], refs[2 + n_kv_refs + n_side:]
    tq = q_ref.shape[0] // sub_tiles

    step_id = ((pl.program_id(0) * pl.num_programs(1) + pl.program_id(1)) * pl.num_programs(2)
               + pl.program_id(2))
    for w_ref, wo_ref, n_blocks in zip(side_in, side_out, side_blocks):
        def cast(w_ref=w_ref, wo_ref=wo_ref):
            wo_ref[...] = w_ref[...].astype(BF16)
        pl.when(step_id < n_blocks)(cast)

    m_rows = n_stack * tq

    def step(q, k, v, carry):
        m, acc = carry
        s = lax.dot_general(q, k, (((1,), (1,)), ((), ())), preferred_element_type=F32)
        m_new = jnp.maximum(m, jnp.max(s, axis=-1, keepdims=True))
        p = jnp.exp2(s - m_new).astype(BF16)
        v_ones = jnp.concatenate([v, jnp.ones((v.shape[0], V7X_LANES), BF16)], axis=1)
        acc = jnp.exp2(m - m_new) * acc + jnp.dot(p, v_ones, preferred_element_type=F32)
        return m_new, acc

    for t in range(sub_tiles):
        rows = slice(t * tq, (t + 1) * tq)
        q = jnp.concatenate([q_ref[rows, h * dk:(h + 1) * dk] for h in range(n_stack)], axis=0)
        carry = (jnp.full((m_rows, 1), -jnp.inf, F32), jnp.zeros((m_rows, dv + V7X_LANES), F32))
        for seg, chunk in enumerate(chunks):
            k_ref, v_ref = kv_refs[2 * seg], kv_refs[2 * seg + 1]
            for c in range(k_ref.shape[0] // chunk):
                keys = slice(c * chunk, (c + 1) * chunk)
                carry = step(q, k_ref[keys, :], v_ref[keys, :], carry)
        _, acc = carry
        out = acc[:, :dv] / acc[:, dv:dv + dv]
        for h in range(n_stack):
            o_ref[rows, h * dv:(h + 1) * dv] = out[h * tq:(h + 1) * tq].astype(BF16)


def _attention(q, segments, side=(), *, batch, n_kv, n_stack, dk, dv, tq, chunk, sub_tiles=1):
    nq = q.shape[0]
    lq = nq // batch
    tq_blk = tq * sub_tiles
    assert lq % tq_blk == 0 and dv == V7X_LANES
    tiles = lq // tq_blk
    n_steps = batch * n_kv * tiles
    in_specs = [pl.BlockSpec((tq_blk, n_stack * dk), lambda b, g, i: (b * tiles + i, g))]
    args, chunks, kv_bytes = [q], [], []
    for k, v in segments:
        lk = k.shape[0] // batch
        chunks.append(min(chunk, lk))
        assert lk % chunks[-1] == 0
        in_specs.append(pl.BlockSpec((lk, dk), lambda b, g, i: (b, g)))
        in_specs.append(pl.BlockSpec((lk, dv), lambda b, g, i: (b, g)))
        args += [k, v]
        kv_bytes += [_nbytes((lk, dk), BF16), _nbytes((lk, dv), BF16)]
    out_shape = [jax.ShapeDtypeStruct((nq, n_kv * n_stack * dv), BF16)]
    out_specs = [pl.BlockSpec((tq_blk, n_stack * dv), lambda b, g, i: (b * tiles + i, g))]
    side_blocks, side_bytes = [], []
    for w, layer, rows in side:
        _, w_rows, w_cols = w.shape
        n_blocks = w_rows // rows
        assert w_rows % rows == 0 and rows % 16 == 0 and n_blocks <= n_steps

        def block(b, g, i, n_blocks=n_blocks):
            return jnp.minimum((b * n_kv + g) * tiles + i, n_blocks - 1)

        in_specs.append(pl.BlockSpec((None, rows, w_cols),
                                     lambda b, g, i, layer=layer, block=block: (layer, block(b, g, i), 0)))
        out_specs.append(pl.BlockSpec((rows, w_cols), lambda b, g, i, block=block: (block(b, g, i), 0)))
        out_shape.append(jax.ShapeDtypeStruct((w_rows, w_cols), BF16))
        args.append(w)
        side_blocks.append(n_blocks)
        side_bytes += [_nbytes((rows, w_cols), F32), _nbytes((rows, w_cols), BF16)]
    m_rows = n_stack * tq
    limit = _vmem_limit(
        [_nbytes((tq_blk, n_stack * dk), BF16), _nbytes((tq_blk, n_stack * dv), BF16)] + kv_bytes
        + side_bytes,
        [], ATTN_LIVE_TILES * _nbytes((m_rows, max(chunks)), F32) + 4 * _nbytes((m_rows, V7X_LANES), F32))
    outs = pl.pallas_call(
        functools.partial(_attn_kernel, n_stack=n_stack, dk=dk, dv=dv, chunks=tuple(chunks),
                          side_blocks=tuple(side_blocks), sub_tiles=sub_tiles),
        out_shape=out_shape,
        grid=(batch, n_kv, tiles),
        in_specs=in_specs,
        out_specs=out_specs,
        compiler_params=pltpu.CompilerParams(
            dimension_semantics=("arbitrary", "arbitrary", "arbitrary"), vmem_limit_bytes=limit),
        name="attention",
    )(*args)
    return outs[0], outs[1:]


def _out_kernel(h_ref, gt_ref, ya_ref, yb_ref, cu_ref, cup_ref, cun_ref, wp_ref, cs_ref, wo_ref,
                o_ref, ext_ref, pooled0_ref, pooled1_ref, *, mod_row, seg_len):
    i = pl.program_id(0)
    n_tiles = pl.num_programs(0) - 1
    tm = h_ref.shape[0]
    halo = V7X_SUBLANES
    gd = C_GROUP_DIM
    n_rows = tm + 2 * halo
    pitch = n_rows // V7X_SUBLANES

    def stage(pooled_ref):
        pos0 = lax.rem(jnp.minimum(i, n_tiles - 1) * tm, seg_len)
        top_ok, bot_ok = pos0 > 0, pos0 + tm < seg_len
        sub = lax.broadcasted_iota(jnp.int32, (V7X_SUBLANES, V7X_LANES), 0)
        top_ind = jnp.where((sub == 0) & jnp.logical_not(top_ok), 0.0, 1.0)
        bot_ind = jnp.where((sub == V7X_SUBLANES - 1) & jnp.logical_not(bot_ok), 0.0, 1.0)

        def extended(vregs):
            ext = {v: vregs[v] for v in range(pitch)}
            for v in range(-halo, 0):
                x = vregs[v + pitch]
                ext[v] = x if isinstance(x, float) else pltpu.roll(x, 1, 0)
            for v in range(pitch, pitch + halo):
                x = vregs[v - pitch]
                ext[v] = x if isinstance(x, float) else pltpu.roll(x, V7X_SUBLANES - 1, 0)
            return ext

        def window_sums(ext):
            c2 = {v: ext[v - 1] + ext[v] for v in range(-halo + 1, pitch + halo)}
            c4 = {v: c2[v - 1] + c2[v + 1] for v in range(-halo + 2, pitch + halo - 1)}
            c8 = {v: c4[v - 2] + c4[v + 2] for v in range(-halo + 4, pitch + halo - 3)}
            c16 = {v: c8[v - 4] + c8[v + 4] for v in range(0, pitch)}
            return dict(zip(C_WINDOWS, (c2, c4, c8, c16)))

        ind = [top_ind if v < halo else bot_ind if v >= pitch - halo else 1.0 for v in range(pitch)]
        counts = window_sums(extended(ind))
        for g, w in enumerate(C_WINDOWS):
            cols = slice(g * gd, (g + 1) * gd)
            ext_ref[g, 0:halo, :] = jnp.where(top_ok, cup_ref[:, cols], 0.0)
            ext_ref[g, halo:halo + tm, :] = cu_ref[:, cols]
            ext_ref[g, halo + tm:, :] = jnp.where(bot_ok, cun_ref[:, cols], 0.0)
            ext_g, pooled_g = ext_ref.at[g], pooled_ref.at[g]
            rows = [ext_g[pl.ds(v, V7X_SUBLANES, stride=pitch), :] for v in range(pitch)]
            sums = window_sums(extended(rows))[w]
            for v in range(pitch):
                cnt = counts[w][v]
                inv = 1.0 / cnt
                pooled_g[pl.ds(v, V7X_SUBLANES, stride=pitch), :] = sums[v] * inv - rows[v]

    def stage_and_mix(nxt_ref, cur_ref):
        a_w, b_w = ya_ref.shape[1], yb_ref.shape[1]
        stage(nxt_ref)
        yc = []
        for g in range(len(C_WINDOWS)):
            cols = slice(g * gd, (g + 1) * gd)
            pooled = cur_ref[g, halo:halo + tm, :].astype(BF16)
            mixed = jnp.dot(pooled, wp_ref[g], preferred_element_type=F32) * cs_ref[:, cols]
            yc.append(mixed.astype(BF16))
        y = jnp.dot(ya_ref[...], wo_ref[0:a_w, :], preferred_element_type=F32)
        y += jnp.dot(yb_ref[...], wo_ref[a_w:a_w + b_w, :], preferred_element_type=F32)
        y += jnp.dot(jnp.concatenate(yc, axis=1), wo_ref[a_w + b_w:, :], preferred_element_type=F32)
        o_ref[...] = h_ref[...] + gt_ref[pl.ds(mod_row(i - 1), 1), :] * y

    pl.when(i == 0)(functools.partial(stage, pooled0_ref))
    for par, (nxt_ref, cur_ref) in enumerate(((pooled0_ref, pooled1_ref), (pooled1_ref, pooled0_ref))):
        pl.when((i > 0) & (lax.rem(i, 2) == par))(functools.partial(stage_and_mix, nxt_ref, cur_ref))


def _mix_out(h, mod, j_gate, ya, yb, cu, w_pool, c_scale, w_out, *, mod_row, seg_len, tm):
    n, d = h.shape
    c_w = cu.shape[1]
    assert n % tm == 0 and seg_len % tm == 0
    halo = V7X_SUBLANES
    per_tile = tm // halo
    n_tiles, last = n // tm, n // halo - 1
    prev_spec = lambda w: pl.BlockSpec((tm, w), lambda i: (jnp.maximum(i - 1, 0), 0))
    staged = lambda i: jnp.minimum(i, n_tiles - 1)
    limit = _vmem_limit(
        [2 * _nbytes((tm, d), F32), _nbytes((tm, ya.shape[1]), BF16),
         _nbytes((tm, yb.shape[1]), BF16), _nbytes((tm, c_w), F32)],
        [_nbytes(w_out.shape, BF16), _nbytes(w_pool.shape, BF16), 3 * _nbytes((tm + 2 * halo, c_w), F32)],
        2 * _nbytes((tm, d), F32))
    group_rows = pltpu.VMEM((c_w // C_GROUP_DIM, tm + 2 * halo, C_GROUP_DIM), F32)
    return pl.pallas_call(
        functools.partial(_out_kernel, mod_row=mod_row, seg_len=seg_len),
        out_shape=jax.ShapeDtypeStruct((n, d), F32),
        grid=(n_tiles + 1,),
        in_specs=[
            prev_spec(d),
            pl.BlockSpec((MOD_ROWS, d), lambda i: (0, j_gate)),
            prev_spec(ya.shape[1]), prev_spec(yb.shape[1]),
            pl.BlockSpec((tm, c_w), lambda i: (staged(i), 0)),
            pl.BlockSpec((halo, c_w), lambda i: (jnp.maximum(staged(i) * per_tile - 1, 0), 0)),
            pl.BlockSpec((halo, c_w), lambda i: (jnp.minimum((staged(i) + 1) * per_tile, last), 0)),
            _resident(w_pool.shape), _resident((1, c_w)), _resident(w_out.shape),
        ],
        out_specs=prev_spec(d),
        scratch_shapes=[group_rows, group_rows, group_rows],
        compiler_params=pltpu.CompilerParams(
            dimension_semantics=("arbitrary",), vmem_limit_bytes=limit),
        name="mixer_out_proj",
    )(h, mod, ya, yb, cu, cu, cu, w_pool, c_scale, w_out)


def _rope_tables(seq, dim):
    n = dim // 4
    t = np.arange(seq)
    rows = (t // GRID_W).astype(np.float32)
    cols = (t % GRID_W).astype(np.float32)
    inv = (np.float32(ROPE_THETA) ** (-np.arange(n, dtype=np.float32) / np.float32(n))).astype(np.float32)
    ang = np.concatenate([rows[:, None] * inv[None, :], cols[:, None] * inv[None, :]], axis=-1)
    cos, sin = np.cos(ang).astype(np.float32), np.sin(ang).astype(np.float32)
    pad = np.zeros((seq, V7X_LANES - dim), np.float32)
    return (np.concatenate([cos, cos, pad], axis=-1), np.concatenate([-sin, sin, pad], axis=-1))


def _identity_tables(rows, dim):
    one = np.concatenate([np.ones((rows, dim), np.float32),
                          np.zeros((rows, V7X_LANES - dim), np.float32)], -1)
    return one, np.zeros((rows, V7X_LANES), np.float32)


def _w_uq_columns(heads):
    hd = B_NOPE_DIM + B_ROPE_DIM
    half = B_ROPE_DIM // 2
    out = []
    for h in range(heads):
        b = h * hd
        x1 = np.arange(b + B_NOPE_DIM, b + B_NOPE_DIM + half)
        x2 = x1 + half
        out += [np.arange(b, b + B_NOPE_DIM), x1, x2, x2, x1]
    return np.concatenate(out)


def _w_ukv_columns(heads):
    hd = B_NOPE_DIM + B_V_DIM
    ks = [np.arange(h * hd, h * hd + B_NOPE_DIM) for h in range(heads)]
    vs = [np.arange(h * hd + B_NOPE_DIM, (h + 1) * hd) for h in range(heads)]
    return np.concatenate(ks + vs)


def kernel(x, c, ctx, c_ctx, w_mod, b_mod, norm_g, ffn1_in, ffn1_out, w_in, a_q_g, a_k_g, b_q_g,
           b_kv_g, b_w_uq, b_w_ukv, c_w_pool, c_scale, w_out, ffn2_in, ffn2_out, final_g):
    batch, seq, d = x.shape
    ctx_len = ctx.shape[1]
    depth = w_mod.shape[0]
    assert batch + 1 <= MOD_ROWS
    b_q, b_kv = b_q_g.shape[1], b_kv_g.shape[1]
    b_heads = b_w_ukv.shape[2] // (B_NOPE_DIM + B_V_DIM)
    c_w = c_scale.shape[1]
    a_q = w_out.shape[1] - b_heads * B_V_DIM - c_w
    a_kv = a_q // A_GROUP
    dims = (a_q, a_kv, b_q, b_kv, b_heads, c_w)

    tm, tm_ctx = 512, ctx_len
    assert seq % 1024 == 0 and seq % GRID_W == 0 and ctx_len % 64 == 0
    tiles_per_seq = seq // tm
    lat_row = lambda i: lax.div(i, tiles_per_seq)
    ctx_row = lambda i: batch
    ffn_lat = dict(tm=1024, tf=512, slice_rows=128, mod_row=lambda t: lax.div(t, seq // 1024))
    ffn_ctx = dict(tm=batch * ctx_len, tf=512, slice_rows=128, mod_row=ctx_row)

    c_rows = jnp.concatenate([c, c_ctx[None, :], jnp.zeros((MOD_ROWS - batch - 1, d), F32)], axis=0)
    mod = _modulation(c_rows, w_mod, b_mod)

    tab_lat = _rope_tables(seq, A_HEAD_DIM) + _rope_tables(seq, B_ROPE_DIM)
    tab_ctx = _identity_tables(tm_ctx, A_HEAD_DIM) + _identity_tables(tm_ctx, B_ROPE_DIM)
    cols_uq, cols_ukv = _w_uq_columns(b_heads), _w_ukv_columns(b_heads)

    h = x.reshape(batch * seq, d)
    hc = ctx.reshape(batch * ctx_len, d)
    tq_a, tq_b, sub = 512, 2048, 2
    a_steps = batch * (a_kv // A_HEAD_DIM) * (seq // (tq_a * sub))
    b_steps = batch * b_heads * (seq // (tq_b * sub))
    rows_in, rows_out = ffn1_in.shape[1] // a_steps, 2 * ffn1_out.shape[1] // b_steps
    f1_in, f1_out = _layer_bf16(ffn1_in, 0), _layer_bf16(ffn1_out, 0)
    w_in_bf = w_in.astype(BF16)
    for i in range(depth):
        last = i == depth - 1
        ng = norm_g[i]
        w_uq_p = b_w_uq[i][:, cols_uq].astype(BF16)
        w_ukv_p = b_w_ukv[i][:, cols_ukv].astype(BF16)
        w_pool = c_w_pool[i].astype(BF16)
        side_a = [(ffn2_in, i, rows_in)] + ([] if last else [(ffn1_in, i + 1, rows_in)])
        side_b = [(ffn2_out, i, rows_out), (w_out, i, w_out.shape[1] // b_steps)]
        side_b += [] if last else [(ffn1_out, i + 1, rows_out)]
        gains = (a_q_g[i][None], a_k_g[i][None], b_q_g[i][None], b_kv_g[i][None])
        fg = final_g[None]

        ffn = functools.partial(_ffn, final_g=fg)
        h = ffn(h, mod[i], 0, ng[0:1], f1_in, f1_out, final_norm=False, **ffn_lat)
        hc = ffn(hc, mod[i], 0, ng[0:1], f1_in, f1_out, final_norm=False, **ffn_ctx)

        proj = functools.partial(_proj, norm_g=ng[1:2], w_in_all=w_in_bf, layer=i, a_q_g=gains[0], a_k_g=gains[1],
                                 b_q_g=gains[2], b_kv_g=gains[3], w_uq_p=w_uq_p, w_ukv_p=w_ukv_p,
                                 dims=dims)
        qa, ka, va, qb, kb, vb, cu = proj(h, mod[i], 3, tables=tab_lat, mod_row=lat_row,
                                          table_block=lambda t: lax.rem(t, tiles_per_seq), tm=tm)
        qac, kac, vac, qbc, kbc, vbc, cuc = proj(hc, mod[i], 3, tables=tab_ctx, mod_row=ctx_row,
                                                 table_block=lambda t: 0, tm=tm_ctx)

        attn_a = functools.partial(_attention, batch=batch, n_kv=a_kv // A_HEAD_DIM, n_stack=A_GROUP,
                                   dk=A_HEAD_DIM, dv=A_HEAD_DIM, tq=tq_a, chunk=ATTN_KEY_CHUNK)
        attn_b = functools.partial(_attention, batch=batch, n_kv=b_heads, n_stack=1,
                                   dk=B_QK_PAD, dv=B_V_DIM, tq=tq_b, chunk=ATTN_KEY_CHUNK)
        ya, cast_a = attn_a(qa, [(kac, vac), (ka, va)], side_a, sub_tiles=sub)
        yb, cast_b = attn_b(qb, [(kbc, vbc), (kb, vb)], side_b, sub_tiles=sub)
        f2_in, f2_out, w_o = cast_a[0], cast_b[0], cast_b[1]
        mix = functools.partial(_mix_out, w_pool=w_pool, c_scale=c_scale[i][None], w_out=w_o)
        h = mix(h, mod[i], 5, ya, yb, cu, mod_row=lat_row, seg_len=seq, tm=tm)
        h = ffn(h, mod[i], 6, ng[2:3], f2_in, f2_out, final_norm=last, **ffn_lat)
        if not last:
            yac, _ = attn_a(qac, [(kac, vac)], tq=ctx_len)
            ybc, _ = attn_b(qbc, [(kbc, vbc)], tq=ctx_len)
            hc = mix(hc, mod[i], 5, yac, ybc, cuc, mod_row=ctx_row, seg_len=ctx_len, tm=tm_ctx)
            hc = ffn(hc, mod[i], 6, ng[2:3], f2_in, f2_out, final_norm=False, **ffn_ctx)
            f1_in, f1_out = cast_a[1], cast_b[2]
    return h.reshape(batch, seq, d)
```
